```python
import math
import jax
import jax.numpy as jnp
from jax import lax
import numpy as np

D_MODEL = 1024
BATCH = 8
SEQ = 4096
DEPTH = 1
DEC_BATCH = 32
DEC_SEQ = 32
PAST_LEN = 2048

CHUNK = 64
EPS = 1e-6
MLA_HEADS = 8
Q_LORA = 256
KV_LORA = 128
NOPE_DIM = 64
ROPE_DIM = 32
V_DIM = 64
ROPE_THETA = 10000.0
QBLK = 128
GLA_HEADS = 4
GLA_DK = 128
GLA_DV = 256
GATE_RANK = 16
GATE_TEMP = 16.0
GLA_BLOCK = 16
N_GROUPS = 4
EXPERTS_PER_GROUP = 8
N_EXPERTS = 32
TOP_K = 2
EXPERT_FF = 512
ROUTE_BLOCK = 128
IN_SPLITS = (Q_LORA, KV_LORA, ROPE_DIM, GLA_HEADS * GLA_DK, GLA_HEADS * GLA_DK, GLA_HEADS * GLA_DV, GATE_RANK, GLA_HEADS * GLA_DV, D_MODEL, D_MODEL)
IN_COLS = Q_LORA + KV_LORA + ROPE_DIM + 2 * GLA_HEADS * GLA_DK + 2 * GLA_HEADS * GLA_DV + GATE_RANK + 2 * D_MODEL

kernel_name = "streaming_mla_gla_hier_moe"


def rmsnorm(x, g):
    xf = x.astype(jnp.float32)
    y = xf * lax.rsqrt(jnp.mean(xf * xf, axis=-1, keepdims=True) + EPS)
    return (y * g.astype(jnp.float32)).astype(x.dtype)


def rope(x, pos):
    half = ROPE_DIM // 2
    inv = ROPE_THETA ** (-jnp.arange(half, dtype=jnp.float32) / half)
    ang = pos.astype(jnp.float32)[:, None] * inv[None, :]
    cos = jnp.cos(ang)[None, :, None, :]
    sin = jnp.sin(ang)[None, :, None, :]
    x1 = x[..., :half].astype(jnp.float32)
    x2 = x[..., half:].astype(jnp.float32)
    return jnp.concatenate([x1 * cos - x2 * sin, x2 * cos + x1 * sin], axis=-1).astype(x.dtype)


def split_cols(z):
    idx = np.cumsum(np.array(IN_SPLITS))[:-1].tolist()
    return jnp.split(z, idx, axis=-1)


def attend_block(qn, qp, q_pos, k_nope, k_pe, v, k_pos):
    scale = 1.0 / math.sqrt(NOPE_DIM + ROPE_DIM)
    s = (jnp.einsum("bqhd,bkhd->bhqk", qn, k_nope) + jnp.einsum("bqhr,bkr->bhqk", qp, k_pe)).astype(jnp.float32) * scale
    allowed = (k_pos[None, :] // CHUNK) <= (q_pos[:, None] // CHUNK)
    s = jnp.where(allowed[None, None], s, -jnp.inf)
    p = jax.nn.softmax(s, axis=-1).astype(v.dtype)
    return jnp.einsum("bhqk,bkhd->bqhd", p, v)


def mla_attention(qn, qp, q_pos, ckv_all, kpe_all, k_pos, w_ukv):
    b, t, _ = ckv_all.shape
    l = qn.shape[1]
    kv = (ckv_all @ w_ukv).reshape(b, t, MLA_HEADS, NOPE_DIM + V_DIM)
    k_nope, v = kv[..., :NOPE_DIM], kv[..., NOPE_DIM:]
    if l % QBLK == 0 and l > QBLK:
        nb = l // QBLK
        qn_b = qn.reshape(b, nb, QBLK, MLA_HEADS, NOPE_DIM).swapaxes(0, 1)
        qp_b = qp.reshape(b, nb, QBLK, MLA_HEADS, ROPE_DIM).swapaxes(0, 1)
        pos_b = q_pos.reshape(nb, QBLK)

        def blk(args):
            a_qn, a_qp, a_pos = args
            return attend_block(a_qn, a_qp, a_pos, k_nope, kpe_all, v, k_pos)

        o = lax.map(blk, (qn_b, qp_b, pos_b))
        o = o.swapaxes(0, 1).reshape(b, l, MLA_HEADS * V_DIM)
    else:
        o = attend_block(qn, qp, q_pos, k_nope, kpe_all, v, k_pos).reshape(b, l, MLA_HEADS * V_DIM)
    return o


def gla_block(q, k, v, la, s):
    q = q.astype(jnp.float32)
    k = k.astype(jnp.float32)
    v = v.astype(jnp.float32)
    bcum = jnp.cumsum(la, axis=1)
    qe = q * jnp.exp(bcum)
    ke = k * jnp.exp(-bcum)
    n = q.shape[1]
    causal = jnp.tril(jnp.ones((n, n), dtype=bool))
    att = jnp.where(causal[None, None], jnp.einsum("blhk,bmhk->bhlm", qe, ke), 0.0)
    o = jnp.einsum("blhk,bhkv->blhv", qe, s) + jnp.einsum("bhlm,bmhv->blhv", att, v)
    b_last = bcum[:, -1]
    s_new = jnp.exp(b_last)[..., None] * s + jnp.einsum("blhk,blhv->bhkv", k * jnp.exp(b_last[:, None] - bcum), v)
    return o, s_new


def gla(q, k, v, la, s0):
    b, l = q.shape[:2]
    if l % GLA_BLOCK == 0:
        nb = l // GLA_BLOCK

        def to_blocks(t):
            return t.reshape((b, nb, GLA_BLOCK) + t.shape[2:]).swapaxes(0, 1)

        def step(s, xs):
            o, s = gla_block(xs[0], xs[1], xs[2], xs[3], s)
            return s, o

        s_fin, o = lax.scan(step, s0, (to_blocks(q), to_blocks(k), to_blocks(v), to_blocks(la)))
        o = o.swapaxes(0, 1).reshape(b, l, GLA_HEADS, GLA_DV)
    else:
        o, s_fin = gla_block(q, k, v, la, s0)
    return o, s_fin


def hier_moe(h, w_rg, b_rg, w_re, b_re, w_gate, w_up, w_down):
    n, d = h.shape
    g_prob = jax.nn.softmax((h @ w_rg + b_rg).astype(jnp.float32), axis=-1)
    p_top, g_top = lax.top_k(g_prob, 1)
    e_logits = (h @ w_re + b_re).astype(jnp.float32).reshape(n, N_GROUPS, EXPERTS_PER_GROUP)
    e_in_group = e_logits[jnp.arange(n), g_top[:, 0]]
    v_top, i_top = lax.top_k(e_in_group, TOP_K)
    gates = jax.nn.softmax(v_top, axis=-1) * p_top
    expert_id = g_top * EXPERTS_PER_GROUP + i_top
    a = n * TOP_K
    flat_e = expert_id.reshape(a)
    flat_w = gates.reshape(a)
    flat_tok = jnp.arange(a, dtype=jnp.int32) // TOP_K
    order = jnp.argsort(flat_e)
    sorted_e = flat_e[order]
    counts = jnp.bincount(flat_e, length=N_EXPERTS)
    padded = (counts + ROUTE_BLOCK - 1) // ROUTE_BLOCK * ROUTE_BLOCK
    pad_end = jnp.cumsum(padded)
    pad_start = pad_end - padded
    seg_start = jnp.cumsum(counts) - counts
    dest = pad_start[sorted_e] + jnp.arange(a, dtype=jnp.int32) - seg_start[sorted_e]
    n_blocks = -(-a // ROUTE_BLOCK) + N_EXPERTS
    total = n_blocks * ROUTE_BLOCK
    tok_buf = jnp.full((total,), n, dtype=jnp.int32).at[dest].set(flat_tok[order])
    w_buf = jnp.zeros((total,), dtype=jnp.float32).at[dest].set(flat_w[order])
    block_expert = jnp.minimum(jnp.searchsorted(pad_end, jnp.arange(n_blocks, dtype=jnp.int32) * ROUTE_BLOCK, side="right"), N_EXPERTS - 1)
    h_pad = jnp.concatenate([h, jnp.zeros((1, d), h.dtype)], axis=0)
    x_blocks = h_pad[tok_buf].reshape(n_blocks, ROUTE_BLOCK, d)

    def expert_block(args):
        xb, e = args
        return (jax.nn.silu(xb @ w_gate[e]) * (xb @ w_up[e])) @ w_down[e]

    out = lax.map(expert_block, (x_blocks, block_expert)).reshape(total, d)
    y = jax.ops.segment_sum(out * w_buf[:, None], tok_buf, num_segments=n + 1)[:n]
    return y.astype(h.dtype)


def trunk_layer(x, ckv_past, kpe_past, s0, p):
    b, l, d = x.shape
    past = ckv_past.shape[1]
    pos = past + jnp.arange(l, dtype=jnp.int32)
    h = rmsnorm(x, p["g_norm_mix"])
    cq, ckv_raw, kpe_raw, gq, gk, gv, glr, gr, ga, gb = split_cols(h @ p["w_in"])
    q = (rmsnorm(cq, p["g_qnorm"]) @ p["w_uq"]).reshape(b, l, MLA_HEADS, NOPE_DIM + ROPE_DIM)
    q_nope = q[..., :NOPE_DIM]
    q_pe = rope(q[..., NOPE_DIM:], pos)
    ckv = rmsnorm(ckv_raw, p["g_kvnorm"])
    kpe = rope(kpe_raw[:, :, None, :], pos)[:, :, 0, :]
    ckv_all = jnp.concatenate([ckv_past.astype(ckv.dtype), ckv], axis=1)
    kpe_all = jnp.concatenate([kpe_past.astype(kpe.dtype), kpe], axis=1)
    k_pos = jnp.arange(past + l, dtype=jnp.int32)
    o_a = mla_attention(q_nope, q_pe, pos, ckv_all, kpe_all, k_pos, p["w_ukv"])
    qg = gq.reshape(b, l, GLA_HEADS, GLA_DK) * (GLA_DK ** -0.5)
    kg = gk.reshape(b, l, GLA_HEADS, GLA_DK)
    vg = gv.reshape(b, l, GLA_HEADS, GLA_DV)
    la = jax.nn.log_sigmoid((glr @ p["w_gate2"] + p["b_gate2"]).astype(jnp.float32)).reshape(b, l, GLA_HEADS, GLA_DK) / GATE_TEMP
    o_g, s_new = gla(qg, kg, vg, la, s0)
    o_b = (rmsnorm(o_g, p["g_gla_norm"]) * jax.nn.silu(gr.astype(jnp.float32)).reshape(b, l, GLA_HEADS, GLA_DV)).reshape(b, l, GLA_HEADS * GLA_DV)
    merged = jax.nn.sigmoid(ga) * (o_a @ p["w_branch_a"]) + jax.nn.sigmoid(gb) * (o_b.astype(x.dtype) @ p["w_branch_b"])
    x = x + (merged.astype(x.dtype) @ p["w_out"]).astype(x.dtype)
    h2 = rmsnorm(x, p["g_norm_ffn"]).reshape(b * l, d)
    y = hier_moe(h2, p["w_router_group"], p["b_router_group"], p["w_router_expert"], p["b_router_expert"], p["w_exp_gate"], p["w_exp_up"], p["w_exp_down"])
    x = x + y.reshape(b, l, d)
    return x, ckv, kpe, s_new


def setup_inputs(seed: int = 0) -> dict:
    key = jax.random.key(seed)
    ks = jax.random.split(key, 28)
    f32 = jnp.float32

    def nrm(k, shape, scale):
        return jax.random.normal(k, shape, f32) * scale

    def gain(k, shape):
        return 1.0 + 0.1 * jax.random.normal(k, shape, f32)

    return {
        "x_prompt": nrm(ks[0], (BATCH, SEQ, D_MODEL), 1.0),
        "x_sample": nrm(ks[1], (DEC_BATCH, DEC_SEQ, D_MODEL), 1.0),
        "cache_ckv": nrm(ks[2], (DEPTH, DEC_BATCH, PAST_LEN, KV_LORA), 1.0),
        "cache_krope": nrm(ks[3], (DEPTH, DEC_BATCH, PAST_LEN, ROPE_DIM), 1.0),
        "state_gla": nrm(ks[4], (DEPTH, DEC_BATCH, GLA_HEADS, GLA_DK, GLA_DV), 0.5),
        "w_in": nrm(ks[5], (DEPTH, D_MODEL, IN_COLS), D_MODEL ** -0.5),
        "g_norm_mix": gain(ks[6], (DEPTH, D_MODEL)),
        "g_qnorm": gain(ks[7], (DEPTH, Q_LORA)),
        "w_uq": nrm(ks[8], (DEPTH, Q_LORA, MLA_HEADS * (NOPE_DIM + ROPE_DIM)), Q_LORA ** -0.5),
        "g_kvnorm": gain(ks[9], (DEPTH, KV_LORA)),
        "w_ukv": nrm(ks[10], (DEPTH, KV_LORA, MLA_HEADS * (NOPE_DIM + V_DIM)), KV_LORA ** -0.5),
        "w_gate2": nrm(ks[11], (DEPTH, GATE_RANK, GLA_HEADS * GLA_DK), GATE_RANK ** -0.5),
        "b_gate2": nrm(ks[12], (DEPTH, GLA_HEADS * GLA_DK), 0.1),
        "g_gla_norm": gain(ks[13], (DEPTH, GLA_DV)),
        "w_branch_a": nrm(ks[14], (DEPTH, MLA_HEADS * V_DIM, D_MODEL), (MLA_HEADS * V_DIM) ** -0.5),
        "w_branch_b": nrm(ks[15], (DEPTH, GLA_HEADS * GLA_DV, D_MODEL), (GLA_HEADS * GLA_DV) ** -0.5),
        "w_out": nrm(ks[16], (DEPTH, D_MODEL, D_MODEL), D_MODEL ** -0.5),
        "g_norm_ffn": gain(ks[17], (DEPTH, D_MODEL)),
        "w_router_group": nrm(ks[18], (DEPTH, D_MODEL, N_GROUPS), D_MODEL ** -0.5),
        "b_router_group": nrm(ks[19], (DEPTH, N_GROUPS), 0.01),
        "w_router_expert": nrm(ks[20], (DEPTH, D_MODEL, N_EXPERTS), D_MODEL ** -0.5),
        "b_router_expert": nrm(ks[21], (DEPTH, N_EXPERTS), 0.01),
        "w_exp_gate": nrm(ks[22], (DEPTH, N_EXPERTS, D_MODEL, EXPERT_FF), D_MODEL ** -0.5),
        "w_exp_up": nrm(ks[23], (DEPTH, N_EXPERTS, D_MODEL, EXPERT_FF), D_MODEL ** -0.5),
        "w_exp_down": nrm(ks[24], (DEPTH, N_EXPERTS, EXPERT_FF, D_MODEL), EXPERT_FF ** -0.5),
        "g_norm_final": gain(ks[25], (D_MODEL,)),
    }


def reference(x_prompt, x_sample, cache_ckv, cache_krope, state_gla, w_in, g_norm_mix, g_qnorm, w_uq, g_kvnorm, w_ukv, w_gate2, b_gate2, g_gla_norm, w_branch_a, w_branch_b, w_out, g_norm_ffn, w_router_group, b_router_group, w_router_expert, b_router_expert, w_exp_gate, w_exp_up, w_exp_down, g_norm_final):
    bp = x_prompt.shape[0]
    yp, ys = x_prompt, x_sample
    ckv_p, kpe_p, gla_p, ckv_s, kpe_s, gla_s = [], [], [], [], [], []
    for l in range(DEPTH):
        p = {
            "w_in": w_in[l], "g_norm_mix": g_norm_mix[l], "g_qnorm": g_qnorm[l], "w_uq": w_uq[l],
            "g_kvnorm": g_kvnorm[l], "w_ukv": w_ukv[l], "w_gate2": w_gate2[l], "b_gate2": b_gate2[l],
            "g_gla_norm": g_gla_norm[l], "w_branch_a": w_branch_a[l], "w_branch_b": w_branch_b[l],
            "w_out": w_out[l], "g_norm_ffn": g_norm_ffn[l], "w_router_group": w_router_group[l],
            "b_router_group": b_router_group[l], "w_router_expert": w_router_expert[l],
            "b_router_expert": b_router_expert[l], "w_exp_gate": w_exp_gate[l], "w_exp_up": w_exp_up[l],
            "w_exp_down": w_exp_down[l],
        }
        yp, c1, k1, s1 = trunk_layer(
            yp,
            jnp.zeros((bp, 0, KV_LORA), x_prompt.dtype),
            jnp.zeros((bp, 0, ROPE_DIM), x_prompt.dtype),
            jnp.zeros((bp, GLA_HEADS, GLA_DK, GLA_DV), jnp.float32),
            p)
        ys, c2, k2, s2 = trunk_layer(ys, cache_ckv[l], cache_krope[l], state_gla[l].astype(jnp.float32), p)
        ckv_p.append(c1); kpe_p.append(k1); gla_p.append(s1)
        ckv_s.append(c2); kpe_s.append(k2); gla_s.append(s2)
    y_prompt = rmsnorm(yp, g_norm_final)
    y_sample = rmsnorm(ys, g_norm_final)
    return (y_prompt, y_sample, jnp.stack(ckv_p), jnp.stack(kpe_p), jnp.stack(gla_p), jnp.stack(ckv_s), jnp.stack(kpe_s), jnp.stack(gla_s))
```

```python
import functools
import math

import jax
import jax.numpy as jnp
from jax import lax
from jax.experimental import pallas as pl
from jax.experimental.pallas import tpu as pltpu

F32 = jnp.float32
BF16 = jnp.bfloat16

D_MODEL = 1024
CHUNK = 64
EPS = 1e-6
MLA_HEADS = 8
Q_LORA = 256
KV_LORA = 128
NOPE_DIM = 64
ROPE_DIM = 32
V_DIM = 64
ROPE_THETA = 10000.0
GLA_HEADS = 4
GLA_DK = 128
GLA_DV = 256
GATE_RANK = 16
GATE_TEMP = 16.0
N_GROUPS = 4
EXPERTS_PER_GROUP = 8
N_EXPERTS = 32
EXPERT_FF = 512

LANES = 128
QCAT = 2 * LANES
ROW_TILE = 256
Q_BLOCK = 128
KEY_TILE = 512
GLA_CHUNK = 64
FFN_ROWS = 256
ROUTER_ROWS = 48
VMEM_LIMIT = 56 * 1024 * 1024
LOG2E = 1.4426950408889634
ATT_SCALE = LOG2E / math.sqrt(NOPE_DIM + ROPE_DIM)

_NT = (((1,), (1,)), ((), ()))
_TN = (((0,), (0,)), ((), ()))


def _params(sem):
    return pltpu.CompilerParams(dimension_semantics=sem, vmem_limit_bytes=VMEM_LIMIT)


def _rms(x, g):
    return x * lax.rsqrt(jnp.mean(x * x, axis=-1, keepdims=True) + EPS) * g


def _sigmoid(x):
    return 1.0 / (1.0 + jnp.exp(-x))


def _full(shape):
    n = len(shape)
    return pl.BlockSpec(shape, lambda *_: (0,) * n)


def _fold_q_kernel(wq_ref, wk_ref, o_ref):
    o_ref[0] = lax.dot_general(wq_ref[0], wk_ref[0], _NT, precision=lax.Precision.HIGHEST,
                               preferred_element_type=F32)


def _fold_q(wq_nope, wk_nope):
    return pl.pallas_call(
        _fold_q_kernel,
        grid=(MLA_HEADS,),
        in_specs=[pl.BlockSpec((1, Q_LORA, NOPE_DIM), lambda h: (h, 0, 0)),
                  pl.BlockSpec((1, KV_LORA, NOPE_DIM), lambda h: (h, 0, 0))],
        out_specs=pl.BlockSpec((1, Q_LORA, KV_LORA), lambda h: (h, 0, 0)),
        out_shape=jax.ShapeDtypeStruct((MLA_HEADS, Q_LORA, KV_LORA), F32),
        compiler_params=_params(("arbitrary",)),
        name="fold_q",
    )(wq_nope, wk_nope)


def _inproj_kernel(x_ref, cos_ref, sin_ref, gmix_ref, wsm_ref, gqn_ref, gkvn_ref, wqc_ref,
                   wg2_ref, bg2_ref, wgqk_ref, wgv_ref, wgr_ref, wga_ref, wgb_ref,
                   qcat_ref, kcat_ref, ckv_ref, kpe_ref, gq_ref, gk_ref, gv_ref, la_ref,
                   sgr_ref, sga_ref, sgb_ref):
    hb = _rms(x_ref[...], gmix_ref[...]).astype(BF16)
    zs = jnp.dot(hb, wsm_ref[...], preferred_element_type=F32)
    cos = cos_ref[...]
    sin = sin_ref[...]
    first_half = lax.broadcasted_iota(jnp.int32, cos.shape, 1) < ROPE_DIM // 2

    def rope(v):
        rot = jnp.where(first_half, pltpu.roll(v, LANES - ROPE_DIM // 2, 1),
                        pltpu.roll(v, ROPE_DIM // 2, 1))
        return v * cos + rot * sin

    cqn = _rms(zs[:, :Q_LORA], gqn_ref[...]).astype(BF16)
    qc = jnp.dot(cqn, wqc_ref[...], preferred_element_type=F32)
    for h in range(MLA_HEADS):
        lat = qc[:, h * QCAT:h * QCAT + LANES]
        pe = qc[:, h * QCAT + LANES:(h + 1) * QCAT]
        qcat_ref[h, :, :LANES] = (lat * ATT_SCALE).astype(BF16)
        qcat_ref[h, :, LANES:] = (rope(pe) * ATT_SCALE).astype(BF16)

    ckv = _rms(zs[:, Q_LORA:Q_LORA + KV_LORA], gkvn_ref[...])
    ckv_ref[...] = ckv
    kpe = rope(zs[:, Q_LORA + KV_LORA:Q_LORA + KV_LORA + LANES])
    kpe_ref[...] = kpe[:, :ROPE_DIM]
    kcat_ref[:, :LANES] = ckv.astype(BF16)
    kcat_ref[:, LANES:] = kpe.astype(BF16)

    glr = zs[:, Q_LORA + KV_LORA + LANES:].astype(BF16)
    xg = jnp.dot(glr, wg2_ref[...], preferred_element_type=F32) + bg2_ref[...]
    la_ref[...] = (jnp.minimum(xg, 0.0) - jnp.log(1.0 + jnp.exp(-jnp.abs(xg)))) * (1.0 / GATE_TEMP)

    zqk = jnp.dot(hb, wgqk_ref[...], preferred_element_type=F32)
    nqk = GLA_HEADS * GLA_DK
    gq_ref[...] = (zqk[:, :nqk] * (GLA_DK ** -0.5)).astype(BF16)
    gk_ref[...] = zqk[:, nqk:].astype(BF16)
    gv_ref[...] = jnp.dot(hb, wgv_ref[...], preferred_element_type=F32).astype(BF16)
    gr = jnp.dot(hb, wgr_ref[...], preferred_element_type=F32)
    sgr_ref[...] = (gr * _sigmoid(gr)).astype(BF16)
    sga_ref[...] = _sigmoid(jnp.dot(hb, wga_ref[...], preferred_element_type=F32)).astype(BF16)
    sgb_ref[...] = _sigmoid(jnp.dot(hb, wgb_ref[...], preferred_element_type=F32)).astype(BF16)


def _inproj(x2, cos_t, sin_t, w):
    n = x2.shape[0]
    tm = ROW_TILE
    nt = n // tm
    tab_blocks = cos_t.shape[0] // tm
    row = lambda width: pl.BlockSpec((tm, width), lambda i: (i, 0))
    tab = pl.BlockSpec((tm, LANES), lambda i: (i % tab_blocks, 0))
    nqk = GLA_HEADS * GLA_DK
    nv = GLA_HEADS * GLA_DV
    weights = [w["g_mix"], w["w_small"], w["g_qn"], w["g_kvn"], w["w_qcat"], w["w_g2"], w["b_g2"],
               w["w_gqk"], w["w_gv"], w["w_gr"], w["w_ga"], w["w_gb"]]
    out_shape = [
        jax.ShapeDtypeStruct((MLA_HEADS, n, QCAT), BF16),
        jax.ShapeDtypeStruct((n, QCAT), BF16),
        jax.ShapeDtypeStruct((n, KV_LORA), F32),
        jax.ShapeDtypeStruct((n, ROPE_DIM), F32),
        jax.ShapeDtypeStruct((n, nqk), BF16),
        jax.ShapeDtypeStruct((n, nqk), BF16),
        jax.ShapeDtypeStruct((n, nv), BF16),
        jax.ShapeDtypeStruct((n, nqk), F32),
        jax.ShapeDtypeStruct((n, nv), BF16),
        jax.ShapeDtypeStruct((n, D_MODEL), BF16),
        jax.ShapeDtypeStruct((n, D_MODEL), BF16),
    ]
    out_specs = [
        pl.BlockSpec((MLA_HEADS, tm, QCAT), lambda i: (0, i, 0)),
        row(QCAT), row(KV_LORA), row(ROPE_DIM), row(nqk), row(nqk), row(nv), row(nqk), row(nv),
        row(D_MODEL), row(D_MODEL),
    ]
    return pl.pallas_call(
        _inproj_kernel,
        grid=(nt,),
        in_specs=[row(D_MODEL), tab, tab] + [_full(a.shape) for a in weights],
        out_specs=out_specs,
        out_shape=out_shape,
        compiler_params=_params(("parallel",)),
        name="inproj",
    )(x2, cos_t, sin_t, *weights)


def _head_out(o_lat, wuv_ref, rows):
    outs = []
    for h in range(MLA_HEADS):
        oh = o_lat[h * rows:(h + 1) * rows].astype(BF16)
        outs.append(jnp.dot(oh, wuv_ref[h], preferred_element_type=F32))
    return jnp.concatenate(outs, axis=1)


def _attn_prompt_kernel(q_ref, k_ref, wuv_ref, o_ref, m_sc, l_sc, acc_sc):
    i = pl.program_id(1)
    rows = MLA_HEADS * Q_BLOCK
    q = q_ref[...].reshape(rows, QCAT)
    m_sc[...] = jnp.full(m_sc.shape, -jnp.inf, F32)
    l_sc[...] = jnp.zeros(l_sc.shape, F32)
    acc_sc[...] = jnp.zeros(acc_sc.shape, F32)
    n_tiles = (i * Q_BLOCK + Q_BLOCK + KEY_TILE - 1) // KEY_TILE

    def tile(kt, masked):
        kb = k_ref[0, pl.ds(pl.multiple_of(kt * KEY_TILE, KEY_TILE), KEY_TILE), :]
        s = lax.dot_general(q, kb, _NT, preferred_element_type=F32)
        if masked:
            r = lax.broadcasted_iota(jnp.int32, (rows, 1), 0)
            q_chunk = (i * Q_BLOCK + (r & (Q_BLOCK - 1))) // CHUNK
            c = lax.broadcasted_iota(jnp.int32, (1, KEY_TILE), 1)
            k_chunk = (kt * KEY_TILE + c) // CHUNK
            s = jnp.where(k_chunk <= q_chunk, s, -jnp.inf)
        m_prev = m_sc[...]
        m_new = jnp.maximum(m_prev, jnp.max(s, axis=1, keepdims=True))
        alpha = jnp.exp2(m_prev - m_new)
        p = jnp.exp2(s - m_new)
        l_sc[...] = alpha * l_sc[...] + jnp.sum(p, axis=1, keepdims=True)
        acc_sc[...] = alpha * acc_sc[...] + jnp.dot(p.astype(BF16), kb[:, :KV_LORA],
                                                    preferred_element_type=F32)
        m_sc[...] = m_new

    def body(kt, carry):
        tile(kt, False)
        return carry

    lax.fori_loop(0, n_tiles - 1, body, 0)
    tile(n_tiles - 1, True)
    o_lat = acc_sc[...] / l_sc[...]
    o_ref[0] = _head_out(o_lat, wuv_ref, Q_BLOCK).astype(BF16)


def _attn_prompt(qcat, kcat, wuv, b, l):
    nq = l // Q_BLOCK
    rows = MLA_HEADS * Q_BLOCK
    return pl.pallas_call(
        _attn_prompt_kernel,
        grid=(b, nq),
        in_specs=[pl.BlockSpec((MLA_HEADS, Q_BLOCK, QCAT), lambda bb, i: (0, bb * nq + i, 0)),
                  pl.BlockSpec((1, l, QCAT), lambda bb, i: (bb, 0, 0)),
                  _full(wuv.shape)],
        out_specs=pl.BlockSpec((1, Q_BLOCK, MLA_HEADS * V_DIM), lambda bb, i: (bb, i, 0)),
        out_shape=jax.ShapeDtypeStruct((b, l, MLA_HEADS * V_DIM), BF16),
        scratch_shapes=[pltpu.VMEM((rows, 1), F32), pltpu.VMEM((rows, 1), F32),
                        pltpu.VMEM((rows, KV_LORA), F32)],
        compiler_params=_params(("parallel", "arbitrary")),
        name="attn_prompt",
    )(qcat, kcat.reshape(b, l, QCAT), wuv)


def _attn_sample_kernel(q_ref, cckv_ref, ckr_ref, kn_ref, wuv_ref, o_ref, *, ls):
    q = q_ref[...].reshape(MLA_HEADS * ls, QCAT)
    ck = cckv_ref[0].astype(BF16)
    kr = ckr_ref[0].astype(BF16)
    kn = kn_ref[...]
    s_c = (lax.dot_general(q[:, :KV_LORA], ck, _NT, preferred_element_type=F32)
           + lax.dot_general(q[:, KV_LORA:KV_LORA + ROPE_DIM], kr, _NT, preferred_element_type=F32))
    s_n = lax.dot_general(q, kn, _NT, preferred_element_type=F32)
    m = jnp.maximum(jnp.max(s_c, axis=1, keepdims=True), jnp.max(s_n, axis=1, keepdims=True))
    p_c = jnp.exp2(s_c - m)
    p_n = jnp.exp2(s_n - m)
    den = jnp.sum(p_c, axis=1, keepdims=True) + jnp.sum(p_n, axis=1, keepdims=True)
    o_lat = (jnp.dot(p_c.astype(BF16), ck, preferred_element_type=F32)
             + jnp.dot(p_n.astype(BF16), kn[:, :KV_LORA], preferred_element_type=F32)) / den
    o_ref[0] = _head_out(o_lat, wuv_ref, ls).astype(BF16)


def _attn_sample(qcat, kcat, cache_ckv, cache_krope, wuv, b, ls):
    past = cache_ckv.shape[1]
    assert past % CHUNK == 0 and ls <= CHUNK
    return pl.pallas_call(
        functools.partial(_attn_sample_kernel, ls=ls),
        grid=(b,),
        in_specs=[pl.BlockSpec((MLA_HEADS, ls, QCAT), lambda bb: (0, bb, 0)),
                  pl.BlockSpec((1, past, KV_LORA), lambda bb: (bb, 0, 0)),
                  pl.BlockSpec((1, past, ROPE_DIM), lambda bb: (bb, 0, 0)),
                  pl.BlockSpec((ls, QCAT), lambda bb: (bb, 0)),
                  _full(wuv.shape)],
        out_specs=pl.BlockSpec((1, ls, MLA_HEADS * V_DIM), lambda bb: (bb, 0, 0)),
        out_shape=jax.ShapeDtypeStruct((b, ls, MLA_HEADS * V_DIM), BF16),
        compiler_params=_params(("parallel",)),
        name="attn_sample",
    )(qcat, cache_ckv, cache_krope, kcat, wuv)


def _cumsum_rows(x):
    c = x.shape[0]
    row = lax.broadcasted_iota(jnp.int32, x.shape, 0)
    s = 1
    while s < c:
        x = x + jnp.where(row >= s, pltpu.roll(x, s, 0), 0.0)
        s *= 2
    return x


def _gla_kernel(q_ref, k_ref, v_ref, la_ref, sgr_ref, gn_ref, s0_ref, o_ref, sout_ref, st_sc, *, c):
    j = pl.program_id(1)

    @pl.when(j == 0)
    def _():
        st_sc[...] = s0_ref[0]

    half = 32
    row = lax.broadcasted_iota(jnp.int32, (c, GLA_DK), 0)
    rr = lax.broadcasted_iota(jnp.int32, (c, c), 0)
    cc = lax.broadcasted_iota(jnp.int32, (c, c), 1)
    causal = cc <= rr
    gn = gn_ref[...]
    for h in range(GLA_HEADS):
        ks = slice(h * GLA_DK, (h + 1) * GLA_DK)
        vs = slice(h * GLA_DV, (h + 1) * GLA_DV)
        q = q_ref[0, :, ks].astype(F32)
        k = k_ref[0, :, ks].astype(F32)
        v = v_ref[0, :, vs]
        b = _cumsum_rows(la_ref[0, :, ks])
        if c > half:
            mid = jnp.where(row < half, b[half // 2 - 1:half // 2, :], b[half + half // 2 - 1:half + half // 2, :])
        else:
            mid = jnp.broadcast_to(b[c // 2 - 1:c // 2, :], b.shape)
        qe = (q * jnp.exp(b - mid)).astype(BF16)
        ke = (k * jnp.exp(mid - b)).astype(BF16)
        att = lax.dot_general(qe, ke, _NT, preferred_element_type=F32)
        if c > half:
            edge = b[half - 1:half, :]
            qo = (q * jnp.exp(jnp.minimum(b - edge, 0.0))).astype(BF16)
            ko = (k * jnp.exp(jnp.minimum(edge - b, 0.0))).astype(BF16)
            att_off = lax.dot_general(qo, ko, _NT, preferred_element_type=F32)
            same = (rr < half) == (cc < half)
            att = jnp.where(same, att, att_off)
        att = jnp.where(causal, att, 0.0).astype(BF16)
        st = st_sc[h]
        q0 = (q * jnp.exp(b)).astype(BF16)
        o = (jnp.dot(q0, st.astype(BF16), preferred_element_type=F32)
             + jnp.dot(att, v, preferred_element_type=F32))
        last = b[c - 1:c, :]
        kd = (k * jnp.exp(last - b)).astype(BF16)
        upd = lax.dot_general(kd, v, _TN, preferred_element_type=F32)
        dcol = jnp.exp(jnp.transpose(jnp.broadcast_to(last, (GLA_DK, GLA_DK))))
        st_sc[h] = jnp.concatenate([dcol, dcol], axis=1) * st + upd
        on = _rms(o, gn) * sgr_ref[0, :, vs].astype(F32)
        o_ref[0, :, vs] = on.astype(BF16)

    @pl.when(j == pl.num_programs(1) - 1)
    def _():
        sout_ref[0] = st_sc[...]


def _gla(gq, gk, gv, la, sgr, gn, s0, b, l):
    c = min(GLA_CHUNK, l)
    nc = l // c
    nqk = GLA_HEADS * GLA_DK
    nv = GLA_HEADS * GLA_DV
    r3 = lambda a: a.reshape(b, l, a.shape[-1])
    tok = lambda width: pl.BlockSpec((1, c, width), lambda bb, j: (bb, j, 0))
    st = pl.BlockSpec((1, GLA_HEADS, GLA_DK, GLA_DV), lambda bb, j: (bb, 0, 0, 0))
    return pl.pallas_call(
        functools.partial(_gla_kernel, c=c),
        grid=(b, nc),
        in_specs=[tok(nqk), tok(nqk), tok(nv), tok(nqk), tok(nv), _full(gn.shape), st],
        out_specs=[tok(nv), st],
        out_shape=[jax.ShapeDtypeStruct((b, l, nv), BF16),
                   jax.ShapeDtypeStruct((b, GLA_HEADS, GLA_DK, GLA_DV), F32)],
        scratch_shapes=[pltpu.VMEM((GLA_HEADS, GLA_DK, GLA_DV), F32)],
        compiler_params=_params(("parallel", "arbitrary")),
        name="gla",
    )(r3(gq), r3(gk), r3(gv), r3(la), r3(sgr), gn, s0)


def _merge_kernel(oa_ref, ob_ref, sga_ref, sgb_ref, x_ref, wa_ref, wb_ref, wo_ref, gffn_ref,
                  wr_ref, br_ref, x1_ref, h2_ref, ids_ref, gate_ref):
    ya = jnp.dot(oa_ref[...], wa_ref[...], preferred_element_type=F32)
    yb = jnp.dot(ob_ref[...], wb_ref[...], preferred_element_type=F32)
    merged = (sga_ref[...].astype(F32) * ya + sgb_ref[...].astype(F32) * yb).astype(BF16)
    x1 = x_ref[...] + jnp.dot(merged, wo_ref[...], preferred_element_type=F32)
    x1_ref[...] = x1
    h2 = _rms(x1, gffn_ref[...])
    h2_ref[...] = h2

    hi = h2.astype(BF16)
    lo = (h2 - hi.astype(F32)).astype(BF16)
    wr = wr_ref[...]
    whi = wr.astype(BF16)
    wlo = (wr - whi.astype(F32)).astype(BF16)
    logits = (lax.dot_general(whi, hi, _NT, preferred_element_type=F32)
              + lax.dot_general(whi, lo, _NT, preferred_element_type=F32)
              + lax.dot_general(wlo, hi, _NT, preferred_element_type=F32)) + br_ref[...]
    tm = logits.shape[1]
    ridx = lax.broadcasted_iota(jnp.int32, (EXPERTS_PER_GROUP, tm), 0)
    big = jnp.int32(1 << 20)

    def top(vals):
        vmax = jnp.max(vals, axis=0, keepdims=True)
        imax = jnp.min(jnp.where(vals == vmax, ridx, big), axis=0, keepdims=True)
        return vmax, imax

    gl = jnp.where(ridx < N_GROUPS, logits[N_EXPERTS:N_EXPERTS + EXPERTS_PER_GROUP], -jnp.inf)
    gmax, g_top = top(gl)
    p_top = 1.0 / jnp.sum(jnp.exp(gl - gmax), axis=0, keepdims=True)
    e_sel = logits[:EXPERTS_PER_GROUP]
    for g in range(1, N_GROUPS):
        e_sel = jnp.where(g_top == g, logits[g * EXPERTS_PER_GROUP:(g + 1) * EXPERTS_PER_GROUP], e_sel)
    v1, i1 = top(e_sel)
    v2, i2 = top(jnp.where(ridx == i1, -jnp.inf, e_sel))
    e21 = jnp.exp(v2 - v1)
    w1 = p_top / (1.0 + e21)
    w2 = p_top * e21 / (1.0 + e21)
    base = g_top * EXPERTS_PER_GROUP
    ids_ref[...] = jnp.where(ridx == 0, base + i1, jnp.where(ridx == 1, base + i2, 0))
    gate_ref[...] = jnp.where(ridx == 0, w1, jnp.where(ridx == 1, w2, 0.0))


def _merge(oa, ob, sga, sgb, x2, w):
    n = x2.shape[0]
    tm = ROW_TILE
    row = lambda width: pl.BlockSpec((tm, width), lambda i: (i, 0))
    col = pl.BlockSpec((EXPERTS_PER_GROUP, tm), lambda i: (0, i))
    weights = [w["w_a"], w["w_b"], w["w_o"], w["g_ffn"], w["w_router"], w["b_router"]]
    return pl.pallas_call(
        _merge_kernel,
        grid=(n // tm,),
        in_specs=[row(MLA_HEADS * V_DIM), row(GLA_HEADS * GLA_DV), row(D_MODEL), row(D_MODEL),
                  row(D_MODEL)] + [_full(a.shape) for a in weights],
        out_specs=[row(D_MODEL), row(D_MODEL), col, col],
        out_shape=[jax.ShapeDtypeStruct((n, D_MODEL), F32), jax.ShapeDtypeStruct((n, D_MODEL), F32),
                   jax.ShapeDtypeStruct((EXPERTS_PER_GROUP, n), jnp.int32),
                   jax.ShapeDtypeStruct((EXPERTS_PER_GROUP, n), F32)],
        compiler_params=_params(("parallel",)),
        name="merge",
    )(oa, ob, sga, sgb, x2, *weights)


def _ffn_kernel(bexp_ref, tok_ref, tokn_ref, h2_hbm, wg_ref, wu_ref, wd_ref, wrow_ref, out_ref,
                xbuf, sem):
    del bexp_ref
    i = pl.program_id(0)
    n = pl.num_programs(0)
    slot = i % 2

    def issue(tok_r, s):
        def body(r, carry):
            t = tok_r[0, 0, r]
            pltpu.make_async_copy(h2_hbm.at[pl.ds(t, 1), :], xbuf.at[s, pl.ds(r, 1), :],
                                  sem.at[s]).start()
            return carry
        lax.fori_loop(0, FFN_ROWS, body, 0)

    @pl.when(i == 0)
    def _():
        issue(tok_ref, 0)

    @pl.when(i + 1 < n)
    def _():
        issue(tokn_ref, 1 - slot)

    pltpu.make_async_copy(h2_hbm.at[pl.ds(0, FFN_ROWS), :], xbuf.at[slot], sem.at[slot]).wait()
    x = xbuf[slot].astype(BF16)
    g = jnp.dot(x, wg_ref[0], preferred_element_type=F32)
    u = jnp.dot(x, wu_ref[0], preferred_element_type=F32)
    mid = (g * _sigmoid(g) * u).astype(BF16)
    out_ref[...] = jnp.dot(mid, wd_ref[0], preferred_element_type=F32) * wrow_ref[...]


def _ffn(block_expert, tok_blocks, h2, wg, wu, wd, w_rows):
    nb = tok_blocks.shape[0]
    grid_spec = pltpu.PrefetchScalarGridSpec(
        num_scalar_prefetch=1,
        grid=(nb,),
        in_specs=[
            pl.BlockSpec((1, 1, FFN_ROWS), lambda i, be: (i, 0, 0), memory_space=pltpu.SMEM),
            pl.BlockSpec((1, 1, FFN_ROWS), lambda i, be: (jnp.minimum(i + 1, nb - 1), 0, 0),
                         memory_space=pltpu.SMEM),
            pl.BlockSpec(memory_space=pl.ANY),
            pl.BlockSpec((1, D_MODEL, EXPERT_FF), lambda i, be: (be[i], 0, 0)),
            pl.BlockSpec((1, D_MODEL, EXPERT_FF), lambda i, be: (be[i], 0, 0)),
            pl.BlockSpec((1, EXPERT_FF, D_MODEL), lambda i, be: (be[i], 0, 0)),
            pl.BlockSpec((FFN_ROWS, 1), lambda i, be: (i, 0)),
        ],
        out_specs=pl.BlockSpec((FFN_ROWS, D_MODEL), lambda i, be: (i, 0)),
        scratch_shapes=[pltpu.VMEM((2, FFN_ROWS, D_MODEL), F32), pltpu.SemaphoreType.DMA((2,))],
    )
    return pl.pallas_call(
        _ffn_kernel,
        grid_spec=grid_spec,
        out_shape=jax.ShapeDtypeStruct((nb * FFN_ROWS, D_MODEL), F32),
        compiler_params=_params(("arbitrary",)),
        name="expert_ffn",
    )(block_expert, tok_blocks, tok_blocks, h2, wg, wu, wd, w_rows)


def _combine_kernel(d_ref, x1_ref, gfin_ref, eo_hbm, y_ref, buf, sem):
    tm = x1_ref.shape[0]

    def body(r, carry):
        for k in range(2):
            d = d_ref[0, 0, k * tm + r]
            pltpu.make_async_copy(eo_hbm.at[pl.ds(d, 1), :], buf.at[k, pl.ds(r, 1), :], sem.at[0]).start()
        return carry

    lax.fori_loop(0, tm, body, 0)
    for k in range(2):
        pltpu.make_async_copy(eo_hbm.at[pl.ds(0, tm), :], buf.at[k], sem.at[0]).wait()
    y = x1_ref[...] + (buf[0] + buf[1])
    y_ref[...] = _rms(y, gfin_ref[...])


def _combine(dest_blocks, x1, g_final, eo):
    n = x1.shape[0]
    tm = ROW_TILE
    return pl.pallas_call(
        _combine_kernel,
        grid=(n // tm,),
        in_specs=[pl.BlockSpec((1, 1, 2 * tm), lambda i: (i, 0, 0), memory_space=pltpu.SMEM),
                  pl.BlockSpec((tm, D_MODEL), lambda i: (i, 0)),
                  _full(g_final.shape),
                  pl.BlockSpec(memory_space=pl.ANY)],
        out_specs=pl.BlockSpec((tm, D_MODEL), lambda i: (i, 0)),
        out_shape=jax.ShapeDtypeStruct((n, D_MODEL), F32),
        scratch_shapes=[pltpu.VMEM((2, tm, D_MODEL), F32), pltpu.SemaphoreType.DMA((1,))],
        compiler_params=_params(("arbitrary",)),
        name="combine",
    )(dest_blocks, x1, g_final, eo)


def _dispatch(ids, gates, n):
    a = 2 * n
    flat_e = ids[:2].reshape(a)
    flat_w = gates[:2].reshape(a)
    flat_tok = jnp.arange(a, dtype=jnp.int32) % n
    order = jnp.argsort(flat_e)
    sorted_e = flat_e[order]
    counts = jnp.bincount(flat_e, length=N_EXPERTS)
    padded = (counts + FFN_ROWS - 1) // FFN_ROWS * FFN_ROWS
    pad_end = jnp.cumsum(padded)
    pad_start = pad_end - padded
    seg_start = jnp.cumsum(counts) - counts
    dest_sorted = (pad_start[sorted_e] + jnp.arange(a, dtype=jnp.int32) - seg_start[sorted_e]).astype(jnp.int32)
    nb = a // FFN_ROWS + N_EXPERTS
    total = nb * FFN_ROWS
    tok_buf = jnp.zeros((total,), jnp.int32).at[dest_sorted].set(flat_tok[order])
    w_buf = jnp.zeros((total,), F32).at[dest_sorted].set(flat_w[order])
    dest = jnp.zeros((a,), jnp.int32).at[order].set(dest_sorted)
    block_expert = jnp.minimum(
        jnp.searchsorted(pad_end, jnp.arange(nb, dtype=jnp.int32) * FFN_ROWS, side="right"),
        N_EXPERTS - 1).astype(jnp.int32)
    tm = ROW_TILE
    dest_blocks = dest.reshape(2, n // tm, tm).transpose(1, 0, 2).reshape(n // tm, 1, 2 * tm)
    return block_expert, tok_buf.reshape(nb, 1, FFN_ROWS), w_buf.reshape(total, 1), dest_blocks


def _rope_tables(l, past, rows):
    half = ROPE_DIM // 2
    inv = ROPE_THETA ** (-jnp.arange(half, dtype=F32) / half)
    pos = (past + (jnp.arange(rows, dtype=jnp.int32) % l)).astype(F32)
    ang = pos[:, None] * inv[None, :]
    pad = jnp.zeros((rows, LANES - ROPE_DIM), F32)
    cos = jnp.concatenate([jnp.cos(ang), jnp.cos(ang), pad], axis=1)
    sin = jnp.concatenate([-jnp.sin(ang), jnp.sin(ang), pad], axis=1)
    return cos, sin


def _group(x, ckv_past, kpe_past, s0, w, g_final):
    b, l, d = x.shape
    n = b * l
    x2 = x.reshape(n, d)
    past = 0 if ckv_past is None else ckv_past.shape[1]
    cos_t, sin_t = _rope_tables(l, past, max(l, ROW_TILE))
    (qcat, kcat, ckv, kpe, gq, gk, gv, la, sgr, sga, sgb) = _inproj(x2, cos_t, sin_t, w)
    if ckv_past is None:
        oa = _attn_prompt(qcat, kcat, w["w_uv"], b, l)
    else:
        oa = _attn_sample(qcat, kcat, ckv_past, kpe_past, w["w_uv"], b, l)
    ob, s_new = _gla(gq, gk, gv, la, sgr, w["g_gla"], s0, b, l)
    x1, h2, ids, gates = _merge(oa.reshape(n, -1), ob.reshape(n, -1), sga, sgb, x2, w)
    block_expert, tok_blocks, w_rows, dest_blocks = _dispatch(ids, gates, n)
    eo = _ffn(block_expert, tok_blocks, h2, w["w_eg"], w["w_eu"], w["w_ed"], w_rows)
    y = _combine(dest_blocks, x1, g_final, eo)
    return (y.reshape(b, l, d), ckv.reshape(b, l, KV_LORA), kpe.reshape(b, l, ROPE_DIM), s_new)


def _prep_weights(w_in, g_norm_mix, g_qnorm, w_uq, g_kvnorm, w_ukv, w_gate2, b_gate2, g_gla_norm,
                  w_branch_a, w_branch_b, w_out, g_norm_ffn, w_router_group, b_router_group,
                  w_router_expert, b_router_expert, w_exp_gate, w_exp_up, w_exp_down):
    nqk = GLA_HEADS * GLA_DK
    nv = GLA_HEADS * GLA_DV
    o = 0
    parts = {}
    for name, width in (("cq", Q_LORA), ("ckv", KV_LORA), ("kpe", ROPE_DIM), ("gq", nqk), ("gk", nqk),
                        ("gv", nv), ("glr", GATE_RANK), ("gr", nv), ("ga", D_MODEL), ("gb", D_MODEL)):
        parts[name] = w_in[:, o:o + width]
        o += width
    padc = lambda a, width: jnp.pad(a, ((0, 0), (0, width - a.shape[1])))
    w_small = jnp.concatenate([parts["cq"], parts["ckv"], padc(parts["kpe"], LANES),
                               padc(parts["glr"], LANES)], axis=1).astype(BF16)
    uq = w_uq.reshape(Q_LORA, MLA_HEADS, NOPE_DIM + ROPE_DIM)
    ukv = w_ukv.reshape(KV_LORA, MLA_HEADS, NOPE_DIM + V_DIM)
    lat = _fold_q(uq[:, :, :NOPE_DIM].transpose(1, 0, 2), ukv[:, :, :NOPE_DIM].transpose(1, 0, 2))
    q_rope = uq[:, :, NOPE_DIM:].transpose(1, 0, 2)
    w_qcat = jnp.concatenate([lat, q_rope, jnp.zeros((MLA_HEADS, Q_LORA, QCAT - KV_LORA - ROPE_DIM), F32)],
                             axis=2)
    w_qcat = w_qcat.transpose(1, 0, 2).reshape(Q_LORA, MLA_HEADS * QCAT).astype(BF16)
    w_router = jnp.concatenate([w_router_expert.T, w_router_group.T,
                                jnp.zeros((ROUTER_ROWS - N_EXPERTS - N_GROUPS, D_MODEL), F32)], axis=0)
    b_router = jnp.concatenate([b_router_expert, b_router_group,
                                jnp.zeros((ROUTER_ROWS - N_EXPERTS - N_GROUPS,), F32)]).reshape(ROUTER_ROWS, 1)
    return {
        "g_mix": g_norm_mix.reshape(1, D_MODEL), "w_small": w_small,
        "g_qn": g_qnorm.reshape(1, Q_LORA), "g_kvn": g_kvnorm.reshape(1, KV_LORA), "w_qcat": w_qcat,
        "w_g2": jnp.pad(w_gate2, ((0, LANES - GATE_RANK), (0, 0))).astype(BF16),
        "b_g2": b_gate2.reshape(1, nqk),
        "w_gqk": jnp.concatenate([parts["gq"], parts["gk"]], axis=1).astype(BF16),
        "w_gv": parts["gv"].astype(BF16), "w_gr": parts["gr"].astype(BF16),
        "w_ga": parts["ga"].astype(BF16), "w_gb": parts["gb"].astype(BF16),
        "w_uv": ukv[:, :, NOPE_DIM:].transpose(1, 0, 2).astype(BF16),
        "g_gla": g_gla_norm.reshape(1, GLA_DV),
        "w_a": w_branch_a.astype(BF16), "w_b": w_branch_b.astype(BF16), "w_o": w_out.astype(BF16),
        "g_ffn": g_norm_ffn.reshape(1, D_MODEL), "w_router": w_router, "b_router": b_router,
        "w_eg": w_exp_gate.astype(BF16), "w_eu": w_exp_up.astype(BF16), "w_ed": w_exp_down.astype(BF16),
    }


def kernel(x_prompt, x_sample, cache_ckv, cache_krope, state_gla, w_in, g_norm_mix, g_qnorm, w_uq, g_kvnorm, w_ukv, w_gate2, b_gate2, g_gla_norm, w_branch_a, w_branch_b, w_out, g_norm_ffn, w_router_group, b_router_group, w_router_expert, b_router_expert, w_exp_gate, w_exp_up, w_exp_down, g_norm_final):
    depth = w_in.shape[0]
    assert depth == 1, "the final norm is fused into the last layer's combine step"
    gfin = g_norm_final.reshape(1, D_MODEL)
    w = _prep_weights(w_in[0], g_norm_mix[0], g_qnorm[0], w_uq[0], g_kvnorm[0], w_ukv[0], w_gate2[0],
                      b_gate2[0], g_gla_norm[0], w_branch_a[0], w_branch_b[0], w_out[0], g_norm_ffn[0],
                      w_router_group[0], b_router_group[0], w_router_expert[0], b_router_expert[0],
                      w_exp_gate[0], w_exp_up[0], w_exp_down[0])
    bp = x_prompt.shape[0]
    zero_state = jnp.zeros((bp, GLA_HEADS, GLA_DK, GLA_DV), F32)
    yp, c1, k1, s1 = _group(x_prompt, None, None, zero_state, w, gfin)
    ys, c2, k2, s2 = _group(x_sample, cache_ckv[0], cache_krope[0], state_gla[0].astype(F32), w, gfin)
    return (yp, ys, c1[None], k1[None], s1[None], c2[None], k2[None], s2[None])
```

```python
import functools
import math

import jax
import jax.numpy as jnp
from jax import lax
from jax.experimental import pallas as pl
from jax.experimental.pallas import tpu as pltpu

F32 = jnp.float32
BF16 = jnp.bfloat16

D_MODEL = 1024
CHUNK = 64
EPS = 1e-6
MLA_HEADS = 8
Q_LORA = 256
KV_LORA = 128
NOPE_DIM = 64
ROPE_DIM = 32
V_DIM = 64
ROPE_THETA = 10000.0
GLA_HEADS = 4
GLA_DK = 128
GLA_DV = 256
GATE_RANK = 16
GATE_TEMP = 16.0
N_GROUPS = 4
EXPERTS_PER_GROUP = 8
N_EXPERTS = 32
EXPERT_FF = 512

LANES = 128
QCAT = 2 * LANES
ROW_TILE = 256
Q_BLOCK = 128
KEY_TILE = ROW_TILE
ATT_LANE_GROUP = 256
GLA_CHUNK = 64
FFN_ROWS = 256
ROUTER_ROWS = 48
VMEM_LIMIT = 56 * 1024 * 1024
LOG2E = 1.4426950408889634
ATT_SCALE = LOG2E / math.sqrt(NOPE_DIM + ROPE_DIM)

_NT = (((1,), (1,)), ((), ()))
_TN = (((0,), (0,)), ((), ()))


def _params(sem):
    return pltpu.CompilerParams(dimension_semantics=sem, vmem_limit_bytes=VMEM_LIMIT)


def _rms(x, g):
    return x * lax.rsqrt(jnp.mean(x * x, axis=-1, keepdims=True) + EPS) * g


def _sigmoid(x):
    return 1.0 / (1.0 + jnp.exp(-x))


def _full(shape):
    n = len(shape)
    return pl.BlockSpec(shape, lambda *_: (0,) * n)


SLABS = D_MODEL // LANES


def _store_token_major(ref, x):
    rows = x.shape[0]
    for j in range(SLABS):
        ref[pl.ds(j, rows, stride=SLABS), :] = x[:, j * LANES:(j + 1) * LANES]


def _load_token_major(ref, rows):
    return jnp.concatenate([ref[pl.ds(j, rows, stride=SLABS), :] for j in range(SLABS)], axis=1)


def _fold_q_kernel(wq_ref, wk_ref, o_ref):
    o_ref[0] = lax.dot_general(wq_ref[0], wk_ref[0], _NT, precision=lax.Precision.HIGHEST,
                               preferred_element_type=F32)


def _fold_q(wq_nope, wk_nope):
    return pl.pallas_call(
        _fold_q_kernel,
        grid=(MLA_HEADS,),
        in_specs=[pl.BlockSpec((1, Q_LORA, NOPE_DIM), lambda h: (h, 0, 0)),
                  pl.BlockSpec((1, KV_LORA, NOPE_DIM), lambda h: (h, 0, 0))],
        out_specs=pl.BlockSpec((1, Q_LORA, KV_LORA), lambda h: (h, 0, 0)),
        out_shape=jax.ShapeDtypeStruct((MLA_HEADS, Q_LORA, KV_LORA), F32),
        compiler_params=_params(("arbitrary",)),
        name="fold_q",
    )(wq_nope, wk_nope)


def _inproj_kernel(x_ref, cos_ref, sin_ref, gmix_ref, wsm_ref, gqn_ref, gkvn_ref, wqc_ref,
                   wg2_ref, bg2_ref, wgqk_ref, wgv_ref, wgr_ref, wga_ref, wgb_ref,
                   qcat_ref, kcat_ref, ckv_ref, kpe_ref, gq_ref, gk_ref, gv_ref, la_ref,
                   sgr_ref, sga_ref, sgb_ref, *maybe_vt_ref, transposed):
    hb = _rms(x_ref[...], gmix_ref[...]).astype(BF16)
    zs = jnp.dot(hb, wsm_ref[...], preferred_element_type=F32)
    cos = cos_ref[...]
    sin = sin_ref[...]
    first_half = lax.broadcasted_iota(jnp.int32, cos.shape, 1) < ROPE_DIM // 2

    def rope(v):
        rot = jnp.where(first_half, pltpu.roll(v, LANES - ROPE_DIM // 2, 1),
                        pltpu.roll(v, ROPE_DIM // 2, 1))
        return v * cos + rot * sin

    cqn = _rms(zs[:, :Q_LORA], gqn_ref[...]).astype(BF16)
    qc = jnp.dot(cqn, wqc_ref[...], preferred_element_type=F32)
    for h in range(MLA_HEADS):
        lat = qc[:, h * QCAT:h * QCAT + LANES]
        pe = qc[:, h * QCAT + LANES:(h + 1) * QCAT]
        if transposed:
            lat_t = jnp.transpose(lat * ATT_SCALE).astype(BF16)
            pe_t = jnp.transpose(rope(pe) * ATT_SCALE).astype(BF16)
            for blk in range(lat.shape[0] // Q_BLOCK):
                cols = slice(blk * Q_BLOCK, (blk + 1) * Q_BLOCK)
                qcat_ref[blk, :LANES, h * Q_BLOCK:(h + 1) * Q_BLOCK] = lat_t[:, cols]
                qcat_ref[blk, LANES:, h * Q_BLOCK:(h + 1) * Q_BLOCK] = pe_t[:, cols]
        else:
            qcat_ref[h, :, :LANES] = (lat * ATT_SCALE).astype(BF16)
            qcat_ref[h, :, LANES:] = (rope(pe) * ATT_SCALE).astype(BF16)

    ckv = _rms(zs[:, Q_LORA:Q_LORA + KV_LORA], gkvn_ref[...])
    ckv_ref[...] = ckv
    if transposed:
        maybe_vt_ref[0][0] = jnp.transpose(ckv).astype(BF16)
    kpe = rope(zs[:, Q_LORA + KV_LORA:Q_LORA + KV_LORA + LANES])
    kpe_ref[...] = kpe[:, :ROPE_DIM]
    kcat_ref[:, :LANES] = ckv.astype(BF16)
    kcat_ref[:, LANES:] = kpe.astype(BF16)

    glr = zs[:, Q_LORA + KV_LORA + LANES:].astype(BF16)
    xg = jnp.dot(glr, wg2_ref[...], preferred_element_type=F32) + bg2_ref[...]
    la_ref[...] = (jnp.minimum(xg, 0.0) - jnp.log(1.0 + jnp.exp(-jnp.abs(xg)))) * (1.0 / GATE_TEMP)

    zqk = jnp.dot(hb, wgqk_ref[...], preferred_element_type=F32)
    nqk = GLA_HEADS * GLA_DK
    gq_ref[...] = (zqk[:, :nqk] * (GLA_DK ** -0.5)).astype(BF16)
    gk_ref[...] = zqk[:, nqk:].astype(BF16)
    gv_ref[...] = jnp.dot(hb, wgv_ref[...], preferred_element_type=F32).astype(BF16)
    gr = jnp.dot(hb, wgr_ref[...], preferred_element_type=F32)
    sgr_ref[...] = (gr * _sigmoid(gr)).astype(BF16)
    sga_ref[...] = _sigmoid(jnp.dot(hb, wga_ref[...], preferred_element_type=F32)).astype(BF16)
    sgb_ref[...] = _sigmoid(jnp.dot(hb, wgb_ref[...], preferred_element_type=F32)).astype(BF16)


def _inproj(x2, cos_t, sin_t, w, transposed):
    n = x2.shape[0]
    tm = ROW_TILE
    nt = n // tm
    tab_blocks = cos_t.shape[0] // tm
    row = lambda width: pl.BlockSpec((tm, width), lambda i: (i, 0))
    tab = pl.BlockSpec((tm, LANES), lambda i: (i % tab_blocks, 0))
    nqk = GLA_HEADS * GLA_DK
    nv = GLA_HEADS * GLA_DV
    weights = [w["g_mix"], w["w_small"], w["g_qn"], w["g_kvn"], w["w_qcat"], w["w_g2"], w["b_g2"],
               w["w_gqk"], w["w_gv"], w["w_gr"], w["w_ga"], w["w_gb"]]
    out_shape = [
        jax.ShapeDtypeStruct((MLA_HEADS, n, QCAT), BF16),
        jax.ShapeDtypeStruct((n, QCAT), BF16),
        jax.ShapeDtypeStruct((n, KV_LORA), F32),
        jax.ShapeDtypeStruct((n, ROPE_DIM), F32),
        jax.ShapeDtypeStruct((n, nqk), BF16),
        jax.ShapeDtypeStruct((n, nqk), BF16),
        jax.ShapeDtypeStruct((n, nv), BF16),
        jax.ShapeDtypeStruct((n, nqk), F32),
        jax.ShapeDtypeStruct((n, nv), BF16),
        jax.ShapeDtypeStruct((n, D_MODEL), BF16),
        jax.ShapeDtypeStruct((n, D_MODEL), BF16),
    ]
    out_specs = [
        pl.BlockSpec((MLA_HEADS, tm, QCAT), lambda i: (0, i, 0)),
        row(QCAT), row(KV_LORA), row(ROPE_DIM), row(nqk), row(nqk), row(nv), row(nqk), row(nv),
        row(D_MODEL), row(D_MODEL),
    ]
    if transposed:
        qb = tm // Q_BLOCK
        out_shape[0] = jax.ShapeDtypeStruct((n // Q_BLOCK, QCAT, MLA_HEADS * Q_BLOCK), BF16)
        out_specs[0] = pl.BlockSpec((qb, QCAT, MLA_HEADS * Q_BLOCK), lambda i: (i, 0, 0))
        out_shape.append(jax.ShapeDtypeStruct((nt, KV_LORA, tm), BF16))
        out_specs.append(pl.BlockSpec((1, KV_LORA, tm), lambda i: (i, 0, 0)))
    return pl.pallas_call(
        functools.partial(_inproj_kernel, transposed=transposed),
        grid=(nt,),
        in_specs=[row(D_MODEL), tab, tab] + [_full(a.shape) for a in weights],
        out_specs=out_specs,
        out_shape=out_shape,
        compiler_params=_params(("parallel",)),
        name="inproj",
    )(x2, cos_t, sin_t, *weights)


def _head_out(o_lat, wuv_ref, rows):
    outs = []
    for h in range(MLA_HEADS):
        oh = o_lat[h * rows:(h + 1) * rows].astype(BF16)
        outs.append(jnp.dot(oh, wuv_ref[h], preferred_element_type=F32))
    return jnp.concatenate(outs, axis=1)


def _attn_prompt_kernel(qt_ref, k_ref, vt_ref, wuv_ref, o_ref, m_sc, l_sc, acc_sc):
    i = pl.program_id(1)
    cols = MLA_HEADS * Q_BLOCK
    m_sc[...] = jnp.full(m_sc.shape, -jnp.inf, F32)
    l_sc[...] = jnp.zeros(l_sc.shape, F32)
    acc_sc[...] = jnp.zeros(acc_sc.shape, F32)
    n_tiles = (i * Q_BLOCK + Q_BLOCK + KEY_TILE - 1) // KEY_TILE
    groups = [slice(g * ATT_LANE_GROUP, (g + 1) * ATT_LANE_GROUP) for g in range(cols // ATT_LANE_GROUP)]

    def process(tiles):
        kbs = [k_ref[0, pl.ds(pl.multiple_of(kt * KEY_TILE, KEY_TILE), KEY_TILE), :] for kt, _ in tiles]
        vts = [vt_ref[kt] for kt, _ in tiles]
        chains = [(t, gs) for t in range(len(tiles)) for gs in groups]
        ss = []
        for t, gs in chains:
            s = jnp.dot(kbs[t], qt_ref[0, :, gs], preferred_element_type=F32)
            if tiles[t][1]:
                r = lax.broadcasted_iota(jnp.int32, (KEY_TILE, 1), 0)
                k_chunk = (tiles[t][0] * KEY_TILE + r) // CHUNK
                c = lax.broadcasted_iota(jnp.int32, (1, ATT_LANE_GROUP), 1)
                q_chunk = (i * Q_BLOCK + (c & (Q_BLOCK - 1))) // CHUNK
                s = jnp.where(k_chunk <= q_chunk, s, -jnp.inf)
            ss.append(s)
        mts = [jnp.max(s, axis=0, keepdims=True) for s in ss]
        ps = [jnp.exp2(s - mt) for s, mt in zip(ss, mts)]
        lts = [jnp.sum(p, axis=0, keepdims=True) for p in ps]
        pvs = [jnp.dot(vts[t], p.astype(BF16), preferred_element_type=F32) for (t, _), p in zip(chains, ps)]
        for gs in groups:
            m, l, acc = m_sc[:, gs], l_sc[:, gs], acc_sc[:, gs]
            for idx, (t, cgs) in enumerate(chains):
                if cgs is not gs:
                    continue
                m_new = jnp.maximum(m, mts[idx])
                a = jnp.exp2(m - m_new)
                b = jnp.exp2(mts[idx] - m_new)
                l = a * l + b * lts[idx]
                acc = a * acc + b * pvs[idx]
                m = m_new
            m_sc[:, gs], l_sc[:, gs], acc_sc[:, gs] = m, l, acc

    def body(j, carry):
        process([(2 * j, False), (2 * j + 1, False)])
        return carry

    n_full = n_tiles - 1
    lax.fori_loop(0, n_full // 2, body, 0)

    @pl.when(n_full % 2 == 0)
    def _():
        process([(n_tiles - 1, True)])

    @pl.when(n_full % 2 == 1)
    def _():
        process([(n_tiles - 2, False), (n_tiles - 1, True)])

    o_lat_t = (acc_sc[...] / l_sc[...]).astype(BF16)
    outs = []
    for h in range(MLA_HEADS):
        outs.append(lax.dot_general(o_lat_t[:, h * Q_BLOCK:(h + 1) * Q_BLOCK], wuv_ref[h], _TN,
                                    preferred_element_type=F32))
    o_ref[0] = jnp.concatenate(outs, axis=1).astype(BF16)


def _attn_prompt(qt, kcat, vt, wuv, b, l):
    nq = l // Q_BLOCK
    nkt = l // KEY_TILE
    cols = MLA_HEADS * Q_BLOCK
    return pl.pallas_call(
        _attn_prompt_kernel,
        grid=(b, nq),
        in_specs=[pl.BlockSpec((1, QCAT, cols), lambda bb, i: (bb * nq + i, 0, 0)),
                  pl.BlockSpec((1, l, QCAT), lambda bb, i: (bb, 0, 0)),
                  pl.BlockSpec((nkt, KV_LORA, KEY_TILE), lambda bb, i: (bb, 0, 0)),
                  _full(wuv.shape)],
        out_specs=pl.BlockSpec((1, Q_BLOCK, MLA_HEADS * V_DIM), lambda bb, i: (bb, i, 0)),
        out_shape=jax.ShapeDtypeStruct((b, l, MLA_HEADS * V_DIM), BF16),
        scratch_shapes=[pltpu.VMEM((1, cols), F32), pltpu.VMEM((1, cols), F32),
                        pltpu.VMEM((KV_LORA, cols), F32)],
        compiler_params=_params(("parallel", "arbitrary")),
        name="attn_prompt",
    )(qt, kcat.reshape(b, l, QCAT), vt, wuv)


def _attn_sample_kernel(q_ref, cckv_ref, ckr_ref, kn_ref, wuv_ref, o_ref, *, ls):
    q = q_ref[...].reshape(MLA_HEADS * ls, QCAT)
    ck = cckv_ref[0].astype(BF16)
    kr = ckr_ref[0].astype(BF16)
    kn = kn_ref[...]
    s_c = (lax.dot_general(q[:, :KV_LORA], ck, _NT, preferred_element_type=F32)
           + lax.dot_general(q[:, KV_LORA:KV_LORA + ROPE_DIM], kr, _NT, preferred_element_type=F32))
    s_n = lax.dot_general(q, kn, _NT, preferred_element_type=F32)
    m = jnp.maximum(jnp.max(s_c, axis=1, keepdims=True), jnp.max(s_n, axis=1, keepdims=True))
    p_c = jnp.exp2(s_c - m)
    p_n = jnp.exp2(s_n - m)
    den = jnp.sum(p_c, axis=1, keepdims=True) + jnp.sum(p_n, axis=1, keepdims=True)
    o_lat = (jnp.dot(p_c.astype(BF16), ck, preferred_element_type=F32)
             + jnp.dot(p_n.astype(BF16), kn[:, :KV_LORA], preferred_element_type=F32)) / den
    o_ref[0] = _head_out(o_lat, wuv_ref, ls).astype(BF16)


def _attn_sample(qcat, kcat, cache_ckv, cache_krope, wuv, b, ls):
    past = cache_ckv.shape[1]
    assert past % CHUNK == 0 and ls <= CHUNK
    return pl.pallas_call(
        functools.partial(_attn_sample_kernel, ls=ls),
        grid=(b,),
        in_specs=[pl.BlockSpec((MLA_HEADS, ls, QCAT), lambda bb: (0, bb, 0)),
                  pl.BlockSpec((1, past, KV_LORA), lambda bb: (bb, 0, 0)),
                  pl.BlockSpec((1, past, ROPE_DIM), lambda bb: (bb, 0, 0)),
                  pl.BlockSpec((ls, QCAT), lambda bb: (bb, 0)),
                  _full(wuv.shape)],
        out_specs=pl.BlockSpec((1, ls, MLA_HEADS * V_DIM), lambda bb: (bb, 0, 0)),
        out_shape=jax.ShapeDtypeStruct((b, ls, MLA_HEADS * V_DIM), BF16),
        compiler_params=_params(("parallel",)),
        name="attn_sample",
    )(qcat, cache_ckv, cache_krope, kcat, wuv)


def _cumsum_rows(x):
    c = x.shape[0]
    row = lax.broadcasted_iota(jnp.int32, x.shape, 0)
    s = 1
    while s < c:
        x = x + jnp.where(row >= s, pltpu.roll(x, s, 0), 0.0)
        s *= 2
    return x


def _gla_kernel(q_ref, k_ref, v_ref, la_ref, sgr_ref, gn_ref, s0_ref, o_ref, sout_ref, st_sc, *, c):
    j = pl.program_id(1)

    @pl.when(j == 0)
    def _():
        st_sc[...] = s0_ref[0]

    half = 32
    row = lax.broadcasted_iota(jnp.int32, (c, GLA_DK), 0)
    rr = lax.broadcasted_iota(jnp.int32, (c, c), 0)
    cc = lax.broadcasted_iota(jnp.int32, (c, c), 1)
    causal = cc <= rr
    gn = gn_ref[...]
    for h in range(GLA_HEADS):
        ks = slice(h * GLA_DK, (h + 1) * GLA_DK)
        vs = slice(h * GLA_DV, (h + 1) * GLA_DV)
        q = q_ref[0, :, ks].astype(F32)
        k = k_ref[0, :, ks].astype(F32)
        v = v_ref[0, :, vs]
        b = _cumsum_rows(la_ref[0, :, ks])
        if c > half:
            mid = jnp.where(row < half, b[half // 2 - 1:half // 2, :], b[half + half // 2 - 1:half + half // 2, :])
        else:
            mid = jnp.broadcast_to(b[c // 2 - 1:c // 2, :], b.shape)
        qe = (q * jnp.exp(b - mid)).astype(BF16)
        ke = (k * jnp.exp(mid - b)).astype(BF16)
        att = lax.dot_general(qe, ke, _NT, preferred_element_type=F32)
        if c > half:
            edge = b[half - 1:half, :]
            qo = (q * jnp.exp(jnp.minimum(b - edge, 0.0))).astype(BF16)
            ko = (k * jnp.exp(jnp.minimum(edge - b, 0.0))).astype(BF16)
            att_off = lax.dot_general(qo, ko, _NT, preferred_element_type=F32)
            same = (rr < half) == (cc < half)
            att = jnp.where(same, att, att_off)
        att = jnp.where(causal, att, 0.0).astype(BF16)
        st = st_sc[h]
        q0 = (q * jnp.exp(b)).astype(BF16)
        o = (jnp.dot(q0, st.astype(BF16), preferred_element_type=F32)
             + jnp.dot(att, v, preferred_element_type=F32))
        last = b[c - 1:c, :]
        kd = (k * jnp.exp(last - b)).astype(BF16)
        upd = lax.dot_general(kd, v, _TN, preferred_element_type=F32)
        dcol = jnp.exp(jnp.transpose(jnp.broadcast_to(last, (GLA_DK, GLA_DK))))
        st_sc[h] = jnp.concatenate([dcol, dcol], axis=1) * st + upd
        on = _rms(o, gn) * sgr_ref[0, :, vs].astype(F32)
        o_ref[0, :, vs] = on.astype(BF16)

    @pl.when(j == pl.num_programs(1) - 1)
    def _():
        sout_ref[0] = st_sc[...]


def _gla(gq, gk, gv, la, sgr, gn, s0, b, l):
    c = min(GLA_CHUNK, l)
    nc = l // c
    nqk = GLA_HEADS * GLA_DK
    nv = GLA_HEADS * GLA_DV
    r3 = lambda a: a.reshape(b, l, a.shape[-1])
    tok = lambda width: pl.BlockSpec((1, c, width), lambda bb, j: (bb, j, 0))
    st = pl.BlockSpec((1, GLA_HEADS, GLA_DK, GLA_DV), lambda bb, j: (bb, 0, 0, 0))
    return pl.pallas_call(
        functools.partial(_gla_kernel, c=c),
        grid=(b, nc),
        in_specs=[tok(nqk), tok(nqk), tok(nv), tok(nqk), tok(nv), _full(gn.shape), st],
        out_specs=[tok(nv), st],
        out_shape=[jax.ShapeDtypeStruct((b, l, nv), BF16),
                   jax.ShapeDtypeStruct((b, GLA_HEADS, GLA_DK, GLA_DV), F32)],
        scratch_shapes=[pltpu.VMEM((GLA_HEADS, GLA_DK, GLA_DV), F32)],
        compiler_params=_params(("parallel", "arbitrary")),
        name="gla",
    )(r3(gq), r3(gk), r3(gv), r3(la), r3(sgr), gn, s0)


def _merge_kernel(oa_ref, ob_ref, sga_ref, sgb_ref, x_ref, wa_ref, wb_ref, wo_ref, gffn_ref,
                  wr_ref, br_ref, x1_ref, h2_ref, ids_ref, gate_ref):
    ya = jnp.dot(oa_ref[...], wa_ref[...], preferred_element_type=F32)
    yb = jnp.dot(ob_ref[...], wb_ref[...], preferred_element_type=F32)
    merged = (sga_ref[...].astype(F32) * ya + sgb_ref[...].astype(F32) * yb).astype(BF16)
    x1 = x_ref[...] + jnp.dot(merged, wo_ref[...], preferred_element_type=F32)
    x1_ref[...] = x1
    h2 = _rms(x1, gffn_ref[...])
    _store_token_major(h2_ref, h2)

    hi = h2.astype(BF16)
    lo = (h2 - hi.astype(F32)).astype(BF16)
    wr = wr_ref[...]
    whi = wr.astype(BF16)
    wlo = (wr - whi.astype(F32)).astype(BF16)
    logits = (lax.dot_general(whi, hi, _NT, preferred_element_type=F32)
              + lax.dot_general(whi, lo, _NT, preferred_element_type=F32)
              + lax.dot_general(wlo, hi, _NT, preferred_element_type=F32)) + br_ref[...]
    tm = logits.shape[1]
    ridx = lax.broadcasted_iota(jnp.int32, (EXPERTS_PER_GROUP, tm), 0)
    big = jnp.int32(1 << 20)

    def top(vals):
        vmax = jnp.max(vals, axis=0, keepdims=True)
        imax = jnp.min(jnp.where(vals == vmax, ridx, big), axis=0, keepdims=True)
        return vmax, imax

    gl = jnp.where(ridx < N_GROUPS, logits[N_EXPERTS:N_EXPERTS + EXPERTS_PER_GROUP], -jnp.inf)
    gmax, g_top = top(gl)
    p_top = 1.0 / jnp.sum(jnp.exp(gl - gmax), axis=0, keepdims=True)
    e_sel = logits[:EXPERTS_PER_GROUP]
    for g in range(1, N_GROUPS):
        e_sel = jnp.where(g_top == g, logits[g * EXPERTS_PER_GROUP:(g + 1) * EXPERTS_PER_GROUP], e_sel)
    v1, i1 = top(e_sel)
    v2, i2 = top(jnp.where(ridx == i1, -jnp.inf, e_sel))
    e21 = jnp.exp(v2 - v1)
    w1 = p_top / (1.0 + e21)
    w2 = p_top * e21 / (1.0 + e21)
    base = g_top * EXPERTS_PER_GROUP
    ids_ref[...] = jnp.where(ridx == 0, base + i1, jnp.where(ridx == 1, base + i2, 0))
    gate_ref[...] = jnp.where(ridx == 0, w1, jnp.where(ridx == 1, w2, 0.0))


def _merge(oa, ob, sga, sgb, x2, w):
    n = x2.shape[0]
    tm = ROW_TILE
    row = lambda width: pl.BlockSpec((tm, width), lambda i: (i, 0))
    col = pl.BlockSpec((EXPERTS_PER_GROUP, tm), lambda i: (0, i))
    weights = [w["w_a"], w["w_b"], w["w_o"], w["g_ffn"], w["w_router"], w["b_router"]]
    return pl.pallas_call(
        _merge_kernel,
        grid=(n // tm,),
        in_specs=[row(MLA_HEADS * V_DIM), row(GLA_HEADS * GLA_DV), row(D_MODEL), row(D_MODEL),
                  row(D_MODEL)] + [_full(a.shape) for a in weights],
        out_specs=[row(D_MODEL), pl.BlockSpec((tm * SLABS, LANES), lambda i: (i, 0)), col, col],
        out_shape=[jax.ShapeDtypeStruct((n, D_MODEL), F32), jax.ShapeDtypeStruct((n * SLABS, LANES), F32),
                   jax.ShapeDtypeStruct((EXPERTS_PER_GROUP, n), jnp.int32),
                   jax.ShapeDtypeStruct((EXPERTS_PER_GROUP, n), F32)],
        compiler_params=_params(("parallel",)),
        name="merge",
    )(oa, ob, sga, sgb, x2, *weights)


def _ffn_kernel(bexp_ref, tok_ref, tokn_ref, h2_hbm, wg_ref, wu_ref, wd_ref, wrow_ref, out_ref,
                xbuf, sem):
    del bexp_ref
    i = pl.program_id(0)
    n = pl.num_programs(0)
    slot = i % 2

    def issue(tok_r, s):
        def body(r, carry):
            t = tok_r[0, 0, r]
            pltpu.make_async_copy(h2_hbm.at[pl.ds(pl.multiple_of(t * SLABS, SLABS), SLABS), :],
                                  xbuf.at[s, pl.ds(pl.multiple_of(r * SLABS, SLABS), SLABS), :],
                                  sem.at[s]).start()
            return carry
        lax.fori_loop(0, FFN_ROWS, body, 0)

    @pl.when(i == 0)
    def _():
        issue(tok_ref, 0)

    @pl.when(i + 1 < n)
    def _():
        issue(tokn_ref, 1 - slot)

    pltpu.make_async_copy(h2_hbm.at[pl.ds(0, FFN_ROWS * SLABS), :], xbuf.at[slot], sem.at[slot]).wait()
    x = _load_token_major(xbuf.at[slot], FFN_ROWS).astype(BF16)
    g = jnp.dot(x, wg_ref[0], preferred_element_type=F32)
    u = jnp.dot(x, wu_ref[0], preferred_element_type=F32)
    mid = (g * _sigmoid(g) * u).astype(BF16)
    _store_token_major(out_ref, jnp.dot(mid, wd_ref[0], preferred_element_type=F32) * wrow_ref[...])


def _ffn(block_expert, tok_blocks, h2, wg, wu, wd, w_rows):
    nb = tok_blocks.shape[0]
    grid_spec = pltpu.PrefetchScalarGridSpec(
        num_scalar_prefetch=1,
        grid=(nb,),
        in_specs=[
            pl.BlockSpec((1, 1, FFN_ROWS), lambda i, be: (i, 0, 0), memory_space=pltpu.SMEM),
            pl.BlockSpec((1, 1, FFN_ROWS), lambda i, be: (jnp.minimum(i + 1, nb - 1), 0, 0),
                         memory_space=pltpu.SMEM),
            pl.BlockSpec(memory_space=pl.ANY),
            pl.BlockSpec((1, D_MODEL, EXPERT_FF), lambda i, be: (be[i], 0, 0)),
            pl.BlockSpec((1, D_MODEL, EXPERT_FF), lambda i, be: (be[i], 0, 0)),
            pl.BlockSpec((1, EXPERT_FF, D_MODEL), lambda i, be: (be[i], 0, 0)),
            pl.BlockSpec((FFN_ROWS, 1), lambda i, be: (i, 0)),
        ],
        out_specs=pl.BlockSpec((FFN_ROWS * SLABS, LANES), lambda i, be: (i, 0)),
        scratch_shapes=[pltpu.VMEM((2, FFN_ROWS * SLABS, LANES), F32), pltpu.SemaphoreType.DMA((2,))],
    )
    return pl.pallas_call(
        _ffn_kernel,
        grid_spec=grid_spec,
        out_shape=jax.ShapeDtypeStruct((nb * FFN_ROWS * SLABS, LANES), F32),
        compiler_params=_params(("arbitrary",)),
        name="expert_ffn",
    )(block_expert, tok_blocks, tok_blocks, h2, wg, wu, wd, w_rows)


def _combine_kernel(d_ref, x1_ref, gfin_ref, eo_hbm, y_ref, buf, sem):
    tm = x1_ref.shape[0]

    def body(r, carry):
        for k in range(2):
            d = d_ref[0, 0, k * tm + r]
            pltpu.make_async_copy(eo_hbm.at[pl.ds(pl.multiple_of(d * SLABS, SLABS), SLABS), :],
                                  buf.at[k, pl.ds(pl.multiple_of(r * SLABS, SLABS), SLABS), :],
                                  sem.at[0]).start()
        return carry

    lax.fori_loop(0, tm, body, 0)
    for k in range(2):
        pltpu.make_async_copy(eo_hbm.at[pl.ds(0, tm * SLABS), :], buf.at[k], sem.at[0]).wait()
    y = x1_ref[...] + (_load_token_major(buf.at[0], tm) + _load_token_major(buf.at[1], tm))
    y_ref[...] = _rms(y, gfin_ref[...])


def _combine(dest_blocks, x1, g_final, eo):
    n = x1.shape[0]
    tm = ROW_TILE
    return pl.pallas_call(
        _combine_kernel,
        grid=(n // tm,),
        in_specs=[pl.BlockSpec((1, 1, 2 * tm), lambda i: (i, 0, 0), memory_space=pltpu.SMEM),
                  pl.BlockSpec((tm, D_MODEL), lambda i: (i, 0)),
                  _full(g_final.shape),
                  pl.BlockSpec(memory_space=pl.ANY)],
        out_specs=pl.BlockSpec((tm, D_MODEL), lambda i: (i, 0)),
        out_shape=jax.ShapeDtypeStruct((n, D_MODEL), F32),
        scratch_shapes=[pltpu.VMEM((2, tm * SLABS, LANES), F32), pltpu.SemaphoreType.DMA((1,))],
        compiler_params=_params(("arbitrary",)),
        name="combine",
    )(dest_blocks, x1, g_final, eo)


def _dispatch(ids, gates, n):
    a = 2 * n
    flat_e = ids[:2].reshape(a)
    flat_w = gates[:2].reshape(a)
    order = jnp.argsort(flat_e).astype(jnp.int32)
    inv = jnp.argsort(order).astype(jnp.int32)
    counts = jnp.sum((flat_e[:, None] == jnp.arange(N_EXPERTS, dtype=jnp.int32)[None, :]).astype(jnp.int32), axis=0)
    padded = (counts + FFN_ROWS - 1) // FFN_ROWS * FFN_ROWS
    pad_end = jnp.cumsum(padded)
    pad_start = pad_end - padded
    seg_start = jnp.cumsum(counts) - counts
    nb = a // FFN_ROWS + N_EXPERTS
    total = nb * FFN_ROWS
    block_start = jnp.arange(nb, dtype=jnp.int32) * FFN_ROWS
    block_expert = jnp.minimum(jnp.sum((block_start[:, None] >= pad_end[None, :]).astype(jnp.int32), axis=1),
                               N_EXPERTS - 1).astype(jnp.int32)
    dest = (pad_start[flat_e] + inv - seg_start[flat_e]).astype(jnp.int32)
    row = jnp.arange(total, dtype=jnp.int32)
    row_e = jnp.repeat(block_expert, FFN_ROWS)
    local = row - pad_start[row_e]
    valid = local < counts[row_e]
    src = order[jnp.clip(seg_start[row_e] + local, 0, a - 1)]
    tok_buf = jnp.where(valid, src % n, 0).astype(jnp.int32)
    w_buf = jnp.where(valid, flat_w[src], 0.0)
    tm = ROW_TILE
    dest_blocks = dest.reshape(2, n // tm, tm).transpose(1, 0, 2).reshape(n // tm, 1, 2 * tm)
    return block_expert, tok_buf.reshape(nb, 1, FFN_ROWS), w_buf.reshape(total, 1), dest_blocks


def _rope_tables(l, past, rows):
    half = ROPE_DIM // 2
    inv = ROPE_THETA ** (-jnp.arange(half, dtype=F32) / half)
    pos = (past + (jnp.arange(rows, dtype=jnp.int32) % l)).astype(F32)
    ang = pos[:, None] * inv[None, :]
    pad = jnp.zeros((rows, LANES - ROPE_DIM), F32)
    cos = jnp.concatenate([jnp.cos(ang), jnp.cos(ang), pad], axis=1)
    sin = jnp.concatenate([-jnp.sin(ang), jnp.sin(ang), pad], axis=1)
    return cos, sin


def _group(x, ckv_past, kpe_past, s0, w, g_final):
    b, l, d = x.shape
    n = b * l
    x2 = x.reshape(n, d)
    past = 0 if ckv_past is None else ckv_past.shape[1]
    cos_t, sin_t = _rope_tables(l, past, max(l, ROW_TILE))
    prompt = ckv_past is None
    (qcat, kcat, ckv, kpe, gq, gk, gv, la, sgr, sga, sgb, *maybe_vt) = _inproj(x2, cos_t, sin_t, w, prompt)
    if prompt:
        oa = _attn_prompt(qcat, kcat, maybe_vt[0], w["w_uv"], b, l)
    else:
        oa = _attn_sample(qcat, kcat, ckv_past, kpe_past, w["w_uv"], b, l)
    ob, s_new = _gla(gq, gk, gv, la, sgr, w["g_gla"], s0, b, l)
    x1, h2, ids, gates = _merge(oa.reshape(n, -1), ob.reshape(n, -1), sga, sgb, x2, w)
    block_expert, tok_blocks, w_rows, dest_blocks = _dispatch(ids, gates, n)
    eo = _ffn(block_expert, tok_blocks, h2, w["w_eg"], w["w_eu"], w["w_ed"], w_rows)
    y = _combine(dest_blocks, x1, g_final, eo)
    return (y.reshape(b, l, d), ckv.reshape(b, l, KV_LORA), kpe.reshape(b, l, ROPE_DIM), s_new)


def _prep_weights(w_in, g_norm_mix, g_qnorm, w_uq, g_kvnorm, w_ukv, w_gate2, b_gate2, g_gla_norm,
                  w_branch_a, w_branch_b, w_out, g_norm_ffn, w_router_group, b_router_group,
                  w_router_expert, b_router_expert, w_exp_gate, w_exp_up, w_exp_down):
    nqk = GLA_HEADS * GLA_DK
    nv = GLA_HEADS * GLA_DV
    o = 0
    parts = {}
    for name, width in (("cq", Q_LORA), ("ckv", KV_LORA), ("kpe", ROPE_DIM), ("gq", nqk), ("gk", nqk),
                        ("gv", nv), ("glr", GATE_RANK), ("gr", nv), ("ga", D_MODEL), ("gb", D_MODEL)):
        parts[name] = w_in[:, o:o + width]
        o += width
    padc = lambda a, width: jnp.pad(a, ((0, 0), (0, width - a.shape[1])))
    w_small = jnp.concatenate([parts["cq"], parts["ckv"], padc(parts["kpe"], LANES),
                               padc(parts["glr"], LANES)], axis=1).astype(BF16)
    uq = w_uq.reshape(Q_LORA, MLA_HEADS, NOPE_DIM + ROPE_DIM)
    ukv = w_ukv.reshape(KV_LORA, MLA_HEADS, NOPE_DIM + V_DIM)
    lat = _fold_q(uq[:, :, :NOPE_DIM].transpose(1, 0, 2), ukv[:, :, :NOPE_DIM].transpose(1, 0, 2))
    q_rope = uq[:, :, NOPE_DIM:].transpose(1, 0, 2)
    w_qcat = jnp.concatenate([lat, q_rope, jnp.zeros((MLA_HEADS, Q_LORA, QCAT - KV_LORA - ROPE_DIM), F32)],
                             axis=2)
    w_qcat = w_qcat.transpose(1, 0, 2).reshape(Q_LORA, MLA_HEADS * QCAT).astype(BF16)
    w_router = jnp.concatenate([w_router_expert.T, w_router_group.T,
                                jnp.zeros((ROUTER_ROWS - N_EXPERTS - N_GROUPS, D_MODEL), F32)], axis=0)
    b_router = jnp.concatenate([b_router_expert, b_router_group,
                                jnp.zeros((ROUTER_ROWS - N_EXPERTS - N_GROUPS,), F32)]).reshape(ROUTER_ROWS, 1)
    return {
        "g_mix": g_norm_mix.reshape(1, D_MODEL), "w_small": w_small,
        "g_qn": g_qnorm.reshape(1, Q_LORA), "g_kvn": g_kvnorm.reshape(1, KV_LORA), "w_qcat": w_qcat,
        "w_g2": jnp.pad(w_gate2, ((0, LANES - GATE_RANK), (0, 0))).astype(BF16),
        "b_g2": b_gate2.reshape(1, nqk),
        "w_gqk": jnp.concatenate([parts["gq"], parts["gk"]], axis=1).astype(BF16),
        "w_gv": parts["gv"].astype(BF16), "w_gr": parts["gr"].astype(BF16),
        "w_ga": parts["ga"].astype(BF16), "w_gb": parts["gb"].astype(BF16),
        "w_uv": ukv[:, :, NOPE_DIM:].transpose(1, 0, 2).astype(BF16),
        "g_gla": g_gla_norm.reshape(1, GLA_DV),
        "w_a": w_branch_a.astype(BF16), "w_b": w_branch_b.astype(BF16), "w_o": w_out.astype(BF16),
        "g_ffn": g_norm_ffn.reshape(1, D_MODEL), "w_router": w_router, "b_router": b_router,
        "w_eg": w_exp_gate.astype(BF16), "w_eu": w_exp_up.astype(BF16), "w_ed": w_exp_down.astype(BF16),
    }


def kernel(x_prompt, x_sample, cache_ckv, cache_krope, state_gla, w_in, g_norm_mix, g_qnorm, w_uq, g_kvnorm, w_ukv, w_gate2, b_gate2, g_gla_norm, w_branch_a, w_branch_b, w_out, g_norm_ffn, w_router_group, b_router_group, w_router_expert, b_router_expert, w_exp_gate, w_exp_up, w_exp_down, g_norm_final):
    depth = w_in.shape[0]
    assert depth == 1, "the final norm is fused into the last layer's combine step"
    gfin = g_norm_final.reshape(1, D_MODEL)
    w = _prep_weights(w_in[0], g_norm_mix[0], g_qnorm[0], w_uq[0], g_kvnorm[0], w_ukv[0], w_gate2[0],
                      b_gate2[0], g_gla_norm[0], w_branch_a[0], w_branch_b[0], w_out[0], g_norm_ffn[0],
                      w_router_group[0], b_router_group[0], w_router_expert[0], b_router_expert[0],
                      w_exp_gate[0], w_exp_up[0], w_exp_down[0])
    bp = x_prompt.shape[0]
    zero_state = jnp.zeros((bp, GLA_HEADS, GLA_DK, GLA_DV), F32)
    yp, c1, k1, s1 = _group(x_prompt, None, None, zero_state, w, gfin)
    ys, c2, k2, s2 = _group(x_sample, cache_ckv[0], cache_krope[0], state_gla[0].astype(F32), w, gfin)
    return (yp, ys, c1[None], k1[None], s1[None], c2[None], k2[None], s2[None])
```

```python
import functools
import math

import jax
import jax.numpy as jnp
from jax import lax
from jax.experimental import pallas as pl
from jax.experimental.pallas import tpu as pltpu

F32 = jnp.float32
BF16 = jnp.bfloat16

D_MODEL = 1024
CHUNK = 64
EPS = 1e-6
MLA_HEADS = 8
Q_LORA = 256
KV_LORA = 128
NOPE_DIM = 64
ROPE_DIM = 32
V_DIM = 64
ROPE_THETA = 10000.0
GLA_HEADS = 4
GLA_DK = 128
GLA_DV = 256
GATE_RANK = 16
GATE_TEMP = 16.0
N_GROUPS = 4
EXPERTS_PER_GROUP = 8
N_EXPERTS = 32
EXPERT_FF = 512

LANES = 128
QCAT = 2 * LANES
ROW_TILE = 256
Q_BLOCK = 128
KEY_TILE = ROW_TILE
ATT_LANE_GROUP = 256
GLA_CHUNK = 64
FFN_ROWS = 256
ROUTER_ROWS = 48
VMEM_LIMIT = 56 * 1024 * 1024
LOG2E = 1.4426950408889634
ATT_SCALE = LOG2E / math.sqrt(NOPE_DIM + ROPE_DIM)

_NT = (((1,), (1,)), ((), ()))
_TN = (((0,), (0,)), ((), ()))


def _params(sem):
    return pltpu.CompilerParams(dimension_semantics=sem, vmem_limit_bytes=VMEM_LIMIT)


def _rms(x, g):
    return x * lax.rsqrt(jnp.mean(x * x, axis=-1, keepdims=True) + EPS) * g


def _sigmoid(x):
    return 1.0 / (1.0 + jnp.exp(-x))


def _full(shape):
    n = len(shape)
    return pl.BlockSpec(shape, lambda *_: (0,) * n)


SLABS = D_MODEL // LANES


def _store_token_major(ref, x):
    rows = x.shape[0]
    for j in range(SLABS):
        ref[pl.ds(j, rows, stride=SLABS), :] = x[:, j * LANES:(j + 1) * LANES]


def _load_token_major(ref, rows):
    return jnp.concatenate([ref[pl.ds(j, rows, stride=SLABS), :] for j in range(SLABS)], axis=1)


def _fold_q_kernel(wq_ref, wk_ref, o_ref):
    o_ref[0] = lax.dot_general(wq_ref[0], wk_ref[0], _NT, precision=lax.Precision.HIGHEST,
                               preferred_element_type=F32)


def _fold_q(wq_nope, wk_nope):
    return pl.pallas_call(
        _fold_q_kernel,
        grid=(MLA_HEADS,),
        in_specs=[pl.BlockSpec((1, Q_LORA, NOPE_DIM), lambda h: (h, 0, 0)),
                  pl.BlockSpec((1, KV_LORA, NOPE_DIM), lambda h: (h, 0, 0))],
        out_specs=pl.BlockSpec((1, Q_LORA, KV_LORA), lambda h: (h, 0, 0)),
        out_shape=jax.ShapeDtypeStruct((MLA_HEADS, Q_LORA, KV_LORA), F32),
        compiler_params=_params(("arbitrary",)),
        name="fold_q",
    )(wq_nope, wk_nope)


def _inproj_kernel(x_ref, cos_ref, sin_ref, gmix_ref, wsm_ref, gqn_ref, gkvn_ref, wqc_ref,
                   wg2_ref, bg2_ref, wgqk_ref, wgv_ref, wgr_ref, wga_ref, wgb_ref,
                   qcat_ref, kcat_ref, ckv_ref, kpe_ref, gq_ref, gk_ref, gv_ref, la_ref,
                   sgr_ref, sga_ref, sgb_ref, *maybe_vt_ref, transposed):
    hb = _rms(x_ref[...], gmix_ref[...]).astype(BF16)
    zs = jnp.dot(hb, wsm_ref[...], preferred_element_type=F32)
    cos = cos_ref[...]
    sin = sin_ref[...]
    first_half = lax.broadcasted_iota(jnp.int32, cos.shape, 1) < ROPE_DIM // 2

    def rope(v):
        rot = jnp.where(first_half, pltpu.roll(v, LANES - ROPE_DIM // 2, 1),
                        pltpu.roll(v, ROPE_DIM // 2, 1))
        return v * cos + rot * sin

    cqn = _rms(zs[:, :Q_LORA], gqn_ref[...]).astype(BF16)
    qc = jnp.dot(cqn, wqc_ref[...], preferred_element_type=F32)
    for h in range(MLA_HEADS):
        lat = qc[:, h * QCAT:h * QCAT + LANES]
        pe = qc[:, h * QCAT + LANES:(h + 1) * QCAT]
        if transposed:
            lat_t = jnp.transpose(lat * ATT_SCALE).astype(BF16)
            pe_t = jnp.transpose(rope(pe) * ATT_SCALE).astype(BF16)
            for blk in range(lat.shape[0] // Q_BLOCK):
                cols = slice(blk * Q_BLOCK, (blk + 1) * Q_BLOCK)
                qcat_ref[blk, :LANES, h * Q_BLOCK:(h + 1) * Q_BLOCK] = lat_t[:, cols]
                qcat_ref[blk, LANES:, h * Q_BLOCK:(h + 1) * Q_BLOCK] = pe_t[:, cols]
        else:
            qcat_ref[h, :, :LANES] = (lat * ATT_SCALE).astype(BF16)
            qcat_ref[h, :, LANES:] = (rope(pe) * ATT_SCALE).astype(BF16)

    ckv = _rms(zs[:, Q_LORA:Q_LORA + KV_LORA], gkvn_ref[...])
    ckv_ref[...] = ckv
    if transposed:
        maybe_vt_ref[0][0] = jnp.transpose(ckv).astype(BF16)
    kpe = rope(zs[:, Q_LORA + KV_LORA:Q_LORA + KV_LORA + LANES])
    kpe_ref[...] = kpe[:, :ROPE_DIM]
    kcat_ref[:, :LANES] = ckv.astype(BF16)
    kcat_ref[:, LANES:] = kpe.astype(BF16)

    glr = zs[:, Q_LORA + KV_LORA + LANES:].astype(BF16)
    xg = jnp.dot(glr, wg2_ref[...], preferred_element_type=F32) + bg2_ref[...]
    la_ref[...] = (jnp.minimum(xg, 0.0) - jnp.log(1.0 + jnp.exp(-jnp.abs(xg)))) * (1.0 / GATE_TEMP)

    zqk = jnp.dot(hb, wgqk_ref[...], preferred_element_type=F32)
    nqk = GLA_HEADS * GLA_DK
    gq_ref[...] = (zqk[:, :nqk] * (GLA_DK ** -0.5)).astype(BF16)
    gk_ref[...] = zqk[:, nqk:].astype(BF16)
    gv_ref[...] = jnp.dot(hb, wgv_ref[...], preferred_element_type=F32).astype(BF16)
    gr = jnp.dot(hb, wgr_ref[...], preferred_element_type=F32)
    sgr_ref[...] = (gr * _sigmoid(gr)).astype(BF16)
    sga_ref[...] = _sigmoid(jnp.dot(hb, wga_ref[...], preferred_element_type=F32)).astype(BF16)
    sgb_ref[...] = _sigmoid(jnp.dot(hb, wgb_ref[...], preferred_element_type=F32)).astype(BF16)


def _inproj(x2, cos_t, sin_t, w, transposed):
    n = x2.shape[0]
    tm = ROW_TILE
    nt = n // tm
    tab_blocks = cos_t.shape[0] // tm
    row = lambda width: pl.BlockSpec((tm, width), lambda i: (i, 0))
    tab = pl.BlockSpec((tm, LANES), lambda i: (i % tab_blocks, 0))
    nqk = GLA_HEADS * GLA_DK
    nv = GLA_HEADS * GLA_DV
    weights = [w["g_mix"], w["w_small"], w["g_qn"], w["g_kvn"], w["w_qcat"], w["w_g2"], w["b_g2"],
               w["w_gqk"], w["w_gv"], w["w_gr"], w["w_ga"], w["w_gb"]]
    out_shape = [
        jax.ShapeDtypeStruct((MLA_HEADS, n, QCAT), BF16),
        jax.ShapeDtypeStruct((n, QCAT), BF16),
        jax.ShapeDtypeStruct((n, KV_LORA), F32),
        jax.ShapeDtypeStruct((n, ROPE_DIM), F32),
        jax.ShapeDtypeStruct((n, nqk), BF16),
        jax.ShapeDtypeStruct((n, nqk), BF16),
        jax.ShapeDtypeStruct((n, nv), BF16),
        jax.ShapeDtypeStruct((n, nqk), F32),
        jax.ShapeDtypeStruct((n, nv), BF16),
        jax.ShapeDtypeStruct((n, D_MODEL), BF16),
        jax.ShapeDtypeStruct((n, D_MODEL), BF16),
    ]
    out_specs = [
        pl.BlockSpec((MLA_HEADS, tm, QCAT), lambda i: (0, i, 0)),
        row(QCAT), row(KV_LORA), row(ROPE_DIM), row(nqk), row(nqk), row(nv), row(nqk), row(nv),
        row(D_MODEL), row(D_MODEL),
    ]
    if transposed:
        qb = tm // Q_BLOCK
        out_shape[0] = jax.ShapeDtypeStruct((n // Q_BLOCK, QCAT, MLA_HEADS * Q_BLOCK), BF16)
        out_specs[0] = pl.BlockSpec((qb, QCAT, MLA_HEADS * Q_BLOCK), lambda i: (i, 0, 0))
        out_shape.append(jax.ShapeDtypeStruct((nt, KV_LORA, tm), BF16))
        out_specs.append(pl.BlockSpec((1, KV_LORA, tm), lambda i: (i, 0, 0)))
    return pl.pallas_call(
        functools.partial(_inproj_kernel, transposed=transposed),
        grid=(nt,),
        in_specs=[row(D_MODEL), tab, tab] + [_full(a.shape) for a in weights],
        out_specs=out_specs,
        out_shape=out_shape,
        compiler_params=_params(("parallel",)),
        name="inproj",
    )(x2, cos_t, sin_t, *weights)


def _head_out(o_lat, wuv_ref, rows):
    outs = []
    for h in range(MLA_HEADS):
        oh = o_lat[h * rows:(h + 1) * rows].astype(BF16)
        outs.append(jnp.dot(oh, wuv_ref[h], preferred_element_type=F32))
    return jnp.concatenate(outs, axis=1)


def _attn_prompt_kernel(qt_ref, k_ref, vt_ref, wuv_ref, o_ref, m_sc, l_sc, acc_sc):
    i = pl.program_id(1)
    cols = MLA_HEADS * Q_BLOCK
    m_sc[...] = jnp.full(m_sc.shape, -jnp.inf, F32)
    l_sc[...] = jnp.zeros(l_sc.shape, F32)
    acc_sc[...] = jnp.zeros(acc_sc.shape, F32)
    n_tiles = (i * Q_BLOCK + Q_BLOCK + KEY_TILE - 1) // KEY_TILE
    groups = [slice(g * ATT_LANE_GROUP, (g + 1) * ATT_LANE_GROUP) for g in range(cols // ATT_LANE_GROUP)]

    def process(tiles):
        kbs = [k_ref[0, pl.ds(pl.multiple_of(kt * KEY_TILE, KEY_TILE), KEY_TILE), :] for kt, _ in tiles]
        vts = [vt_ref[kt] for kt, _ in tiles]
        chains = [(t, gs) for t in range(len(tiles)) for gs in groups]
        ss = []
        for t, gs in chains:
            s = jnp.dot(kbs[t], qt_ref[0, :, gs], preferred_element_type=F32)
            if tiles[t][1]:
                r = lax.broadcasted_iota(jnp.int32, (KEY_TILE, 1), 0)
                k_chunk = (tiles[t][0] * KEY_TILE + r) // CHUNK
                c = lax.broadcasted_iota(jnp.int32, (1, ATT_LANE_GROUP), 1)
                q_chunk = (i * Q_BLOCK + (c & (Q_BLOCK - 1))) // CHUNK
                s = jnp.where(k_chunk <= q_chunk, s, -jnp.inf)
            ss.append(s)
        mts = [jnp.max(s, axis=0, keepdims=True) for s in ss]
        ps = [jnp.exp2(s - mt) for s, mt in zip(ss, mts)]
        lts = [jnp.sum(p, axis=0, keepdims=True) for p in ps]
        pvs = [jnp.dot(vts[t], p.astype(BF16), preferred_element_type=F32) for (t, _), p in zip(chains, ps)]
        for gs in groups:
            m, l, acc = m_sc[:, gs], l_sc[:, gs], acc_sc[:, gs]
            for idx, (t, cgs) in enumerate(chains):
                if cgs is not gs:
                    continue
                m_new = jnp.maximum(m, mts[idx])
                a = jnp.exp2(m - m_new)
                b = jnp.exp2(mts[idx] - m_new)
                l = a * l + b * lts[idx]
                acc = a * acc + b * pvs[idx]
                m = m_new
            m_sc[:, gs], l_sc[:, gs], acc_sc[:, gs] = m, l, acc

    def body(j, carry):
        process([(2 * j, False), (2 * j + 1, False)])
        return carry

    n_full = n_tiles - 1
    lax.fori_loop(0, n_full // 2, body, 0)

    @pl.when(n_full % 2 == 0)
    def _():
        process([(n_tiles - 1, True)])

    @pl.when(n_full % 2 == 1)
    def _():
        process([(n_tiles - 2, False), (n_tiles - 1, True)])

    o_lat_t = (acc_sc[...] / l_sc[...]).astype(BF16)
    outs = []
    for h in range(MLA_HEADS):
        outs.append(lax.dot_general(o_lat_t[:, h * Q_BLOCK:(h + 1) * Q_BLOCK], wuv_ref[h], _TN,
                                    preferred_element_type=F32))
    o_ref[0] = jnp.concatenate(outs, axis=1).astype(BF16)


def _attn_prompt(qt, kcat, vt, wuv, b, l):
    nq = l // Q_BLOCK
    nkt = l // KEY_TILE
    cols = MLA_HEADS * Q_BLOCK
    return pl.pallas_call(
        _attn_prompt_kernel,
        grid=(b, nq),
        in_specs=[pl.BlockSpec((1, QCAT, cols), lambda bb, i: (bb * nq + i, 0, 0)),
                  pl.BlockSpec((1, l, QCAT), lambda bb, i: (bb, 0, 0)),
                  pl.BlockSpec((nkt, KV_LORA, KEY_TILE), lambda bb, i: (bb, 0, 0)),
                  _full(wuv.shape)],
        out_specs=pl.BlockSpec((1, Q_BLOCK, MLA_HEADS * V_DIM), lambda bb, i: (bb, i, 0)),
        out_shape=jax.ShapeDtypeStruct((b, l, MLA_HEADS * V_DIM), BF16),
        scratch_shapes=[pltpu.VMEM((1, cols), F32), pltpu.VMEM((1, cols), F32),
                        pltpu.VMEM((KV_LORA, cols), F32)],
        compiler_params=_params(("parallel", "arbitrary")),
        name="attn_prompt",
    )(qt, kcat.reshape(b, l, QCAT), vt, wuv)


def _attn_sample_kernel(q_ref, cckv_ref, ckr_ref, kn_ref, wuv_ref, o_ref, *, ls):
    q = q_ref[...].reshape(MLA_HEADS * ls, QCAT)
    ck = cckv_ref[0].astype(BF16)
    kr = ckr_ref[0].astype(BF16)
    kn = kn_ref[...]
    s_c = (lax.dot_general(q[:, :KV_LORA], ck, _NT, preferred_element_type=F32)
           + lax.dot_general(q[:, KV_LORA:KV_LORA + ROPE_DIM], kr, _NT, preferred_element_type=F32))
    s_n = lax.dot_general(q, kn, _NT, preferred_element_type=F32)
    m = jnp.maximum(jnp.max(s_c, axis=1, keepdims=True), jnp.max(s_n, axis=1, keepdims=True))
    p_c = jnp.exp2(s_c - m)
    p_n = jnp.exp2(s_n - m)
    den = jnp.sum(p_c, axis=1, keepdims=True) + jnp.sum(p_n, axis=1, keepdims=True)
    o_lat = (jnp.dot(p_c.astype(BF16), ck, preferred_element_type=F32)
             + jnp.dot(p_n.astype(BF16), kn[:, :KV_LORA], preferred_element_type=F32)) / den
    o_ref[0] = _head_out(o_lat, wuv_ref, ls).astype(BF16)


def _attn_sample(qcat, kcat, cache_ckv, cache_krope, wuv, b, ls):
    past = cache_ckv.shape[1]
    assert past % CHUNK == 0 and ls <= CHUNK
    return pl.pallas_call(
        functools.partial(_attn_sample_kernel, ls=ls),
        grid=(b,),
        in_specs=[pl.BlockSpec((MLA_HEADS, ls, QCAT), lambda bb: (0, bb, 0)),
                  pl.BlockSpec((1, past, KV_LORA), lambda bb: (bb, 0, 0)),
                  pl.BlockSpec((1, past, ROPE_DIM), lambda bb: (bb, 0, 0)),
                  pl.BlockSpec((ls, QCAT), lambda bb: (bb, 0)),
                  _full(wuv.shape)],
        out_specs=pl.BlockSpec((1, ls, MLA_HEADS * V_DIM), lambda bb: (bb, 0, 0)),
        out_shape=jax.ShapeDtypeStruct((b, ls, MLA_HEADS * V_DIM), BF16),
        compiler_params=_params(("parallel",)),
        name="attn_sample",
    )(qcat, cache_ckv, cache_krope, kcat, wuv)


def _cumsum_rows(x):
    c = x.shape[0]
    row = lax.broadcasted_iota(jnp.int32, x.shape, 0)
    s = 1
    while s < c:
        x = x + jnp.where(row >= s, pltpu.roll(x, s, 0), 0.0)
        s *= 2
    return x


def _gla_kernel(q_ref, k_ref, v_ref, la_ref, sgr_ref, gn_ref, s0_ref, o_ref, sout_ref, st_sc, *, c):
    j = pl.program_id(1)

    @pl.when(j == 0)
    def _():
        st_sc[...] = s0_ref[0]

    half = 32
    row = lax.broadcasted_iota(jnp.int32, (c, GLA_DK), 0)
    rr = lax.broadcasted_iota(jnp.int32, (c, c), 0)
    cc = lax.broadcasted_iota(jnp.int32, (c, c), 1)
    causal = cc <= rr
    gn = gn_ref[...]
    for h in range(GLA_HEADS):
        ks = slice(h * GLA_DK, (h + 1) * GLA_DK)
        vs = slice(h * GLA_DV, (h + 1) * GLA_DV)
        q = q_ref[0, :, ks].astype(F32)
        k = k_ref[0, :, ks].astype(F32)
        v = v_ref[0, :, vs]
        b = _cumsum_rows(la_ref[0, :, ks])
        if c > half:
            mid = jnp.where(row < half, b[half // 2 - 1:half // 2, :], b[half + half // 2 - 1:half + half // 2, :])
        else:
            mid = jnp.broadcast_to(b[c // 2 - 1:c // 2, :], b.shape)
        qe = (q * jnp.exp(b - mid)).astype(BF16)
        ke = (k * jnp.exp(mid - b)).astype(BF16)
        att = lax.dot_general(qe, ke, _NT, preferred_element_type=F32)
        if c > half:
            edge = b[half - 1:half, :]
            qo = (q * jnp.exp(jnp.minimum(b - edge, 0.0))).astype(BF16)
            ko = (k * jnp.exp(jnp.minimum(edge - b, 0.0))).astype(BF16)
            att_off = lax.dot_general(qo, ko, _NT, preferred_element_type=F32)
            same = (rr < half) == (cc < half)
            att = jnp.where(same, att, att_off)
        att = jnp.where(causal, att, 0.0).astype(BF16)
        st = st_sc[h]
        q0 = (q * jnp.exp(b)).astype(BF16)
        o = (jnp.dot(q0, st.astype(BF16), preferred_element_type=F32)
             + jnp.dot(att, v, preferred_element_type=F32))
        last = b[c - 1:c, :]
        kd = (k * jnp.exp(last - b)).astype(BF16)
        upd = lax.dot_general(kd, v, _TN, preferred_element_type=F32)
        dcol = jnp.exp(jnp.transpose(jnp.broadcast_to(last, (GLA_DK, GLA_DK))))
        st_sc[h] = jnp.concatenate([dcol, dcol], axis=1) * st + upd
        on = _rms(o, gn) * sgr_ref[0, :, vs].astype(F32)
        o_ref[0, :, vs] = on.astype(BF16)

    @pl.when(j == pl.num_programs(1) - 1)
    def _():
        sout_ref[0] = st_sc[...]


def _gla(gq, gk, gv, la, sgr, gn, s0, b, l):
    c = min(GLA_CHUNK, l)
    nc = l // c
    nqk = GLA_HEADS * GLA_DK
    nv = GLA_HEADS * GLA_DV
    r3 = lambda a: a.reshape(b, l, a.shape[-1])
    tok = lambda width: pl.BlockSpec((1, c, width), lambda bb, j: (bb, j, 0))
    st = pl.BlockSpec((1, GLA_HEADS, GLA_DK, GLA_DV), lambda bb, j: (bb, 0, 0, 0))
    return pl.pallas_call(
        functools.partial(_gla_kernel, c=c),
        grid=(b, nc),
        in_specs=[tok(nqk), tok(nqk), tok(nv), tok(nqk), tok(nv), _full(gn.shape), st],
        out_specs=[tok(nv), st],
        out_shape=[jax.ShapeDtypeStruct((b, l, nv), BF16),
                   jax.ShapeDtypeStruct((b, GLA_HEADS, GLA_DK, GLA_DV), F32)],
        scratch_shapes=[pltpu.VMEM((GLA_HEADS, GLA_DK, GLA_DV), F32)],
        compiler_params=_params(("parallel", "arbitrary")),
        name="gla",
    )(r3(gq), r3(gk), r3(gv), r3(la), r3(sgr), gn, s0)


def _merge_kernel(oa_ref, ob_ref, sga_ref, sgb_ref, x_ref, wa_ref, wb_ref, wo_ref, gffn_ref,
                  wr_ref, br_ref, x1_ref, h2_ref, route_ref, gcol_ref, cnt_ref, run_sc):
    ya = jnp.dot(oa_ref[...], wa_ref[...], preferred_element_type=F32)
    yb = jnp.dot(ob_ref[...], wb_ref[...], preferred_element_type=F32)
    merged = (sga_ref[...].astype(F32) * ya + sgb_ref[...].astype(F32) * yb).astype(BF16)
    x1 = x_ref[...] + jnp.dot(merged, wo_ref[...], preferred_element_type=F32)
    x1_ref[...] = x1
    h2 = _rms(x1, gffn_ref[...])
    _store_token_major(h2_ref, h2)

    hi = h2.astype(BF16)
    lo = (h2 - hi.astype(F32)).astype(BF16)
    wr = wr_ref[...]
    whi = wr.astype(BF16)
    wlo = (wr - whi.astype(F32)).astype(BF16)
    logits = (lax.dot_general(whi, hi, _NT, preferred_element_type=F32)
              + lax.dot_general(whi, lo, _NT, preferred_element_type=F32)
              + lax.dot_general(wlo, hi, _NT, preferred_element_type=F32)) + br_ref[...]
    tm = logits.shape[1]
    ridx = lax.broadcasted_iota(jnp.int32, (EXPERTS_PER_GROUP, tm), 0)
    big = jnp.int32(1 << 20)

    def top(vals):
        vmax = jnp.max(vals, axis=0, keepdims=True)
        imax = jnp.min(jnp.where(vals == vmax, ridx, big), axis=0, keepdims=True)
        return vmax, imax

    gl = jnp.where(ridx < N_GROUPS, logits[N_EXPERTS:N_EXPERTS + EXPERTS_PER_GROUP], -jnp.inf)
    gmax, g_top = top(gl)
    p_top = 1.0 / jnp.sum(jnp.exp(gl - gmax), axis=0, keepdims=True)
    e_sel = logits[:EXPERTS_PER_GROUP]
    for g in range(1, N_GROUPS):
        e_sel = jnp.where(g_top == g, logits[g * EXPERTS_PER_GROUP:(g + 1) * EXPERTS_PER_GROUP], e_sel)
    v1, i1 = top(e_sel)
    v2, i2 = top(jnp.where(ridx == i1, -jnp.inf, e_sel))
    e21 = jnp.exp(v2 - v1)
    w1 = p_top / (1.0 + e21)
    w2 = p_top * e21 / (1.0 + e21)
    base = g_top * EXPERTS_PER_GROUP
    id0 = base + i1
    id1 = base + i2
    gates = jnp.where(ridx == 0, w1, jnp.where(ridx == 1, w2, 0.0))
    gcol_ref[...] = jnp.transpose(jnp.concatenate([gates, jnp.zeros((LANES - EXPERTS_PER_GROUP, tm), F32)], axis=0))

    @pl.when(pl.program_id(0) == 0)
    def _():
        run_sc[...] = jnp.zeros(run_sc.shape, F32)

    eidx = lax.broadcasted_iota(jnp.int32, (N_EXPERTS, tm), 0)
    oh0 = jnp.where(eidx == id0, 1.0, 0.0)
    oh1 = jnp.where(eidx == id1, 1.0, 0.0)
    earlier = jnp.where(lax.broadcasted_iota(jnp.int32, (tm, tm), 0) < lax.broadcasted_iota(jnp.int32, (tm, tm), 1),
                        1.0, 0.0).astype(BF16)
    p0 = jnp.dot(oh0.astype(BF16), earlier, preferred_element_type=F32)
    p1 = jnp.dot(oh1.astype(BF16), earlier, preferred_element_type=F32)
    c0 = jnp.sum(oh0, axis=1, keepdims=True)
    c1 = jnp.sum(oh1, axis=1, keepdims=True)
    run = run_sc[...]
    rank0 = jnp.sum(oh0 * (run + p0), axis=0, keepdims=True)
    rank1 = jnp.sum(oh1 * (run + c0 + p1), axis=0, keepdims=True)
    run = run + c0 + c1
    run_sc[...] = run
    cnt_ref[...] = jnp.broadcast_to(run, cnt_ref.shape)
    route_ref[...] = jnp.where(ridx == 0, id0, jnp.where(ridx == 1, id1, jnp.where(
        ridx == 2, rank0.astype(jnp.int32), jnp.where(ridx == 3, rank1.astype(jnp.int32), 0))))


def _merge(oa, ob, sga, sgb, x2, w):
    n = x2.shape[0]
    tm = ROW_TILE
    row = lambda width: pl.BlockSpec((tm, width), lambda i: (i, 0))
    weights = [w["w_a"], w["w_b"], w["w_o"], w["g_ffn"], w["w_router"], w["b_router"]]
    return pl.pallas_call(
        _merge_kernel,
        grid=(n // tm,),
        in_specs=[row(MLA_HEADS * V_DIM), row(GLA_HEADS * GLA_DV), row(D_MODEL), row(D_MODEL),
                  row(D_MODEL)] + [_full(a.shape) for a in weights],
        out_specs=[row(D_MODEL), pl.BlockSpec((tm * SLABS, LANES), lambda i: (i, 0)),
                   pl.BlockSpec((EXPERTS_PER_GROUP, tm), lambda i: (0, i)), row(LANES),
                   _full((N_EXPERTS, LANES))],
        out_shape=[jax.ShapeDtypeStruct((n, D_MODEL), F32), jax.ShapeDtypeStruct((n * SLABS, LANES), F32),
                   jax.ShapeDtypeStruct((EXPERTS_PER_GROUP, n), jnp.int32),
                   jax.ShapeDtypeStruct((n, LANES), F32),
                   jax.ShapeDtypeStruct((N_EXPERTS, LANES), F32)],
        scratch_shapes=[pltpu.VMEM((N_EXPERTS, 1), F32)],
        compiler_params=_params(("arbitrary",)),
        name="merge",
    )(oa, ob, sga, sgb, x2, *weights)


ROW_DMA_UNROLL = 8


def _row_tile(ref, r):
    return ref.at[pl.ds(pl.multiple_of(r * SLABS, SLABS), SLABS), :]


def _scatter_kernel(ps_ref, cnt_ref, pe_ref, route_ref, h2_ref, xd_hbm, zero_sc, sem, zsem):
    tm = route_ref.shape[1]

    @pl.when(pl.program_id(0) == 0)
    def _():
        zero_sc[...] = jnp.zeros(zero_sc.shape, F32)
        zero_row = zero_sc.at[pl.ds(0, SLABS), :]

        def per_expert(e, carry):
            lo = ps_ref[e] + cnt_ref[e]
            hi = pe_ref[e]

            def start(r, c):
                pltpu.make_async_copy(zero_row, _row_tile(xd_hbm, r), zsem.at[0]).start()
                return c

            def wait(r, c):
                pltpu.make_async_copy(zero_row, _row_tile(xd_hbm, r), zsem.at[0]).wait()
                return c

            lax.fori_loop(lo, hi, start, 0)
            lax.fori_loop(lo, hi, wait, 0)
            return carry

        lax.fori_loop(0, N_EXPERTS, per_expert, 0)

        def block_copy(b):
            start_row = pl.multiple_of(b * (FFN_ROWS * SLABS), FFN_ROWS * SLABS)
            return pltpu.make_async_copy(zero_sc, xd_hbm.at[pl.ds(start_row, FFN_ROWS * SLABS), :], zsem.at[0])

        first_unused = pe_ref[N_EXPERTS - 1] // FFN_ROWS
        n_blocks = xd_hbm.shape[0] // (FFN_ROWS * SLABS)

        def start_block(b, c):
            block_copy(b).start()
            return c

        def wait_block(b, c):
            block_copy(b).wait()
            return c

        lax.fori_loop(first_unused, n_blocks, start_block, 0)
        lax.fori_loop(first_unused, n_blocks, wait_block, 0)

    def body(t, carry):
        for k in range(2):
            d = ps_ref[route_ref[k, t]] + route_ref[2 + k, t]
            pltpu.make_async_copy(_row_tile(h2_ref, t), _row_tile(xd_hbm, d), sem.at[0]).start()
        return carry

    lax.fori_loop(0, tm, body, 0, unroll=ROW_DMA_UNROLL)
    for k in range(2):
        pltpu.make_async_copy(h2_ref, xd_hbm.at[pl.ds(0, tm * SLABS), :], sem.at[0]).wait()


def _scatter(pad_start, counts, pad_end, route, h2, total_rows):
    n = route.shape[1]
    tm = ROW_TILE
    grid_spec = pltpu.PrefetchScalarGridSpec(
        num_scalar_prefetch=3,
        grid=(n // tm,),
        in_specs=[pl.BlockSpec((EXPERTS_PER_GROUP, tm), lambda i, *_: (0, i), memory_space=pltpu.SMEM),
                  pl.BlockSpec((tm * SLABS, LANES), lambda i, *_: (i, 0))],
        out_specs=pl.BlockSpec(memory_space=pl.ANY),
        scratch_shapes=[pltpu.VMEM((FFN_ROWS * SLABS, LANES), F32), pltpu.SemaphoreType.DMA((1,)),
                        pltpu.SemaphoreType.DMA((1,))],
    )
    return pl.pallas_call(
        _scatter_kernel,
        grid_spec=grid_spec,
        out_shape=jax.ShapeDtypeStruct((total_rows * SLABS, LANES), F32),
        compiler_params=_params(("arbitrary",)),
        name="moe_scatter",
    )(pad_start, counts, pad_end, route, h2)


def _ffn_kernel(be_ref, nbu_ref, x_ref, wg_ref, wu_ref, wd_ref, out_ref):
    del be_ref

    @pl.when(pl.program_id(0) < nbu_ref[0])
    def _():
        x = _load_token_major(x_ref, FFN_ROWS).astype(BF16)
        g = jnp.dot(x, wg_ref[0], preferred_element_type=F32)
        u = jnp.dot(x, wu_ref[0], preferred_element_type=F32)
        mid = (g * _sigmoid(g) * u).astype(BF16)
        _store_token_major(out_ref, jnp.dot(mid, wd_ref[0], preferred_element_type=F32))

    @pl.when(pl.program_id(0) >= nbu_ref[0])
    def _():
        out_ref[...] = jnp.zeros(out_ref.shape, F32)


def _ffn(block_expert, nb_used, xd, wg, wu, wd):
    nb = block_expert.shape[0]
    blk = lambda i, be, nbu: (jnp.minimum(i, nbu[0] - 1), 0)
    wsel = lambda i, be, nbu: (be[jnp.minimum(i, nbu[0] - 1)], 0, 0)
    grid_spec = pltpu.PrefetchScalarGridSpec(
        num_scalar_prefetch=2,
        grid=(nb,),
        in_specs=[pl.BlockSpec((FFN_ROWS * SLABS, LANES), blk),
                  pl.BlockSpec((1, D_MODEL, EXPERT_FF), wsel),
                  pl.BlockSpec((1, D_MODEL, EXPERT_FF), wsel),
                  pl.BlockSpec((1, EXPERT_FF, D_MODEL), wsel)],
        out_specs=pl.BlockSpec((FFN_ROWS * SLABS, LANES), lambda i, be, nbu: (i, 0)),
    )
    return pl.pallas_call(
        _ffn_kernel,
        grid_spec=grid_spec,
        out_shape=jax.ShapeDtypeStruct((nb * FFN_ROWS * SLABS, LANES), F32),
        compiler_params=_params(("arbitrary",)),
        name="expert_ffn",
    )(block_expert, nb_used, xd, wg, wu, wd)


def _combine_kernel(ps_ref, route_ref, x1_ref, gcol_ref, gfin_ref, eo_hbm, y_ref, buf, sem):
    tm = x1_ref.shape[0]

    def body(t, carry):
        for k in range(2):
            d = ps_ref[route_ref[k, t]] + route_ref[2 + k, t]
            pltpu.make_async_copy(_row_tile(eo_hbm, d), _row_tile(buf.at[k], t), sem.at[0]).start()
        return carry

    lax.fori_loop(0, tm, body, 0, unroll=ROW_DMA_UNROLL)
    for k in range(2):
        pltpu.make_async_copy(eo_hbm.at[pl.ds(0, tm * SLABS), :], buf.at[k], sem.at[0]).wait()
    g = gcol_ref[...]
    y = x1_ref[...] + (g[:, 0:1] * _load_token_major(buf.at[0], tm) + g[:, 1:2] * _load_token_major(buf.at[1], tm))
    y_ref[...] = _rms(y, gfin_ref[...])


def _combine(pad_start, route, x1, gcol, g_final, eo):
    n = x1.shape[0]
    tm = ROW_TILE
    grid_spec = pltpu.PrefetchScalarGridSpec(
        num_scalar_prefetch=1,
        grid=(n // tm,),
        in_specs=[pl.BlockSpec((EXPERTS_PER_GROUP, tm), lambda i, ps: (0, i), memory_space=pltpu.SMEM),
                  pl.BlockSpec((tm, D_MODEL), lambda i, ps: (i, 0)),
                  pl.BlockSpec((tm, LANES), lambda i, ps: (i, 0)),
                  pl.BlockSpec(g_final.shape, lambda i, ps: (0, 0)),
                  pl.BlockSpec(memory_space=pl.ANY)],
        out_specs=pl.BlockSpec((tm, D_MODEL), lambda i, ps: (i, 0)),
        scratch_shapes=[pltpu.VMEM((2, tm * SLABS, LANES), F32), pltpu.SemaphoreType.DMA((1,))],
    )
    return pl.pallas_call(
        _combine_kernel,
        grid_spec=grid_spec,
        out_shape=jax.ShapeDtypeStruct((n, D_MODEL), F32),
        compiler_params=_params(("arbitrary",)),
        name="combine",
    )(pad_start, route, x1, gcol, g_final, eo)


def _layout(cnt, n):
    counts = cnt[:, 0].astype(jnp.int32)
    padded = (counts + FFN_ROWS - 1) // FFN_ROWS * FFN_ROWS
    pad_end = jnp.cumsum(padded).astype(jnp.int32)
    pad_start = pad_end - padded
    nb = 2 * n // FFN_ROWS + N_EXPERTS
    block_start = jnp.arange(nb, dtype=jnp.int32) * FFN_ROWS
    block_expert = jnp.minimum(jnp.sum((block_start[:, None] >= pad_end[None, :]).astype(jnp.int32), axis=1),
                               N_EXPERTS - 1).astype(jnp.int32)
    nb_used = (pad_end[N_EXPERTS - 1:] // FFN_ROWS).astype(jnp.int32)
    return pad_start, counts, pad_end, block_expert, nb_used, nb * FFN_ROWS


def _rope_tables(l, past, rows):
    half = ROPE_DIM // 2
    inv = ROPE_THETA ** (-jnp.arange(half, dtype=F32) / half)
    pos = (past + (jnp.arange(rows, dtype=jnp.int32) % l)).astype(F32)
    ang = pos[:, None] * inv[None, :]
    pad = jnp.zeros((rows, LANES - ROPE_DIM), F32)
    cos = jnp.concatenate([jnp.cos(ang), jnp.cos(ang), pad], axis=1)
    sin = jnp.concatenate([-jnp.sin(ang), jnp.sin(ang), pad], axis=1)
    return cos, sin


def _group(x, ckv_past, kpe_past, s0, w, g_final):
    b, l, d = x.shape
    n = b * l
    x2 = x.reshape(n, d)
    past = 0 if ckv_past is None else ckv_past.shape[1]
    cos_t, sin_t = _rope_tables(l, past, max(l, ROW_TILE))
    prompt = ckv_past is None
    (qcat, kcat, ckv, kpe, gq, gk, gv, la, sgr, sga, sgb, *maybe_vt) = _inproj(x2, cos_t, sin_t, w, prompt)
    if prompt:
        oa = _attn_prompt(qcat, kcat, maybe_vt[0], w["w_uv"], b, l)
    else:
        oa = _attn_sample(qcat, kcat, ckv_past, kpe_past, w["w_uv"], b, l)
    ob, s_new = _gla(gq, gk, gv, la, sgr, w["g_gla"], s0, b, l)
    x1, h2, route, gcol, cnt = _merge(oa.reshape(n, -1), ob.reshape(n, -1), sga, sgb, x2, w)
    pad_start, counts, pad_end, block_expert, nb_used, total_rows = _layout(cnt, n)
    xd = _scatter(pad_start, counts, pad_end, route, h2, total_rows)
    eo = _ffn(block_expert, nb_used, xd, w["w_eg"], w["w_eu"], w["w_ed"])
    y = _combine(pad_start, route, x1, gcol, g_final, eo)
    return (y.reshape(b, l, d), ckv.reshape(b, l, KV_LORA), kpe.reshape(b, l, ROPE_DIM), s_new)


def _prep_weights(w_in, g_norm_mix, g_qnorm, w_uq, g_kvnorm, w_ukv, w_gate2, b_gate2, g_gla_norm,
                  w_branch_a, w_branch_b, w_out, g_norm_ffn, w_router_group, b_router_group,
                  w_router_expert, b_router_expert, w_exp_gate, w_exp_up, w_exp_down):
    nqk = GLA_HEADS * GLA_DK
    nv = GLA_HEADS * GLA_DV
    o = 0
    parts = {}
    for name, width in (("cq", Q_LORA), ("ckv", KV_LORA), ("kpe", ROPE_DIM), ("gq", nqk), ("gk", nqk),
                        ("gv", nv), ("glr", GATE_RANK), ("gr", nv), ("ga", D_MODEL), ("gb", D_MODEL)):
        parts[name] = w_in[:, o:o + width]
        o += width
    padc = lambda a, width: jnp.pad(a, ((0, 0), (0, width - a.shape[1])))
    w_small = jnp.concatenate([parts["cq"], parts["ckv"], padc(parts["kpe"], LANES),
                               padc(parts["glr"], LANES)], axis=1).astype(BF16)
    uq = w_uq.reshape(Q_LORA, MLA_HEADS, NOPE_DIM + ROPE_DIM)
    ukv = w_ukv.reshape(KV_LORA, MLA_HEADS, NOPE_DIM + V_DIM)
    lat = _fold_q(uq[:, :, :NOPE_DIM].transpose(1, 0, 2), ukv[:, :, :NOPE_DIM].transpose(1, 0, 2))
    q_rope = uq[:, :, NOPE_DIM:].transpose(1, 0, 2)
    w_qcat = jnp.concatenate([lat, q_rope, jnp.zeros((MLA_HEADS, Q_LORA, QCAT - KV_LORA - ROPE_DIM), F32)],
                             axis=2)
    w_qcat = w_qcat.transpose(1, 0, 2).reshape(Q_LORA, MLA_HEADS * QCAT).astype(BF16)
    w_router = jnp.concatenate([w_router_expert.T, w_router_group.T,
                                jnp.zeros((ROUTER_ROWS - N_EXPERTS - N_GROUPS, D_MODEL), F32)], axis=0)
    b_router = jnp.concatenate([b_router_expert, b_router_group,
                                jnp.zeros((ROUTER_ROWS - N_EXPERTS - N_GROUPS,), F32)]).reshape(ROUTER_ROWS, 1)
    return {
        "g_mix": g_norm_mix.reshape(1, D_MODEL), "w_small": w_small,
        "g_qn": g_qnorm.reshape(1, Q_LORA), "g_kvn": g_kvnorm.reshape(1, KV_LORA), "w_qcat": w_qcat,
        "w_g2": jnp.pad(w_gate2, ((0, LANES - GATE_RANK), (0, 0))).astype(BF16),
        "b_g2": b_gate2.reshape(1, nqk),
        "w_gqk": jnp.concatenate([parts["gq"], parts["gk"]], axis=1).astype(BF16),
        "w_gv": parts["gv"].astype(BF16), "w_gr": parts["gr"].astype(BF16),
        "w_ga": parts["ga"].astype(BF16), "w_gb": parts["gb"].astype(BF16),
        "w_uv": ukv[:, :, NOPE_DIM:].transpose(1, 0, 2).astype(BF16),
        "g_gla": g_gla_norm.reshape(1, GLA_DV),
        "w_a": w_branch_a.astype(BF16), "w_b": w_branch_b.astype(BF16), "w_o": w_out.astype(BF16),
        "g_ffn": g_norm_ffn.reshape(1, D_MODEL), "w_router": w_router, "b_router": b_router,
        "w_eg": w_exp_gate.astype(BF16), "w_eu": w_exp_up.astype(BF16), "w_ed": w_exp_down.astype(BF16),
    }


def kernel(x_prompt, x_sample, cache_ckv, cache_krope, state_gla, w_in, g_norm_mix, g_qnorm, w_uq, g_kvnorm, w_ukv, w_gate2, b_gate2, g_gla_norm, w_branch_a, w_branch_b, w_out, g_norm_ffn, w_router_group, b_router_group, w_router_expert, b_router_expert, w_exp_gate, w_exp_up, w_exp_down, g_norm_final):
    depth = w_in.shape[0]
    assert depth == 1, "the final norm is fused into the last layer's combine step"
    gfin = g_norm_final.reshape(1, D_MODEL)
    w = _prep_weights(w_in[0], g_norm_mix[0], g_qnorm[0], w_uq[0], g_kvnorm[0], w_ukv[0], w_gate2[0],
                      b_gate2[0], g_gla_norm[0], w_branch_a[0], w_branch_b[0], w_out[0], g_norm_ffn[0],
                      w_router_group[0], b_router_group[0], w_router_expert[0], b_router_expert[0],
                      w_exp_gate[0], w_exp_up[0], w_exp_down[0])
    bp = x_prompt.shape[0]
    zero_state = jnp.zeros((bp, GLA_HEADS, GLA_DK, GLA_DV), F32)
    yp, c1, k1, s1 = _group(x_prompt, None, None, zero_state, w, gfin)
    ys, c2, k2, s2 = _group(x_sample, cache_ckv[0], cache_krope[0], state_gla[0].astype(F32), w, gfin)
    return (yp, ys, c1[None], k1[None], s1[None], c2[None], k2[None], s2[None])
```

```python
import functools
import math

import jax
import jax.numpy as jnp
from jax import lax
from jax.experimental import pallas as pl
from jax.experimental.pallas import tpu as pltpu

F32 = jnp.float32
BF16 = jnp.bfloat16

D_MODEL = 1024
CHUNK = 64
EPS = 1e-6
MLA_HEADS = 8
Q_LORA = 256
KV_LORA = 128
NOPE_DIM = 64
ROPE_DIM = 32
V_DIM = 64
ROPE_THETA = 10000.0
GLA_HEADS = 4
GLA_DK = 128
GLA_DV = 256
GATE_RANK = 16
GATE_TEMP = 16.0
N_GROUPS = 4
EXPERTS_PER_GROUP = 8
N_EXPERTS = 32
EXPERT_FF = 512

LANES = 128
QCAT = 2 * LANES
ROW_TILE = 256
MERGE_TILE = 512
GLA_BATCH = 2
Q_BLOCK = 128
KEY_TILE = ROW_TILE
ATT_LANE_GROUP = 256
GLA_CHUNK = 64
FFN_ROWS = 256
ROUTER_ROWS = 48
VMEM_LIMIT = 56 * 1024 * 1024
LOG2E = 1.4426950408889634
ATT_SCALE = LOG2E / math.sqrt(NOPE_DIM + ROPE_DIM)

_NT = (((1,), (1,)), ((), ()))
_TN = (((0,), (0,)), ((), ()))


def _params(sem):
    return pltpu.CompilerParams(dimension_semantics=sem, vmem_limit_bytes=VMEM_LIMIT)


def _rms(x, g):
    return x * lax.rsqrt(jnp.mean(x * x, axis=-1, keepdims=True) + EPS) * g


def _sigmoid(x):
    return 1.0 / (1.0 + jnp.exp(-x))


def _full(shape):
    n = len(shape)
    return pl.BlockSpec(shape, lambda *_: (0,) * n)


SLABS = D_MODEL // LANES


def _store_token_major(ref, x):
    rows = x.shape[0]
    for j in range(SLABS):
        ref[pl.ds(j, rows, stride=SLABS), :] = x[:, j * LANES:(j + 1) * LANES]


def _load_token_major(ref, rows):
    return jnp.concatenate([ref[pl.ds(j, rows, stride=SLABS), :] for j in range(SLABS)], axis=1)


def _fold_q_kernel(wq_ref, wk_ref, o_ref):
    o_ref[0] = lax.dot_general(wq_ref[0], wk_ref[0], _NT, precision=lax.Precision.HIGHEST,
                               preferred_element_type=F32)


def _fold_q(wq_nope, wk_nope):
    return pl.pallas_call(
        _fold_q_kernel,
        grid=(MLA_HEADS,),
        in_specs=[pl.BlockSpec((1, Q_LORA, NOPE_DIM), lambda h: (h, 0, 0)),
                  pl.BlockSpec((1, KV_LORA, NOPE_DIM), lambda h: (h, 0, 0))],
        out_specs=pl.BlockSpec((1, Q_LORA, KV_LORA), lambda h: (h, 0, 0)),
        out_shape=jax.ShapeDtypeStruct((MLA_HEADS, Q_LORA, KV_LORA), F32),
        compiler_params=_params(("arbitrary",)),
        name="fold_q",
    )(wq_nope, wk_nope)


def _inproj_kernel(x_ref, cos_ref, sin_ref, gmix_ref, wsm_ref, gqn_ref, gkvn_ref, wqc_ref,
                   wg2_ref, bg2_ref, wgqk_ref, wgv_ref, wgr_ref, wga_ref, wgb_ref,
                   qcat_ref, kcat_ref, ckv_ref, kpe_ref, gq_ref, gk_ref, gv_ref, la_ref,
                   sgr_ref, sga_ref, sgb_ref, *maybe_vt_ref, transposed):
    hb = _rms(x_ref[...], gmix_ref[...]).astype(BF16)
    zs = jnp.dot(hb, wsm_ref[...], preferred_element_type=F32)
    cos = cos_ref[...]
    sin = sin_ref[...]
    first_half = lax.broadcasted_iota(jnp.int32, cos.shape, 1) < ROPE_DIM // 2

    def rope(v):
        rot = jnp.where(first_half, pltpu.roll(v, LANES - ROPE_DIM // 2, 1),
                        pltpu.roll(v, ROPE_DIM // 2, 1))
        return v * cos + rot * sin

    cqn = _rms(zs[:, :Q_LORA], gqn_ref[...]).astype(BF16)
    qc = jnp.dot(cqn, wqc_ref[...], preferred_element_type=F32)
    for h in range(MLA_HEADS):
        lat = qc[:, h * QCAT:h * QCAT + LANES]
        pe = qc[:, h * QCAT + LANES:(h + 1) * QCAT]
        if transposed:
            lat_t = jnp.transpose(lat * ATT_SCALE).astype(BF16)
            pe_t = jnp.transpose(rope(pe) * ATT_SCALE).astype(BF16)
            for blk in range(lat.shape[0] // Q_BLOCK):
                cols = slice(blk * Q_BLOCK, (blk + 1) * Q_BLOCK)
                qcat_ref[blk, :LANES, h * Q_BLOCK:(h + 1) * Q_BLOCK] = lat_t[:, cols]
                qcat_ref[blk, LANES:, h * Q_BLOCK:(h + 1) * Q_BLOCK] = pe_t[:, cols]
        else:
            qcat_ref[h, :, :LANES] = (lat * ATT_SCALE).astype(BF16)
            qcat_ref[h, :, LANES:] = (rope(pe) * ATT_SCALE).astype(BF16)

    ckv = _rms(zs[:, Q_LORA:Q_LORA + KV_LORA], gkvn_ref[...])
    ckv_ref[...] = ckv
    if transposed:
        maybe_vt_ref[0][0] = jnp.transpose(ckv).astype(BF16)
    kpe = rope(zs[:, Q_LORA + KV_LORA:Q_LORA + KV_LORA + LANES])
    kpe_ref[...] = kpe[:, :ROPE_DIM]
    kcat_ref[:, :LANES] = ckv.astype(BF16)
    kcat_ref[:, LANES:] = kpe.astype(BF16)

    glr = zs[:, Q_LORA + KV_LORA + LANES:].astype(BF16)
    xg = jnp.dot(glr, wg2_ref[...], preferred_element_type=F32) + bg2_ref[...]
    la_ref[...] = (jnp.minimum(xg, 0.0) - jnp.log(1.0 + jnp.exp(-jnp.abs(xg)))) * (1.0 / GATE_TEMP)

    zqk = jnp.dot(hb, wgqk_ref[...], preferred_element_type=F32)
    nqk = GLA_HEADS * GLA_DK
    gq_ref[...] = (zqk[:, :nqk] * (GLA_DK ** -0.5)).astype(BF16)
    gk_ref[...] = zqk[:, nqk:].astype(BF16)
    gv_ref[...] = jnp.dot(hb, wgv_ref[...], preferred_element_type=F32).astype(BF16)
    gr = jnp.dot(hb, wgr_ref[...], preferred_element_type=F32)
    sgr_ref[...] = (gr * _sigmoid(gr)).astype(BF16)
    sga_ref[...] = _sigmoid(jnp.dot(hb, wga_ref[...], preferred_element_type=F32)).astype(BF16)
    sgb_ref[...] = _sigmoid(jnp.dot(hb, wgb_ref[...], preferred_element_type=F32)).astype(BF16)


def _inproj(x2, cos_t, sin_t, w, transposed):
    n = x2.shape[0]
    tm = ROW_TILE
    nt = n // tm
    tab_blocks = cos_t.shape[0] // tm
    row = lambda width: pl.BlockSpec((tm, width), lambda i: (i, 0))
    tab = pl.BlockSpec((tm, LANES), lambda i: (i % tab_blocks, 0))
    nqk = GLA_HEADS * GLA_DK
    nv = GLA_HEADS * GLA_DV
    weights = [w["g_mix"], w["w_small"], w["g_qn"], w["g_kvn"], w["w_qcat"], w["w_g2"], w["b_g2"],
               w["w_gqk"], w["w_gv"], w["w_gr"], w["w_ga"], w["w_gb"]]
    out_shape = [
        jax.ShapeDtypeStruct((MLA_HEADS, n, QCAT), BF16),
        jax.ShapeDtypeStruct((n, QCAT), BF16),
        jax.ShapeDtypeStruct((n, KV_LORA), F32),
        jax.ShapeDtypeStruct((n, ROPE_DIM), F32),
        jax.ShapeDtypeStruct((n, nqk), BF16),
        jax.ShapeDtypeStruct((n, nqk), BF16),
        jax.ShapeDtypeStruct((n, nv), BF16),
        jax.ShapeDtypeStruct((n, nqk), F32),
        jax.ShapeDtypeStruct((n, nv), BF16),
        jax.ShapeDtypeStruct((n, D_MODEL), BF16),
        jax.ShapeDtypeStruct((n, D_MODEL), BF16),
    ]
    out_specs = [
        pl.BlockSpec((MLA_HEADS, tm, QCAT), lambda i: (0, i, 0)),
        row(QCAT), row(KV_LORA), row(ROPE_DIM), row(nqk), row(nqk), row(nv), row(nqk), row(nv),
        row(D_MODEL), row(D_MODEL),
    ]
    if transposed:
        qb = tm // Q_BLOCK
        out_shape[0] = jax.ShapeDtypeStruct((n // Q_BLOCK, QCAT, MLA_HEADS * Q_BLOCK), BF16)
        out_specs[0] = pl.BlockSpec((qb, QCAT, MLA_HEADS * Q_BLOCK), lambda i: (i, 0, 0))
        out_shape.append(jax.ShapeDtypeStruct((nt, KV_LORA, tm), BF16))
        out_specs.append(pl.BlockSpec((1, KV_LORA, tm), lambda i: (i, 0, 0)))
    return pl.pallas_call(
        functools.partial(_inproj_kernel, transposed=transposed),
        grid=(nt,),
        in_specs=[row(D_MODEL), tab, tab] + [_full(a.shape) for a in weights],
        out_specs=out_specs,
        out_shape=out_shape,
        compiler_params=_params(("parallel",)),
        name="inproj",
    )(x2, cos_t, sin_t, *weights)


def _head_out(o_lat, wuv_ref, rows):
    outs = []
    for h in range(MLA_HEADS):
        oh = o_lat[h * rows:(h + 1) * rows].astype(BF16)
        outs.append(jnp.dot(oh, wuv_ref[h], preferred_element_type=F32))
    return jnp.concatenate(outs, axis=1)


def _attn_prompt_kernel(qt_ref, k_ref, vt_ref, wuv_ref, o_ref, m_sc, l_sc, acc_sc):
    i = pl.program_id(1)
    cols = MLA_HEADS * Q_BLOCK
    m_sc[...] = jnp.full(m_sc.shape, -jnp.inf, F32)
    l_sc[...] = jnp.zeros(l_sc.shape, F32)
    acc_sc[...] = jnp.zeros(acc_sc.shape, F32)
    n_tiles = (i * Q_BLOCK + Q_BLOCK + KEY_TILE - 1) // KEY_TILE
    groups = [slice(g * ATT_LANE_GROUP, (g + 1) * ATT_LANE_GROUP) for g in range(cols // ATT_LANE_GROUP)]

    def process(tiles):
        kbs = [k_ref[0, pl.ds(pl.multiple_of(kt * KEY_TILE, KEY_TILE), KEY_TILE), :] for kt, _ in tiles]
        vts = [vt_ref[kt] for kt, _ in tiles]
        chains = [(t, gs) for t in range(len(tiles)) for gs in groups]
        ss = []
        for t, gs in chains:
            s = jnp.dot(kbs[t], qt_ref[0, :, gs], preferred_element_type=F32)
            if tiles[t][1]:
                r = lax.broadcasted_iota(jnp.int32, (KEY_TILE, 1), 0)
                k_chunk = (tiles[t][0] * KEY_TILE + r) // CHUNK
                c = lax.broadcasted_iota(jnp.int32, (1, ATT_LANE_GROUP), 1)
                q_chunk = (i * Q_BLOCK + (c & (Q_BLOCK - 1))) // CHUNK
                s = jnp.where(k_chunk <= q_chunk, s, -jnp.inf)
            ss.append(s)
        mts = [jnp.max(s, axis=0, keepdims=True) for s in ss]
        ps = [jnp.exp2(s - mt) for s, mt in zip(ss, mts)]
        lts = [jnp.sum(p, axis=0, keepdims=True) for p in ps]
        pvs = [jnp.dot(vts[t], p.astype(BF16), preferred_element_type=F32) for (t, _), p in zip(chains, ps)]
        for gs in groups:
            m, l, acc = m_sc[:, gs], l_sc[:, gs], acc_sc[:, gs]
            for idx, (t, cgs) in enumerate(chains):
                if cgs is not gs:
                    continue
                m_new = jnp.maximum(m, mts[idx])
                a = jnp.exp2(m - m_new)
                b = jnp.exp2(mts[idx] - m_new)
                l = a * l + b * lts[idx]
                acc = a * acc + b * pvs[idx]
                m = m_new
            m_sc[:, gs], l_sc[:, gs], acc_sc[:, gs] = m, l, acc

    def body(j, carry):
        process([(2 * j, False), (2 * j + 1, False)])
        return carry

    n_full = n_tiles - 1
    lax.fori_loop(0, n_full // 2, body, 0)

    @pl.when(n_full % 2 == 0)
    def _():
        process([(n_tiles - 1, True)])

    @pl.when(n_full % 2 == 1)
    def _():
        process([(n_tiles - 2, False), (n_tiles - 1, True)])

    o_lat_t = (acc_sc[...] / l_sc[...]).astype(BF16)
    outs = []
    for h in range(MLA_HEADS):
        outs.append(lax.dot_general(o_lat_t[:, h * Q_BLOCK:(h + 1) * Q_BLOCK], wuv_ref[h], _TN,
                                    preferred_element_type=F32))
    o_ref[0] = jnp.concatenate(outs, axis=1).astype(BF16)


def _attn_prompt(qt, kcat, vt, wuv, b, l):
    nq = l // Q_BLOCK
    nkt = l // KEY_TILE
    cols = MLA_HEADS * Q_BLOCK
    return pl.pallas_call(
        _attn_prompt_kernel,
        grid=(b, nq),
        in_specs=[pl.BlockSpec((1, QCAT, cols), lambda bb, i: (bb * nq + i, 0, 0)),
                  pl.BlockSpec((1, l, QCAT), lambda bb, i: (bb, 0, 0)),
                  pl.BlockSpec((nkt, KV_LORA, KEY_TILE), lambda bb, i: (bb, 0, 0)),
                  _full(wuv.shape)],
        out_specs=pl.BlockSpec((1, Q_BLOCK, MLA_HEADS * V_DIM), lambda bb, i: (bb, i, 0)),
        out_shape=jax.ShapeDtypeStruct((b, l, MLA_HEADS * V_DIM), BF16),
        scratch_shapes=[pltpu.VMEM((1, cols), F32), pltpu.VMEM((1, cols), F32),
                        pltpu.VMEM((KV_LORA, cols), F32)],
        compiler_params=_params(("parallel", "arbitrary")),
        name="attn_prompt",
    )(qt, kcat.reshape(b, l, QCAT), vt, wuv)


def _attn_sample_kernel(q_ref, cckv_ref, ckr_ref, kn_ref, wuv_ref, o_ref, *, ls):
    q = q_ref[...].reshape(MLA_HEADS * ls, QCAT)
    ck = cckv_ref[0].astype(BF16)
    kr = ckr_ref[0].astype(BF16)
    kn = kn_ref[...]
    s_c = (lax.dot_general(q[:, :KV_LORA], ck, _NT, preferred_element_type=F32)
           + lax.dot_general(q[:, KV_LORA:KV_LORA + ROPE_DIM], kr, _NT, preferred_element_type=F32))
    s_n = lax.dot_general(q, kn, _NT, preferred_element_type=F32)
    m = jnp.maximum(jnp.max(s_c, axis=1, keepdims=True), jnp.max(s_n, axis=1, keepdims=True))
    p_c = jnp.exp2(s_c - m)
    p_n = jnp.exp2(s_n - m)
    den = jnp.sum(p_c, axis=1, keepdims=True) + jnp.sum(p_n, axis=1, keepdims=True)
    o_lat = (jnp.dot(p_c.astype(BF16), ck, preferred_element_type=F32)
             + jnp.dot(p_n.astype(BF16), kn[:, :KV_LORA], preferred_element_type=F32)) / den
    o_ref[0] = _head_out(o_lat, wuv_ref, ls).astype(BF16)


def _attn_sample(qcat, kcat, cache_ckv, cache_krope, wuv, b, ls):
    past = cache_ckv.shape[1]
    assert past % CHUNK == 0 and ls <= CHUNK
    return pl.pallas_call(
        functools.partial(_attn_sample_kernel, ls=ls),
        grid=(b,),
        in_specs=[pl.BlockSpec((MLA_HEADS, ls, QCAT), lambda bb: (0, bb, 0)),
                  pl.BlockSpec((1, past, KV_LORA), lambda bb: (bb, 0, 0)),
                  pl.BlockSpec((1, past, ROPE_DIM), lambda bb: (bb, 0, 0)),
                  pl.BlockSpec((ls, QCAT), lambda bb: (bb, 0)),
                  _full(wuv.shape)],
        out_specs=pl.BlockSpec((1, ls, MLA_HEADS * V_DIM), lambda bb: (bb, 0, 0)),
        out_shape=jax.ShapeDtypeStruct((b, ls, MLA_HEADS * V_DIM), BF16),
        compiler_params=_params(("parallel",)),
        name="attn_sample",
    )(qcat, cache_ckv, cache_krope, kcat, wuv)


def _cumsum_rows(x):
    c = x.shape[0]
    row = lax.broadcasted_iota(jnp.int32, x.shape, 0)
    s = 1
    while s < c:
        x = x + jnp.where(row >= s, pltpu.roll(x, s, 0), 0.0)
        s *= 2
    return x


def _gla_kernel(q_ref, k_ref, v_ref, la_ref, sgr_ref, gn_ref, s0_ref, o_ref, sout_ref, st_sc, *, c):
    j = pl.program_id(1)

    @pl.when(j == 0)
    def _():
        st_sc[...] = s0_ref[...]

    half = 32
    row = lax.broadcasted_iota(jnp.int32, (c, GLA_DK), 0)
    rr = lax.broadcasted_iota(jnp.int32, (c, c), 0)
    cc = lax.broadcasted_iota(jnp.int32, (c, c), 1)
    causal = cc <= rr
    gn = gn_ref[...]
    for s, h in [(s, h) for s in range(GLA_BATCH) for h in range(GLA_HEADS)]:
        ks = slice(h * GLA_DK, (h + 1) * GLA_DK)
        vs = slice(h * GLA_DV, (h + 1) * GLA_DV)
        q = q_ref[s, :, ks].astype(F32)
        k = k_ref[s, :, ks].astype(F32)
        v = v_ref[s, :, vs]
        b = _cumsum_rows(la_ref[s, :, ks])
        if c > half:
            mid = jnp.where(row < half, b[half // 2 - 1:half // 2, :], b[half + half // 2 - 1:half + half // 2, :])
        else:
            mid = jnp.broadcast_to(b[c // 2 - 1:c // 2, :], b.shape)
        qe = (q * jnp.exp(b - mid)).astype(BF16)
        ke = (k * jnp.exp(mid - b)).astype(BF16)
        att = lax.dot_general(qe, ke, _NT, preferred_element_type=F32)
        if c > half:
            edge = b[half - 1:half, :]
            qo = (q * jnp.exp(jnp.minimum(b - edge, 0.0))).astype(BF16)
            ko = (k * jnp.exp(jnp.minimum(edge - b, 0.0))).astype(BF16)
            att_off = lax.dot_general(qo, ko, _NT, preferred_element_type=F32)
            same = (rr < half) == (cc < half)
            att = jnp.where(same, att, att_off)
        att = jnp.where(causal, att, 0.0).astype(BF16)
        st = st_sc[s, h]
        q0 = (q * jnp.exp(b)).astype(BF16)
        o = (jnp.dot(q0, st.astype(BF16), preferred_element_type=F32)
             + jnp.dot(att, v, preferred_element_type=F32))
        last = b[c - 1:c, :]
        kd = (k * jnp.exp(last - b)).astype(BF16)
        upd = lax.dot_general(kd, v, _TN, preferred_element_type=F32)
        dcol = jnp.exp(jnp.transpose(jnp.broadcast_to(last, (GLA_DK, GLA_DK))))
        st_sc[s, h] = jnp.concatenate([dcol, dcol], axis=1) * st + upd
        on = _rms(o, gn) * sgr_ref[s, :, vs].astype(F32)
        o_ref[s, :, vs] = on.astype(BF16)

    @pl.when(j == pl.num_programs(1) - 1)
    def _():
        sout_ref[...] = st_sc[...]


def _gla(gq, gk, gv, la, sgr, gn, s0, b, l):
    c = min(GLA_CHUNK, l)
    nc = l // c
    nqk = GLA_HEADS * GLA_DK
    nv = GLA_HEADS * GLA_DV
    r3 = lambda a: a.reshape(b, l, a.shape[-1])
    assert b % GLA_BATCH == 0
    tok = lambda width: pl.BlockSpec((GLA_BATCH, c, width), lambda bb, j: (bb, j, 0))
    st = pl.BlockSpec((GLA_BATCH, GLA_HEADS, GLA_DK, GLA_DV), lambda bb, j: (bb, 0, 0, 0))
    return pl.pallas_call(
        functools.partial(_gla_kernel, c=c),
        grid=(b // GLA_BATCH, nc),
        in_specs=[tok(nqk), tok(nqk), tok(nv), tok(nqk), tok(nv), _full(gn.shape), st],
        out_specs=[tok(nv), st],
        out_shape=[jax.ShapeDtypeStruct((b, l, nv), BF16),
                   jax.ShapeDtypeStruct((b, GLA_HEADS, GLA_DK, GLA_DV), F32)],
        scratch_shapes=[pltpu.VMEM((GLA_BATCH, GLA_HEADS, GLA_DK, GLA_DV), F32)],
        compiler_params=_params(("parallel", "arbitrary")),
        name="gla",
    )(r3(gq), r3(gk), r3(gv), r3(la), r3(sgr), gn, s0)


def _merge_kernel(oa_ref, ob_ref, sga_ref, sgb_ref, x_ref, wa_ref, wb_ref, wo_ref, gffn_ref,
                  wr_ref, br_ref, x1_ref, h2_ref, route_ref, gcol_ref, cnt_ref, run_sc):
    ya = jnp.dot(oa_ref[...], wa_ref[...], preferred_element_type=F32)
    yb = jnp.dot(ob_ref[...], wb_ref[...], preferred_element_type=F32)
    merged = (sga_ref[...].astype(F32) * ya + sgb_ref[...].astype(F32) * yb).astype(BF16)
    x1 = x_ref[...] + jnp.dot(merged, wo_ref[...], preferred_element_type=F32)
    x1_ref[...] = x1
    h2 = _rms(x1, gffn_ref[...])
    _store_token_major(h2_ref, h2)

    hi = h2.astype(BF16)
    lo = (h2 - hi.astype(F32)).astype(BF16)
    wr = wr_ref[...]
    whi = wr.astype(BF16)
    wlo = (wr - whi.astype(F32)).astype(BF16)
    logits = (lax.dot_general(whi, hi, _NT, preferred_element_type=F32)
              + lax.dot_general(whi, lo, _NT, preferred_element_type=F32)
              + lax.dot_general(wlo, hi, _NT, preferred_element_type=F32)) + br_ref[...]
    tm = logits.shape[1]
    ridx = lax.broadcasted_iota(jnp.int32, (EXPERTS_PER_GROUP, tm), 0)
    big = jnp.int32(1 << 20)

    def top(vals):
        vmax = jnp.max(vals, axis=0, keepdims=True)
        imax = jnp.min(jnp.where(vals == vmax, ridx, big), axis=0, keepdims=True)
        return vmax, imax

    gl = jnp.where(ridx < N_GROUPS, logits[N_EXPERTS:N_EXPERTS + EXPERTS_PER_GROUP], -jnp.inf)
    gmax, g_top = top(gl)
    p_top = 1.0 / jnp.sum(jnp.exp(gl - gmax), axis=0, keepdims=True)
    e_sel = logits[:EXPERTS_PER_GROUP]
    for g in range(1, N_GROUPS):
        e_sel = jnp.where(g_top == g, logits[g * EXPERTS_PER_GROUP:(g + 1) * EXPERTS_PER_GROUP], e_sel)
    v1, i1 = top(e_sel)
    v2, i2 = top(jnp.where(ridx == i1, -jnp.inf, e_sel))
    e21 = jnp.exp(v2 - v1)
    w1 = p_top / (1.0 + e21)
    w2 = p_top * e21 / (1.0 + e21)
    base = g_top * EXPERTS_PER_GROUP
    id0 = base + i1
    id1 = base + i2
    gates = jnp.where(ridx == 0, w1, jnp.where(ridx == 1, w2, 0.0))
    gcol_ref[...] = jnp.transpose(jnp.concatenate([gates, jnp.zeros((LANES - EXPERTS_PER_GROUP, tm), F32)], axis=0))

    @pl.when(pl.program_id(0) == 0)
    def _():
        run_sc[...] = jnp.zeros(run_sc.shape, F32)

    eidx = lax.broadcasted_iota(jnp.int32, (N_EXPERTS, tm), 0)
    oh0 = jnp.where(eidx == id0, 1.0, 0.0)
    oh1 = jnp.where(eidx == id1, 1.0, 0.0)
    earlier = jnp.where(lax.broadcasted_iota(jnp.int32, (tm, tm), 0) < lax.broadcasted_iota(jnp.int32, (tm, tm), 1),
                        1.0, 0.0).astype(BF16)
    p0 = jnp.dot(oh0.astype(BF16), earlier, preferred_element_type=F32)
    p1 = jnp.dot(oh1.astype(BF16), earlier, preferred_element_type=F32)
    c0 = jnp.sum(oh0, axis=1, keepdims=True)
    c1 = jnp.sum(oh1, axis=1, keepdims=True)
    run = run_sc[...]
    rank0 = jnp.sum(oh0 * (run + p0), axis=0, keepdims=True)
    rank1 = jnp.sum(oh1 * (run + c0 + p1), axis=0, keepdims=True)
    run = run + c0 + c1
    run_sc[...] = run
    cnt_ref[...] = jnp.broadcast_to(run, cnt_ref.shape)
    route_ref[...] = jnp.where(ridx == 0, id0, jnp.where(ridx == 1, id1, jnp.where(
        ridx == 2, rank0.astype(jnp.int32), jnp.where(ridx == 3, rank1.astype(jnp.int32), 0))))


def _merge(oa, ob, sga, sgb, x2, w):
    n = x2.shape[0]
    tm = min(MERGE_TILE, n)
    assert n % tm == 0
    row = lambda width: pl.BlockSpec((tm, width), lambda i: (i, 0))
    weights = [w["w_a"], w["w_b"], w["w_o"], w["g_ffn"], w["w_router"], w["b_router"]]
    return pl.pallas_call(
        _merge_kernel,
        grid=(n // tm,),
        in_specs=[row(MLA_HEADS * V_DIM), row(GLA_HEADS * GLA_DV), row(D_MODEL), row(D_MODEL),
                  row(D_MODEL)] + [_full(a.shape) for a in weights],
        out_specs=[row(D_MODEL), pl.BlockSpec((tm * SLABS, LANES), lambda i: (i, 0)),
                   pl.BlockSpec((EXPERTS_PER_GROUP, tm), lambda i: (0, i)), row(LANES),
                   _full((N_EXPERTS, LANES))],
        out_shape=[jax.ShapeDtypeStruct((n, D_MODEL), F32), jax.ShapeDtypeStruct((n * SLABS, LANES), F32),
                   jax.ShapeDtypeStruct((EXPERTS_PER_GROUP, n), jnp.int32),
                   jax.ShapeDtypeStruct((n, LANES), F32),
                   jax.ShapeDtypeStruct((N_EXPERTS, LANES), F32)],
        scratch_shapes=[pltpu.VMEM((N_EXPERTS, 1), F32)],
        compiler_params=_params(("arbitrary",)),
        name="merge",
    )(oa, ob, sga, sgb, x2, *weights)


ROW_DMA_UNROLL = 8


def _row_tile(ref, r):
    return ref.at[pl.ds(pl.multiple_of(r * SLABS, SLABS), SLABS), :]


def _scatter_kernel(ps_ref, cnt_ref, pe_ref, dest_ref, h2_ref, xd_hbm, zero_sc, sem, zsem):
    tm = h2_ref.shape[0] // SLABS

    @pl.when(pl.program_id(0) == 0)
    def _():
        zero_sc[...] = jnp.zeros(zero_sc.shape, F32)

        def pad_copies(e, act):
            lo = ps_ref[e] + cnt_ref[e]
            pad = pe_ref[e] - lo
            bit = FFN_ROWS // 2
            while bit >= 1:
                @pl.when((pad & bit) != 0)
                def _(bit=bit):
                    first = lo + (pad & ~(2 * bit - 1))
                    act(pltpu.make_async_copy(
                        zero_sc.at[pl.ds(0, bit * SLABS), :],
                        xd_hbm.at[pl.ds(pl.multiple_of(first * SLABS, SLABS), bit * SLABS), :], zsem.at[0]))
                bit //= 2

        def start_pads(e, c):
            pad_copies(e, lambda cp: cp.start())
            return c

        def wait_pads(e, c):
            pad_copies(e, lambda cp: cp.wait())
            return c

        lax.fori_loop(0, N_EXPERTS, start_pads, 0)
        lax.fori_loop(0, N_EXPERTS, wait_pads, 0)

        def block_copy(b):
            start_row = pl.multiple_of(b * (FFN_ROWS * SLABS), FFN_ROWS * SLABS)
            return pltpu.make_async_copy(zero_sc, xd_hbm.at[pl.ds(start_row, FFN_ROWS * SLABS), :], zsem.at[0])

        first_unused = pe_ref[N_EXPERTS - 1] // FFN_ROWS
        n_blocks = xd_hbm.shape[0] // (FFN_ROWS * SLABS)

        def start_block(b, c):
            block_copy(b).start()
            return c

        def wait_block(b, c):
            block_copy(b).wait()
            return c

        lax.fori_loop(first_unused, n_blocks, start_block, 0)
        lax.fori_loop(first_unused, n_blocks, wait_block, 0)

    def body(t, carry):
        for k in range(2):
            d = dest_ref[0, 0, k * tm + t]
            pltpu.make_async_copy(_row_tile(h2_ref, t), _row_tile(xd_hbm, d), sem.at[0]).start()
        return carry

    lax.fori_loop(0, tm, body, 0, unroll=ROW_DMA_UNROLL)
    for k in range(2):
        pltpu.make_async_copy(h2_ref, xd_hbm.at[pl.ds(0, tm * SLABS), :], sem.at[0]).wait()


def _scatter(pad_start, counts, pad_end, dest_blocks, h2, total_rows):
    tm = ROW_TILE
    nt = dest_blocks.shape[0]
    grid_spec = pltpu.PrefetchScalarGridSpec(
        num_scalar_prefetch=3,
        grid=(nt,),
        in_specs=[pl.BlockSpec((1, 1, 2 * tm), lambda i, *_: (i, 0, 0), memory_space=pltpu.SMEM),
                  pl.BlockSpec((tm * SLABS, LANES), lambda i, *_: (i, 0))],
        out_specs=pl.BlockSpec(memory_space=pl.ANY),
        scratch_shapes=[pltpu.VMEM((FFN_ROWS * SLABS, LANES), F32), pltpu.SemaphoreType.DMA((1,)),
                        pltpu.SemaphoreType.DMA((1,))],
    )
    return pl.pallas_call(
        _scatter_kernel,
        grid_spec=grid_spec,
        out_shape=jax.ShapeDtypeStruct((total_rows * SLABS, LANES), F32),
        compiler_params=_params(("arbitrary",)),
        name="moe_scatter",
    )(pad_start, counts, pad_end, dest_blocks, h2)


def _ffn_kernel(be_ref, nbu_ref, x_ref, wg_ref, wu_ref, wd_ref, out_ref):
    del be_ref

    @pl.when(pl.program_id(0) < nbu_ref[0])
    def _():
        x = _load_token_major(x_ref, FFN_ROWS).astype(BF16)
        g = jnp.dot(x, wg_ref[0], preferred_element_type=F32)
        u = jnp.dot(x, wu_ref[0], preferred_element_type=F32)
        mid = (g * _sigmoid(g) * u).astype(BF16)
        _store_token_major(out_ref, jnp.dot(mid, wd_ref[0], preferred_element_type=F32))

    @pl.when(pl.program_id(0) >= nbu_ref[0])
    def _():
        out_ref[...] = jnp.zeros(out_ref.shape, F32)


def _ffn(block_expert, nb_used, xd, wg, wu, wd):
    nb = block_expert.shape[0]
    blk = lambda i, be, nbu: (jnp.minimum(i, nbu[0] - 1), 0)
    wsel = lambda i, be, nbu: (be[jnp.minimum(i, nbu[0] - 1)], 0, 0)
    grid_spec = pltpu.PrefetchScalarGridSpec(
        num_scalar_prefetch=2,
        grid=(nb,),
        in_specs=[pl.BlockSpec((FFN_ROWS * SLABS, LANES), blk),
                  pl.BlockSpec((1, D_MODEL, EXPERT_FF), wsel),
                  pl.BlockSpec((1, D_MODEL, EXPERT_FF), wsel),
                  pl.BlockSpec((1, EXPERT_FF, D_MODEL), wsel)],
        out_specs=pl.BlockSpec((FFN_ROWS * SLABS, LANES), lambda i, be, nbu: (i, 0)),
    )
    return pl.pallas_call(
        _ffn_kernel,
        grid_spec=grid_spec,
        out_shape=jax.ShapeDtypeStruct((nb * FFN_ROWS * SLABS, LANES), F32),
        compiler_params=_params(("arbitrary",)),
        name="expert_ffn",
    )(block_expert, nb_used, xd, wg, wu, wd)


def _combine_kernel(dest_ref, x1_ref, gcol_ref, gfin_ref, eo_hbm, y_ref, buf, sem):
    tm = x1_ref.shape[0]

    def body(t, carry):
        for k in range(2):
            d = dest_ref[0, 0, k * tm + t]
            pltpu.make_async_copy(_row_tile(eo_hbm, d), _row_tile(buf.at[k], t), sem.at[0]).start()
        return carry

    lax.fori_loop(0, tm, body, 0, unroll=ROW_DMA_UNROLL)
    for k in range(2):
        pltpu.make_async_copy(eo_hbm.at[pl.ds(0, tm * SLABS), :], buf.at[k], sem.at[0]).wait()
    g = gcol_ref[...]
    y = x1_ref[...] + (g[:, 0:1] * _load_token_major(buf.at[0], tm) + g[:, 1:2] * _load_token_major(buf.at[1], tm))
    y_ref[...] = _rms(y, gfin_ref[...])


def _combine(dest_blocks, x1, gcol, g_final, eo):
    n = x1.shape[0]
    tm = ROW_TILE
    return pl.pallas_call(
        _combine_kernel,
        grid=(n // tm,),
        in_specs=[pl.BlockSpec((1, 1, 2 * tm), lambda i: (i, 0, 0), memory_space=pltpu.SMEM),
                  pl.BlockSpec((tm, D_MODEL), lambda i: (i, 0)),
                  pl.BlockSpec((tm, LANES), lambda i: (i, 0)),
                  _full(g_final.shape),
                  pl.BlockSpec(memory_space=pl.ANY)],
        out_specs=pl.BlockSpec((tm, D_MODEL), lambda i: (i, 0)),
        out_shape=jax.ShapeDtypeStruct((n, D_MODEL), F32),
        scratch_shapes=[pltpu.VMEM((2, tm * SLABS, LANES), F32), pltpu.SemaphoreType.DMA((1,))],
        compiler_params=_params(("arbitrary",)),
        name="combine",
    )(dest_blocks, x1, gcol, g_final, eo)


def _layout(cnt, route, n):
    counts = cnt[:, 0].astype(jnp.int32)
    padded = (counts + FFN_ROWS - 1) // FFN_ROWS * FFN_ROWS
    pad_end = jnp.cumsum(padded).astype(jnp.int32)
    pad_start = pad_end - padded
    nb = 2 * n // FFN_ROWS + N_EXPERTS
    block_start = jnp.arange(nb, dtype=jnp.int32) * FFN_ROWS
    block_expert = jnp.minimum(jnp.sum((block_start[:, None] >= pad_end[None, :]).astype(jnp.int32), axis=1),
                               N_EXPERTS - 1).astype(jnp.int32)
    nb_used = (pad_end[N_EXPERTS - 1:] // FFN_ROWS).astype(jnp.int32)
    dest = pad_start[route[:2]] + route[2:4]
    tm = ROW_TILE
    dest_blocks = dest.reshape(2, n // tm, tm).transpose(1, 0, 2).reshape(n // tm, 1, 2 * tm)
    return pad_start, counts, pad_end, block_expert, nb_used, nb * FFN_ROWS, dest_blocks


def _rope_tables(l, past, rows):
    half = ROPE_DIM // 2
    inv = ROPE_THETA ** (-jnp.arange(half, dtype=F32) / half)
    pos = (past + (jnp.arange(rows, dtype=jnp.int32) % l)).astype(F32)
    ang = pos[:, None] * inv[None, :]
    pad = jnp.zeros((rows, LANES - ROPE_DIM), F32)
    cos = jnp.concatenate([jnp.cos(ang), jnp.cos(ang), pad], axis=1)
    sin = jnp.concatenate([-jnp.sin(ang), jnp.sin(ang), pad], axis=1)
    return cos, sin


def _group(x, ckv_past, kpe_past, s0, w, g_final):
    b, l, d = x.shape
    n = b * l
    x2 = x.reshape(n, d)
    past = 0 if ckv_past is None else ckv_past.shape[1]
    cos_t, sin_t = _rope_tables(l, past, max(l, ROW_TILE))
    prompt = ckv_past is None
    (qcat, kcat, ckv, kpe, gq, gk, gv, la, sgr, sga, sgb, *maybe_vt) = _inproj(x2, cos_t, sin_t, w, prompt)
    if prompt:
        oa = _attn_prompt(qcat, kcat, maybe_vt[0], w["w_uv"], b, l)
    else:
        oa = _attn_sample(qcat, kcat, ckv_past, kpe_past, w["w_uv"], b, l)
    ob, s_new = _gla(gq, gk, gv, la, sgr, w["g_gla"], s0, b, l)
    x1, h2, route, gcol, cnt = _merge(oa.reshape(n, -1), ob.reshape(n, -1), sga, sgb, x2, w)
    pad_start, counts, pad_end, block_expert, nb_used, total_rows, dest_blocks = _layout(cnt, route, n)
    xd = _scatter(pad_start, counts, pad_end, dest_blocks, h2, total_rows)
    eo = _ffn(block_expert, nb_used, xd, w["w_eg"], w["w_eu"], w["w_ed"])
    y = _combine(dest_blocks, x1, gcol, g_final, eo)
    return (y.reshape(b, l, d), ckv.reshape(b, l, KV_LORA), kpe.reshape(b, l, ROPE_DIM), s_new)


def _prep_weights(w_in, g_norm_mix, g_qnorm, w_uq, g_kvnorm, w_ukv, w_gate2, b_gate2, g_gla_norm,
                  w_branch_a, w_branch_b, w_out, g_norm_ffn, w_router_group, b_router_group,
                  w_router_expert, b_router_expert, w_exp_gate, w_exp_up, w_exp_down):
    nqk = GLA_HEADS * GLA_DK
    nv = GLA_HEADS * GLA_DV
    o = 0
    parts = {}
    for name, width in (("cq", Q_LORA), ("ckv", KV_LORA), ("kpe", ROPE_DIM), ("gq", nqk), ("gk", nqk),
                        ("gv", nv), ("glr", GATE_RANK), ("gr", nv), ("ga", D_MODEL), ("gb", D_MODEL)):
        parts[name] = w_in[:, o:o + width]
        o += width
    padc = lambda a, width: jnp.pad(a, ((0, 0), (0, width - a.shape[1])))
    w_small = jnp.concatenate([parts["cq"], parts["ckv"], padc(parts["kpe"], LANES),
                               padc(parts["glr"], LANES)], axis=1).astype(BF16)
    uq = w_uq.reshape(Q_LORA, MLA_HEADS, NOPE_DIM + ROPE_DIM)
    ukv = w_ukv.reshape(KV_LORA, MLA_HEADS, NOPE_DIM + V_DIM)
    lat = _fold_q(uq[:, :, :NOPE_DIM].transpose(1, 0, 2), ukv[:, :, :NOPE_DIM].transpose(1, 0, 2))
    q_rope = uq[:, :, NOPE_DIM:].transpose(1, 0, 2)
    w_qcat = jnp.concatenate([lat, q_rope, jnp.zeros((MLA_HEADS, Q_LORA, QCAT - KV_LORA - ROPE_DIM), F32)],
                             axis=2)
    w_qcat = w_qcat.transpose(1, 0, 2).reshape(Q_LORA, MLA_HEADS * QCAT).astype(BF16)
    w_router = jnp.concatenate([w_router_expert.T, w_router_group.T,
                                jnp.zeros((ROUTER_ROWS - N_EXPERTS - N_GROUPS, D_MODEL), F32)], axis=0)
    b_router = jnp.concatenate([b_router_expert, b_router_group,
                                jnp.zeros((ROUTER_ROWS - N_EXPERTS - N_GROUPS,), F32)]).reshape(ROUTER_ROWS, 1)
    return {
        "g_mix": g_norm_mix.reshape(1, D_MODEL), "w_small": w_small,
        "g_qn": g_qnorm.reshape(1, Q_LORA), "g_kvn": g_kvnorm.reshape(1, KV_LORA), "w_qcat": w_qcat,
        "w_g2": jnp.pad(w_gate2, ((0, LANES - GATE_RANK), (0, 0))).astype(BF16),
        "b_g2": b_gate2.reshape(1, nqk),
        "w_gqk": jnp.concatenate([parts["gq"], parts["gk"]], axis=1).astype(BF16),
        "w_gv": parts["gv"].astype(BF16), "w_gr": parts["gr"].astype(BF16),
        "w_ga": parts["ga"].astype(BF16), "w_gb": parts["gb"].astype(BF16),
        "w_uv": ukv[:, :, NOPE_DIM:].transpose(1, 0, 2).astype(BF16),
        "g_gla": g_gla_norm.reshape(1, GLA_DV),
        "w_a": w_branch_a.astype(BF16), "w_b": w_branch_b.astype(BF16), "w_o": w_out.astype(BF16),
        "g_ffn": g_norm_ffn.reshape(1, D_MODEL), "w_router": w_router, "b_router": b_router,
        "w_eg": w_exp_gate.astype(BF16), "w_eu": w_exp_up.astype(BF16), "w_ed": w_exp_down.astype(BF16),
    }


def kernel(x_prompt, x_sample, cache_ckv, cache_krope, state_gla, w_in, g_norm_mix, g_qnorm, w_uq, g_kvnorm, w_ukv, w_gate2, b_gate2, g_gla_norm, w_branch_a, w_branch_b, w_out, g_norm_ffn, w_router_group, b_router_group, w_router_expert, b_router_expert, w_exp_gate, w_exp_up, w_exp_down, g_norm_final):
    depth = w_in.shape[0]
    assert depth == 1, "the final norm is fused into the last layer's combine step"
    gfin = g_norm_final.reshape(1, D_MODEL)
    w = _prep_weights(w_in[0], g_norm_mix[0], g_qnorm[0], w_uq[0], g_kvnorm[0], w_ukv[0], w_gate2[0],
                      b_gate2[0], g_gla_norm[0], w_branch_a[0], w_branch_b[0], w_out[0], g_norm_ffn[0],
                      w_router_group[0], b_router_group[0], w_router_expert[0], b_router_expert[0],
                      w_exp_gate[0], w_exp_up[0], w_exp_down[0])
    bp = x_prompt.shape[0]
    zero_state = jnp.zeros((bp, GLA_HEADS, GLA_DK, GLA_DV), F32)
    yp, c1, k1, s1 = _group(x_prompt, None, None, zero_state, w, gfin)
    ys, c2, k2, s2 = _group(x_sample, cache_ckv[0], cache_krope[0], state_gla[0].astype(F32), w, gfin)
    return (yp, ys, c1[None], k1[None], s1[None], c2[None], k2[None], s2[None])
```

```python
import functools
import math

import jax
import jax.numpy as jnp
from jax import lax
from jax.experimental import pallas as pl
from jax.experimental.pallas import tpu as pltpu

F32 = jnp.float32
BF16 = jnp.bfloat16

D_MODEL = 1024
CHUNK = 64
EPS = 1e-6
MLA_HEADS = 8
Q_LORA = 256
KV_LORA = 128
NOPE_DIM = 64
ROPE_DIM = 32
V_DIM = 64
ROPE_THETA = 10000.0
GLA_HEADS = 4
GLA_DK = 128
GLA_DV = 256
GATE_RANK = 16
GATE_TEMP = 16.0
N_GROUPS = 4
EXPERTS_PER_GROUP = 8
N_EXPERTS = 32
EXPERT_FF = 512

LANES = 128
QCAT = 2 * LANES
ROW_TILE = 256
MERGE_TILE = 512
GLA_BATCH = 2
Q_BLOCK = 128
KEY_TILE = ROW_TILE
ATT_LANE_GROUP = 256
GLA_CHUNK = 64
FFN_ROWS = 256
ROUTER_ROWS = 48
VMEM_LIMIT = 56 * 1024 * 1024
LOG2E = 1.4426950408889634
ATT_SCALE = LOG2E / math.sqrt(NOPE_DIM + ROPE_DIM)

_NT = (((1,), (1,)), ((), ()))
_TN = (((0,), (0,)), ((), ()))


def _params(sem):
    return pltpu.CompilerParams(dimension_semantics=sem, vmem_limit_bytes=VMEM_LIMIT)


def _rms(x, g):
    return x * lax.rsqrt(jnp.mean(x * x, axis=-1, keepdims=True) + EPS) * g


def _sigmoid(x):
    return 1.0 / (1.0 + jnp.exp(-x))


def _full(shape):
    n = len(shape)
    return pl.BlockSpec(shape, lambda *_: (0,) * n)


SLABS = D_MODEL // LANES


def _store_token_major(ref, x):
    rows = x.shape[0]
    for j in range(SLABS):
        ref[pl.ds(j, rows, stride=SLABS), :] = x[:, j * LANES:(j + 1) * LANES]


def _load_token_major(ref, rows):
    return jnp.concatenate([ref[pl.ds(j, rows, stride=SLABS), :] for j in range(SLABS)], axis=1)


def _fold_q_kernel(wq_ref, wk_ref, o_ref):
    o_ref[0] = lax.dot_general(wq_ref[0], wk_ref[0], _NT, precision=lax.Precision.HIGHEST,
                               preferred_element_type=F32)


def _fold_q(wq_nope, wk_nope):
    return pl.pallas_call(
        _fold_q_kernel,
        grid=(MLA_HEADS,),
        in_specs=[pl.BlockSpec((1, Q_LORA, NOPE_DIM), lambda h: (h, 0, 0)),
                  pl.BlockSpec((1, KV_LORA, NOPE_DIM), lambda h: (h, 0, 0))],
        out_specs=pl.BlockSpec((1, Q_LORA, KV_LORA), lambda h: (h, 0, 0)),
        out_shape=jax.ShapeDtypeStruct((MLA_HEADS, Q_LORA, KV_LORA), F32),
        compiler_params=_params(("arbitrary",)),
        name="fold_q",
    )(wq_nope, wk_nope)


def _inproj_kernel(x_ref, cos_ref, sin_ref, gmix_ref, wsm_ref, gqn_ref, gkvn_ref, wqc_ref,
                   wg2_ref, bg2_ref, wgqk_ref, wgv_ref, wgr_ref, wga_ref, wgb_ref,
                   qcat_ref, kcat_ref, ckv_ref, kpe_ref, gq_ref, gk_ref, gv_ref, la_ref,
                   sgr_ref, sga_ref, sgb_ref, *maybe_vt_ref, transposed):
    hb = _rms(x_ref[...], gmix_ref[...]).astype(BF16)
    zs = jnp.dot(hb, wsm_ref[...], preferred_element_type=F32)
    cos = cos_ref[...]
    sin = sin_ref[...]
    first_half = lax.broadcasted_iota(jnp.int32, cos.shape, 1) < ROPE_DIM // 2

    def rope(v):
        rot = jnp.where(first_half, pltpu.roll(v, LANES - ROPE_DIM // 2, 1),
                        pltpu.roll(v, ROPE_DIM // 2, 1))
        return v * cos + rot * sin

    cqn = _rms(zs[:, :Q_LORA], gqn_ref[...]).astype(BF16)
    qc = jnp.dot(cqn, wqc_ref[...], preferred_element_type=F32)
    for h in range(MLA_HEADS):
        lat = qc[:, h * QCAT:h * QCAT + LANES]
        pe = qc[:, h * QCAT + LANES:(h + 1) * QCAT]
        if transposed:
            lat_t = jnp.transpose(lat * ATT_SCALE).astype(BF16)
            pe_t = jnp.transpose(rope(pe) * ATT_SCALE).astype(BF16)
            for blk in range(lat.shape[0] // Q_BLOCK):
                cols = slice(blk * Q_BLOCK, (blk + 1) * Q_BLOCK)
                qcat_ref[blk, :LANES, h * Q_BLOCK:(h + 1) * Q_BLOCK] = lat_t[:, cols]
                qcat_ref[blk, LANES:, h * Q_BLOCK:(h + 1) * Q_BLOCK] = pe_t[:, cols]
        else:
            qcat_ref[h, :, :LANES] = (lat * ATT_SCALE).astype(BF16)
            qcat_ref[h, :, LANES:] = (rope(pe) * ATT_SCALE).astype(BF16)

    ckv = _rms(zs[:, Q_LORA:Q_LORA + KV_LORA], gkvn_ref[...])
    ckv_ref[...] = ckv
    if transposed:
        maybe_vt_ref[0][0] = jnp.transpose(ckv).astype(BF16)
    kpe = rope(zs[:, Q_LORA + KV_LORA:Q_LORA + KV_LORA + LANES])
    kpe_ref[...] = kpe[:, :ROPE_DIM]
    kcat_ref[:, :LANES] = ckv.astype(BF16)
    kcat_ref[:, LANES:] = kpe.astype(BF16)

    glr = zs[:, Q_LORA + KV_LORA + LANES:].astype(BF16)
    xg = jnp.dot(glr, wg2_ref[...], preferred_element_type=F32) + bg2_ref[...]
    la_ref[...] = (jnp.minimum(xg, 0.0) - jnp.log(1.0 + jnp.exp(-jnp.abs(xg)))) * (1.0 / GATE_TEMP)

    zqk = jnp.dot(hb, wgqk_ref[...], preferred_element_type=F32)
    nqk = GLA_HEADS * GLA_DK
    gq_ref[...] = (zqk[:, :nqk] * (GLA_DK ** -0.5)).astype(BF16)
    gk_ref[...] = zqk[:, nqk:].astype(BF16)
    gv_ref[...] = jnp.dot(hb, wgv_ref[...], preferred_element_type=F32).astype(BF16)
    gr = jnp.dot(hb, wgr_ref[...], preferred_element_type=F32)
    sgr_ref[...] = (gr * _sigmoid(gr)).astype(BF16)
    sga_ref[...] = _sigmoid(jnp.dot(hb, wga_ref[...], preferred_element_type=F32)).astype(BF16)
    sgb_ref[...] = _sigmoid(jnp.dot(hb, wgb_ref[...], preferred_element_type=F32)).astype(BF16)


def _inproj(x2, cos_t, sin_t, w, transposed):
    n = x2.shape[0]
    tm = ROW_TILE
    nt = n // tm
    tab_blocks = cos_t.shape[0] // tm
    row = lambda width: pl.BlockSpec((tm, width), lambda i: (i, 0))
    tab = pl.BlockSpec((tm, LANES), lambda i: (i % tab_blocks, 0))
    nqk = GLA_HEADS * GLA_DK
    nv = GLA_HEADS * GLA_DV
    weights = [w["g_mix"], w["w_small"], w["g_qn"], w["g_kvn"], w["w_qcat"], w["w_g2"], w["b_g2"],
               w["w_gqk"], w["w_gv"], w["w_gr"], w["w_ga"], w["w_gb"]]
    out_shape = [
        jax.ShapeDtypeStruct((MLA_HEADS, n, QCAT), BF16),
        jax.ShapeDtypeStruct((n, QCAT), BF16),
        jax.ShapeDtypeStruct((n, KV_LORA), F32),
        jax.ShapeDtypeStruct((n, ROPE_DIM), F32),
        jax.ShapeDtypeStruct((n, nqk), BF16),
        jax.ShapeDtypeStruct((n, nqk), BF16),
        jax.ShapeDtypeStruct((n, nv), BF16),
        jax.ShapeDtypeStruct((n, nqk), F32),
        jax.ShapeDtypeStruct((n, nv), BF16),
        jax.ShapeDtypeStruct((n, D_MODEL), BF16),
        jax.ShapeDtypeStruct((n, D_MODEL), BF16),
    ]
    out_specs = [
        pl.BlockSpec((MLA_HEADS, tm, QCAT), lambda i: (0, i, 0)),
        row(QCAT), row(KV_LORA), row(ROPE_DIM), row(nqk), row(nqk), row(nv), row(nqk), row(nv),
        row(D_MODEL), row(D_MODEL),
    ]
    if transposed:
        qb = tm // Q_BLOCK
        out_shape[0] = jax.ShapeDtypeStruct((n // Q_BLOCK, QCAT, MLA_HEADS * Q_BLOCK), BF16)
        out_specs[0] = pl.BlockSpec((qb, QCAT, MLA_HEADS * Q_BLOCK), lambda i: (i, 0, 0))
        out_shape.append(jax.ShapeDtypeStruct((nt, KV_LORA, tm), BF16))
        out_specs.append(pl.BlockSpec((1, KV_LORA, tm), lambda i: (i, 0, 0)))
    return pl.pallas_call(
        functools.partial(_inproj_kernel, transposed=transposed),
        grid=(nt,),
        in_specs=[row(D_MODEL), tab, tab] + [_full(a.shape) for a in weights],
        out_specs=out_specs,
        out_shape=out_shape,
        compiler_params=_params(("parallel",)),
        name="inproj",
    )(x2, cos_t, sin_t, *weights)


def _head_out(o_lat, wuv_ref, rows):
    outs = []
    for h in range(MLA_HEADS):
        oh = o_lat[h * rows:(h + 1) * rows].astype(BF16)
        outs.append(jnp.dot(oh, wuv_ref[h], preferred_element_type=F32))
    return jnp.concatenate(outs, axis=1)


def _attn_prompt_kernel(qt_ref, k_ref, vt_ref, wuv_ref, o_ref, m_sc, l_sc, acc_sc):
    i = pl.program_id(1)
    cols = MLA_HEADS * Q_BLOCK
    m_sc[...] = jnp.full(m_sc.shape, -jnp.inf, F32)
    l_sc[...] = jnp.zeros(l_sc.shape, F32)
    acc_sc[...] = jnp.zeros(acc_sc.shape, F32)
    n_tiles = (i * Q_BLOCK + Q_BLOCK + KEY_TILE - 1) // KEY_TILE
    groups = [slice(g * ATT_LANE_GROUP, (g + 1) * ATT_LANE_GROUP) for g in range(cols // ATT_LANE_GROUP)]

    def process(tiles):
        kbs = [k_ref[0, pl.ds(pl.multiple_of(kt * KEY_TILE, KEY_TILE), KEY_TILE), :] for kt, _ in tiles]
        vts = [vt_ref[kt] for kt, _ in tiles]
        chains = [(t, gs) for t in range(len(tiles)) for gs in groups]
        ss = []
        for t, gs in chains:
            s = jnp.dot(kbs[t], qt_ref[0, :, gs], preferred_element_type=F32)
            if tiles[t][1]:
                r = lax.broadcasted_iota(jnp.int32, (KEY_TILE, 1), 0)
                k_chunk = (tiles[t][0] * KEY_TILE + r) // CHUNK
                c = lax.broadcasted_iota(jnp.int32, (1, ATT_LANE_GROUP), 1)
                q_chunk = (i * Q_BLOCK + (c & (Q_BLOCK - 1))) // CHUNK
                s = jnp.where(k_chunk <= q_chunk, s, -jnp.inf)
            ss.append(s)
        mts = [jnp.max(s, axis=0, keepdims=True) for s in ss]
        ps = [jnp.exp2(s - mt) for s, mt in zip(ss, mts)]
        lts = [jnp.sum(p, axis=0, keepdims=True) for p in ps]
        pvs = [jnp.dot(vts[t], p.astype(BF16), preferred_element_type=F32) for (t, _), p in zip(chains, ps)]
        for gs in groups:
            m, l, acc = m_sc[:, gs], l_sc[:, gs], acc_sc[:, gs]
            for idx, (t, cgs) in enumerate(chains):
                if cgs is not gs:
                    continue
                m_new = jnp.maximum(m, mts[idx])
                a = jnp.exp2(m - m_new)
                b = jnp.exp2(mts[idx] - m_new)
                l = a * l + b * lts[idx]
                acc = a * acc + b * pvs[idx]
                m = m_new
            m_sc[:, gs], l_sc[:, gs], acc_sc[:, gs] = m, l, acc

    def body(j, carry):
        process([(2 * j, False), (2 * j + 1, False)])
        return carry

    n_full = n_tiles - 1
    lax.fori_loop(0, n_full // 2, body, 0)

    @pl.when(n_full % 2 == 0)
    def _():
        process([(n_tiles - 1, True)])

    @pl.when(n_full % 2 == 1)
    def _():
        process([(n_tiles - 2, False), (n_tiles - 1, True)])

    o_lat_t = (acc_sc[...] / l_sc[...]).astype(BF16)
    outs = []
    for h in range(MLA_HEADS):
        outs.append(lax.dot_general(o_lat_t[:, h * Q_BLOCK:(h + 1) * Q_BLOCK], wuv_ref[h], _TN,
                                    preferred_element_type=F32))
    o_ref[0] = jnp.concatenate(outs, axis=1).astype(BF16)


def _attn_prompt(qt, kcat, vt, wuv, b, l):
    nq = l // Q_BLOCK
    nkt = l // KEY_TILE
    cols = MLA_HEADS * Q_BLOCK
    return pl.pallas_call(
        _attn_prompt_kernel,
        grid=(b, nq),
        in_specs=[pl.BlockSpec((1, QCAT, cols), lambda bb, i: (bb * nq + i, 0, 0)),
                  pl.BlockSpec((1, l, QCAT), lambda bb, i: (bb, 0, 0)),
                  pl.BlockSpec((nkt, KV_LORA, KEY_TILE), lambda bb, i: (bb, 0, 0)),
                  _full(wuv.shape)],
        out_specs=pl.BlockSpec((1, Q_BLOCK, MLA_HEADS * V_DIM), lambda bb, i: (bb, i, 0)),
        out_shape=jax.ShapeDtypeStruct((b, l, MLA_HEADS * V_DIM), BF16),
        scratch_shapes=[pltpu.VMEM((1, cols), F32), pltpu.VMEM((1, cols), F32),
                        pltpu.VMEM((KV_LORA, cols), F32)],
        compiler_params=_params(("parallel", "arbitrary")),
        name="attn_prompt",
    )(qt, kcat.reshape(b, l, QCAT), vt, wuv)


def _attn_sample_kernel(q_ref, cckv_ref, ckr_ref, kn_ref, wuv_ref, o_ref, *, ls):
    q = q_ref[...].reshape(MLA_HEADS * ls, QCAT)
    ck = cckv_ref[0].astype(BF16)
    kr = ckr_ref[0].astype(BF16)
    kn = kn_ref[...]
    s_c = (lax.dot_general(q[:, :KV_LORA], ck, _NT, preferred_element_type=F32)
           + lax.dot_general(q[:, KV_LORA:KV_LORA + ROPE_DIM], kr, _NT, preferred_element_type=F32))
    s_n = lax.dot_general(q, kn, _NT, preferred_element_type=F32)
    m = jnp.maximum(jnp.max(s_c, axis=1, keepdims=True), jnp.max(s_n, axis=1, keepdims=True))
    p_c = jnp.exp2(s_c - m)
    p_n = jnp.exp2(s_n - m)
    den = jnp.sum(p_c, axis=1, keepdims=True) + jnp.sum(p_n, axis=1, keepdims=True)
    o_lat = (jnp.dot(p_c.astype(BF16), ck, preferred_element_type=F32)
             + jnp.dot(p_n.astype(BF16), kn[:, :KV_LORA], preferred_element_type=F32)) / den
    o_ref[0] = _head_out(o_lat, wuv_ref, ls).astype(BF16)


def _attn_sample(qcat, kcat, cache_ckv, cache_krope, wuv, b, ls):
    past = cache_ckv.shape[1]
    assert past % CHUNK == 0 and ls <= CHUNK
    return pl.pallas_call(
        functools.partial(_attn_sample_kernel, ls=ls),
        grid=(b,),
        in_specs=[pl.BlockSpec((MLA_HEADS, ls, QCAT), lambda bb: (0, bb, 0)),
                  pl.BlockSpec((1, past, KV_LORA), lambda bb: (bb, 0, 0)),
                  pl.BlockSpec((1, past, ROPE_DIM), lambda bb: (bb, 0, 0)),
                  pl.BlockSpec((ls, QCAT), lambda bb: (bb, 0)),
                  _full(wuv.shape)],
        out_specs=pl.BlockSpec((1, ls, MLA_HEADS * V_DIM), lambda bb: (bb, 0, 0)),
        out_shape=jax.ShapeDtypeStruct((b, ls, MLA_HEADS * V_DIM), BF16),
        compiler_params=_params(("parallel",)),
        name="attn_sample",
    )(qcat, cache_ckv, cache_krope, kcat, wuv)


def _cumsum_rows(x):
    c = x.shape[0]
    row = lax.broadcasted_iota(jnp.int32, x.shape, 0)
    s = 1
    while s < c:
        x = x + jnp.where(row >= s, pltpu.roll(x, s, 0), 0.0)
        s *= 2
    return x


def _gla_kernel(q_ref, k_ref, v_ref, la_ref, sgr_ref, gn_ref, s0_ref, o_ref, sout_ref, st_sc, *, c):
    j = pl.program_id(1)

    @pl.when(j == 0)
    def _():
        st_sc[...] = s0_ref[...]

    half = 32
    row = lax.broadcasted_iota(jnp.int32, (c, GLA_DK), 0)
    rr = lax.broadcasted_iota(jnp.int32, (c, c), 0)
    cc = lax.broadcasted_iota(jnp.int32, (c, c), 1)
    causal = cc <= rr
    gn = gn_ref[...]
    for s, h in [(s, h) for s in range(GLA_BATCH) for h in range(GLA_HEADS)]:
        ks = slice(h * GLA_DK, (h + 1) * GLA_DK)
        vs = slice(h * GLA_DV, (h + 1) * GLA_DV)
        q = q_ref[s, :, ks].astype(F32)
        k = k_ref[s, :, ks].astype(F32)
        v = v_ref[s, :, vs]
        b = _cumsum_rows(la_ref[s, :, ks])
        if c > half:
            mid = jnp.where(row < half, b[half // 2 - 1:half // 2, :], b[half + half // 2 - 1:half + half // 2, :])
        else:
            mid = jnp.broadcast_to(b[c // 2 - 1:c // 2, :], b.shape)
        qe = (q * jnp.exp(b - mid)).astype(BF16)
        ke = (k * jnp.exp(mid - b)).astype(BF16)
        att = lax.dot_general(qe, ke, _NT, preferred_element_type=F32)
        if c > half:
            edge = b[half - 1:half, :]
            qo = (q * jnp.exp(jnp.minimum(b - edge, 0.0))).astype(BF16)
            ko = (k * jnp.exp(jnp.minimum(edge - b, 0.0))).astype(BF16)
            att_off = lax.dot_general(qo, ko, _NT, preferred_element_type=F32)
            same = (rr < half) == (cc < half)
            att = jnp.where(same, att, att_off)
        att = jnp.where(causal, att, 0.0).astype(BF16)
        st = st_sc[s, h]
        q0 = (q * jnp.exp(b)).astype(BF16)
        o = (jnp.dot(q0, st.astype(BF16), preferred_element_type=F32)
             + jnp.dot(att, v, preferred_element_type=F32))
        last = b[c - 1:c, :]
        kd = (k * jnp.exp(last - b)).astype(BF16)
        upd = lax.dot_general(kd, v, _TN, preferred_element_type=F32)
        dcol = jnp.exp(jnp.transpose(jnp.broadcast_to(last, (GLA_DK, GLA_DK))))
        st_sc[s, h] = jnp.concatenate([dcol, dcol], axis=1) * st + upd
        on = _rms(o, gn) * sgr_ref[s, :, vs].astype(F32)
        o_ref[s, :, vs] = on.astype(BF16)

    @pl.when(j == pl.num_programs(1) - 1)
    def _():
        sout_ref[...] = st_sc[...]


def _gla(gq, gk, gv, la, sgr, gn, s0, b, l):
    c = min(GLA_CHUNK, l)
    nc = l // c
    nqk = GLA_HEADS * GLA_DK
    nv = GLA_HEADS * GLA_DV
    r3 = lambda a: a.reshape(b, l, a.shape[-1])
    assert b % GLA_BATCH == 0
    tok = lambda width: pl.BlockSpec((GLA_BATCH, c, width), lambda bb, j: (bb, j, 0))
    st = pl.BlockSpec((GLA_BATCH, GLA_HEADS, GLA_DK, GLA_DV), lambda bb, j: (bb, 0, 0, 0))
    return pl.pallas_call(
        functools.partial(_gla_kernel, c=c),
        grid=(b // GLA_BATCH, nc),
        in_specs=[tok(nqk), tok(nqk), tok(nv), tok(nqk), tok(nv), _full(gn.shape), st],
        out_specs=[tok(nv), st],
        out_shape=[jax.ShapeDtypeStruct((b, l, nv), BF16),
                   jax.ShapeDtypeStruct((b, GLA_HEADS, GLA_DK, GLA_DV), F32)],
        scratch_shapes=[pltpu.VMEM((GLA_BATCH, GLA_HEADS, GLA_DK, GLA_DV), F32)],
        compiler_params=_params(("parallel", "arbitrary")),
        name="gla",
    )(r3(gq), r3(gk), r3(gv), r3(la), r3(sgr), gn, s0)


def _merge_kernel(oa_ref, ob_ref, sga_ref, sgb_ref, x_ref, wa_ref, wb_ref, wo_ref, gffn_ref,
                  wr_ref, br_ref, x1_ref, h2_ref, route_ref, gcol_ref, cnt_ref, run_sc):
    ya = jnp.dot(oa_ref[...], wa_ref[...], preferred_element_type=F32)
    yb = jnp.dot(ob_ref[...], wb_ref[...], preferred_element_type=F32)
    merged = (sga_ref[...].astype(F32) * ya + sgb_ref[...].astype(F32) * yb).astype(BF16)
    x1 = x_ref[...] + jnp.dot(merged, wo_ref[...], preferred_element_type=F32)
    x1_ref[...] = x1
    h2 = _rms(x1, gffn_ref[...])
    _store_token_major(h2_ref, h2)

    hi = h2.astype(BF16)
    lo = (h2 - hi.astype(F32)).astype(BF16)
    wr = wr_ref[...]
    whi = wr.astype(BF16)
    wlo = (wr - whi.astype(F32)).astype(BF16)
    logits = (lax.dot_general(whi, hi, _NT, preferred_element_type=F32)
              + lax.dot_general(whi, lo, _NT, preferred_element_type=F32)
              + lax.dot_general(wlo, hi, _NT, preferred_element_type=F32)) + br_ref[...]
    tm = logits.shape[1]
    ridx = lax.broadcasted_iota(jnp.int32, (EXPERTS_PER_GROUP, tm), 0)
    big = jnp.int32(1 << 20)

    def top(vals):
        vmax = jnp.max(vals, axis=0, keepdims=True)
        imax = jnp.min(jnp.where(vals == vmax, ridx, big), axis=0, keepdims=True)
        return vmax, imax

    gl = jnp.where(ridx < N_GROUPS, logits[N_EXPERTS:N_EXPERTS + EXPERTS_PER_GROUP], -jnp.inf)
    gmax, g_top = top(gl)
    p_top = 1.0 / jnp.sum(jnp.exp(gl - gmax), axis=0, keepdims=True)
    e_sel = logits[:EXPERTS_PER_GROUP]
    for g in range(1, N_GROUPS):
        e_sel = jnp.where(g_top == g, logits[g * EXPERTS_PER_GROUP:(g + 1) * EXPERTS_PER_GROUP], e_sel)
    v1, i1 = top(e_sel)
    v2, i2 = top(jnp.where(ridx == i1, -jnp.inf, e_sel))
    e21 = jnp.exp(v2 - v1)
    w1 = p_top / (1.0 + e21)
    w2 = p_top * e21 / (1.0 + e21)
    base = g_top * EXPERTS_PER_GROUP
    id0 = base + i1
    id1 = base + i2
    gates = jnp.where(ridx == 0, w1, jnp.where(ridx == 1, w2, 0.0))
    gcol_ref[...] = jnp.transpose(jnp.concatenate([gates, jnp.zeros((LANES - EXPERTS_PER_GROUP, tm), F32)], axis=0))

    @pl.when(pl.program_id(0) == 0)
    def _():
        run_sc[...] = jnp.zeros(run_sc.shape, F32)

    eidx = lax.broadcasted_iota(jnp.int32, (N_EXPERTS, tm), 0)
    oh0 = jnp.where(eidx == id0, 1.0, 0.0)
    oh1 = jnp.where(eidx == id1, 1.0, 0.0)
    earlier = jnp.where(lax.broadcasted_iota(jnp.int32, (tm, tm), 0) < lax.broadcasted_iota(jnp.int32, (tm, tm), 1),
                        1.0, 0.0).astype(BF16)
    p0 = jnp.dot(oh0.astype(BF16), earlier, preferred_element_type=F32)
    p1 = jnp.dot(oh1.astype(BF16), earlier, preferred_element_type=F32)
    c0 = jnp.sum(oh0, axis=1, keepdims=True)
    c1 = jnp.sum(oh1, axis=1, keepdims=True)
    run = run_sc[...]
    rank0 = jnp.sum(oh0 * (run + p0), axis=0, keepdims=True)
    rank1 = jnp.sum(oh1 * (run + c0 + p1), axis=0, keepdims=True)
    run = run + c0 + c1
    run_sc[...] = run
    cnt_ref[...] = jnp.broadcast_to(run, cnt_ref.shape)
    route_ref[...] = jnp.where(ridx == 0, id0, jnp.where(ridx == 1, id1, jnp.where(
        ridx == 2, rank0.astype(jnp.int32), jnp.where(ridx == 3, rank1.astype(jnp.int32), 0))))


def _merge(oa, ob, sga, sgb, x2, w):
    n = x2.shape[0]
    tm = min(MERGE_TILE, n)
    assert n % tm == 0
    row = lambda width: pl.BlockSpec((tm, width), lambda i: (i, 0))
    weights = [w["w_a"], w["w_b"], w["w_o"], w["g_ffn"], w["w_router"], w["b_router"]]
    return pl.pallas_call(
        _merge_kernel,
        grid=(n // tm,),
        in_specs=[row(MLA_HEADS * V_DIM), row(GLA_HEADS * GLA_DV), row(D_MODEL), row(D_MODEL),
                  row(D_MODEL)] + [_full(a.shape) for a in weights],
        out_specs=[row(D_MODEL), pl.BlockSpec((tm * SLABS, LANES), lambda i: (i, 0)),
                   pl.BlockSpec((EXPERTS_PER_GROUP, tm), lambda i: (0, i)), row(LANES),
                   _full((N_EXPERTS, LANES))],
        out_shape=[jax.ShapeDtypeStruct((n, D_MODEL), F32), jax.ShapeDtypeStruct((n * SLABS, LANES), F32),
                   jax.ShapeDtypeStruct((EXPERTS_PER_GROUP, n), jnp.int32),
                   jax.ShapeDtypeStruct((n, LANES), F32),
                   jax.ShapeDtypeStruct((N_EXPERTS, LANES), F32)],
        scratch_shapes=[pltpu.VMEM((N_EXPERTS, 1), F32)],
        compiler_params=_params(("arbitrary",)),
        name="merge",
    )(oa, ob, sga, sgb, x2, *weights)


ROW_DMA_UNROLL = 8


def _row_tile(ref, r):
    return ref.at[pl.ds(pl.multiple_of(r * SLABS, SLABS), SLABS), :]


def _scatter_kernel(ps_ref, cnt_ref, pe_ref, dest_ref, h2_ref, xd_hbm, zero_sc, sem, zsem):
    tm = h2_ref.shape[0] // SLABS

    @pl.when(pl.program_id(0) == 0)
    def _():
        zero_sc[...] = jnp.zeros(zero_sc.shape, F32)

        def pad_copies(e, act):
            lo = ps_ref[e] + cnt_ref[e]
            pad = pe_ref[e] - lo
            bit = FFN_ROWS // 2
            while bit >= 1:
                @pl.when((pad & bit) != 0)
                def _(bit=bit):
                    first = lo + (pad & ~(2 * bit - 1))
                    act(pltpu.make_async_copy(
                        zero_sc.at[pl.ds(0, bit * SLABS), :],
                        xd_hbm.at[pl.ds(pl.multiple_of(first * SLABS, SLABS), bit * SLABS), :], zsem.at[0]))
                bit //= 2

        def start_pads(e, c):
            pad_copies(e, lambda cp: cp.start())
            return c

        def wait_pads(e, c):
            pad_copies(e, lambda cp: cp.wait())
            return c

        lax.fori_loop(0, N_EXPERTS, start_pads, 0)
        lax.fori_loop(0, N_EXPERTS, wait_pads, 0)

        def block_copy(b):
            start_row = pl.multiple_of(b * (FFN_ROWS * SLABS), FFN_ROWS * SLABS)
            return pltpu.make_async_copy(zero_sc, xd_hbm.at[pl.ds(start_row, FFN_ROWS * SLABS), :], zsem.at[0])

        first_unused = pe_ref[N_EXPERTS - 1] // FFN_ROWS
        n_blocks = xd_hbm.shape[0] // (FFN_ROWS * SLABS)

        def start_block(b, c):
            block_copy(b).start()
            return c

        def wait_block(b, c):
            block_copy(b).wait()
            return c

        lax.fori_loop(first_unused, n_blocks, start_block, 0)
        lax.fori_loop(first_unused, n_blocks, wait_block, 0)

    def body(t, carry):
        for k in range(2):
            d = dest_ref[0, 0, k * tm + t]
            pltpu.make_async_copy(_row_tile(h2_ref, t), _row_tile(xd_hbm, d), sem.at[0]).start(priority=k)
        return carry

    lax.fori_loop(0, tm, body, 0, unroll=ROW_DMA_UNROLL)
    for k in range(2):
        pltpu.make_async_copy(h2_ref, xd_hbm.at[pl.ds(0, tm * SLABS), :], sem.at[0]).wait()


def _scatter(pad_start, counts, pad_end, dest_blocks, h2, total_rows):
    tm = ROW_TILE
    nt = dest_blocks.shape[0]
    grid_spec = pltpu.PrefetchScalarGridSpec(
        num_scalar_prefetch=3,
        grid=(nt,),
        in_specs=[pl.BlockSpec((1, 1, 2 * tm), lambda i, *_: (i, 0, 0), memory_space=pltpu.SMEM),
                  pl.BlockSpec((tm * SLABS, LANES), lambda i, *_: (i, 0))],
        out_specs=pl.BlockSpec(memory_space=pl.ANY),
        scratch_shapes=[pltpu.VMEM((FFN_ROWS * SLABS, LANES), F32), pltpu.SemaphoreType.DMA((1,)),
                        pltpu.SemaphoreType.DMA((1,))],
    )
    return pl.pallas_call(
        _scatter_kernel,
        grid_spec=grid_spec,
        out_shape=jax.ShapeDtypeStruct((total_rows * SLABS, LANES), F32),
        compiler_params=_params(("arbitrary",)),
        name="moe_scatter",
    )(pad_start, counts, pad_end, dest_blocks, h2)


def _ffn_kernel(be_ref, nbu_ref, x_ref, wg_ref, wu_ref, wd_ref, out_ref):
    del be_ref

    @pl.when(pl.program_id(0) < nbu_ref[0])
    def _():
        x = _load_token_major(x_ref, FFN_ROWS).astype(BF16)
        g = jnp.dot(x, wg_ref[0], preferred_element_type=F32)
        u = jnp.dot(x, wu_ref[0], preferred_element_type=F32)
        mid = (g * _sigmoid(g) * u).astype(BF16)
        _store_token_major(out_ref, jnp.dot(mid, wd_ref[0], preferred_element_type=F32))

    @pl.when(pl.program_id(0) >= nbu_ref[0])
    def _():
        out_ref[...] = jnp.zeros(out_ref.shape, F32)


def _ffn(block_expert, nb_used, xd, wg, wu, wd):
    nb = block_expert.shape[0]
    blk = lambda i, be, nbu: (jnp.minimum(i, nbu[0] - 1), 0)
    wsel = lambda i, be, nbu: (be[jnp.minimum(i, nbu[0] - 1)], 0, 0)
    grid_spec = pltpu.PrefetchScalarGridSpec(
        num_scalar_prefetch=2,
        grid=(nb,),
        in_specs=[pl.BlockSpec((FFN_ROWS * SLABS, LANES), blk),
                  pl.BlockSpec((1, D_MODEL, EXPERT_FF), wsel),
                  pl.BlockSpec((1, D_MODEL, EXPERT_FF), wsel),
                  pl.BlockSpec((1, EXPERT_FF, D_MODEL), wsel)],
        out_specs=pl.BlockSpec((FFN_ROWS * SLABS, LANES), lambda i, be, nbu: (i, 0)),
    )
    return pl.pallas_call(
        _ffn_kernel,
        grid_spec=grid_spec,
        out_shape=jax.ShapeDtypeStruct((nb * FFN_ROWS * SLABS, LANES), F32),
        compiler_params=_params(("arbitrary",)),
        name="expert_ffn",
    )(block_expert, nb_used, xd, wg, wu, wd)


def _combine_kernel(dest_ref, x1_ref, gcol_ref, gfin_ref, eo_hbm, y_ref, buf, sem):
    tm = x1_ref.shape[0]

    def body(t, carry):
        for k in range(2):
            d = dest_ref[0, 0, k * tm + t]
            pltpu.make_async_copy(_row_tile(eo_hbm, d), _row_tile(buf.at[k], t), sem.at[0]).start(priority=k)
        return carry

    lax.fori_loop(0, tm, body, 0, unroll=ROW_DMA_UNROLL)
    for k in range(2):
        pltpu.make_async_copy(eo_hbm.at[pl.ds(0, tm * SLABS), :], buf.at[k], sem.at[0]).wait()
    g = gcol_ref[...]
    y = x1_ref[...] + (g[:, 0:1] * _load_token_major(buf.at[0], tm) + g[:, 1:2] * _load_token_major(buf.at[1], tm))
    y_ref[...] = _rms(y, gfin_ref[...])


def _combine(dest_blocks, x1, gcol, g_final, eo):
    n = x1.shape[0]
    tm = ROW_TILE
    return pl.pallas_call(
        _combine_kernel,
        grid=(n // tm,),
        in_specs=[pl.BlockSpec((1, 1, 2 * tm), lambda i: (i, 0, 0), memory_space=pltpu.SMEM),
                  pl.BlockSpec((tm, D_MODEL), lambda i: (i, 0)),
                  pl.BlockSpec((tm, LANES), lambda i: (i, 0)),
                  _full(g_final.shape),
                  pl.BlockSpec(memory_space=pl.ANY)],
        out_specs=pl.BlockSpec((tm, D_MODEL), lambda i: (i, 0)),
        out_shape=jax.ShapeDtypeStruct((n, D_MODEL), F32),
        scratch_shapes=[pltpu.VMEM((2, tm * SLABS, LANES), F32), pltpu.SemaphoreType.DMA((1,))],
        compiler_params=_params(("arbitrary",)),
        name="combine",
    )(dest_blocks, x1, gcol, g_final, eo)


def _dest_kernel(ps_ref, route_ref, o_ref):
    route = route_ref[...]
    start = jnp.zeros(route.shape, jnp.int32)
    for e in range(N_EXPERTS):
        start = jnp.where(route == e, ps_ref[e], start)
    o_ref[...] = start + pltpu.roll(route, EXPERTS_PER_GROUP - 2, 0)


def _dest_rows(pad_start, route):
    n = route.shape[1]
    tile = min(n, 8192)
    assert n % tile == 0
    grid_spec = pltpu.PrefetchScalarGridSpec(
        num_scalar_prefetch=1,
        grid=(n // tile,),
        in_specs=[pl.BlockSpec((EXPERTS_PER_GROUP, tile), lambda i, ps: (0, i))],
        out_specs=pl.BlockSpec((EXPERTS_PER_GROUP, tile), lambda i, ps: (0, i)),
    )
    return pl.pallas_call(
        _dest_kernel,
        grid_spec=grid_spec,
        out_shape=jax.ShapeDtypeStruct(route.shape, jnp.int32),
        compiler_params=_params(("arbitrary",)),
        name="moe_dest",
    )(pad_start, route)


def _layout(cnt, route, n):
    counts = cnt[:, 0].astype(jnp.int32)
    padded = (counts + FFN_ROWS - 1) // FFN_ROWS * FFN_ROWS
    pad_end = jnp.cumsum(padded).astype(jnp.int32)
    pad_start = pad_end - padded
    nb = 2 * n // FFN_ROWS + N_EXPERTS
    block_start = jnp.arange(nb, dtype=jnp.int32) * FFN_ROWS
    block_expert = jnp.minimum(jnp.sum((block_start[:, None] >= pad_end[None, :]).astype(jnp.int32), axis=1),
                               N_EXPERTS - 1).astype(jnp.int32)
    nb_used = (pad_end[N_EXPERTS - 1:] // FFN_ROWS).astype(jnp.int32)
    dest = _dest_rows(pad_start, route)[:2]
    tm = ROW_TILE
    dest_blocks = dest.reshape(2, n // tm, tm).transpose(1, 0, 2).reshape(n // tm, 1, 2 * tm)
    return pad_start, counts, pad_end, block_expert, nb_used, nb * FFN_ROWS, dest_blocks


def _rope_tables(l, past, rows):
    half = ROPE_DIM // 2
    inv = ROPE_THETA ** (-jnp.arange(half, dtype=F32) / half)
    pos = (past + (jnp.arange(rows, dtype=jnp.int32) % l)).astype(F32)
    ang = pos[:, None] * inv[None, :]
    pad = jnp.zeros((rows, LANES - ROPE_DIM), F32)
    cos = jnp.concatenate([jnp.cos(ang), jnp.cos(ang), pad], axis=1)
    sin = jnp.concatenate([-jnp.sin(ang), jnp.sin(ang), pad], axis=1)
    return cos, sin


def _group(x, ckv_past, kpe_past, s0, w, g_final):
    b, l, d = x.shape
    n = b * l
    x2 = x.reshape(n, d)
    past = 0 if ckv_past is None else ckv_past.shape[1]
    cos_t, sin_t = _rope_tables(l, past, max(l, ROW_TILE))
    prompt = ckv_past is None
    (qcat, kcat, ckv, kpe, gq, gk, gv, la, sgr, sga, sgb, *maybe_vt) = _inproj(x2, cos_t, sin_t, w, prompt)
    if prompt:
        oa = _attn_prompt(qcat, kcat, maybe_vt[0], w["w_uv"], b, l)
    else:
        oa = _attn_sample(qcat, kcat, ckv_past, kpe_past, w["w_uv"], b, l)
    ob, s_new = _gla(gq, gk, gv, la, sgr, w["g_gla"], s0, b, l)
    x1, h2, route, gcol, cnt = _merge(oa.reshape(n, -1), ob.reshape(n, -1), sga, sgb, x2, w)
    pad_start, counts, pad_end, block_expert, nb_used, total_rows, dest_blocks = _layout(cnt, route, n)
    xd = _scatter(pad_start, counts, pad_end, dest_blocks, h2, total_rows)
    eo = _ffn(block_expert, nb_used, xd, w["w_eg"], w["w_eu"], w["w_ed"])
    y = _combine(dest_blocks, x1, gcol, g_final, eo)
    return (y.reshape(b, l, d), ckv.reshape(b, l, KV_LORA), kpe.reshape(b, l, ROPE_DIM), s_new)


def _prep_weights(w_in, g_norm_mix, g_qnorm, w_uq, g_kvnorm, w_ukv, w_gate2, b_gate2, g_gla_norm,
                  w_branch_a, w_branch_b, w_out, g_norm_ffn, w_router_group, b_router_group,
                  w_router_expert, b_router_expert, w_exp_gate, w_exp_up, w_exp_down):
    nqk = GLA_HEADS * GLA_DK
    nv = GLA_HEADS * GLA_DV
    o = 0
    parts = {}
    for name, width in (("cq", Q_LORA), ("ckv", KV_LORA), ("kpe", ROPE_DIM), ("gq", nqk), ("gk", nqk),
                        ("gv", nv), ("glr", GATE_RANK), ("gr", nv), ("ga", D_MODEL), ("gb", D_MODEL)):
        parts[name] = w_in[:, o:o + width]
        o += width
    padc = lambda a, width: jnp.pad(a, ((0, 0), (0, width - a.shape[1])))
    w_small = jnp.concatenate([parts["cq"], parts["ckv"], padc(parts["kpe"], LANES),
                               padc(parts["glr"], LANES)], axis=1).astype(BF16)
    uq = w_uq.reshape(Q_LORA, MLA_HEADS, NOPE_DIM + ROPE_DIM)
    ukv = w_ukv.reshape(KV_LORA, MLA_HEADS, NOPE_DIM + V_DIM)
    lat = _fold_q(uq[:, :, :NOPE_DIM].transpose(1, 0, 2), ukv[:, :, :NOPE_DIM].transpose(1, 0, 2))
    q_rope = uq[:, :, NOPE_DIM:].transpose(1, 0, 2)
    w_qcat = jnp.concatenate([lat, q_rope, jnp.zeros((MLA_HEADS, Q_LORA, QCAT - KV_LORA - ROPE_DIM), F32)],
                             axis=2)
    w_qcat = w_qcat.transpose(1, 0, 2).reshape(Q_LORA, MLA_HEADS * QCAT).astype(BF16)
    w_router = jnp.concatenate([w_router_expert.T, w_router_group.T,
                                jnp.zeros((ROUTER_ROWS - N_EXPERTS - N_GROUPS, D_MODEL), F32)], axis=0)
    b_router = jnp.concatenate([b_router_expert, b_router_group,
                                jnp.zeros((ROUTER_ROWS - N_EXPERTS - N_GROUPS,), F32)]).reshape(ROUTER_ROWS, 1)
    return {
        "g_mix": g_norm_mix.reshape(1, D_MODEL), "w_small": w_small,
        "g_qn": g_qnorm.reshape(1, Q_LORA), "g_kvn": g_kvnorm.reshape(1, KV_LORA), "w_qcat": w_qcat,
        "w_g2": jnp.pad(w_gate2, ((0, LANES - GATE_RANK), (0, 0))).astype(BF16),
        "b_g2": b_gate2.reshape(1, nqk),
        "w_gqk": jnp.concatenate([parts["gq"], parts["gk"]], axis=1).astype(BF16),
        "w_gv": parts["gv"].astype(BF16), "w_gr": parts["gr"].astype(BF16),
        "w_ga": parts["ga"].astype(BF16), "w_gb": parts["gb"].astype(BF16),
        "w_uv": ukv[:, :, NOPE_DIM:].transpose(1, 0, 2).astype(BF16),
        "g_gla": g_gla_norm.reshape(1, GLA_DV),
        "w_a": w_branch_a.astype(BF16), "w_b": w_branch_b.astype(BF16), "w_o": w_out.astype(BF16),
        "g_ffn": g_norm_ffn.reshape(1, D_MODEL), "w_router": w_router, "b_router": b_router,
        "w_eg": w_exp_gate.astype(BF16), "w_eu": w_exp_up.astype(BF16), "w_ed": w_exp_down.astype(BF16),
    }


def kernel(x_prompt, x_sample, cache_ckv, cache_krope, state_gla, w_in, g_norm_mix, g_qnorm, w_uq, g_kvnorm, w_ukv, w_gate2, b_gate2, g_gla_norm, w_branch_a, w_branch_b, w_out, g_norm_ffn, w_router_group, b_router_group, w_router_expert, b_router_expert, w_exp_gate, w_exp_up, w_exp_down, g_norm_final):
    depth = w_in.shape[0]
    assert depth == 1, "the final norm is fused into the last layer's combine step"
    gfin = g_norm_final.reshape(1, D_MODEL)
    w = _prep_weights(w_in[0], g_norm_mix[0], g_qnorm[0], w_uq[0], g_kvnorm[0], w_ukv[0], w_gate2[0],
                      b_gate2[0], g_gla_norm[0], w_branch_a[0], w_branch_b[0], w_out[0], g_norm_ffn[0],
                      w_router_group[0], b_router_group[0], w_router_expert[0], b_router_expert[0],
                      w_exp_gate[0], w_exp_up[0], w_exp_down[0])
    bp = x_prompt.shape[0]
    zero_state = jnp.zeros((bp, GLA_HEADS, GLA_DK, GLA_DV), F32)
    yp, c1, k1, s1 = _group(x_prompt, None, None, zero_state, w, gfin)
    ys, c2, k2, s2 = _group(x_sample, cache_ckv[0], cache_krope[0], state_gla[0].astype(F32), w, gfin)
    return (yp, ys, c1[None], k1[None], s1[None], c2[None], k2[None], s2[None])
```

```python
import functools
import math

import jax
import jax.numpy as jnp
from jax import lax
from jax.experimental import pallas as pl
from jax.experimental.pallas import tpu as pltpu

F32 = jnp.float32
BF16 = jnp.bfloat16

D_MODEL = 1024
CHUNK = 64
EPS = 1e-6
MLA_HEADS = 8
Q_LORA = 256
KV_LORA = 128
NOPE_DIM = 64
ROPE_DIM = 32
V_DIM = 64
ROPE_THETA = 10000.0
GLA_HEADS = 4
GLA_DK = 128
GLA_DV = 256
GATE_RANK = 16
GATE_TEMP = 16.0
N_GROUPS = 4
EXPERTS_PER_GROUP = 8
N_EXPERTS = 32
EXPERT_FF = 512

LANES = 128
QCAT = 2 * LANES
ROW_TILE = 256
MERGE_TILE = 512
GLA_BATCH = 2
Q_BLOCK = ROW_TILE
KEY_TILE = ROW_TILE
SUM_ROWS = 16
GLA_CHUNK = 64
FFN_ROWS = 256
ROUTER_ROWS = 48
VMEM_LIMIT = 56 * 1024 * 1024
LOG2E = 1.4426950408889634
ATT_SCALE = LOG2E / math.sqrt(NOPE_DIM + ROPE_DIM)

_NT = (((1,), (1,)), ((), ()))
_TN = (((0,), (0,)), ((), ()))


def _params(sem):
    return pltpu.CompilerParams(dimension_semantics=sem, vmem_limit_bytes=VMEM_LIMIT)


def _rms(x, g):
    return x * lax.rsqrt(jnp.mean(x * x, axis=-1, keepdims=True) + EPS) * g


def _sigmoid(x):
    return 1.0 / (1.0 + jnp.exp(-x))


def _full(shape):
    n = len(shape)
    return pl.BlockSpec(shape, lambda *_: (0,) * n)


SLABS = D_MODEL // LANES


def _store_token_major(ref, x):
    rows = x.shape[0]
    for j in range(SLABS):
        ref[pl.ds(j, rows, stride=SLABS), :] = x[:, j * LANES:(j + 1) * LANES]


def _load_token_major(ref, rows):
    return jnp.concatenate([ref[pl.ds(j, rows, stride=SLABS), :] for j in range(SLABS)], axis=1)


def _fold_q_kernel(wq_ref, wk_ref, o_ref):
    o_ref[0] = lax.dot_general(wq_ref[0], wk_ref[0], _NT, precision=lax.Precision.HIGHEST,
                               preferred_element_type=F32)


def _fold_q(wq_nope, wk_nope):
    return pl.pallas_call(
        _fold_q_kernel,
        grid=(MLA_HEADS,),
        in_specs=[pl.BlockSpec((1, Q_LORA, NOPE_DIM), lambda h: (h, 0, 0)),
                  pl.BlockSpec((1, KV_LORA, NOPE_DIM), lambda h: (h, 0, 0))],
        out_specs=pl.BlockSpec((1, Q_LORA, KV_LORA), lambda h: (h, 0, 0)),
        out_shape=jax.ShapeDtypeStruct((MLA_HEADS, Q_LORA, KV_LORA), F32),
        compiler_params=_params(("arbitrary",)),
        name="fold_q",
    )(wq_nope, wk_nope)


def _inproj_kernel(x_ref, cos_ref, sin_ref, gmix_ref, wsm_ref, gqn_ref, gkvn_ref, wqc_ref,
                   wg2_ref, bg2_ref, wgqk_ref, wgv_ref, wgr_ref, wga_ref, wgb_ref,
                   qcat_ref, kcat_ref, ckv_ref, kpe_ref, gq_ref, gk_ref, gv_ref, la_ref,
                   sgr_ref, sga_ref, sgb_ref, *maybe_vt_ref, transposed):
    hb = _rms(x_ref[...], gmix_ref[...]).astype(BF16)
    zs = jnp.dot(hb, wsm_ref[...], preferred_element_type=F32)
    cos = cos_ref[...]
    sin = sin_ref[...]
    first_half = lax.broadcasted_iota(jnp.int32, cos.shape, 1) < ROPE_DIM // 2

    def rope(v):
        rot = jnp.where(first_half, pltpu.roll(v, LANES - ROPE_DIM // 2, 1),
                        pltpu.roll(v, ROPE_DIM // 2, 1))
        return v * cos + rot * sin

    cqn = _rms(zs[:, :Q_LORA], gqn_ref[...]).astype(BF16)
    qc = jnp.dot(cqn, wqc_ref[...], preferred_element_type=F32)
    for h in range(MLA_HEADS):
        lat = qc[:, h * QCAT:h * QCAT + LANES]
        pe = qc[:, h * QCAT + LANES:(h + 1) * QCAT]
        if transposed:
            lat_t = jnp.transpose(lat * ATT_SCALE).astype(BF16)
            pe_t = jnp.transpose(rope(pe) * ATT_SCALE).astype(BF16)
            for blk in range(lat.shape[0] // Q_BLOCK):
                cols = slice(blk * Q_BLOCK, (blk + 1) * Q_BLOCK)
                qcat_ref[blk, :LANES, h * Q_BLOCK:(h + 1) * Q_BLOCK] = lat_t[:, cols]
                qcat_ref[blk, LANES:, h * Q_BLOCK:(h + 1) * Q_BLOCK] = pe_t[:, cols]
        else:
            qcat_ref[h, :, :LANES] = (lat * ATT_SCALE).astype(BF16)
            qcat_ref[h, :, LANES:] = (rope(pe) * ATT_SCALE).astype(BF16)

    ckv = _rms(zs[:, Q_LORA:Q_LORA + KV_LORA], gkvn_ref[...])
    ckv_ref[...] = ckv
    if transposed:
        maybe_vt_ref[0][0] = jnp.transpose(ckv).astype(BF16)
    kpe = rope(zs[:, Q_LORA + KV_LORA:Q_LORA + KV_LORA + LANES])
    kpe_ref[...] = kpe[:, :ROPE_DIM]
    kcat_ref[:, :LANES] = ckv.astype(BF16)
    kcat_ref[:, LANES:] = kpe.astype(BF16)

    glr = zs[:, Q_LORA + KV_LORA + LANES:].astype(BF16)
    xg = jnp.dot(glr, wg2_ref[...], preferred_element_type=F32) + bg2_ref[...]
    la_ref[...] = (jnp.minimum(xg, 0.0) - jnp.log(1.0 + jnp.exp(-jnp.abs(xg)))) * (1.0 / GATE_TEMP)

    zqk = jnp.dot(hb, wgqk_ref[...], preferred_element_type=F32)
    nqk = GLA_HEADS * GLA_DK
    gq_ref[...] = (zqk[:, :nqk] * (GLA_DK ** -0.5)).astype(BF16)
    gk_ref[...] = zqk[:, nqk:].astype(BF16)
    gv_ref[...] = jnp.dot(hb, wgv_ref[...], preferred_element_type=F32).astype(BF16)
    gr = jnp.dot(hb, wgr_ref[...], preferred_element_type=F32)
    sgr_ref[...] = (gr * _sigmoid(gr)).astype(BF16)
    sga_ref[...] = _sigmoid(jnp.dot(hb, wga_ref[...], preferred_element_type=F32)).astype(BF16)
    sgb_ref[...] = _sigmoid(jnp.dot(hb, wgb_ref[...], preferred_element_type=F32)).astype(BF16)


def _inproj(x2, cos_t, sin_t, w, transposed):
    n = x2.shape[0]
    tm = ROW_TILE
    nt = n // tm
    tab_blocks = cos_t.shape[0] // tm
    row = lambda width: pl.BlockSpec((tm, width), lambda i: (i, 0))
    tab = pl.BlockSpec((tm, LANES), lambda i: (i % tab_blocks, 0))
    nqk = GLA_HEADS * GLA_DK
    nv = GLA_HEADS * GLA_DV
    weights = [w["g_mix"], w["w_small"], w["g_qn"], w["g_kvn"], w["w_qcat"], w["w_g2"], w["b_g2"],
               w["w_gqk"], w["w_gv"], w["w_gr"], w["w_ga"], w["w_gb"]]
    out_shape = [
        jax.ShapeDtypeStruct((MLA_HEADS, n, QCAT), BF16),
        jax.ShapeDtypeStruct((n, QCAT), BF16),
        jax.ShapeDtypeStruct((n, KV_LORA), F32),
        jax.ShapeDtypeStruct((n, ROPE_DIM), F32),
        jax.ShapeDtypeStruct((n, nqk), BF16),
        jax.ShapeDtypeStruct((n, nqk), BF16),
        jax.ShapeDtypeStruct((n, nv), BF16),
        jax.ShapeDtypeStruct((n, nqk), F32),
        jax.ShapeDtypeStruct((n, nv), BF16),
        jax.ShapeDtypeStruct((n, D_MODEL), BF16),
        jax.ShapeDtypeStruct((n, D_MODEL), BF16),
    ]
    out_specs = [
        pl.BlockSpec((MLA_HEADS, tm, QCAT), lambda i: (0, i, 0)),
        row(QCAT), row(KV_LORA), row(ROPE_DIM), row(nqk), row(nqk), row(nv), row(nqk), row(nv),
        row(D_MODEL), row(D_MODEL),
    ]
    if transposed:
        qb = tm // Q_BLOCK
        out_shape[0] = jax.ShapeDtypeStruct((n // Q_BLOCK, QCAT, MLA_HEADS * Q_BLOCK), BF16)
        out_specs[0] = pl.BlockSpec((qb, QCAT, MLA_HEADS * Q_BLOCK), lambda i: (i, 0, 0))
        out_shape.append(jax.ShapeDtypeStruct((nt, KV_LORA, tm), BF16))
        out_specs.append(pl.BlockSpec((1, KV_LORA, tm), lambda i: (i, 0, 0)))
    return pl.pallas_call(
        functools.partial(_inproj_kernel, transposed=transposed),
        grid=(nt,),
        in_specs=[row(D_MODEL), tab, tab] + [_full(a.shape) for a in weights],
        out_specs=out_specs,
        out_shape=out_shape,
        compiler_params=_params(("parallel",)),
        name="inproj",
    )(x2, cos_t, sin_t, *weights)


def _head_out(o_lat, wuv_ref, rows):
    outs = []
    for h in range(MLA_HEADS):
        oh = o_lat[h * rows:(h + 1) * rows].astype(BF16)
        outs.append(jnp.dot(oh, wuv_ref[h], preferred_element_type=F32))
    return jnp.concatenate(outs, axis=1)


def _attn_prompt_kernel(qt_ref, k_ref, vt_ref, bias_ref, wuv_ref, o_ref, m_sc, l_sc, acc_sc):
    i = pl.program_id(1)
    m_sc[...] = jnp.full(m_sc.shape, -jnp.inf, F32)
    l_sc[...] = jnp.zeros(l_sc.shape, F32)
    acc_sc[...] = jnp.zeros(acc_sc.shape, F32)
    heads = [slice(h * Q_BLOCK, (h + 1) * Q_BLOCK) for h in range(MLA_HEADS)]
    ones_rows = jnp.ones((SUM_ROWS, KEY_TILE), BF16)

    def process(tiles, diagonal_last):
        nt = len(tiles)
        kbs = [k_ref[0, pl.ds(pl.multiple_of(kt * KEY_TILE, KEY_TILE), KEY_TILE), :] for kt in tiles]
        vts = [jnp.concatenate([vt_ref[kt], ones_rows], axis=0) for kt in tiles]
        ss = [[None] * MLA_HEADS for _ in range(nt)]
        for t in range(nt):
            for h, hs in enumerate(heads):
                s = jnp.dot(kbs[t], qt_ref[0, :, hs], preferred_element_type=F32)
                if diagonal_last and t == nt - 1:
                    s = s + bias_ref[...]
                ss[t][h] = s
        tmax = [[jnp.max(ss[t][h], axis=0, keepdims=True) for h in range(MLA_HEADS)] for t in range(nt)]
        scale = [[None] * MLA_HEADS for _ in range(nt)]
        mrun = [[None] * MLA_HEADS for _ in range(nt)]
        for h, hs in enumerate(heads):
            m = m_sc[:, hs]
            for t in range(nt):
                m_new = jnp.maximum(m, tmax[t][h])
                scale[t][h] = jnp.exp2(m - m_new)
                mrun[t][h] = m_new
                m = m_new
            m_sc[:, hs] = m
        pvs = [[jnp.dot(vts[t], jnp.exp2(ss[t][h] - mrun[t][h]).astype(BF16), preferred_element_type=F32)
                for h in range(MLA_HEADS)] for t in range(nt)]
        for h, hs in enumerate(heads):
            l, acc = l_sc[:, hs], acc_sc[:, hs]
            for t in range(nt):
                acc = scale[t][h] * acc + pvs[t][h][:KV_LORA]
                l = scale[t][h] * l + pvs[t][h][KV_LORA:KV_LORA + 1]
            l_sc[:, hs], acc_sc[:, hs] = l, acc

    def body(j, carry):
        process([2 * j, 2 * j + 1], False)
        return carry

    lax.fori_loop(0, i // 2, body, 0)

    @pl.when(i % 2 == 1)
    def _():
        process([i - 1], False)

    process([i], True)

    o_lat_t = (acc_sc[...] / l_sc[...]).astype(BF16)
    outs = []
    for h, hs in enumerate(heads):
        outs.append(lax.dot_general(o_lat_t[:, hs], wuv_ref[h], _TN, preferred_element_type=F32))
    o_ref[0] = jnp.concatenate(outs, axis=1).astype(BF16)


def _attn_prompt(qt, kcat, vt, wuv, b, l):
    assert Q_BLOCK == KEY_TILE and Q_BLOCK % CHUNK == 0
    nq = l // Q_BLOCK
    nkt = l // KEY_TILE
    cols = MLA_HEADS * Q_BLOCK
    pos_chunk = jnp.arange(Q_BLOCK, dtype=jnp.int32) // CHUNK
    bias = jnp.where(pos_chunk[:, None] <= pos_chunk[None, :], 0.0, -jnp.inf).astype(F32)
    return pl.pallas_call(
        _attn_prompt_kernel,
        grid=(b, nq),
        in_specs=[pl.BlockSpec((1, QCAT, cols), lambda bb, i: (bb * nq + i, 0, 0)),
                  pl.BlockSpec((1, l, QCAT), lambda bb, i: (bb, 0, 0)),
                  pl.BlockSpec((nkt, KV_LORA, KEY_TILE), lambda bb, i: (bb, 0, 0)),
                  _full(bias.shape), _full(wuv.shape)],
        out_specs=pl.BlockSpec((1, Q_BLOCK, MLA_HEADS * V_DIM), lambda bb, i: (bb, i, 0)),
        out_shape=jax.ShapeDtypeStruct((b, l, MLA_HEADS * V_DIM), BF16),
        scratch_shapes=[pltpu.VMEM((1, cols), F32), pltpu.VMEM((1, cols), F32),
                        pltpu.VMEM((KV_LORA, cols), F32)],
        compiler_params=_params(("parallel", "arbitrary")),
        name="attn_prompt",
    )(qt, kcat.reshape(b, l, QCAT), vt, bias, wuv)


def _attn_sample_kernel(q_ref, cckv_ref, ckr_ref, kn_ref, wuv_ref, o_ref, *, ls):
    q = q_ref[...].reshape(MLA_HEADS * ls, QCAT)
    ck = cckv_ref[0].astype(BF16)
    kr = ckr_ref[0].astype(BF16)
    kn = kn_ref[...]
    s_c = (lax.dot_general(q[:, :KV_LORA], ck, _NT, preferred_element_type=F32)
           + lax.dot_general(q[:, KV_LORA:KV_LORA + ROPE_DIM], kr, _NT, preferred_element_type=F32))
    s_n = lax.dot_general(q, kn, _NT, preferred_element_type=F32)
    m = jnp.maximum(jnp.max(s_c, axis=1, keepdims=True), jnp.max(s_n, axis=1, keepdims=True))
    p_c = jnp.exp2(s_c - m)
    p_n = jnp.exp2(s_n - m)
    den = jnp.sum(p_c, axis=1, keepdims=True) + jnp.sum(p_n, axis=1, keepdims=True)
    o_lat = (jnp.dot(p_c.astype(BF16), ck, preferred_element_type=F32)
             + jnp.dot(p_n.astype(BF16), kn[:, :KV_LORA], preferred_element_type=F32)) / den
    o_ref[0] = _head_out(o_lat, wuv_ref, ls).astype(BF16)


def _attn_sample(qcat, kcat, cache_ckv, cache_krope, wuv, b, ls):
    past = cache_ckv.shape[1]
    assert past % CHUNK == 0 and ls <= CHUNK
    return pl.pallas_call(
        functools.partial(_attn_sample_kernel, ls=ls),
        grid=(b,),
        in_specs=[pl.BlockSpec((MLA_HEADS, ls, QCAT), lambda bb: (0, bb, 0)),
                  pl.BlockSpec((1, past, KV_LORA), lambda bb: (bb, 0, 0)),
                  pl.BlockSpec((1, past, ROPE_DIM), lambda bb: (bb, 0, 0)),
                  pl.BlockSpec((ls, QCAT), lambda bb: (bb, 0)),
                  _full(wuv.shape)],
        out_specs=pl.BlockSpec((1, ls, MLA_HEADS * V_DIM), lambda bb: (bb, 0, 0)),
        out_shape=jax.ShapeDtypeStruct((b, ls, MLA_HEADS * V_DIM), BF16),
        compiler_params=_params(("parallel",)),
        name="attn_sample",
    )(qcat, cache_ckv, cache_krope, kcat, wuv)


def _cumsum_rows(x):
    c = x.shape[0]
    row = lax.broadcasted_iota(jnp.int32, x.shape, 0)
    s = 1
    while s < c:
        x = x + jnp.where(row >= s, pltpu.roll(x, s, 0), 0.0)
        s *= 2
    return x


def _gla_kernel(q_ref, k_ref, v_ref, la_ref, sgr_ref, gn_ref, s0_ref, o_ref, sout_ref, st_sc, *, c):
    j = pl.program_id(1)

    @pl.when(j == 0)
    def _():
        st_sc[...] = s0_ref[...]

    half = 32
    row = lax.broadcasted_iota(jnp.int32, (c, GLA_DK), 0)
    rr = lax.broadcasted_iota(jnp.int32, (c, c), 0)
    cc = lax.broadcasted_iota(jnp.int32, (c, c), 1)
    causal = cc <= rr
    gn = gn_ref[...]
    for s, h in [(s, h) for s in range(GLA_BATCH) for h in range(GLA_HEADS)]:
        ks = slice(h * GLA_DK, (h + 1) * GLA_DK)
        vs = slice(h * GLA_DV, (h + 1) * GLA_DV)
        q = q_ref[s, :, ks].astype(F32)
        k = k_ref[s, :, ks].astype(F32)
        v = v_ref[s, :, vs]
        b = _cumsum_rows(la_ref[s, :, ks])
        if c > half:
            mid = jnp.where(row < half, b[half // 2 - 1:half // 2, :], b[half + half // 2 - 1:half + half // 2, :])
        else:
            mid = jnp.broadcast_to(b[c // 2 - 1:c // 2, :], b.shape)
        qe = (q * jnp.exp(b - mid)).astype(BF16)
        ke = (k * jnp.exp(mid - b)).astype(BF16)
        att = lax.dot_general(qe, ke, _NT, preferred_element_type=F32)
        if c > half:
            edge = b[half - 1:half, :]
            qo = (q * jnp.exp(jnp.minimum(b - edge, 0.0))).astype(BF16)
            ko = (k * jnp.exp(jnp.minimum(edge - b, 0.0))).astype(BF16)
            att_off = lax.dot_general(qo, ko, _NT, preferred_element_type=F32)
            same = (rr < half) == (cc < half)
            att = jnp.where(same, att, att_off)
        att = jnp.where(causal, att, 0.0).astype(BF16)
        st = st_sc[s, h]
        q0 = (q * jnp.exp(b)).astype(BF16)
        o = (jnp.dot(q0, st.astype(BF16), preferred_element_type=F32)
             + jnp.dot(att, v, preferred_element_type=F32))
        last = b[c - 1:c, :]
        kd = (k * jnp.exp(last - b)).astype(BF16)
        upd = lax.dot_general(kd, v, _TN, preferred_element_type=F32)
        dcol = jnp.exp(jnp.transpose(jnp.broadcast_to(last, (GLA_DK, GLA_DK))))
        st_sc[s, h] = jnp.concatenate([dcol, dcol], axis=1) * st + upd
        on = _rms(o, gn) * sgr_ref[s, :, vs].astype(F32)
        o_ref[s, :, vs] = on.astype(BF16)

    @pl.when(j == pl.num_programs(1) - 1)
    def _():
        sout_ref[...] = st_sc[...]


def _gla(gq, gk, gv, la, sgr, gn, s0, b, l):
    c = min(GLA_CHUNK, l)
    nc = l // c
    nqk = GLA_HEADS * GLA_DK
    nv = GLA_HEADS * GLA_DV
    r3 = lambda a: a.reshape(b, l, a.shape[-1])
    assert b % GLA_BATCH == 0
    tok = lambda width: pl.BlockSpec((GLA_BATCH, c, width), lambda bb, j: (bb, j, 0))
    st = pl.BlockSpec((GLA_BATCH, GLA_HEADS, GLA_DK, GLA_DV), lambda bb, j: (bb, 0, 0, 0))
    return pl.pallas_call(
        functools.partial(_gla_kernel, c=c),
        grid=(b // GLA_BATCH, nc),
        in_specs=[tok(nqk), tok(nqk), tok(nv), tok(nqk), tok(nv), _full(gn.shape), st],
        out_specs=[tok(nv), st],
        out_shape=[jax.ShapeDtypeStruct((b, l, nv), BF16),
                   jax.ShapeDtypeStruct((b, GLA_HEADS, GLA_DK, GLA_DV), F32)],
        scratch_shapes=[pltpu.VMEM((GLA_BATCH, GLA_HEADS, GLA_DK, GLA_DV), F32)],
        compiler_params=_params(("parallel", "arbitrary")),
        name="gla",
    )(r3(gq), r3(gk), r3(gv), r3(la), r3(sgr), gn, s0)


def _merge_kernel(oa_ref, ob_ref, sga_ref, sgb_ref, x_ref, wa_ref, wb_ref, wo_ref, gffn_ref,
                  wr_ref, br_ref, x1_ref, h2_ref, route_ref, gcol_ref, cnt_ref, run_sc):
    ya = jnp.dot(oa_ref[...], wa_ref[...], preferred_element_type=F32)
    yb = jnp.dot(ob_ref[...], wb_ref[...], preferred_element_type=F32)
    merged = (sga_ref[...].astype(F32) * ya + sgb_ref[...].astype(F32) * yb).astype(BF16)
    x1 = x_ref[...] + jnp.dot(merged, wo_ref[...], preferred_element_type=F32)
    x1_ref[...] = x1
    h2 = _rms(x1, gffn_ref[...])
    _store_token_major(h2_ref, h2)

    hi = h2.astype(BF16)
    lo = (h2 - hi.astype(F32)).astype(BF16)
    wr = wr_ref[...]
    whi = wr.astype(BF16)
    wlo = (wr - whi.astype(F32)).astype(BF16)
    logits = (lax.dot_general(whi, hi, _NT, preferred_element_type=F32)
              + lax.dot_general(whi, lo, _NT, preferred_element_type=F32)
              + lax.dot_general(wlo, hi, _NT, preferred_element_type=F32)) + br_ref[...]
    tm = logits.shape[1]
    ridx = lax.broadcasted_iota(jnp.int32, (EXPERTS_PER_GROUP, tm), 0)
    big = jnp.int32(1 << 20)

    def top(vals):
        vmax = jnp.max(vals, axis=0, keepdims=True)
        imax = jnp.min(jnp.where(vals == vmax, ridx, big), axis=0, keepdims=True)
        return vmax, imax

    gl = jnp.where(ridx < N_GROUPS, logits[N_EXPERTS:N_EXPERTS + EXPERTS_PER_GROUP], -jnp.inf)
    gmax, g_top = top(gl)
    p_top = 1.0 / jnp.sum(jnp.exp(gl - gmax), axis=0, keepdims=True)
    e_sel = logits[:EXPERTS_PER_GROUP]
    for g in range(1, N_GROUPS):
        e_sel = jnp.where(g_top == g, logits[g * EXPERTS_PER_GROUP:(g + 1) * EXPERTS_PER_GROUP], e_sel)
    v1, i1 = top(e_sel)
    v2, i2 = top(jnp.where(ridx == i1, -jnp.inf, e_sel))
    e21 = jnp.exp(v2 - v1)
    w1 = p_top / (1.0 + e21)
    w2 = p_top * e21 / (1.0 + e21)
    base = g_top * EXPERTS_PER_GROUP
    id0 = base + i1
    id1 = base + i2
    gates = jnp.where(ridx == 0, w1, jnp.where(ridx == 1, w2, 0.0))
    gcol_ref[...] = jnp.transpose(jnp.concatenate([gates, jnp.zeros((LANES - EXPERTS_PER_GROUP, tm), F32)], axis=0))

    @pl.when(pl.program_id(0) == 0)
    def _():
        run_sc[...] = jnp.zeros(run_sc.shape, F32)

    eidx = lax.broadcasted_iota(jnp.int32, (N_EXPERTS, tm), 0)
    oh0 = jnp.where(eidx == id0, 1.0, 0.0)
    oh1 = jnp.where(eidx == id1, 1.0, 0.0)
    earlier = jnp.where(lax.broadcasted_iota(jnp.int32, (tm, tm), 0) < lax.broadcasted_iota(jnp.int32, (tm, tm), 1),
                        1.0, 0.0).astype(BF16)
    p0 = jnp.dot(oh0.astype(BF16), earlier, preferred_element_type=F32)
    p1 = jnp.dot(oh1.astype(BF16), earlier, preferred_element_type=F32)
    c0 = jnp.sum(oh0, axis=1, keepdims=True)
    c1 = jnp.sum(oh1, axis=1, keepdims=True)
    run = run_sc[...]
    rank0 = jnp.sum(oh0 * (run + p0), axis=0, keepdims=True)
    rank1 = jnp.sum(oh1 * (run + c0 + p1), axis=0, keepdims=True)
    run = run + c0 + c1
    run_sc[...] = run
    cnt_ref[...] = jnp.broadcast_to(run, cnt_ref.shape)
    route_ref[...] = jnp.where(ridx == 0, id0, jnp.where(ridx == 1, id1, jnp.where(
        ridx == 2, rank0.astype(jnp.int32), jnp.where(ridx == 3, rank1.astype(jnp.int32), 0))))


def _merge(oa, ob, sga, sgb, x2, w):
    n = x2.shape[0]
    tm = min(MERGE_TILE, n)
    assert n % tm == 0
    row = lambda width: pl.BlockSpec((tm, width), lambda i: (i, 0))
    weights = [w["w_a"], w["w_b"], w["w_o"], w["g_ffn"], w["w_router"], w["b_router"]]
    return pl.pallas_call(
        _merge_kernel,
        grid=(n // tm,),
        in_specs=[row(MLA_HEADS * V_DIM), row(GLA_HEADS * GLA_DV), row(D_MODEL), row(D_MODEL),
                  row(D_MODEL)] + [_full(a.shape) for a in weights],
        out_specs=[row(D_MODEL), pl.BlockSpec((tm * SLABS, LANES), lambda i: (i, 0)),
                   pl.BlockSpec((EXPERTS_PER_GROUP, tm), lambda i: (0, i)), row(LANES),
                   _full((N_EXPERTS, LANES))],
        out_shape=[jax.ShapeDtypeStruct((n, D_MODEL), F32), jax.ShapeDtypeStruct((n * SLABS, LANES), F32),
                   jax.ShapeDtypeStruct((EXPERTS_PER_GROUP, n), jnp.int32),
                   jax.ShapeDtypeStruct((n, LANES), F32),
                   jax.ShapeDtypeStruct((N_EXPERTS, LANES), F32)],
        scratch_shapes=[pltpu.VMEM((N_EXPERTS, 1), F32)],
        compiler_params=_params(("arbitrary",)),
        name="merge",
    )(oa, ob, sga, sgb, x2, *weights)


ROW_DMA_UNROLL = 8


def _row_tile(ref, r):
    return ref.at[pl.ds(pl.multiple_of(r * SLABS, SLABS), SLABS), :]


def _scatter_kernel(ps_ref, cnt_ref, pe_ref, dest_ref, h2_ref, xd_hbm, zero_sc, sem, zsem):
    tm = h2_ref.shape[0] // SLABS

    @pl.when(pl.program_id(0) == 0)
    def _():
        zero_sc[...] = jnp.zeros(zero_sc.shape, F32)

        def pad_copies(e, act):
            lo = ps_ref[e] + cnt_ref[e]
            pad = pe_ref[e] - lo
            bit = FFN_ROWS // 2
            while bit >= 1:
                @pl.when((pad & bit) != 0)
                def _(bit=bit):
                    first = lo + (pad & ~(2 * bit - 1))
                    act(pltpu.make_async_copy(
                        zero_sc.at[pl.ds(0, bit * SLABS), :],
                        xd_hbm.at[pl.ds(pl.multiple_of(first * SLABS, SLABS), bit * SLABS), :], zsem.at[0]))
                bit //= 2

        def start_pads(e, c):
            pad_copies(e, lambda cp: cp.start())
            return c

        def wait_pads(e, c):
            pad_copies(e, lambda cp: cp.wait())
            return c

        lax.fori_loop(0, N_EXPERTS, start_pads, 0)
        lax.fori_loop(0, N_EXPERTS, wait_pads, 0)

        def block_copy(b):
            start_row = pl.multiple_of(b * (FFN_ROWS * SLABS), FFN_ROWS * SLABS)
            return pltpu.make_async_copy(zero_sc, xd_hbm.at[pl.ds(start_row, FFN_ROWS * SLABS), :], zsem.at[0])

        first_unused = pe_ref[N_EXPERTS - 1] // FFN_ROWS
        n_blocks = xd_hbm.shape[0] // (FFN_ROWS * SLABS)

        def start_block(b, c):
            block_copy(b).start()
            return c

        def wait_block(b, c):
            block_copy(b).wait()
            return c

        lax.fori_loop(first_unused, n_blocks, start_block, 0)
        lax.fori_loop(first_unused, n_blocks, wait_block, 0)

    def body(t, carry):
        for k in range(2):
            d = dest_ref[0, 0, k * tm + t]
            pltpu.make_async_copy(_row_tile(h2_ref, t), _row_tile(xd_hbm, d), sem.at[0]).start(priority=k)
        return carry

    lax.fori_loop(0, tm, body, 0, unroll=ROW_DMA_UNROLL)
    for k in range(2):
        pltpu.make_async_copy(h2_ref, xd_hbm.at[pl.ds(0, tm * SLABS), :], sem.at[0]).wait()


def _scatter(pad_start, counts, pad_end, dest_blocks, h2, total_rows):
    tm = ROW_TILE
    nt = dest_blocks.shape[0]
    grid_spec = pltpu.PrefetchScalarGridSpec(
        num_scalar_prefetch=3,
        grid=(nt,),
        in_specs=[pl.BlockSpec((1, 1, 2 * tm), lambda i, *_: (i, 0, 0), memory_space=pltpu.SMEM),
                  pl.BlockSpec((tm * SLABS, LANES), lambda i, *_: (i, 0))],
        out_specs=pl.BlockSpec(memory_space=pl.ANY),
        scratch_shapes=[pltpu.VMEM((FFN_ROWS * SLABS, LANES), F32), pltpu.SemaphoreType.DMA((1,)),
                        pltpu.SemaphoreType.DMA((1,))],
    )
    return pl.pallas_call(
        _scatter_kernel,
        grid_spec=grid_spec,
        out_shape=jax.ShapeDtypeStruct((total_rows * SLABS, LANES), F32),
        compiler_params=_params(("arbitrary",)),
        name="moe_scatter",
    )(pad_start, counts, pad_end, dest_blocks, h2)


def _ffn_kernel(be_ref, nbu_ref, x_ref, wg_ref, wu_ref, wd_ref, out_ref):
    del be_ref

    @pl.when(pl.program_id(0) < nbu_ref[0])
    def _():
        x = _load_token_major(x_ref, FFN_ROWS).astype(BF16)
        g = jnp.dot(x, wg_ref[0], preferred_element_type=F32)
        u = jnp.dot(x, wu_ref[0], preferred_element_type=F32)
        mid = (g * _sigmoid(g) * u).astype(BF16)
        _store_token_major(out_ref, jnp.dot(mid, wd_ref[0], preferred_element_type=F32))

    @pl.when(pl.program_id(0) >= nbu_ref[0])
    def _():
        out_ref[...] = jnp.zeros(out_ref.shape, F32)


def _ffn(block_expert, nb_used, xd, wg, wu, wd):
    nb = block_expert.shape[0]
    blk = lambda i, be, nbu: (jnp.minimum(i, nbu[0] - 1), 0)
    wsel = lambda i, be, nbu: (be[jnp.minimum(i, nbu[0] - 1)], 0, 0)
    grid_spec = pltpu.PrefetchScalarGridSpec(
        num_scalar_prefetch=2,
        grid=(nb,),
        in_specs=[pl.BlockSpec((FFN_ROWS * SLABS, LANES), blk),
                  pl.BlockSpec((1, D_MODEL, EXPERT_FF), wsel),
                  pl.BlockSpec((1, D_MODEL, EXPERT_FF), wsel),
                  pl.BlockSpec((1, EXPERT_FF, D_MODEL), wsel)],
        out_specs=pl.BlockSpec((FFN_ROWS * SLABS, LANES), lambda i, be, nbu: (i, 0)),
    )
    return pl.pallas_call(
        _ffn_kernel,
        grid_spec=grid_spec,
        out_shape=jax.ShapeDtypeStruct((nb * FFN_ROWS * SLABS, LANES), F32),
        compiler_params=_params(("arbitrary",)),
        name="expert_ffn",
    )(block_expert, nb_used, xd, wg, wu, wd)


def _combine_kernel(dest_ref, x1_ref, gcol_ref, gfin_ref, eo_hbm, y_ref, buf, sem):
    tm = x1_ref.shape[0]

    def body(t, carry):
        for k in range(2):
            d = dest_ref[0, 0, k * tm + t]
            pltpu.make_async_copy(_row_tile(eo_hbm, d), _row_tile(buf.at[k], t), sem.at[0]).start(priority=k)
        return carry

    lax.fori_loop(0, tm, body, 0, unroll=ROW_DMA_UNROLL)
    for k in range(2):
        pltpu.make_async_copy(eo_hbm.at[pl.ds(0, tm * SLABS), :], buf.at[k], sem.at[0]).wait()
    g = gcol_ref[...]
    y = x1_ref[...] + (g[:, 0:1] * _load_token_major(buf.at[0], tm) + g[:, 1:2] * _load_token_major(buf.at[1], tm))
    y_ref[...] = _rms(y, gfin_ref[...])


def _combine(dest_blocks, x1, gcol, g_final, eo):
    n = x1.shape[0]
    tm = ROW_TILE
    return pl.pallas_call(
        _combine_kernel,
        grid=(n // tm,),
        in_specs=[pl.BlockSpec((1, 1, 2 * tm), lambda i: (i, 0, 0), memory_space=pltpu.SMEM),
                  pl.BlockSpec((tm, D_MODEL), lambda i: (i, 0)),
                  pl.BlockSpec((tm, LANES), lambda i: (i, 0)),
                  _full(g_final.shape),
                  pl.BlockSpec(memory_space=pl.ANY)],
        out_specs=pl.BlockSpec((tm, D_MODEL), lambda i: (i, 0)),
        out_shape=jax.ShapeDtypeStruct((n, D_MODEL), F32),
        scratch_shapes=[pltpu.VMEM((2, tm * SLABS, LANES), F32), pltpu.SemaphoreType.DMA((1,))],
        compiler_params=_params(("arbitrary",)),
        name="combine",
    )(dest_blocks, x1, gcol, g_final, eo)


def _dest_kernel(ps_ref, route_ref, o_ref):
    route = route_ref[...]
    start = jnp.zeros(route.shape, jnp.int32)
    for e in range(N_EXPERTS):
        start = jnp.where(route == e, ps_ref[e], start)
    o_ref[...] = start + pltpu.roll(route, EXPERTS_PER_GROUP - 2, 0)


def _dest_rows(pad_start, route):
    n = route.shape[1]
    tile = min(n, 8192)
    assert n % tile == 0
    grid_spec = pltpu.PrefetchScalarGridSpec(
        num_scalar_prefetch=1,
        grid=(n // tile,),
        in_specs=[pl.BlockSpec((EXPERTS_PER_GROUP, tile), lambda i, ps: (0, i))],
        out_specs=pl.BlockSpec((EXPERTS_PER_GROUP, tile), lambda i, ps: (0, i)),
    )
    return pl.pallas_call(
        _dest_kernel,
        grid_spec=grid_spec,
        out_shape=jax.ShapeDtypeStruct(route.shape, jnp.int32),
        compiler_params=_params(("arbitrary",)),
        name="moe_dest",
    )(pad_start, route)


def _layout(cnt, route, n):
    counts = cnt[:, 0].astype(jnp.int32)
    padded = (counts + FFN_ROWS - 1) // FFN_ROWS * FFN_ROWS
    pad_end = jnp.cumsum(padded).astype(jnp.int32)
    pad_start = pad_end - padded
    nb = 2 * n // FFN_ROWS + N_EXPERTS
    block_start = jnp.arange(nb, dtype=jnp.int32) * FFN_ROWS
    block_expert = jnp.minimum(jnp.sum((block_start[:, None] >= pad_end[None, :]).astype(jnp.int32), axis=1),
                               N_EXPERTS - 1).astype(jnp.int32)
    nb_used = (pad_end[N_EXPERTS - 1:] // FFN_ROWS).astype(jnp.int32)
    dest = _dest_rows(pad_start, route)[:2]
    tm = ROW_TILE
    dest_blocks = dest.reshape(2, n // tm, tm).transpose(1, 0, 2).reshape(n // tm, 1, 2 * tm)
    return pad_start, counts, pad_end, block_expert, nb_used, nb * FFN_ROWS, dest_blocks


def _rope_tables(l, past, rows):
    half = ROPE_DIM // 2
    inv = ROPE_THETA ** (-jnp.arange(half, dtype=F32) / half)
    pos = (past + (jnp.arange(rows, dtype=jnp.int32) % l)).astype(F32)
    ang = pos[:, None] * inv[None, :]
    pad = jnp.zeros((rows, LANES - ROPE_DIM), F32)
    cos = jnp.concatenate([jnp.cos(ang), jnp.cos(ang), pad], axis=1)
    sin = jnp.concatenate([-jnp.sin(ang), jnp.sin(ang), pad], axis=1)
    return cos, sin


def _group(x, ckv_past, kpe_past, s0, w, g_final):
    b, l, d = x.shape
    n = b * l
    x2 = x.reshape(n, d)
    past = 0 if ckv_past is None else ckv_past.shape[1]
    cos_t, sin_t = _rope_tables(l, past, max(l, ROW_TILE))
    prompt = ckv_past is None
    (qcat, kcat, ckv, kpe, gq, gk, gv, la, sgr, sga, sgb, *maybe_vt) = _inproj(x2, cos_t, sin_t, w, prompt)
    if prompt:
        oa = _attn_prompt(qcat, kcat, maybe_vt[0], w["w_uv"], b, l)
    else:
        oa = _attn_sample(qcat, kcat, ckv_past, kpe_past, w["w_uv"], b, l)
    ob, s_new = _gla(gq, gk, gv, la, sgr, w["g_gla"], s0, b, l)
    x1, h2, route, gcol, cnt = _merge(oa.reshape(n, -1), ob.reshape(n, -1), sga, sgb, x2, w)
    pad_start, counts, pad_end, block_expert, nb_used, total_rows, dest_blocks = _layout(cnt, route, n)
    xd = _scatter(pad_start, counts, pad_end, dest_blocks, h2, total_rows)
    eo = _ffn(block_expert, nb_used, xd, w["w_eg"], w["w_eu"], w["w_ed"])
    y = _combine(dest_blocks, x1, gcol, g_final, eo)
    return (y.reshape(b, l, d), ckv.reshape(b, l, KV_LORA), kpe.reshape(b, l, ROPE_DIM), s_new)


def _prep_weights(w_in, g_norm_mix, g_qnorm, w_uq, g_kvnorm, w_ukv, w_gate2, b_gate2, g_gla_norm,
                  w_branch_a, w_branch_b, w_out, g_norm_ffn, w_router_group, b_router_group,
                  w_router_expert, b_router_expert, w_exp_gate, w_exp_up, w_exp_down):
    nqk = GLA_HEADS * GLA_DK
    nv = GLA_HEADS * GLA_DV
    o = 0
    parts = {}
    for name, width in (("cq", Q_LORA), ("ckv", KV_LORA), ("kpe", ROPE_DIM), ("gq", nqk), ("gk", nqk),
                        ("gv", nv), ("glr", GATE_RANK), ("gr", nv), ("ga", D_MODEL), ("gb", D_MODEL)):
        parts[name] = w_in[:, o:o + width]
        o += width
    padc = lambda a, width: jnp.pad(a, ((0, 0), (0, width - a.shape[1])))
    w_small = jnp.concatenate([parts["cq"], parts["ckv"], padc(parts["kpe"], LANES),
                               padc(parts["glr"], LANES)], axis=1).astype(BF16)
    uq = w_uq.reshape(Q_LORA, MLA_HEADS, NOPE_DIM + ROPE_DIM)
    ukv = w_ukv.reshape(KV_LORA, MLA_HEADS, NOPE_DIM + V_DIM)
    lat = _fold_q(uq[:, :, :NOPE_DIM].transpose(1, 0, 2), ukv[:, :, :NOPE_DIM].transpose(1, 0, 2))
    q_rope = uq[:, :, NOPE_DIM:].transpose(1, 0, 2)
    w_qcat = jnp.concatenate([lat, q_rope, jnp.zeros((MLA_HEADS, Q_LORA, QCAT - KV_LORA - ROPE_DIM), F32)],
                             axis=2)
    w_qcat = w_qcat.transpose(1, 0, 2).reshape(Q_LORA, MLA_HEADS * QCAT).astype(BF16)
    w_router = jnp.concatenate([w_router_expert.T, w_router_group.T,
                                jnp.zeros((ROUTER_ROWS - N_EXPERTS - N_GROUPS, D_MODEL), F32)], axis=0)
    b_router = jnp.concatenate([b_router_expert, b_router_group,
                                jnp.zeros((ROUTER_ROWS - N_EXPERTS - N_GROUPS,), F32)]).reshape(ROUTER_ROWS, 1)
    return {
        "g_mix": g_norm_mix.reshape(1, D_MODEL), "w_small": w_small,
        "g_qn": g_qnorm.reshape(1, Q_LORA), "g_kvn": g_kvnorm.reshape(1, KV_LORA), "w_qcat": w_qcat,
        "w_g2": jnp.pad(w_gate2, ((0, LANES - GATE_RANK), (0, 0))).astype(BF16),
        "b_g2": b_gate2.reshape(1, nqk),
        "w_gqk": jnp.concatenate([parts["gq"], parts["gk"]], axis=1).astype(BF16),
        "w_gv": parts["gv"].astype(BF16), "w_gr": parts["gr"].astype(BF16),
        "w_ga": parts["ga"].astype(BF16), "w_gb": parts["gb"].astype(BF16),
        "w_uv": ukv[:, :, NOPE_DIM:].transpose(1, 0, 2).astype(BF16),
        "g_gla": g_gla_norm.reshape(1, GLA_DV),
        "w_a": w_branch_a.astype(BF16), "w_b": w_branch_b.astype(BF16), "w_o": w_out.astype(BF16),
        "g_ffn": g_norm_ffn.reshape(1, D_MODEL), "w_router": w_router, "b_router": b_router,
        "w_eg": w_exp_gate.astype(BF16), "w_eu": w_exp_up.astype(BF16), "w_ed": w_exp_down.astype(BF16),
    }


def kernel(x_prompt, x_sample, cache_ckv, cache_krope, state_gla, w_in, g_norm_mix, g_qnorm, w_uq, g_kvnorm, w_ukv, w_gate2, b_gate2, g_gla_norm, w_branch_a, w_branch_b, w_out, g_norm_ffn, w_router_group, b_router_group, w_router_expert, b_router_expert, w_exp_gate, w_exp_up, w_exp_down, g_norm_final):
    depth = w_in.shape[0]
    assert depth == 1, "the final norm is fused into the last layer's combine step"
    gfin = g_norm_final.reshape(1, D_MODEL)
    w = _prep_weights(w_in[0], g_norm_mix[0], g_qnorm[0], w_uq[0], g_kvnorm[0], w_ukv[0], w_gate2[0],
                      b_gate2[0], g_gla_norm[0], w_branch_a[0], w_branch_b[0], w_out[0], g_norm_ffn[0],
                      w_router_group[0], b_router_group[0], w_router_expert[0], b_router_expert[0],
                      w_exp_gate[0], w_exp_up[0], w_exp_down[0])
    bp = x_prompt.shape[0]
    zero_state = jnp.zeros((bp, GLA_HEADS, GLA_DK, GLA_DV), F32)
    yp, c1, k1, s1 = _group(x_prompt, None, None, zero_state, w, gfin)
    ys, c2, k2, s2 = _group(x_sample, cache_ckv[0], cache_krope[0], state_gla[0].astype(F32), w, gfin)
    return (yp, ys, c1[None], k1[None], s1[None], c2[None], k2[None], s2[None])
```

```python
import functools
import math

import jax
import jax.numpy as jnp
from jax import lax
from jax.experimental import pallas as pl
from jax.experimental.pallas import tpu as pltpu

F32 = jnp.float32
BF16 = jnp.bfloat16

D_MODEL = 1024
CHUNK = 64
EPS = 1e-6
MLA_HEADS = 8
Q_LORA = 256
KV_LORA = 128
NOPE_DIM = 64
ROPE_DIM = 32
V_DIM = 64
ROPE_THETA = 10000.0
GLA_HEADS = 4
GLA_DK = 128
GLA_DV = 256
GATE_RANK = 16
GATE_TEMP = 16.0
N_GROUPS = 4
EXPERTS_PER_GROUP = 8
N_EXPERTS = 32
EXPERT_FF = 512

LANES = 128
QCAT = 2 * LANES
ROW_TILE = 256
MERGE_TILE = 512
GLA_BATCH = 2
GLA_STEP_TOKENS = 256
Q_BLOCK = ROW_TILE
KEY_TILE = ROW_TILE
SUM_ROWS = 16
GLA_CHUNK = 64
FFN_ROWS = 256
ROUTER_ROWS = 48
VMEM_LIMIT = 56 * 1024 * 1024
LOG2E = 1.4426950408889634
ATT_SCALE = LOG2E / math.sqrt(NOPE_DIM + ROPE_DIM)

_NT = (((1,), (1,)), ((), ()))
_TN = (((0,), (0,)), ((), ()))


def _params(sem):
    return pltpu.CompilerParams(dimension_semantics=sem, vmem_limit_bytes=VMEM_LIMIT)


def _rms(x, g):
    return x * lax.rsqrt(jnp.mean(x * x, axis=-1, keepdims=True) + EPS) * g


def _sigmoid(x):
    return 1.0 / (1.0 + jnp.exp(-x))


def _full(shape):
    n = len(shape)
    return pl.BlockSpec(shape, lambda *_: (0,) * n)


SLABS = D_MODEL // LANES


def _store_token_major(ref, x):
    rows = x.shape[0]
    for j in range(SLABS):
        ref[pl.ds(j, rows, stride=SLABS), :] = x[:, j * LANES:(j + 1) * LANES]


def _load_token_major(ref, rows):
    return jnp.concatenate([ref[pl.ds(j, rows, stride=SLABS), :] for j in range(SLABS)], axis=1)


def _fold_q_kernel(wq_ref, wk_ref, o_ref):
    o_ref[0] = lax.dot_general(wq_ref[0], wk_ref[0], _NT, precision=lax.Precision.HIGHEST,
                               preferred_element_type=F32)


def _fold_q(wq_nope, wk_nope):
    return pl.pallas_call(
        _fold_q_kernel,
        grid=(MLA_HEADS,),
        in_specs=[pl.BlockSpec((1, Q_LORA, NOPE_DIM), lambda h: (h, 0, 0)),
                  pl.BlockSpec((1, KV_LORA, NOPE_DIM), lambda h: (h, 0, 0))],
        out_specs=pl.BlockSpec((1, Q_LORA, KV_LORA), lambda h: (h, 0, 0)),
        out_shape=jax.ShapeDtypeStruct((MLA_HEADS, Q_LORA, KV_LORA), F32),
        compiler_params=_params(("arbitrary",)),
        name="fold_q",
    )(wq_nope, wk_nope)


def _inproj_kernel(x_ref, cos_ref, sin_ref, gmix_ref, wsm_ref, gqn_ref, gkvn_ref, wqc_ref,
                   wg2_ref, bg2_ref, wgqk_ref, wgv_ref, wgr_ref, wga_ref, wgb_ref,
                   qcat_ref, kcat_ref, ckv_ref, kpe_ref, gq_ref, gk_ref, gv_ref, la_ref,
                   sgr_ref, sga_ref, sgb_ref, *maybe_vt_ref, transposed):
    hb = _rms(x_ref[...], gmix_ref[...]).astype(BF16)
    zs = jnp.dot(hb, wsm_ref[...], preferred_element_type=F32)
    cos = cos_ref[...]
    sin = sin_ref[...]
    first_half = lax.broadcasted_iota(jnp.int32, cos.shape, 1) < ROPE_DIM // 2

    def rope(v):
        rot = jnp.where(first_half, pltpu.roll(v, LANES - ROPE_DIM // 2, 1),
                        pltpu.roll(v, ROPE_DIM // 2, 1))
        return v * cos + rot * sin

    cqn = _rms(zs[:, :Q_LORA], gqn_ref[...]).astype(BF16)
    qc = jnp.dot(cqn, wqc_ref[...], preferred_element_type=F32)
    for h in range(MLA_HEADS):
        lat = qc[:, h * QCAT:h * QCAT + LANES]
        pe = qc[:, h * QCAT + LANES:(h + 1) * QCAT]
        if transposed:
            lat_t = jnp.transpose(lat * ATT_SCALE).astype(BF16)
            pe_t = jnp.transpose(rope(pe) * ATT_SCALE).astype(BF16)
            for blk in range(lat.shape[0] // Q_BLOCK):
                cols = slice(blk * Q_BLOCK, (blk + 1) * Q_BLOCK)
                qcat_ref[blk, :LANES, h * Q_BLOCK:(h + 1) * Q_BLOCK] = lat_t[:, cols]
                qcat_ref[blk, LANES:, h * Q_BLOCK:(h + 1) * Q_BLOCK] = pe_t[:, cols]
        else:
            qcat_ref[h, :, :LANES] = (lat * ATT_SCALE).astype(BF16)
            qcat_ref[h, :, LANES:] = (rope(pe) * ATT_SCALE).astype(BF16)

    ckv = _rms(zs[:, Q_LORA:Q_LORA + KV_LORA], gkvn_ref[...])
    ckv_ref[...] = ckv
    if transposed:
        maybe_vt_ref[0][0] = jnp.transpose(ckv).astype(BF16)
    kpe = rope(zs[:, Q_LORA + KV_LORA:Q_LORA + KV_LORA + LANES])
    kpe_ref[...] = kpe[:, :ROPE_DIM]
    kcat_ref[:, :LANES] = ckv.astype(BF16)
    kcat_ref[:, LANES:] = kpe.astype(BF16)

    glr = zs[:, Q_LORA + KV_LORA + LANES:].astype(BF16)
    xg = jnp.dot(glr, wg2_ref[...], preferred_element_type=F32) + bg2_ref[...]
    la_ref[...] = (jnp.minimum(xg, 0.0) - jnp.log(1.0 + jnp.exp(-jnp.abs(xg)))) * (1.0 / GATE_TEMP)

    zqk = jnp.dot(hb, wgqk_ref[...], preferred_element_type=F32)
    nqk = GLA_HEADS * GLA_DK
    gq_ref[...] = (zqk[:, :nqk] * (GLA_DK ** -0.5)).astype(BF16)
    gk_ref[...] = zqk[:, nqk:].astype(BF16)
    gv_ref[...] = jnp.dot(hb, wgv_ref[...], preferred_element_type=F32).astype(BF16)
    gr = jnp.dot(hb, wgr_ref[...], preferred_element_type=F32)
    sgr_ref[...] = (gr * _sigmoid(gr)).astype(BF16)
    sga_ref[...] = _sigmoid(jnp.dot(hb, wga_ref[...], preferred_element_type=F32)).astype(BF16)
    sgb_ref[...] = _sigmoid(jnp.dot(hb, wgb_ref[...], preferred_element_type=F32)).astype(BF16)


def _inproj(x2, cos_t, sin_t, w, transposed):
    n = x2.shape[0]
    tm = ROW_TILE
    nt = n // tm
    tab_blocks = cos_t.shape[0] // tm
    row = lambda width: pl.BlockSpec((tm, width), lambda i: (i, 0))
    tab = pl.BlockSpec((tm, LANES), lambda i: (i % tab_blocks, 0))
    nqk = GLA_HEADS * GLA_DK
    nv = GLA_HEADS * GLA_DV
    weights = [w["g_mix"], w["w_small"], w["g_qn"], w["g_kvn"], w["w_qcat"], w["w_g2"], w["b_g2"],
               w["w_gqk"], w["w_gv"], w["w_gr"], w["w_ga"], w["w_gb"]]
    out_shape = [
        jax.ShapeDtypeStruct((MLA_HEADS, n, QCAT), BF16),
        jax.ShapeDtypeStruct((n, QCAT), BF16),
        jax.ShapeDtypeStruct((n, KV_LORA), F32),
        jax.ShapeDtypeStruct((n, ROPE_DIM), F32),
        jax.ShapeDtypeStruct((n, nqk), BF16),
        jax.ShapeDtypeStruct((n, nqk), BF16),
        jax.ShapeDtypeStruct((n, nv), BF16),
        jax.ShapeDtypeStruct((n, nqk), F32),
        jax.ShapeDtypeStruct((n, nv), BF16),
        jax.ShapeDtypeStruct((n, D_MODEL), BF16),
        jax.ShapeDtypeStruct((n, D_MODEL), BF16),
    ]
    out_specs = [
        pl.BlockSpec((MLA_HEADS, tm, QCAT), lambda i: (0, i, 0)),
        row(QCAT), row(KV_LORA), row(ROPE_DIM), row(nqk), row(nqk), row(nv), row(nqk), row(nv),
        row(D_MODEL), row(D_MODEL),
    ]
    if transposed:
        qb = tm // Q_BLOCK
        out_shape[0] = jax.ShapeDtypeStruct((n // Q_BLOCK, QCAT, MLA_HEADS * Q_BLOCK), BF16)
        out_specs[0] = pl.BlockSpec((qb, QCAT, MLA_HEADS * Q_BLOCK), lambda i: (i, 0, 0))
        out_shape.append(jax.ShapeDtypeStruct((nt, KV_LORA, tm), BF16))
        out_specs.append(pl.BlockSpec((1, KV_LORA, tm), lambda i: (i, 0, 0)))
    return pl.pallas_call(
        functools.partial(_inproj_kernel, transposed=transposed),
        grid=(nt,),
        in_specs=[row(D_MODEL), tab, tab] + [_full(a.shape) for a in weights],
        out_specs=out_specs,
        out_shape=out_shape,
        compiler_params=_params(("parallel",)),
        name="inproj",
    )(x2, cos_t, sin_t, *weights)


def _head_out(o_lat, wuv_ref, rows):
    outs = []
    for h in range(MLA_HEADS):
        oh = o_lat[h * rows:(h + 1) * rows].astype(BF16)
        outs.append(jnp.dot(oh, wuv_ref[h], preferred_element_type=F32))
    return jnp.concatenate(outs, axis=1)


def _attn_prompt_kernel(qt_ref, k_ref, vt_ref, bias_ref, wuv_ref, o_ref, m_sc, l_sc, acc_sc):
    i = pl.program_id(1)
    m_sc[...] = jnp.full(m_sc.shape, -jnp.inf, F32)
    l_sc[...] = jnp.zeros(l_sc.shape, F32)
    acc_sc[...] = jnp.zeros(acc_sc.shape, F32)
    heads = [slice(h * Q_BLOCK, (h + 1) * Q_BLOCK) for h in range(MLA_HEADS)]
    ones_rows = jnp.ones((SUM_ROWS, KEY_TILE), BF16)

    def process(tiles, diagonal_last):
        nt = len(tiles)
        kbs = [k_ref[0, pl.ds(pl.multiple_of(kt * KEY_TILE, KEY_TILE), KEY_TILE), :] for kt in tiles]
        vts = [jnp.concatenate([vt_ref[kt], ones_rows], axis=0) for kt in tiles]
        ss = [[None] * MLA_HEADS for _ in range(nt)]
        for t in range(nt):
            for h, hs in enumerate(heads):
                s = jnp.dot(kbs[t], qt_ref[0, :, hs], preferred_element_type=F32)
                if diagonal_last and t == nt - 1:
                    s = s + bias_ref[...]
                ss[t][h] = s
        tmax = [[jnp.max(ss[t][h], axis=0, keepdims=True) for h in range(MLA_HEADS)] for t in range(nt)]
        scale = [[None] * MLA_HEADS for _ in range(nt)]
        mrun = [[None] * MLA_HEADS for _ in range(nt)]
        for h, hs in enumerate(heads):
            m = m_sc[:, hs]
            for t in range(nt):
                m_new = jnp.maximum(m, tmax[t][h])
                scale[t][h] = jnp.exp2(m - m_new)
                mrun[t][h] = m_new
                m = m_new
            m_sc[:, hs] = m
        pvs = [[jnp.dot(vts[t], jnp.exp2(ss[t][h] - mrun[t][h]).astype(BF16), preferred_element_type=F32)
                for h in range(MLA_HEADS)] for t in range(nt)]
        for h, hs in enumerate(heads):
            l, acc = l_sc[:, hs], acc_sc[:, hs]
            for t in range(nt):
                acc = scale[t][h] * acc + pvs[t][h][:KV_LORA]
                l = scale[t][h] * l + pvs[t][h][KV_LORA:KV_LORA + 1]
            l_sc[:, hs], acc_sc[:, hs] = l, acc

    def body(j, carry):
        process([2 * j, 2 * j + 1], False)
        return carry

    lax.fori_loop(0, i // 2, body, 0)

    @pl.when(i % 2 == 1)
    def _():
        process([i - 1], False)

    process([i], True)

    o_lat_t = (acc_sc[...] / l_sc[...]).astype(BF16)
    outs = []
    for h, hs in enumerate(heads):
        outs.append(lax.dot_general(o_lat_t[:, hs], wuv_ref[h], _TN, preferred_element_type=F32))
    o_ref[0] = jnp.concatenate(outs, axis=1).astype(BF16)


def _attn_prompt(qt, kcat, vt, wuv, b, l):
    assert Q_BLOCK == KEY_TILE and Q_BLOCK % CHUNK == 0
    nq = l // Q_BLOCK
    nkt = l // KEY_TILE
    cols = MLA_HEADS * Q_BLOCK
    pos_chunk = jnp.arange(Q_BLOCK, dtype=jnp.int32) // CHUNK
    bias = jnp.where(pos_chunk[:, None] <= pos_chunk[None, :], 0.0, -jnp.inf).astype(F32)
    return pl.pallas_call(
        _attn_prompt_kernel,
        grid=(b, nq),
        in_specs=[pl.BlockSpec((1, QCAT, cols), lambda bb, i: (bb * nq + i, 0, 0)),
                  pl.BlockSpec((1, l, QCAT), lambda bb, i: (bb, 0, 0)),
                  pl.BlockSpec((nkt, KV_LORA, KEY_TILE), lambda bb, i: (bb, 0, 0)),
                  _full(bias.shape), _full(wuv.shape)],
        out_specs=pl.BlockSpec((1, Q_BLOCK, MLA_HEADS * V_DIM), lambda bb, i: (bb, i, 0)),
        out_shape=jax.ShapeDtypeStruct((b, l, MLA_HEADS * V_DIM), BF16),
        scratch_shapes=[pltpu.VMEM((1, cols), F32), pltpu.VMEM((1, cols), F32),
                        pltpu.VMEM((KV_LORA, cols), F32)],
        compiler_params=_params(("parallel", "arbitrary")),
        name="attn_prompt",
    )(qt, kcat.reshape(b, l, QCAT), vt, bias, wuv)


def _attn_sample_kernel(q_ref, cckv_ref, ckr_ref, kn_ref, wuv_ref, o_ref, *, ls):
    q = q_ref[...].reshape(MLA_HEADS * ls, QCAT)
    ck = cckv_ref[0].astype(BF16)
    kr = ckr_ref[0].astype(BF16)
    kn = kn_ref[...]
    s_c = (lax.dot_general(q[:, :KV_LORA], ck, _NT, preferred_element_type=F32)
           + lax.dot_general(q[:, KV_LORA:KV_LORA + ROPE_DIM], kr, _NT, preferred_element_type=F32))
    s_n = lax.dot_general(q, kn, _NT, preferred_element_type=F32)
    m = jnp.maximum(jnp.max(s_c, axis=1, keepdims=True), jnp.max(s_n, axis=1, keepdims=True))
    p_c = jnp.exp2(s_c - m)
    p_n = jnp.exp2(s_n - m)
    den = jnp.sum(p_c, axis=1, keepdims=True) + jnp.sum(p_n, axis=1, keepdims=True)
    o_lat = (jnp.dot(p_c.astype(BF16), ck, preferred_element_type=F32)
             + jnp.dot(p_n.astype(BF16), kn[:, :KV_LORA], preferred_element_type=F32)) / den
    o_ref[0] = _head_out(o_lat, wuv_ref, ls).astype(BF16)


def _attn_sample(qcat, kcat, cache_ckv, cache_krope, wuv, b, ls):
    past = cache_ckv.shape[1]
    assert past % CHUNK == 0 and ls <= CHUNK
    return pl.pallas_call(
        functools.partial(_attn_sample_kernel, ls=ls),
        grid=(b,),
        in_specs=[pl.BlockSpec((MLA_HEADS, ls, QCAT), lambda bb: (0, bb, 0)),
                  pl.BlockSpec((1, past, KV_LORA), lambda bb: (bb, 0, 0)),
                  pl.BlockSpec((1, past, ROPE_DIM), lambda bb: (bb, 0, 0)),
                  pl.BlockSpec((ls, QCAT), lambda bb: (bb, 0)),
                  _full(wuv.shape)],
        out_specs=pl.BlockSpec((1, ls, MLA_HEADS * V_DIM), lambda bb: (bb, 0, 0)),
        out_shape=jax.ShapeDtypeStruct((b, ls, MLA_HEADS * V_DIM), BF16),
        compiler_params=_params(("parallel",)),
        name="attn_sample",
    )(qcat, cache_ckv, cache_krope, kcat, wuv)


def _cumsum_rows(x):
    c = x.shape[0]
    row = lax.broadcasted_iota(jnp.int32, x.shape, 0)
    s = 1
    while s < c:
        x = x + jnp.where(row >= s, pltpu.roll(x, s, 0), 0.0)
        s *= 2
    return x


def _gla_kernel(q_ref, k_ref, v_ref, la_ref, sgr_ref, gn_ref, s0_ref, o_ref, sout_ref, st_sc, *, c):
    j = pl.program_id(1)

    @pl.when(j == 0)
    def _():
        st_sc[...] = s0_ref[...]

    half = 32
    row = lax.broadcasted_iota(jnp.int32, (c, GLA_DK), 0)
    rr = lax.broadcasted_iota(jnp.int32, (c, c), 0)
    cc = lax.broadcasted_iota(jnp.int32, (c, c), 1)
    causal = cc <= rr
    same_half = (rr < half) == (cc < half)
    gn = gn_ref[...]
    n_chunks = q_ref.shape[1] // c
    seq_heads = [(s, h) for s in range(GLA_BATCH) for h in range(GLA_HEADS)]
    chains = [(s, h, ci) for s, h in seq_heads for ci in range(n_chunks)]
    ksl = lambda h: slice(h * GLA_DK, (h + 1) * GLA_DK)
    vsl = lambda h: slice(h * GLA_DV, (h + 1) * GLA_DV)
    rsl = lambda ci: slice(ci * c, (ci + 1) * c)

    prep = {}
    for s, h, ci in chains:
        q = q_ref[s, rsl(ci), ksl(h)].astype(F32)
        k = k_ref[s, rsl(ci), ksl(h)].astype(F32)
        b = _cumsum_rows(la_ref[s, rsl(ci), ksl(h)])
        if c > half:
            mid = jnp.where(row < half, b[half // 2 - 1:half // 2, :], b[half + half // 2 - 1:half + half // 2, :])
        else:
            mid = jnp.broadcast_to(b[c // 2 - 1:c // 2, :], b.shape)
        last = b[c - 1:c, :]
        ops = {"qe": (q * jnp.exp(b - mid)).astype(BF16), "ke": (k * jnp.exp(mid - b)).astype(BF16),
               "q0": (q * jnp.exp(b)).astype(BF16), "kd": (k * jnp.exp(last - b)).astype(BF16), "last": last}
        if c > half:
            edge = b[half - 1:half, :]
            ops["qo"] = (q * jnp.exp(jnp.minimum(b - edge, 0.0))).astype(BF16)
            ops["ko"] = (k * jnp.exp(jnp.minimum(edge - b, 0.0))).astype(BF16)
        prep[s, h, ci] = ops
    att = {}
    for key in chains:
        a = lax.dot_general(prep[key]["qe"], prep[key]["ke"], _NT, preferred_element_type=F32)
        if c > half:
            a_off = lax.dot_general(prep[key]["qo"], prep[key]["ko"], _NT, preferred_element_type=F32)
            a = jnp.where(same_half, a, a_off)
        att[key] = jnp.where(causal, a, 0.0).astype(BF16)
    intra, upd, decay = {}, {}, {}
    for s, h, ci in chains:
        v = v_ref[s, rsl(ci), vsl(h)]
        intra[s, h, ci] = jnp.dot(att[s, h, ci], v, preferred_element_type=F32)
        upd[s, h, ci] = lax.dot_general(prep[s, h, ci]["kd"], v, _TN, preferred_element_type=F32)
        dcol = jnp.exp(jnp.transpose(jnp.broadcast_to(prep[s, h, ci]["last"], (GLA_DK, GLA_DK))))
        decay[s, h, ci] = jnp.concatenate([dcol, dcol], axis=1)
    for s, h in seq_heads:
        st = st_sc[s, h]
        for ci in range(n_chunks):
            o = jnp.dot(prep[s, h, ci]["q0"], st.astype(BF16), preferred_element_type=F32) + intra[s, h, ci]
            st = decay[s, h, ci] * st + upd[s, h, ci]
            on = _rms(o, gn) * sgr_ref[s, rsl(ci), vsl(h)].astype(F32)
            o_ref[s, rsl(ci), vsl(h)] = on.astype(BF16)
        st_sc[s, h] = st

    @pl.when(j == pl.num_programs(1) - 1)
    def _():
        sout_ref[...] = st_sc[...]


def _gla(gq, gk, gv, la, sgr, gn, s0, b, l):
    c = min(GLA_CHUNK, l)
    step = min(GLA_STEP_TOKENS, l)
    nc = l // step
    nqk = GLA_HEADS * GLA_DK
    nv = GLA_HEADS * GLA_DV
    r3 = lambda a: a.reshape(b, l, a.shape[-1])
    assert b % GLA_BATCH == 0 and l % step == 0 and step % c == 0
    tok = lambda width: pl.BlockSpec((GLA_BATCH, step, width), lambda bb, j: (bb, j, 0))
    st = pl.BlockSpec((GLA_BATCH, GLA_HEADS, GLA_DK, GLA_DV), lambda bb, j: (bb, 0, 0, 0))
    return pl.pallas_call(
        functools.partial(_gla_kernel, c=c),
        grid=(b // GLA_BATCH, nc),
        in_specs=[tok(nqk), tok(nqk), tok(nv), tok(nqk), tok(nv), _full(gn.shape), st],
        out_specs=[tok(nv), st],
        out_shape=[jax.ShapeDtypeStruct((b, l, nv), BF16),
                   jax.ShapeDtypeStruct((b, GLA_HEADS, GLA_DK, GLA_DV), F32)],
        scratch_shapes=[pltpu.VMEM((GLA_BATCH, GLA_HEADS, GLA_DK, GLA_DV), F32)],
        compiler_params=_params(("parallel", "arbitrary")),
        name="gla",
    )(r3(gq), r3(gk), r3(gv), r3(la), r3(sgr), gn, s0)


def _merge_kernel(oa_ref, ob_ref, sga_ref, sgb_ref, x_ref, wa_ref, wb_ref, wo_ref, gffn_ref,
                  wr_ref, br_ref, x1_ref, h2_ref, route_ref, gcol_ref, cnt_ref, run_sc):
    ya = jnp.dot(oa_ref[...], wa_ref[...], preferred_element_type=F32)
    yb = jnp.dot(ob_ref[...], wb_ref[...], preferred_element_type=F32)
    merged = (sga_ref[...].astype(F32) * ya + sgb_ref[...].astype(F32) * yb).astype(BF16)
    x1 = x_ref[...] + jnp.dot(merged, wo_ref[...], preferred_element_type=F32)
    x1_ref[...] = x1
    h2 = _rms(x1, gffn_ref[...])
    _store_token_major(h2_ref, h2)

    hi = h2.astype(BF16)
    lo = (h2 - hi.astype(F32)).astype(BF16)
    wr = wr_ref[...]
    whi = wr.astype(BF16)
    wlo = (wr - whi.astype(F32)).astype(BF16)
    logits = (lax.dot_general(whi, hi, _NT, preferred_element_type=F32)
              + lax.dot_general(whi, lo, _NT, preferred_element_type=F32)
              + lax.dot_general(wlo, hi, _NT, preferred_element_type=F32)) + br_ref[...]
    tm = logits.shape[1]
    ridx = lax.broadcasted_iota(jnp.int32, (EXPERTS_PER_GROUP, tm), 0)
    big = jnp.int32(1 << 20)

    def top(vals):
        vmax = jnp.max(vals, axis=0, keepdims=True)
        imax = jnp.min(jnp.where(vals == vmax, ridx, big), axis=0, keepdims=True)
        return vmax, imax

    gl = jnp.where(ridx < N_GROUPS, logits[N_EXPERTS:N_EXPERTS + EXPERTS_PER_GROUP], -jnp.inf)
    gmax, g_top = top(gl)
    p_top = 1.0 / jnp.sum(jnp.exp(gl - gmax), axis=0, keepdims=True)
    e_sel = logits[:EXPERTS_PER_GROUP]
    for g in range(1, N_GROUPS):
        e_sel = jnp.where(g_top == g, logits[g * EXPERTS_PER_GROUP:(g + 1) * EXPERTS_PER_GROUP], e_sel)
    v1, i1 = top(e_sel)
    v2, i2 = top(jnp.where(ridx == i1, -jnp.inf, e_sel))
    e21 = jnp.exp(v2 - v1)
    w1 = p_top / (1.0 + e21)
    w2 = p_top * e21 / (1.0 + e21)
    base = g_top * EXPERTS_PER_GROUP
    id0 = base + i1
    id1 = base + i2
    gates = jnp.where(ridx == 0, w1, jnp.where(ridx == 1, w2, 0.0))
    gcol_ref[...] = jnp.transpose(jnp.concatenate([gates, jnp.zeros((LANES - EXPERTS_PER_GROUP, tm), F32)], axis=0))

    @pl.when(pl.program_id(0) == 0)
    def _():
        run_sc[...] = jnp.zeros(run_sc.shape, F32)

    eidx = lax.broadcasted_iota(jnp.int32, (N_EXPERTS, tm), 0)
    oh0 = jnp.where(eidx == id0, 1.0, 0.0)
    oh1 = jnp.where(eidx == id1, 1.0, 0.0)
    earlier = jnp.where(lax.broadcasted_iota(jnp.int32, (tm, tm), 0) < lax.broadcasted_iota(jnp.int32, (tm, tm), 1),
                        1.0, 0.0).astype(BF16)
    p0 = jnp.dot(oh0.astype(BF16), earlier, preferred_element_type=F32)
    p1 = jnp.dot(oh1.astype(BF16), earlier, preferred_element_type=F32)
    c0 = jnp.sum(oh0, axis=1, keepdims=True)
    c1 = jnp.sum(oh1, axis=1, keepdims=True)
    run = run_sc[...]
    rank0 = jnp.sum(oh0 * (run + p0), axis=0, keepdims=True)
    rank1 = jnp.sum(oh1 * (run + c0 + p1), axis=0, keepdims=True)
    run = run + c0 + c1
    run_sc[...] = run
    cnt_ref[...] = jnp.broadcast_to(run, cnt_ref.shape)
    route_ref[...] = jnp.where(ridx == 0, id0, jnp.where(ridx == 1, id1, jnp.where(
        ridx == 2, rank0.astype(jnp.int32), jnp.where(ridx == 3, rank1.astype(jnp.int32), 0))))


def _merge(oa, ob, sga, sgb, x2, w):
    n = x2.shape[0]
    tm = min(MERGE_TILE, n)
    assert n % tm == 0
    row = lambda width: pl.BlockSpec((tm, width), lambda i: (i, 0))
    weights = [w["w_a"], w["w_b"], w["w_o"], w["g_ffn"], w["w_router"], w["b_router"]]
    return pl.pallas_call(
        _merge_kernel,
        grid=(n // tm,),
        in_specs=[row(MLA_HEADS * V_DIM), row(GLA_HEADS * GLA_DV), row(D_MODEL), row(D_MODEL),
                  row(D_MODEL)] + [_full(a.shape) for a in weights],
        out_specs=[row(D_MODEL), pl.BlockSpec((tm * SLABS, LANES), lambda i: (i, 0)),
                   pl.BlockSpec((EXPERTS_PER_GROUP, tm), lambda i: (0, i)), row(LANES),
                   _full((N_EXPERTS, LANES))],
        out_shape=[jax.ShapeDtypeStruct((n, D_MODEL), F32), jax.ShapeDtypeStruct((n * SLABS, LANES), F32),
                   jax.ShapeDtypeStruct((EXPERTS_PER_GROUP, n), jnp.int32),
                   jax.ShapeDtypeStruct((n, LANES), F32),
                   jax.ShapeDtypeStruct((N_EXPERTS, LANES), F32)],
        scratch_shapes=[pltpu.VMEM((N_EXPERTS, 1), F32)],
        compiler_params=_params(("arbitrary",)),
        name="merge",
    )(oa, ob, sga, sgb, x2, *weights)


ROW_DMA_UNROLL = 8


def _row_tile(ref, r):
    return ref.at[pl.ds(pl.multiple_of(r * SLABS, SLABS), SLABS), :]


def _scatter_kernel(ps_ref, cnt_ref, pe_ref, dest_ref, h2_ref, xd_hbm, zero_sc, sem, zsem):
    tm = h2_ref.shape[0] // SLABS

    @pl.when(pl.program_id(0) == 0)
    def _():
        zero_sc[...] = jnp.zeros(zero_sc.shape, F32)

        def pad_copies(e, act):
            lo = ps_ref[e] + cnt_ref[e]
            pad = pe_ref[e] - lo
            bit = FFN_ROWS // 2
            while bit >= 1:
                @pl.when((pad & bit) != 0)
                def _(bit=bit):
                    first = lo + (pad & ~(2 * bit - 1))
                    act(pltpu.make_async_copy(
                        zero_sc.at[pl.ds(0, bit * SLABS), :],
                        xd_hbm.at[pl.ds(pl.multiple_of(first * SLABS, SLABS), bit * SLABS), :], zsem.at[0]))
                bit //= 2

        def start_pads(e, c):
            pad_copies(e, lambda cp: cp.start())
            return c

        def wait_pads(e, c):
            pad_copies(e, lambda cp: cp.wait())
            return c

        lax.fori_loop(0, N_EXPERTS, start_pads, 0)
        lax.fori_loop(0, N_EXPERTS, wait_pads, 0)

        def block_copy(b):
            start_row = pl.multiple_of(b * (FFN_ROWS * SLABS), FFN_ROWS * SLABS)
            return pltpu.make_async_copy(zero_sc, xd_hbm.at[pl.ds(start_row, FFN_ROWS * SLABS), :], zsem.at[0])

        first_unused = pe_ref[N_EXPERTS - 1] // FFN_ROWS
        n_blocks = xd_hbm.shape[0] // (FFN_ROWS * SLABS)

        def start_block(b, c):
            block_copy(b).start()
            return c

        def wait_block(b, c):
            block_copy(b).wait()
            return c

        lax.fori_loop(first_unused, n_blocks, start_block, 0)
        lax.fori_loop(first_unused, n_blocks, wait_block, 0)

    def body(t, carry):
        for k in range(2):
            d = dest_ref[0, 0, k * tm + t]
            pltpu.make_async_copy(_row_tile(h2_ref, t), _row_tile(xd_hbm, d), sem.at[0]).start(priority=k)
        return carry

    lax.fori_loop(0, tm, body, 0, unroll=ROW_DMA_UNROLL)
    for k in range(2):
        pltpu.make_async_copy(h2_ref, xd_hbm.at[pl.ds(0, tm * SLABS), :], sem.at[0]).wait()


def _scatter(pad_start, counts, pad_end, dest_blocks, h2, total_rows):
    tm = ROW_TILE
    nt = dest_blocks.shape[0]
    grid_spec = pltpu.PrefetchScalarGridSpec(
        num_scalar_prefetch=3,
        grid=(nt,),
        in_specs=[pl.BlockSpec((1, 1, 2 * tm), lambda i, *_: (i, 0, 0), memory_space=pltpu.SMEM),
                  pl.BlockSpec((tm * SLABS, LANES), lambda i, *_: (i, 0))],
        out_specs=pl.BlockSpec(memory_space=pl.ANY),
        scratch_shapes=[pltpu.VMEM((FFN_ROWS * SLABS, LANES), F32), pltpu.SemaphoreType.DMA((1,)),
                        pltpu.SemaphoreType.DMA((1,))],
    )
    return pl.pallas_call(
        _scatter_kernel,
        grid_spec=grid_spec,
        out_shape=jax.ShapeDtypeStruct((total_rows * SLABS, LANES), F32),
        compiler_params=_params(("arbitrary",)),
        name="moe_scatter",
    )(pad_start, counts, pad_end, dest_blocks, h2)


def _ffn_kernel(be_ref, nbu_ref, x_ref, wg_ref, wu_ref, wd_ref, out_ref):
    del be_ref

    @pl.when(pl.program_id(0) < nbu_ref[0])
    def _():
        x = _load_token_major(x_ref, FFN_ROWS).astype(BF16)
        g = jnp.dot(x, wg_ref[0], preferred_element_type=F32)
        u = jnp.dot(x, wu_ref[0], preferred_element_type=F32)
        mid = (g * _sigmoid(g) * u).astype(BF16)
        _store_token_major(out_ref, jnp.dot(mid, wd_ref[0], preferred_element_type=F32))

    @pl.when(pl.program_id(0) >= nbu_ref[0])
    def _():
        out_ref[...] = jnp.zeros(out_ref.shape, F32)


def _ffn(block_expert, nb_used, xd, wg, wu, wd):
    nb = block_expert.shape[0]
    blk = lambda i, be, nbu: (jnp.minimum(i, nbu[0] - 1), 0)
    wsel = lambda i, be, nbu: (be[jnp.minimum(i, nbu[0] - 1)], 0, 0)
    grid_spec = pltpu.PrefetchScalarGridSpec(
        num_scalar_prefetch=2,
        grid=(nb,),
        in_specs=[pl.BlockSpec((FFN_ROWS * SLABS, LANES), blk),
                  pl.BlockSpec((1, D_MODEL, EXPERT_FF), wsel),
                  pl.BlockSpec((1, D_MODEL, EXPERT_FF), wsel),
                  pl.BlockSpec((1, EXPERT_FF, D_MODEL), wsel)],
        out_specs=pl.BlockSpec((FFN_ROWS * SLABS, LANES), lambda i, be, nbu: (i, 0)),
    )
    return pl.pallas_call(
        _ffn_kernel,
        grid_spec=grid_spec,
        out_shape=jax.ShapeDtypeStruct((nb * FFN_ROWS * SLABS, LANES), F32),
        compiler_params=_params(("arbitrary",)),
        name="expert_ffn",
    )(block_expert, nb_used, xd, wg, wu, wd)


def _combine_kernel(dest_ref, x1_ref, gcol_ref, gfin_ref, eo_hbm, y_ref, buf, sem):
    tm = x1_ref.shape[0]

    def body(t, carry):
        for k in range(2):
            d = dest_ref[0, 0, k * tm + t]
            pltpu.make_async_copy(_row_tile(eo_hbm, d), _row_tile(buf.at[k], t), sem.at[0]).start(priority=k)
        return carry

    lax.fori_loop(0, tm, body, 0, unroll=ROW_DMA_UNROLL)
    for k in range(2):
        pltpu.make_async_copy(eo_hbm.at[pl.ds(0, tm * SLABS), :], buf.at[k], sem.at[0]).wait()
    g = gcol_ref[...]
    y = x1_ref[...] + (g[:, 0:1] * _load_token_major(buf.at[0], tm) + g[:, 1:2] * _load_token_major(buf.at[1], tm))
    y_ref[...] = _rms(y, gfin_ref[...])


def _combine(dest_blocks, x1, gcol, g_final, eo):
    n = x1.shape[0]
    tm = ROW_TILE
    return pl.pallas_call(
        _combine_kernel,
        grid=(n // tm,),
        in_specs=[pl.BlockSpec((1, 1, 2 * tm), lambda i: (i, 0, 0), memory_space=pltpu.SMEM),
                  pl.BlockSpec((tm, D_MODEL), lambda i: (i, 0)),
                  pl.BlockSpec((tm, LANES), lambda i: (i, 0)),
                  _full(g_final.shape),
                  pl.BlockSpec(memory_space=pl.ANY)],
        out_specs=pl.BlockSpec((tm, D_MODEL), lambda i: (i, 0)),
        out_shape=jax.ShapeDtypeStruct((n, D_MODEL), F32),
        scratch_shapes=[pltpu.VMEM((2, tm * SLABS, LANES), F32), pltpu.SemaphoreType.DMA((1,))],
        compiler_params=_params(("arbitrary",)),
        name="combine",
    )(dest_blocks, x1, gcol, g_final, eo)


def _dest_kernel(ps_ref, route_ref, o_ref):
    route = route_ref[...]
    start = jnp.zeros(route.shape, jnp.int32)
    for e in range(N_EXPERTS):
        start = jnp.where(route == e, ps_ref[e], start)
    o_ref[...] = start + pltpu.roll(route, EXPERTS_PER_GROUP - 2, 0)


def _dest_rows(pad_start, route):
    n = route.shape[1]
    tile = min(n, 8192)
    assert n % tile == 0
    grid_spec = pltpu.PrefetchScalarGridSpec(
        num_scalar_prefetch=1,
        grid=(n // tile,),
        in_specs=[pl.BlockSpec((EXPERTS_PER_GROUP, tile), lambda i, ps: (0, i))],
        out_specs=pl.BlockSpec((EXPERTS_PER_GROUP, tile), lambda i, ps: (0, i)),
    )
    return pl.pallas_call(
        _dest_kernel,
        grid_spec=grid_spec,
        out_shape=jax.ShapeDtypeStruct(route.shape, jnp.int32),
        compiler_params=_params(("arbitrary",)),
        name="moe_dest",
    )(pad_start, route)


def _layout(cnt, route, n):
    counts = cnt[:, 0].astype(jnp.int32)
    padded = (counts + FFN_ROWS - 1) // FFN_ROWS * FFN_ROWS
    pad_end = jnp.cumsum(padded).astype(jnp.int32)
    pad_start = pad_end - padded
    nb = 2 * n // FFN_ROWS + N_EXPERTS
    block_start = jnp.arange(nb, dtype=jnp.int32) * FFN_ROWS
    block_expert = jnp.minimum(jnp.sum((block_start[:, None] >= pad_end[None, :]).astype(jnp.int32), axis=1),
                               N_EXPERTS - 1).astype(jnp.int32)
    nb_used = (pad_end[N_EXPERTS - 1:] // FFN_ROWS).astype(jnp.int32)
    dest = _dest_rows(pad_start, route)[:2]
    tm = ROW_TILE
    dest_blocks = dest.reshape(2, n // tm, tm).transpose(1, 0, 2).reshape(n // tm, 1, 2 * tm)
    return pad_start, counts, pad_end, block_expert, nb_used, nb * FFN_ROWS, dest_blocks


def _rope_tables(l, past, rows):
    half = ROPE_DIM // 2
    inv = ROPE_THETA ** (-jnp.arange(half, dtype=F32) / half)
    pos = (past + (jnp.arange(rows, dtype=jnp.int32) % l)).astype(F32)
    ang = pos[:, None] * inv[None, :]
    pad = jnp.zeros((rows, LANES - ROPE_DIM), F32)
    cos = jnp.concatenate([jnp.cos(ang), jnp.cos(ang), pad], axis=1)
    sin = jnp.concatenate([-jnp.sin(ang), jnp.sin(ang), pad], axis=1)
    return cos, sin


def _group(x, ckv_past, kpe_past, s0, w, g_final):
    b, l, d = x.shape
    n = b * l
    x2 = x.reshape(n, d)
    past = 0 if ckv_past is None else ckv_past.shape[1]
    cos_t, sin_t = _rope_tables(l, past, max(l, ROW_TILE))
    prompt = ckv_past is None
    (qcat, kcat, ckv, kpe, gq, gk, gv, la, sgr, sga, sgb, *maybe_vt) = _inproj(x2, cos_t, sin_t, w, prompt)
    if prompt:
        oa = _attn_prompt(qcat, kcat, maybe_vt[0], w["w_uv"], b, l)
    else:
        oa = _attn_sample(qcat, kcat, ckv_past, kpe_past, w["w_uv"], b, l)
    ob, s_new = _gla(gq, gk, gv, la, sgr, w["g_gla"], s0, b, l)
    x1, h2, route, gcol, cnt = _merge(oa.reshape(n, -1), ob.reshape(n, -1), sga, sgb, x2, w)
    pad_start, counts, pad_end, block_expert, nb_used, total_rows, dest_blocks = _layout(cnt, route, n)
    xd = _scatter(pad_start, counts, pad_end, dest_blocks, h2, total_rows)
    eo = _ffn(block_expert, nb_used, xd, w["w_eg"], w["w_eu"], w["w_ed"])
    y = _combine(dest_blocks, x1, gcol, g_final, eo)
    return (y.reshape(b, l, d), ckv.reshape(b, l, KV_LORA), kpe.reshape(b, l, ROPE_DIM), s_new)


def _prep_weights(w_in, g_norm_mix, g_qnorm, w_uq, g_kvnorm, w_ukv, w_gate2, b_gate2, g_gla_norm,
                  w_branch_a, w_branch_b, w_out, g_norm_ffn, w_router_group, b_router_group,
                  w_router_expert, b_router_expert, w_exp_gate, w_exp_up, w_exp_down):
    nqk = GLA_HEADS * GLA_DK
    nv = GLA_HEADS * GLA_DV
    o = 0
    parts = {}
    for name, width in (("cq", Q_LORA), ("ckv", KV_LORA), ("kpe", ROPE_DIM), ("gq", nqk), ("gk", nqk),
                        ("gv", nv), ("glr", GATE_RANK), ("gr", nv), ("ga", D_MODEL), ("gb", D_MODEL)):
        parts[name] = w_in[:, o:o + width]
        o += width
    padc = lambda a, width: jnp.pad(a, ((0, 0), (0, width - a.shape[1])))
    w_small = jnp.concatenate([parts["cq"], parts["ckv"], padc(parts["kpe"], LANES),
                               padc(parts["glr"], LANES)], axis=1).astype(BF16)
    uq = w_uq.reshape(Q_LORA, MLA_HEADS, NOPE_DIM + ROPE_DIM)
    ukv = w_ukv.reshape(KV_LORA, MLA_HEADS, NOPE_DIM + V_DIM)
    lat = _fold_q(uq[:, :, :NOPE_DIM].transpose(1, 0, 2), ukv[:, :, :NOPE_DIM].transpose(1, 0, 2))
    q_rope = uq[:, :, NOPE_DIM:].transpose(1, 0, 2)
    w_qcat = jnp.concatenate([lat, q_rope, jnp.zeros((MLA_HEADS, Q_LORA, QCAT - KV_LORA - ROPE_DIM), F32)],
                             axis=2)
    w_qcat = w_qcat.transpose(1, 0, 2).reshape(Q_LORA, MLA_HEADS * QCAT).astype(BF16)
    w_router = jnp.concatenate([w_router_expert.T, w_router_group.T,
                                jnp.zeros((ROUTER_ROWS - N_EXPERTS - N_GROUPS, D_MODEL), F32)], axis=0)
    b_router = jnp.concatenate([b_router_expert, b_router_group,
                                jnp.zeros((ROUTER_ROWS - N_EXPERTS - N_GROUPS,), F32)]).reshape(ROUTER_ROWS, 1)
    return {
        "g_mix": g_norm_mix.reshape(1, D_MODEL), "w_small": w_small,
        "g_qn": g_qnorm.reshape(1, Q_LORA), "g_kvn": g_kvnorm.reshape(1, KV_LORA), "w_qcat": w_qcat,
        "w_g2": jnp.pad(w_gate2, ((0, LANES - GATE_RANK), (0, 0))).astype(BF16),
        "b_g2": b_gate2.reshape(1, nqk),
        "w_gqk": jnp.concatenate([parts["gq"], parts["gk"]], axis=1).astype(BF16),
        "w_gv": parts["gv"].astype(BF16), "w_gr": parts["gr"].astype(BF16),
        "w_ga": parts["ga"].astype(BF16), "w_gb": parts["gb"].astype(BF16),
        "w_uv": ukv[:, :, NOPE_DIM:].transpose(1, 0, 2).astype(BF16),
        "g_gla": g_gla_norm.reshape(1, GLA_DV),
        "w_a": w_branch_a.astype(BF16), "w_b": w_branch_b.astype(BF16), "w_o": w_out.astype(BF16),
        "g_ffn": g_norm_ffn.reshape(1, D_MODEL), "w_router": w_router, "b_router": b_router,
        "w_eg": w_exp_gate.astype(BF16), "w_eu": w_exp_up.astype(BF16), "w_ed": w_exp_down.astype(BF16),
    }


def kernel(x_prompt, x_sample, cache_ckv, cache_krope, state_gla, w_in, g_norm_mix, g_qnorm, w_uq, g_kvnorm, w_ukv, w_gate2, b_gate2, g_gla_norm, w_branch_a, w_branch_b, w_out, g_norm_ffn, w_router_group, b_router_group, w_router_expert, b_router_expert, w_exp_gate, w_exp_up, w_exp_down, g_norm_final):
    depth = w_in.shape[0]
    assert depth == 1, "the final norm is fused into the last layer's combine step"
    gfin = g_norm_final.reshape(1, D_MODEL)
    w = _prep_weights(w_in[0], g_norm_mix[0], g_qnorm[0], w_uq[0], g_kvnorm[0], w_ukv[0], w_gate2[0],
                      b_gate2[0], g_gla_norm[0], w_branch_a[0], w_branch_b[0], w_out[0], g_norm_ffn[0],
                      w_router_group[0], b_router_group[0], w_router_expert[0], b_router_expert[0],
                      w_exp_gate[0], w_exp_up[0], w_exp_down[0])
    bp = x_prompt.shape[0]
    zero_state = jnp.zeros((bp, GLA_HEADS, GLA_DK, GLA_DV), F32)
    yp, c1, k1, s1 = _group(x_prompt, None, None, zero_state, w, gfin)
    ys, c2, k2, s2 = _group(x_sample, cache_ckv[0], cache_krope[0], state_gla[0].astype(F32), w, gfin)
    return (yp, ys, c1[None], k1[None], s1[None], c2[None], k2[None], s2[None])
```

```python
import functools
import math

import jax
import jax.numpy as jnp
from jax import lax
from jax.experimental import pallas as pl
from jax.experimental.pallas import tpu as pltpu

F32 = jnp.float32
BF16 = jnp.bfloat16

D_MODEL = 1024
CHUNK = 64
EPS = 1e-6
MLA_HEADS = 8
Q_LORA = 256
KV_LORA = 128
NOPE_DIM = 64
ROPE_DIM = 32
V_DIM = 64
ROPE_THETA = 10000.0
GLA_HEADS = 4
GLA_DK = 128
GLA_DV = 256
GATE_RANK = 16
GATE_TEMP = 16.0
N_GROUPS = 4
EXPERTS_PER_GROUP = 8
N_EXPERTS = 32
EXPERT_FF = 512

LANES = 128
QCAT = 2 * LANES
ROW_TILE = 256
MERGE_TILE = 512
GLA_BATCH = 2
GLA_STEP_TOKENS = 256
Q_BLOCK = ROW_TILE
KEY_TILE = ROW_TILE
SUM_ROWS = 16
GLA_CHUNK = 64
FFN_ROWS = 512
ROUTER_ROWS = 48
VMEM_LIMIT = 56 * 1024 * 1024
LOG2E = 1.4426950408889634
ATT_SCALE = LOG2E / math.sqrt(NOPE_DIM + ROPE_DIM)

_NT = (((1,), (1,)), ((), ()))
_TN = (((0,), (0,)), ((), ()))


def _params(sem):
    return pltpu.CompilerParams(dimension_semantics=sem, vmem_limit_bytes=VMEM_LIMIT)


def _rms(x, g):
    return x * lax.rsqrt(jnp.mean(x * x, axis=-1, keepdims=True) + EPS) * g


def _sigmoid(x):
    return 1.0 / (1.0 + jnp.exp(-x))


def _full(shape):
    n = len(shape)
    return pl.BlockSpec(shape, lambda *_: (0,) * n)


SLABS = D_MODEL // LANES


def _store_token_major(ref, x):
    rows = x.shape[0]
    for j in range(SLABS):
        ref[pl.ds(j, rows, stride=SLABS), :] = x[:, j * LANES:(j + 1) * LANES]


def _load_token_major(ref, rows):
    return jnp.concatenate([ref[pl.ds(j, rows, stride=SLABS), :] for j in range(SLABS)], axis=1)


def _fold_q_kernel(wq_ref, wk_ref, o_ref):
    o_ref[0] = lax.dot_general(wq_ref[0], wk_ref[0], _NT, precision=lax.Precision.HIGHEST,
                               preferred_element_type=F32)


def _fold_q(wq_nope, wk_nope):
    return pl.pallas_call(
        _fold_q_kernel,
        grid=(MLA_HEADS,),
        in_specs=[pl.BlockSpec((1, Q_LORA, NOPE_DIM), lambda h: (h, 0, 0)),
                  pl.BlockSpec((1, KV_LORA, NOPE_DIM), lambda h: (h, 0, 0))],
        out_specs=pl.BlockSpec((1, Q_LORA, KV_LORA), lambda h: (h, 0, 0)),
        out_shape=jax.ShapeDtypeStruct((MLA_HEADS, Q_LORA, KV_LORA), F32),
        compiler_params=_params(("arbitrary",)),
        name="fold_q",
    )(wq_nope, wk_nope)


def _inproj_kernel(x_ref, cos_ref, sin_ref, gmix_ref, wsm_ref, gqn_ref, gkvn_ref, wqc_ref,
                   wg2_ref, bg2_ref, wgqk_ref, wgv_ref, wgr_ref, wga_ref, wgb_ref,
                   qcat_ref, kcat_ref, ckv_ref, kpe_ref, gq_ref, gk_ref, gv_ref, la_ref,
                   sgr_ref, sga_ref, sgb_ref, *maybe_vt_ref, transposed):
    hb = _rms(x_ref[...], gmix_ref[...]).astype(BF16)
    zs = jnp.dot(hb, wsm_ref[...], preferred_element_type=F32)
    cos = cos_ref[...]
    sin = sin_ref[...]
    first_half = lax.broadcasted_iota(jnp.int32, cos.shape, 1) < ROPE_DIM // 2

    def rope(v):
        rot = jnp.where(first_half, pltpu.roll(v, LANES - ROPE_DIM // 2, 1),
                        pltpu.roll(v, ROPE_DIM // 2, 1))
        return v * cos + rot * sin

    cqn = _rms(zs[:, :Q_LORA], gqn_ref[...]).astype(BF16)
    qc = jnp.dot(cqn, wqc_ref[...], preferred_element_type=F32)
    for h in range(MLA_HEADS):
        lat = qc[:, h * QCAT:h * QCAT + LANES]
        pe = qc[:, h * QCAT + LANES:(h + 1) * QCAT]
        if transposed:
            lat_t = jnp.transpose(lat * ATT_SCALE).astype(BF16)
            pe_t = jnp.transpose(rope(pe) * ATT_SCALE).astype(BF16)
            for blk in range(lat.shape[0] // Q_BLOCK):
                cols = slice(blk * Q_BLOCK, (blk + 1) * Q_BLOCK)
                qcat_ref[blk, :LANES, h * Q_BLOCK:(h + 1) * Q_BLOCK] = lat_t[:, cols]
                qcat_ref[blk, LANES:, h * Q_BLOCK:(h + 1) * Q_BLOCK] = pe_t[:, cols]
        else:
            qcat_ref[h, :, :LANES] = (lat * ATT_SCALE).astype(BF16)
            qcat_ref[h, :, LANES:] = (rope(pe) * ATT_SCALE).astype(BF16)

    ckv = _rms(zs[:, Q_LORA:Q_LORA + KV_LORA], gkvn_ref[...])
    ckv_ref[...] = ckv
    if transposed:
        maybe_vt_ref[0][0] = jnp.transpose(ckv).astype(BF16)
    kpe = rope(zs[:, Q_LORA + KV_LORA:Q_LORA + KV_LORA + LANES])
    kpe_ref[...] = kpe[:, :ROPE_DIM]
    kcat_ref[:, :LANES] = ckv.astype(BF16)
    kcat_ref[:, LANES:] = kpe.astype(BF16)

    glr = zs[:, Q_LORA + KV_LORA + LANES:].astype(BF16)
    xg = jnp.dot(glr, wg2_ref[...], preferred_element_type=F32) + bg2_ref[...]
    la_ref[...] = (jnp.minimum(xg, 0.0) - jnp.log(1.0 + jnp.exp(-jnp.abs(xg)))) * (1.0 / GATE_TEMP)

    zqk = jnp.dot(hb, wgqk_ref[...], preferred_element_type=F32)
    nqk = GLA_HEADS * GLA_DK
    gq_ref[...] = (zqk[:, :nqk] * (GLA_DK ** -0.5)).astype(BF16)
    gk_ref[...] = zqk[:, nqk:].astype(BF16)
    gv_ref[...] = jnp.dot(hb, wgv_ref[...], preferred_element_type=F32).astype(BF16)
    gr = jnp.dot(hb, wgr_ref[...], preferred_element_type=F32)
    sgr_ref[...] = (gr * _sigmoid(gr)).astype(BF16)
    sga_ref[...] = _sigmoid(jnp.dot(hb, wga_ref[...], preferred_element_type=F32)).astype(BF16)
    sgb_ref[...] = _sigmoid(jnp.dot(hb, wgb_ref[...], preferred_element_type=F32)).astype(BF16)


def _inproj(x2, cos_t, sin_t, w, transposed):
    n = x2.shape[0]
    tm = ROW_TILE
    nt = n // tm
    tab_blocks = cos_t.shape[0] // tm
    row = lambda width: pl.BlockSpec((tm, width), lambda i: (i, 0))
    tab = pl.BlockSpec((tm, LANES), lambda i: (i % tab_blocks, 0))
    nqk = GLA_HEADS * GLA_DK
    nv = GLA_HEADS * GLA_DV
    weights = [w["g_mix"], w["w_small"], w["g_qn"], w["g_kvn"], w["w_qcat"], w["w_g2"], w["b_g2"],
               w["w_gqk"], w["w_gv"], w["w_gr"], w["w_ga"], w["w_gb"]]
    out_shape = [
        jax.ShapeDtypeStruct((MLA_HEADS, n, QCAT), BF16),
        jax.ShapeDtypeStruct((n, QCAT), BF16),
        jax.ShapeDtypeStruct((n, KV_LORA), F32),
        jax.ShapeDtypeStruct((n, ROPE_DIM), F32),
        jax.ShapeDtypeStruct((n, nqk), BF16),
        jax.ShapeDtypeStruct((n, nqk), BF16),
        jax.ShapeDtypeStruct((n, nv), BF16),
        jax.ShapeDtypeStruct((n, nqk), F32),
        jax.ShapeDtypeStruct((n, nv), BF16),
        jax.ShapeDtypeStruct((n, D_MODEL), BF16),
        jax.ShapeDtypeStruct((n, D_MODEL), BF16),
    ]
    out_specs = [
        pl.BlockSpec((MLA_HEADS, tm, QCAT), lambda i: (0, i, 0)),
        row(QCAT), row(KV_LORA), row(ROPE_DIM), row(nqk), row(nqk), row(nv), row(nqk), row(nv),
        row(D_MODEL), row(D_MODEL),
    ]
    if transposed:
        qb = tm // Q_BLOCK
        out_shape[0] = jax.ShapeDtypeStruct((n // Q_BLOCK, QCAT, MLA_HEADS * Q_BLOCK), BF16)
        out_specs[0] = pl.BlockSpec((qb, QCAT, MLA_HEADS * Q_BLOCK), lambda i: (i, 0, 0))
        out_shape.append(jax.ShapeDtypeStruct((nt, KV_LORA, tm), BF16))
        out_specs.append(pl.BlockSpec((1, KV_LORA, tm), lambda i: (i, 0, 0)))
    return pl.pallas_call(
        functools.partial(_inproj_kernel, transposed=transposed),
        grid=(nt,),
        in_specs=[row(D_MODEL), tab, tab] + [_full(a.shape) for a in weights],
        out_specs=out_specs,
        out_shape=out_shape,
        compiler_params=_params(("parallel",)),
        name="inproj",
    )(x2, cos_t, sin_t, *weights)


def _head_out(o_lat, wuv_ref, rows):
    outs = []
    for h in range(MLA_HEADS):
        oh = o_lat[h * rows:(h + 1) * rows].astype(BF16)
        outs.append(jnp.dot(oh, wuv_ref[h], preferred_element_type=F32))
    return jnp.concatenate(outs, axis=1)


def _attn_prompt_kernel(qt_ref, k_ref, vt_ref, bias_ref, wuv_ref, o_ref, m_sc, l_sc, acc_sc):
    i = pl.program_id(1)
    m_sc[...] = jnp.full(m_sc.shape, -jnp.inf, F32)
    l_sc[...] = jnp.zeros(l_sc.shape, F32)
    acc_sc[...] = jnp.zeros(acc_sc.shape, F32)
    heads = [slice(h * Q_BLOCK, (h + 1) * Q_BLOCK) for h in range(MLA_HEADS)]
    ones_rows = jnp.ones((SUM_ROWS, KEY_TILE), BF16)

    def process(tiles, diagonal_last):
        nt = len(tiles)
        kbs = [k_ref[0, pl.ds(pl.multiple_of(kt * KEY_TILE, KEY_TILE), KEY_TILE), :] for kt in tiles]
        vts = [jnp.concatenate([vt_ref[kt], ones_rows], axis=0) for kt in tiles]
        ss = [[None] * MLA_HEADS for _ in range(nt)]
        for t in range(nt):
            for h, hs in enumerate(heads):
                s = jnp.dot(kbs[t], qt_ref[0, :, hs], preferred_element_type=F32)
                if diagonal_last and t == nt - 1:
                    s = s + bias_ref[...]
                ss[t][h] = s
        tmax = [[jnp.max(ss[t][h], axis=0, keepdims=True) for h in range(MLA_HEADS)] for t in range(nt)]
        scale = [[None] * MLA_HEADS for _ in range(nt)]
        mrun = [[None] * MLA_HEADS for _ in range(nt)]
        for h, hs in enumerate(heads):
            m = m_sc[:, hs]
            for t in range(nt):
                m_new = jnp.maximum(m, tmax[t][h])
                scale[t][h] = jnp.exp2(m - m_new)
                mrun[t][h] = m_new
                m = m_new
            m_sc[:, hs] = m
        pvs = [[jnp.dot(vts[t], jnp.exp2(ss[t][h] - mrun[t][h]).astype(BF16), preferred_element_type=F32)
                for h in range(MLA_HEADS)] for t in range(nt)]
        for h, hs in enumerate(heads):
            l, acc = l_sc[:, hs], acc_sc[:, hs]
            for t in range(nt):
                acc = scale[t][h] * acc + pvs[t][h][:KV_LORA]
                l = scale[t][h] * l + pvs[t][h][KV_LORA:KV_LORA + 1]
            l_sc[:, hs], acc_sc[:, hs] = l, acc

    def body(j, carry):
        process([2 * j, 2 * j + 1], False)
        return carry

    lax.fori_loop(0, i // 2, body, 0)

    @pl.when(i % 2 == 1)
    def _():
        process([i - 1], False)

    process([i], True)

    o_lat_t = (acc_sc[...] / l_sc[...]).astype(BF16)
    outs = []
    for h, hs in enumerate(heads):
        outs.append(lax.dot_general(o_lat_t[:, hs], wuv_ref[h], _TN, preferred_element_type=F32))
    o_ref[0] = jnp.concatenate(outs, axis=1).astype(BF16)


def _attn_prompt(qt, kcat, vt, wuv, b, l):
    assert Q_BLOCK == KEY_TILE and Q_BLOCK % CHUNK == 0
    nq = l // Q_BLOCK
    nkt = l // KEY_TILE
    cols = MLA_HEADS * Q_BLOCK
    pos_chunk = jnp.arange(Q_BLOCK, dtype=jnp.int32) // CHUNK
    bias = jnp.where(pos_chunk[:, None] <= pos_chunk[None, :], 0.0, -jnp.inf).astype(F32)
    return pl.pallas_call(
        _attn_prompt_kernel,
        grid=(b, nq),
        in_specs=[pl.BlockSpec((1, QCAT, cols), lambda bb, i: (bb * nq + i, 0, 0)),
                  pl.BlockSpec((1, l, QCAT), lambda bb, i: (bb, 0, 0)),
                  pl.BlockSpec((nkt, KV_LORA, KEY_TILE), lambda bb, i: (bb, 0, 0)),
                  _full(bias.shape), _full(wuv.shape)],
        out_specs=pl.BlockSpec((1, Q_BLOCK, MLA_HEADS * V_DIM), lambda bb, i: (bb, i, 0)),
        out_shape=jax.ShapeDtypeStruct((b, l, MLA_HEADS * V_DIM), BF16),
        scratch_shapes=[pltpu.VMEM((1, cols), F32), pltpu.VMEM((1, cols), F32),
                        pltpu.VMEM((KV_LORA, cols), F32)],
        compiler_params=_params(("parallel", "arbitrary")),
        name="attn_prompt",
    )(qt, kcat.reshape(b, l, QCAT), vt, bias, wuv)


def _attn_sample_kernel(q_ref, cckv_ref, ckr_ref, kn_ref, wuv_ref, o_ref, *, ls):
    q = q_ref[...].reshape(MLA_HEADS * ls, QCAT)
    ck = cckv_ref[0].astype(BF16)
    kr = ckr_ref[0].astype(BF16)
    kn = kn_ref[...]
    s_c = (lax.dot_general(q[:, :KV_LORA], ck, _NT, preferred_element_type=F32)
           + lax.dot_general(q[:, KV_LORA:KV_LORA + ROPE_DIM], kr, _NT, preferred_element_type=F32))
    s_n = lax.dot_general(q, kn, _NT, preferred_element_type=F32)
    m = jnp.maximum(jnp.max(s_c, axis=1, keepdims=True), jnp.max(s_n, axis=1, keepdims=True))
    p_c = jnp.exp2(s_c - m)
    p_n = jnp.exp2(s_n - m)
    den = jnp.sum(p_c, axis=1, keepdims=True) + jnp.sum(p_n, axis=1, keepdims=True)
    o_lat = (jnp.dot(p_c.astype(BF16), ck, preferred_element_type=F32)
             + jnp.dot(p_n.astype(BF16), kn[:, :KV_LORA], preferred_element_type=F32)) / den
    o_ref[0] = _head_out(o_lat, wuv_ref, ls).astype(BF16)


def _attn_sample(qcat, kcat, cache_ckv, cache_krope, wuv, b, ls):
    past = cache_ckv.shape[1]
    assert past % CHUNK == 0 and ls <= CHUNK
    return pl.pallas_call(
        functools.partial(_attn_sample_kernel, ls=ls),
        grid=(b,),
        in_specs=[pl.BlockSpec((MLA_HEADS, ls, QCAT), lambda bb: (0, bb, 0)),
                  pl.BlockSpec((1, past, KV_LORA), lambda bb: (bb, 0, 0)),
                  pl.BlockSpec((1, past, ROPE_DIM), lambda bb: (bb, 0, 0)),
                  pl.BlockSpec((ls, QCAT), lambda bb: (bb, 0)),
                  _full(wuv.shape)],
        out_specs=pl.BlockSpec((1, ls, MLA_HEADS * V_DIM), lambda bb: (bb, 0, 0)),
        out_shape=jax.ShapeDtypeStruct((b, ls, MLA_HEADS * V_DIM), BF16),
        compiler_params=_params(("parallel",)),
        name="attn_sample",
    )(qcat, cache_ckv, cache_krope, kcat, wuv)


def _cumsum_rows(x):
    c = x.shape[0]
    row = lax.broadcasted_iota(jnp.int32, x.shape, 0)
    s = 1
    while s < c:
        x = x + jnp.where(row >= s, pltpu.roll(x, s, 0), 0.0)
        s *= 2
    return x


def _gla_kernel(q_ref, k_ref, v_ref, la_ref, sgr_ref, gn_ref, s0_ref, o_ref, sout_ref, st_sc, *, c):
    j = pl.program_id(1)

    @pl.when(j == 0)
    def _():
        st_sc[...] = s0_ref[...]

    half = 32
    row = lax.broadcasted_iota(jnp.int32, (c, GLA_DK), 0)
    rr = lax.broadcasted_iota(jnp.int32, (c, c), 0)
    cc = lax.broadcasted_iota(jnp.int32, (c, c), 1)
    causal = cc <= rr
    same_half = (rr < half) == (cc < half)
    gn = gn_ref[...]
    n_chunks = q_ref.shape[1] // c
    seq_heads = [(s, h) for s in range(GLA_BATCH) for h in range(GLA_HEADS)]
    chains = [(s, h, ci) for s, h in seq_heads for ci in range(n_chunks)]
    ksl = lambda h: slice(h * GLA_DK, (h + 1) * GLA_DK)
    vsl = lambda h: slice(h * GLA_DV, (h + 1) * GLA_DV)
    rsl = lambda ci: slice(ci * c, (ci + 1) * c)

    prep = {}
    for s, h, ci in chains:
        q = q_ref[s, rsl(ci), ksl(h)].astype(F32)
        k = k_ref[s, rsl(ci), ksl(h)].astype(F32)
        b = _cumsum_rows(la_ref[s, rsl(ci), ksl(h)])
        if c > half:
            mid = jnp.where(row < half, b[half // 2 - 1:half // 2, :], b[half + half // 2 - 1:half + half // 2, :])
        else:
            mid = jnp.broadcast_to(b[c // 2 - 1:c // 2, :], b.shape)
        last = b[c - 1:c, :]
        ops = {"qe": (q * jnp.exp(b - mid)).astype(BF16), "ke": (k * jnp.exp(mid - b)).astype(BF16),
               "q0": (q * jnp.exp(b)).astype(BF16), "kd": (k * jnp.exp(last - b)).astype(BF16), "last": last}
        if c > half:
            edge = b[half - 1:half, :]
            ops["qo"] = (q * jnp.exp(jnp.minimum(b - edge, 0.0))).astype(BF16)
            ops["ko"] = (k * jnp.exp(jnp.minimum(edge - b, 0.0))).astype(BF16)
        prep[s, h, ci] = ops
    att = {}
    for key in chains:
        a = lax.dot_general(prep[key]["qe"], prep[key]["ke"], _NT, preferred_element_type=F32)
        if c > half:
            a_off = lax.dot_general(prep[key]["qo"], prep[key]["ko"], _NT, preferred_element_type=F32)
            a = jnp.where(same_half, a, a_off)
        att[key] = jnp.where(causal, a, 0.0).astype(BF16)
    intra, upd, decay = {}, {}, {}
    for s, h, ci in chains:
        v = v_ref[s, rsl(ci), vsl(h)]
        intra[s, h, ci] = jnp.dot(att[s, h, ci], v, preferred_element_type=F32)
        upd[s, h, ci] = lax.dot_general(prep[s, h, ci]["kd"], v, _TN, preferred_element_type=F32)
        dcol = jnp.exp(jnp.transpose(jnp.broadcast_to(prep[s, h, ci]["last"], (GLA_DK, GLA_DK))))
        decay[s, h, ci] = jnp.concatenate([dcol, dcol], axis=1)
    for s, h in seq_heads:
        st = st_sc[s, h]
        for ci in range(n_chunks):
            o = jnp.dot(prep[s, h, ci]["q0"], st.astype(BF16), preferred_element_type=F32) + intra[s, h, ci]
            st = decay[s, h, ci] * st + upd[s, h, ci]
            on = _rms(o, gn) * sgr_ref[s, rsl(ci), vsl(h)].astype(F32)
            o_ref[s, rsl(ci), vsl(h)] = on.astype(BF16)
        st_sc[s, h] = st

    @pl.when(j == pl.num_programs(1) - 1)
    def _():
        sout_ref[...] = st_sc[...]


def _gla(gq, gk, gv, la, sgr, gn, s0, b, l):
    c = min(GLA_CHUNK, l)
    step = min(GLA_STEP_TOKENS, l)
    nc = l // step
    nqk = GLA_HEADS * GLA_DK
    nv = GLA_HEADS * GLA_DV
    r3 = lambda a: a.reshape(b, l, a.shape[-1])
    assert b % GLA_BATCH == 0 and l % step == 0 and step % c == 0
    tok = lambda width: pl.BlockSpec((GLA_BATCH, step, width), lambda bb, j: (bb, j, 0))
    st = pl.BlockSpec((GLA_BATCH, GLA_HEADS, GLA_DK, GLA_DV), lambda bb, j: (bb, 0, 0, 0))
    return pl.pallas_call(
        functools.partial(_gla_kernel, c=c),
        grid=(b // GLA_BATCH, nc),
        in_specs=[tok(nqk), tok(nqk), tok(nv), tok(nqk), tok(nv), _full(gn.shape), st],
        out_specs=[tok(nv), st],
        out_shape=[jax.ShapeDtypeStruct((b, l, nv), BF16),
                   jax.ShapeDtypeStruct((b, GLA_HEADS, GLA_DK, GLA_DV), F32)],
        scratch_shapes=[pltpu.VMEM((GLA_BATCH, GLA_HEADS, GLA_DK, GLA_DV), F32)],
        compiler_params=_params(("parallel", "arbitrary")),
        name="gla",
    )(r3(gq), r3(gk), r3(gv), r3(la), r3(sgr), gn, s0)


def _merge_kernel(oa_ref, ob_ref, sga_ref, sgb_ref, x_ref, wa_ref, wb_ref, wo_ref, gffn_ref,
                  wr_ref, br_ref, cnt0_ref, x1_ref, h2_ref, route_ref, gcol_ref, cnt_ref, run_sc):
    ya = jnp.dot(oa_ref[...], wa_ref[...], preferred_element_type=F32)
    yb = jnp.dot(ob_ref[...], wb_ref[...], preferred_element_type=F32)
    merged = (sga_ref[...].astype(F32) * ya + sgb_ref[...].astype(F32) * yb).astype(BF16)
    x1 = x_ref[...] + jnp.dot(merged, wo_ref[...], preferred_element_type=F32)
    x1_ref[...] = x1
    h2 = _rms(x1, gffn_ref[...])
    _store_token_major(h2_ref, h2)

    hi = h2.astype(BF16)
    lo = (h2 - hi.astype(F32)).astype(BF16)
    wr = wr_ref[...]
    whi = wr.astype(BF16)
    wlo = (wr - whi.astype(F32)).astype(BF16)
    logits = (lax.dot_general(whi, hi, _NT, preferred_element_type=F32)
              + lax.dot_general(whi, lo, _NT, preferred_element_type=F32)
              + lax.dot_general(wlo, hi, _NT, preferred_element_type=F32)) + br_ref[...]
    tm = logits.shape[1]
    ridx = lax.broadcasted_iota(jnp.int32, (EXPERTS_PER_GROUP, tm), 0)
    big = jnp.int32(1 << 20)

    def top(vals):
        vmax = jnp.max(vals, axis=0, keepdims=True)
        imax = jnp.min(jnp.where(vals == vmax, ridx, big), axis=0, keepdims=True)
        return vmax, imax

    gl = jnp.where(ridx < N_GROUPS, logits[N_EXPERTS:N_EXPERTS + EXPERTS_PER_GROUP], -jnp.inf)
    gmax, g_top = top(gl)
    p_top = 1.0 / jnp.sum(jnp.exp(gl - gmax), axis=0, keepdims=True)
    e_sel = logits[:EXPERTS_PER_GROUP]
    for g in range(1, N_GROUPS):
        e_sel = jnp.where(g_top == g, logits[g * EXPERTS_PER_GROUP:(g + 1) * EXPERTS_PER_GROUP], e_sel)
    v1, i1 = top(e_sel)
    v2, i2 = top(jnp.where(ridx == i1, -jnp.inf, e_sel))
    e21 = jnp.exp(v2 - v1)
    w1 = p_top / (1.0 + e21)
    w2 = p_top * e21 / (1.0 + e21)
    base = g_top * EXPERTS_PER_GROUP
    id0 = base + i1
    id1 = base + i2
    gates = jnp.where(ridx == 0, w1, jnp.where(ridx == 1, w2, 0.0))
    gcol_ref[...] = jnp.transpose(jnp.concatenate([gates, jnp.zeros((LANES - EXPERTS_PER_GROUP, tm), F32)], axis=0))

    @pl.when(pl.program_id(0) == 0)
    def _():
        run_sc[...] = cnt0_ref[:, :1]

    eidx = lax.broadcasted_iota(jnp.int32, (N_EXPERTS, tm), 0)
    oh0 = jnp.where(eidx == id0, 1.0, 0.0)
    oh1 = jnp.where(eidx == id1, 1.0, 0.0)
    earlier = jnp.where(lax.broadcasted_iota(jnp.int32, (tm, tm), 0) < lax.broadcasted_iota(jnp.int32, (tm, tm), 1),
                        1.0, 0.0).astype(BF16)
    p0 = jnp.dot(oh0.astype(BF16), earlier, preferred_element_type=F32)
    p1 = jnp.dot(oh1.astype(BF16), earlier, preferred_element_type=F32)
    c0 = jnp.sum(oh0, axis=1, keepdims=True)
    c1 = jnp.sum(oh1, axis=1, keepdims=True)
    run = run_sc[...]
    rank0 = jnp.sum(oh0 * (run + p0), axis=0, keepdims=True)
    rank1 = jnp.sum(oh1 * (run + c0 + p1), axis=0, keepdims=True)
    run = run + c0 + c1
    run_sc[...] = run
    cnt_ref[...] = jnp.broadcast_to(run, cnt_ref.shape)
    route_ref[...] = jnp.where(ridx == 0, id0, jnp.where(ridx == 1, id1, jnp.where(
        ridx == 2, rank0.astype(jnp.int32), jnp.where(ridx == 3, rank1.astype(jnp.int32), 0))))


def _merge(oa, ob, sga, sgb, x2, w, cnt0):
    n = x2.shape[0]
    tm = min(MERGE_TILE, n)
    assert n % tm == 0
    row = lambda width: pl.BlockSpec((tm, width), lambda i: (i, 0))
    weights = [w["w_a"], w["w_b"], w["w_o"], w["g_ffn"], w["w_router"], w["b_router"], cnt0]
    return pl.pallas_call(
        _merge_kernel,
        grid=(n // tm,),
        in_specs=[row(MLA_HEADS * V_DIM), row(GLA_HEADS * GLA_DV), row(D_MODEL), row(D_MODEL),
                  row(D_MODEL)] + [_full(a.shape) for a in weights],
        out_specs=[row(D_MODEL), pl.BlockSpec((tm * SLABS, LANES), lambda i: (i, 0)),
                   pl.BlockSpec((EXPERTS_PER_GROUP, tm), lambda i: (0, i)), row(LANES),
                   _full((N_EXPERTS, LANES))],
        out_shape=[jax.ShapeDtypeStruct((n, D_MODEL), F32), jax.ShapeDtypeStruct((n * SLABS, LANES), F32),
                   jax.ShapeDtypeStruct((EXPERTS_PER_GROUP, n), jnp.int32),
                   jax.ShapeDtypeStruct((n, LANES), F32),
                   jax.ShapeDtypeStruct((N_EXPERTS, LANES), F32)],
        scratch_shapes=[pltpu.VMEM((N_EXPERTS, 1), F32)],
        compiler_params=_params(("arbitrary",)),
        name="merge",
    )(oa, ob, sga, sgb, x2, *weights)


ROW_DMA_UNROLL = 8


def _row_tile(ref, r):
    return ref.at[pl.ds(pl.multiple_of(r * SLABS, SLABS), SLABS), :]


def _scatter_kernel(ps_ref, cnt_ref, pe_ref, *rest, tiles):
    ng = len(tiles)
    dest_refs, h2_refs = rest[:ng], rest[ng:2 * ng]
    xd_hbm, zero_sc, sem, zsem = rest[2 * ng:]
    tm = h2_refs[0].shape[0] // SLABS

    def zero_padding():
        zero_sc[...] = jnp.zeros(zero_sc.shape, F32)

        def pad_copies(e, act):
            lo = ps_ref[e] + cnt_ref[e]
            pad = pe_ref[e] - lo
            bit = FFN_ROWS // 2
            while bit >= 1:
                @pl.when((pad & bit) != 0)
                def _(bit=bit):
                    first = lo + (pad & ~(2 * bit - 1))
                    act(pltpu.make_async_copy(
                        zero_sc.at[pl.ds(0, bit * SLABS), :],
                        xd_hbm.at[pl.ds(pl.multiple_of(first * SLABS, SLABS), bit * SLABS), :], zsem.at[0]))
                bit //= 2

        def start_pads(e, c):
            pad_copies(e, lambda cp: cp.start())
            return c

        def wait_pads(e, c):
            pad_copies(e, lambda cp: cp.wait())
            return c

        lax.fori_loop(0, N_EXPERTS, start_pads, 0)
        lax.fori_loop(0, N_EXPERTS, wait_pads, 0)

        def block_copy(b):
            start_row = pl.multiple_of(b * (FFN_ROWS * SLABS), FFN_ROWS * SLABS)
            return pltpu.make_async_copy(zero_sc, xd_hbm.at[pl.ds(start_row, FFN_ROWS * SLABS), :], zsem.at[0])

        first_unused = pe_ref[N_EXPERTS - 1] // FFN_ROWS
        n_blocks = xd_hbm.shape[0] // (FFN_ROWS * SLABS)

        def start_block(b, c):
            block_copy(b).start()
            return c

        def wait_block(b, c):
            block_copy(b).wait()
            return c

        lax.fori_loop(first_unused, n_blocks, start_block, 0)
        lax.fori_loop(first_unused, n_blocks, wait_block, 0)

    step = pl.program_id(0)
    pl.when(step == 0)(zero_padding)

    first = 0
    for g, nt in enumerate(tiles):
        @pl.when((step >= first) & (step < first + nt))
        def _(dest_ref=dest_refs[g], h2_ref=h2_refs[g]):
            def body(t, carry):
                for k in range(2):
                    d = dest_ref[0, 0, k * tm + t]
                    pltpu.make_async_copy(_row_tile(h2_ref, t), _row_tile(xd_hbm, d), sem.at[0]).start(priority=k)
                return carry

            lax.fori_loop(0, tm, body, 0, unroll=ROW_DMA_UNROLL)
            for k in range(2):
                pltpu.make_async_copy(h2_ref, xd_hbm.at[pl.ds(0, tm * SLABS), :], sem.at[0]).wait()
        first += nt


def _scatter(pad_start, counts, pad_end, dest_blocks, h2s, total_rows):
    tm = ROW_TILE
    tiles = [d.shape[0] for d in dest_blocks]
    firsts = [sum(tiles[:g]) for g in range(len(tiles))]
    local = lambda g: (lambda i, *_: jnp.clip(i - firsts[g], 0, tiles[g] - 1))
    in_specs = ([pl.BlockSpec((1, 1, 2 * tm), (lambda i, *_, f=local(g): (f(i), 0, 0)), memory_space=pltpu.SMEM)
                 for g in range(len(tiles))]
                + [pl.BlockSpec((tm * SLABS, LANES), (lambda i, *_, f=local(g): (f(i), 0)))
                   for g in range(len(tiles))])
    grid_spec = pltpu.PrefetchScalarGridSpec(
        num_scalar_prefetch=3,
        grid=(sum(tiles),),
        in_specs=in_specs,
        out_specs=pl.BlockSpec(memory_space=pl.ANY),
        scratch_shapes=[pltpu.VMEM((FFN_ROWS * SLABS, LANES), F32), pltpu.SemaphoreType.DMA((1,)),
                        pltpu.SemaphoreType.DMA((1,))],
    )
    return pl.pallas_call(
        functools.partial(_scatter_kernel, tiles=tuple(tiles)),
        grid_spec=grid_spec,
        out_shape=jax.ShapeDtypeStruct((total_rows * SLABS, LANES), F32),
        compiler_params=_params(("arbitrary",)),
        name="moe_scatter",
    )(pad_start, counts, pad_end, *dest_blocks, *h2s)


def _ffn_kernel(be_ref, nbu_ref, x_ref, wg_ref, wu_ref, wd_ref, out_ref, wg_sc, wu_sc, wd_sc):
    i = pl.program_id(0)
    used = i < nbu_ref[0]
    new_expert = (i == 0) | (be_ref[i] != be_ref[jnp.maximum(i - 1, 0)])

    @pl.when(used & new_expert)
    def _():
        wg_sc[...] = wg_ref[0].astype(BF16)
        wu_sc[...] = wu_ref[0].astype(BF16)
        wd_sc[...] = wd_ref[0].astype(BF16)

    @pl.when(used)
    def _():
        x = _load_token_major(x_ref, FFN_ROWS).astype(BF16)
        g = jnp.dot(x, wg_sc[...], preferred_element_type=F32)
        u = jnp.dot(x, wu_sc[...], preferred_element_type=F32)
        mid = (g * _sigmoid(g) * u).astype(BF16)
        _store_token_major(out_ref, jnp.dot(mid, wd_sc[...], preferred_element_type=F32))

    @pl.when(jnp.logical_not(used))
    def _():
        out_ref[...] = jnp.zeros(out_ref.shape, F32)


def _ffn(block_expert, nb_used, xd, wg, wu, wd):
    nb = block_expert.shape[0]
    blk = lambda i, be, nbu: (jnp.minimum(i, nbu[0] - 1), 0)
    wsel = lambda i, be, nbu: (be[jnp.minimum(i, nbu[0] - 1)], 0, 0)
    grid_spec = pltpu.PrefetchScalarGridSpec(
        num_scalar_prefetch=2,
        grid=(nb,),
        in_specs=[pl.BlockSpec((FFN_ROWS * SLABS, LANES), blk),
                  pl.BlockSpec((1, D_MODEL, EXPERT_FF), wsel),
                  pl.BlockSpec((1, D_MODEL, EXPERT_FF), wsel),
                  pl.BlockSpec((1, EXPERT_FF, D_MODEL), wsel)],
        out_specs=pl.BlockSpec((FFN_ROWS * SLABS, LANES), lambda i, be, nbu: (i, 0)),
        scratch_shapes=[pltpu.VMEM((D_MODEL, EXPERT_FF), BF16), pltpu.VMEM((D_MODEL, EXPERT_FF), BF16),
                        pltpu.VMEM((EXPERT_FF, D_MODEL), BF16)],
    )
    return pl.pallas_call(
        _ffn_kernel,
        grid_spec=grid_spec,
        out_shape=jax.ShapeDtypeStruct((nb * FFN_ROWS * SLABS, LANES), F32),
        compiler_params=_params(("arbitrary",)),
        name="expert_ffn",
    )(block_expert, nb_used, xd, wg, wu, wd)


def _combine_kernel(dest_ref, x1_ref, gcol_ref, gfin_ref, eo_hbm, y_ref, buf, sem):
    tm = x1_ref.shape[0]

    def body(t, carry):
        for k in range(2):
            d = dest_ref[0, 0, k * tm + t]
            pltpu.make_async_copy(_row_tile(eo_hbm, d), _row_tile(buf.at[k], t), sem.at[0]).start(priority=k)
        return carry

    lax.fori_loop(0, tm, body, 0, unroll=ROW_DMA_UNROLL)
    for k in range(2):
        pltpu.make_async_copy(eo_hbm.at[pl.ds(0, tm * SLABS), :], buf.at[k], sem.at[0]).wait()
    g = gcol_ref[...]
    y = x1_ref[...] + (g[:, 0:1] * _load_token_major(buf.at[0], tm) + g[:, 1:2] * _load_token_major(buf.at[1], tm))
    y_ref[...] = _rms(y, gfin_ref[...])


def _combine(dest_blocks, x1, gcol, g_final, eo):
    n = x1.shape[0]
    tm = ROW_TILE
    return pl.pallas_call(
        _combine_kernel,
        grid=(n // tm,),
        in_specs=[pl.BlockSpec((1, 1, 2 * tm), lambda i: (i, 0, 0), memory_space=pltpu.SMEM),
                  pl.BlockSpec((tm, D_MODEL), lambda i: (i, 0)),
                  pl.BlockSpec((tm, LANES), lambda i: (i, 0)),
                  _full(g_final.shape),
                  pl.BlockSpec(memory_space=pl.ANY)],
        out_specs=pl.BlockSpec((tm, D_MODEL), lambda i: (i, 0)),
        out_shape=jax.ShapeDtypeStruct((n, D_MODEL), F32),
        scratch_shapes=[pltpu.VMEM((2, tm * SLABS, LANES), F32), pltpu.SemaphoreType.DMA((1,))],
        compiler_params=_params(("arbitrary",)),
        name="combine",
    )(dest_blocks, x1, gcol, g_final, eo)


def _dest_kernel(ps_ref, route_ref, o_ref):
    route = route_ref[...]
    start = jnp.zeros(route.shape, jnp.int32)
    for e in range(N_EXPERTS):
        start = jnp.where(route == e, ps_ref[e], start)
    o_ref[...] = start + pltpu.roll(route, EXPERTS_PER_GROUP - 2, 0)


def _dest_rows(pad_start, route):
    n = route.shape[1]
    tile = min(n, 8192)
    assert n % tile == 0
    grid_spec = pltpu.PrefetchScalarGridSpec(
        num_scalar_prefetch=1,
        grid=(n // tile,),
        in_specs=[pl.BlockSpec((EXPERTS_PER_GROUP, tile), lambda i, ps: (0, i))],
        out_specs=pl.BlockSpec((EXPERTS_PER_GROUP, tile), lambda i, ps: (0, i)),
    )
    return pl.pallas_call(
        _dest_kernel,
        grid_spec=grid_spec,
        out_shape=jax.ShapeDtypeStruct(route.shape, jnp.int32),
        compiler_params=_params(("arbitrary",)),
        name="moe_dest",
    )(pad_start, route)


def _layout(cnt, n_tokens):
    counts = cnt[:, 0].astype(jnp.int32)
    padded = (counts + FFN_ROWS - 1) // FFN_ROWS * FFN_ROWS
    pad_end = jnp.cumsum(padded).astype(jnp.int32)
    pad_start = pad_end - padded
    nb = -(-2 * n_tokens // FFN_ROWS) + N_EXPERTS
    block_start = jnp.arange(nb, dtype=jnp.int32) * FFN_ROWS
    block_expert = jnp.minimum(jnp.sum((block_start[:, None] >= pad_end[None, :]).astype(jnp.int32), axis=1),
                               N_EXPERTS - 1).astype(jnp.int32)
    nb_used = (pad_end[N_EXPERTS - 1:] // FFN_ROWS).astype(jnp.int32)
    return pad_start, counts, pad_end, block_expert, nb_used, nb * FFN_ROWS


def _dest_blocks(pad_start, route):
    n = route.shape[1]
    dest = _dest_rows(pad_start, route)[:2]
    tm = ROW_TILE
    return dest.reshape(2, n // tm, tm).transpose(1, 0, 2).reshape(n // tm, 1, 2 * tm)


def _rope_tables(l, past, rows):
    half = ROPE_DIM // 2
    inv = ROPE_THETA ** (-jnp.arange(half, dtype=F32) / half)
    pos = (past + (jnp.arange(rows, dtype=jnp.int32) % l)).astype(F32)
    ang = pos[:, None] * inv[None, :]
    pad = jnp.zeros((rows, LANES - ROPE_DIM), F32)
    cos = jnp.concatenate([jnp.cos(ang), jnp.cos(ang), pad], axis=1)
    sin = jnp.concatenate([-jnp.sin(ang), jnp.sin(ang), pad], axis=1)
    return cos, sin


def _mixers(x, ckv_past, kpe_past, s0, w):
    b, l, d = x.shape
    n = b * l
    x2 = x.reshape(n, d)
    past = 0 if ckv_past is None else ckv_past.shape[1]
    cos_t, sin_t = _rope_tables(l, past, max(l, ROW_TILE))
    prompt = ckv_past is None
    (qcat, kcat, ckv, kpe, gq, gk, gv, la, sgr, sga, sgb, *maybe_vt) = _inproj(x2, cos_t, sin_t, w, prompt)
    if prompt:
        oa = _attn_prompt(qcat, kcat, maybe_vt[0], w["w_uv"], b, l)
    else:
        oa = _attn_sample(qcat, kcat, ckv_past, kpe_past, w["w_uv"], b, l)
    ob, s_new = _gla(gq, gk, gv, la, sgr, w["g_gla"], s0, b, l)
    merge_in = (oa.reshape(n, -1), ob.reshape(n, -1), sga, sgb, x2)
    return merge_in, (ckv.reshape(b, l, KV_LORA), kpe.reshape(b, l, ROPE_DIM), s_new)


def _layer(groups, w, g_final):
    fronts = [_mixers(*g, w) for g in groups]
    cnt = jnp.zeros((N_EXPERTS, LANES), F32)
    merged = []
    for merge_in, _ in fronts:
        x1, h2, route, gcol, cnt = _merge(*merge_in, w, cnt)
        merged.append((x1, h2, route, gcol))
    n_tokens = sum(m[0].shape[0] for m in merged)
    pad_start, counts, pad_end, block_expert, nb_used, total_rows = _layout(cnt, n_tokens)
    dests = [_dest_blocks(pad_start, m[2]) for m in merged]
    xd = _scatter(pad_start, counts, pad_end, dests, [m[1] for m in merged], total_rows)
    eo = _ffn(block_expert, nb_used, xd, w["w_eg"], w["w_eu"], w["w_ed"])
    outs = []
    for (x1, h2, route, gcol), dest, (_, extras), g in zip(merged, dests, fronts, groups):
        y = _combine(dest, x1, gcol, g_final, eo)
        outs.append((y.reshape(g[0].shape),) + extras)
    return outs


def _prep_weights(w_in, g_norm_mix, g_qnorm, w_uq, g_kvnorm, w_ukv, w_gate2, b_gate2, g_gla_norm,
                  w_branch_a, w_branch_b, w_out, g_norm_ffn, w_router_group, b_router_group,
                  w_router_expert, b_router_expert, w_exp_gate, w_exp_up, w_exp_down):
    nqk = GLA_HEADS * GLA_DK
    nv = GLA_HEADS * GLA_DV
    o = 0
    parts = {}
    for name, width in (("cq", Q_LORA), ("ckv", KV_LORA), ("kpe", ROPE_DIM), ("gq", nqk), ("gk", nqk),
                        ("gv", nv), ("glr", GATE_RANK), ("gr", nv), ("ga", D_MODEL), ("gb", D_MODEL)):
        parts[name] = w_in[:, o:o + width]
        o += width
    padc = lambda a, width: jnp.pad(a, ((0, 0), (0, width - a.shape[1])))
    w_small = jnp.concatenate([parts["cq"], parts["ckv"], padc(parts["kpe"], LANES),
                               padc(parts["glr"], LANES)], axis=1).astype(BF16)
    uq = w_uq.reshape(Q_LORA, MLA_HEADS, NOPE_DIM + ROPE_DIM)
    ukv = w_ukv.reshape(KV_LORA, MLA_HEADS, NOPE_DIM + V_DIM)
    lat = _fold_q(uq[:, :, :NOPE_DIM].transpose(1, 0, 2), ukv[:, :, :NOPE_DIM].transpose(1, 0, 2))
    q_rope = uq[:, :, NOPE_DIM:].transpose(1, 0, 2)
    w_qcat = jnp.concatenate([lat, q_rope, jnp.zeros((MLA_HEADS, Q_LORA, QCAT - KV_LORA - ROPE_DIM), F32)],
                             axis=2)
    w_qcat = w_qcat.transpose(1, 0, 2).reshape(Q_LORA, MLA_HEADS * QCAT).astype(BF16)
    w_router = jnp.concatenate([w_router_expert.T, w_router_group.T,
                                jnp.zeros((ROUTER_ROWS - N_EXPERTS - N_GROUPS, D_MODEL), F32)], axis=0)
    b_router = jnp.concatenate([b_router_expert, b_router_group,
                                jnp.zeros((ROUTER_ROWS - N_EXPERTS - N_GROUPS,), F32)]).reshape(ROUTER_ROWS, 1)
    return {
        "g_mix": g_norm_mix.reshape(1, D_MODEL), "w_small": w_small,
        "g_qn": g_qnorm.reshape(1, Q_LORA), "g_kvn": g_kvnorm.reshape(1, KV_LORA), "w_qcat": w_qcat,
        "w_g2": jnp.pad(w_gate2, ((0, LANES - GATE_RANK), (0, 0))).astype(BF16),
        "b_g2": b_gate2.reshape(1, nqk),
        "w_gqk": jnp.concatenate([parts["gq"], parts["gk"]], axis=1).astype(BF16),
        "w_gv": parts["gv"].astype(BF16), "w_gr": parts["gr"].astype(BF16),
        "w_ga": parts["ga"].astype(BF16), "w_gb": parts["gb"].astype(BF16),
        "w_uv": ukv[:, :, NOPE_DIM:].transpose(1, 0, 2).astype(BF16),
        "g_gla": g_gla_norm.reshape(1, GLA_DV),
        "w_a": w_branch_a.astype(BF16), "w_b": w_branch_b.astype(BF16), "w_o": w_out.astype(BF16),
        "g_ffn": g_norm_ffn.reshape(1, D_MODEL), "w_router": w_router, "b_router": b_router,
        "w_eg": w_exp_gate, "w_eu": w_exp_up, "w_ed": w_exp_down,
    }


def kernel(x_prompt, x_sample, cache_ckv, cache_krope, state_gla, w_in, g_norm_mix, g_qnorm, w_uq, g_kvnorm, w_ukv, w_gate2, b_gate2, g_gla_norm, w_branch_a, w_branch_b, w_out, g_norm_ffn, w_router_group, b_router_group, w_router_expert, b_router_expert, w_exp_gate, w_exp_up, w_exp_down, g_norm_final):
    depth = w_in.shape[0]
    assert depth == 1, "the final norm is fused into the last layer's combine step"
    gfin = g_norm_final.reshape(1, D_MODEL)
    w = _prep_weights(w_in[0], g_norm_mix[0], g_qnorm[0], w_uq[0], g_kvnorm[0], w_ukv[0], w_gate2[0],
                      b_gate2[0], g_gla_norm[0], w_branch_a[0], w_branch_b[0], w_out[0], g_norm_ffn[0],
                      w_router_group[0], b_router_group[0], w_router_expert[0], b_router_expert[0],
                      w_exp_gate[0], w_exp_up[0], w_exp_down[0])
    bp = x_prompt.shape[0]
    zero_state = jnp.zeros((bp, GLA_HEADS, GLA_DK, GLA_DV), F32)
    (yp, c1, k1, s1), (ys, c2, k2, s2) = _layer(
        [(x_prompt, None, None, zero_state),
         (x_sample, cache_ckv[0], cache_krope[0], state_gla[0].astype(F32))], w, gfin)
    return (yp, ys, c1[None], k1[None], s1[None], c2[None], k2[None], s2[None])
```

```python
import functools
import math

import jax
import jax.numpy as jnp
from jax import lax
from jax.experimental import pallas as pl
from jax.experimental.pallas import tpu as pltpu

F32 = jnp.float32
BF16 = jnp.bfloat16

D_MODEL = 1024
CHUNK = 64
EPS = 1e-6
MLA_HEADS = 8
Q_LORA = 256
KV_LORA = 128
NOPE_DIM = 64
ROPE_DIM = 32
V_DIM = 64
ROPE_THETA = 10000.0
GLA_HEADS = 4
GLA_DK = 128
GLA_DV = 256
GATE_RANK = 16
GATE_TEMP = 16.0
N_GROUPS = 4
EXPERTS_PER_GROUP = 8
N_EXPERTS = 32
EXPERT_FF = 512

LANES = 128
QCAT = 2 * LANES
ROW_TILE = 256
MERGE_TILE = 512
MERGE_PARTS = 1
RUN_WINDOW = 32
GLA_BATCH = 2
GLA_STEP_TOKENS = 256
Q_BLOCK = ROW_TILE
KEY_TILE = ROW_TILE
SUM_ROWS = 16
GLA_CHUNK = 64
FFN_ROWS = 512
ROUTER_ROWS = 48
VMEM_LIMIT = 56 * 1024 * 1024
LOG2E = 1.4426950408889634
ATT_SCALE = LOG2E / math.sqrt(NOPE_DIM + ROPE_DIM)

_NT = (((1,), (1,)), ((), ()))
_TN = (((0,), (0,)), ((), ()))


def _params(sem):
    return pltpu.CompilerParams(dimension_semantics=sem, vmem_limit_bytes=VMEM_LIMIT)


def _rms(x, g):
    return x * lax.rsqrt(jnp.mean(x * x, axis=-1, keepdims=True) + EPS) * g


def _sigmoid(x):
    return 1.0 / (1.0 + jnp.exp(-x))


def _full(shape):
    n = len(shape)
    return pl.BlockSpec(shape, lambda *_: (0,) * n)


SLABS = D_MODEL // LANES


def _store_token_major(ref, x, first_row=0):
    rows = x.shape[0]
    for j in range(SLABS):
        ref[pl.ds(first_row * SLABS + j, rows, stride=SLABS), :] = x[:, j * LANES:(j + 1) * LANES]


def _load_token_major(ref, rows):
    return jnp.concatenate([ref[pl.ds(j, rows, stride=SLABS), :] for j in range(SLABS)], axis=1)


def _fold_q_kernel(wq_ref, wk_ref, o_ref):
    o_ref[0] = lax.dot_general(wq_ref[0], wk_ref[0], _NT, precision=lax.Precision.HIGHEST,
                               preferred_element_type=F32)


def _fold_q(wq_nope, wk_nope):
    return pl.pallas_call(
        _fold_q_kernel,
        grid=(MLA_HEADS,),
        in_specs=[pl.BlockSpec((1, Q_LORA, NOPE_DIM), lambda h: (h, 0, 0)),
                  pl.BlockSpec((1, KV_LORA, NOPE_DIM), lambda h: (h, 0, 0))],
        out_specs=pl.BlockSpec((1, Q_LORA, KV_LORA), lambda h: (h, 0, 0)),
        out_shape=jax.ShapeDtypeStruct((MLA_HEADS, Q_LORA, KV_LORA), F32),
        compiler_params=_params(("arbitrary",)),
        name="fold_q",
    )(wq_nope, wk_nope)


def _inproj_kernel(x_ref, cos_ref, sin_ref, gmix_ref, wsm_ref, gqn_ref, gkvn_ref, wqc_ref,
                   wg2_ref, bg2_ref, wgqk_ref, wgv_ref, wgr_ref, wga_ref, wgb_ref,
                   qcat_ref, kcat_ref, ckv_ref, kpe_ref, gq_ref, gk_ref, gv_ref, la_ref,
                   sgr_ref, sga_ref, sgb_ref, *maybe_vt_ref, transposed):
    hb = _rms(x_ref[...], gmix_ref[...]).astype(BF16)
    zs = jnp.dot(hb, wsm_ref[...], preferred_element_type=F32)
    cos = cos_ref[...]
    sin = sin_ref[...]
    first_half = lax.broadcasted_iota(jnp.int32, cos.shape, 1) < ROPE_DIM // 2

    def rope(v):
        rot = jnp.where(first_half, pltpu.roll(v, LANES - ROPE_DIM // 2, 1),
                        pltpu.roll(v, ROPE_DIM // 2, 1))
        return v * cos + rot * sin

    cqn = _rms(zs[:, :Q_LORA], gqn_ref[...]).astype(BF16)
    qc = jnp.dot(cqn, wqc_ref[...], preferred_element_type=F32)
    for h in range(MLA_HEADS):
        lat = qc[:, h * QCAT:h * QCAT + LANES]
        pe = qc[:, h * QCAT + LANES:(h + 1) * QCAT]
        if transposed:
            lat_t = jnp.transpose(lat * ATT_SCALE).astype(BF16)
            pe_t = jnp.transpose(rope(pe) * ATT_SCALE).astype(BF16)
            for blk in range(lat.shape[0] // Q_BLOCK):
                cols = slice(blk * Q_BLOCK, (blk + 1) * Q_BLOCK)
                qcat_ref[blk, :LANES, h * Q_BLOCK:(h + 1) * Q_BLOCK] = lat_t[:, cols]
                qcat_ref[blk, LANES:, h * Q_BLOCK:(h + 1) * Q_BLOCK] = pe_t[:, cols]
        else:
            qcat_ref[h, :, :LANES] = (lat * ATT_SCALE).astype(BF16)
            qcat_ref[h, :, LANES:] = (rope(pe) * ATT_SCALE).astype(BF16)

    ckv = _rms(zs[:, Q_LORA:Q_LORA + KV_LORA], gkvn_ref[...])
    ckv_ref[...] = ckv
    if transposed:
        maybe_vt_ref[0][0] = jnp.transpose(ckv).astype(BF16)
    kpe = rope(zs[:, Q_LORA + KV_LORA:Q_LORA + KV_LORA + LANES])
    kpe_ref[...] = kpe[:, :ROPE_DIM]
    kcat_ref[:, :LANES] = ckv.astype(BF16)
    kcat_ref[:, LANES:] = kpe.astype(BF16)

    glr = zs[:, Q_LORA + KV_LORA + LANES:].astype(BF16)
    xg = jnp.dot(glr, wg2_ref[...], preferred_element_type=F32) + bg2_ref[...]
    la_ref[...] = (jnp.minimum(xg, 0.0) - jnp.log(1.0 + jnp.exp(-jnp.abs(xg)))) * (1.0 / GATE_TEMP)

    zqk = jnp.dot(hb, wgqk_ref[...], preferred_element_type=F32)
    nqk = GLA_HEADS * GLA_DK
    gq_ref[...] = (zqk[:, :nqk] * (GLA_DK ** -0.5)).astype(BF16)
    gk_ref[...] = zqk[:, nqk:].astype(BF16)
    gv_ref[...] = jnp.dot(hb, wgv_ref[...], preferred_element_type=F32).astype(BF16)
    gr = jnp.dot(hb, wgr_ref[...], preferred_element_type=F32)
    sgr_ref[...] = (gr * _sigmoid(gr)).astype(BF16)
    sga_ref[...] = _sigmoid(jnp.dot(hb, wga_ref[...], preferred_element_type=F32)).astype(BF16)
    sgb_ref[...] = _sigmoid(jnp.dot(hb, wgb_ref[...], preferred_element_type=F32)).astype(BF16)


def _inproj(x2, cos_t, sin_t, w, transposed):
    n = x2.shape[0]
    tm = ROW_TILE
    nt = n // tm
    tab_blocks = cos_t.shape[0] // tm
    row = lambda width: pl.BlockSpec((tm, width), lambda i: (i, 0))
    tab = pl.BlockSpec((tm, LANES), lambda i: (i % tab_blocks, 0))
    nqk = GLA_HEADS * GLA_DK
    nv = GLA_HEADS * GLA_DV
    weights = [w["g_mix"], w["w_small"], w["g_qn"], w["g_kvn"], w["w_qcat"], w["w_g2"], w["b_g2"],
               w["w_gqk"], w["w_gv"], w["w_gr"], w["w_ga"], w["w_gb"]]
    out_shape = [
        jax.ShapeDtypeStruct((MLA_HEADS, n, QCAT), BF16),
        jax.ShapeDtypeStruct((n, QCAT), BF16),
        jax.ShapeDtypeStruct((n, KV_LORA), F32),
        jax.ShapeDtypeStruct((n, ROPE_DIM), F32),
        jax.ShapeDtypeStruct((n, nqk), BF16),
        jax.ShapeDtypeStruct((n, nqk), BF16),
        jax.ShapeDtypeStruct((n, nv), BF16),
        jax.ShapeDtypeStruct((n, nqk), F32),
        jax.ShapeDtypeStruct((n, nv), BF16),
        jax.ShapeDtypeStruct((n, D_MODEL), BF16),
        jax.ShapeDtypeStruct((n, D_MODEL), BF16),
    ]
    out_specs = [
        pl.BlockSpec((MLA_HEADS, tm, QCAT), lambda i: (0, i, 0)),
        row(QCAT), row(KV_LORA), row(ROPE_DIM), row(nqk), row(nqk), row(nv), row(nqk), row(nv),
        row(D_MODEL), row(D_MODEL),
    ]
    if transposed:
        qb = tm // Q_BLOCK
        out_shape[0] = jax.ShapeDtypeStruct((n // Q_BLOCK, QCAT, MLA_HEADS * Q_BLOCK), BF16)
        out_specs[0] = pl.BlockSpec((qb, QCAT, MLA_HEADS * Q_BLOCK), lambda i: (i, 0, 0))
        out_shape.append(jax.ShapeDtypeStruct((nt, KV_LORA, tm), BF16))
        out_specs.append(pl.BlockSpec((1, KV_LORA, tm), lambda i: (i, 0, 0)))
    return pl.pallas_call(
        functools.partial(_inproj_kernel, transposed=transposed),
        grid=(nt,),
        in_specs=[row(D_MODEL), tab, tab] + [_full(a.shape) for a in weights],
        out_specs=out_specs,
        out_shape=out_shape,
        compiler_params=_params(("parallel",)),
        name="inproj",
    )(x2, cos_t, sin_t, *weights)


def _head_out(o_lat, wuv_ref, rows):
    outs = []
    for h in range(MLA_HEADS):
        oh = o_lat[h * rows:(h + 1) * rows].astype(BF16)
        outs.append(jnp.dot(oh, wuv_ref[h], preferred_element_type=F32))
    return jnp.concatenate(outs, axis=1)


def _attn_prompt_kernel(qt_ref, k_ref, vt_ref, bias_ref, wuv_ref, o_ref, m_sc, l_sc, acc_sc):
    i = pl.program_id(1)
    m_sc[...] = jnp.full(m_sc.shape, -jnp.inf, F32)
    l_sc[...] = jnp.zeros(l_sc.shape, F32)
    acc_sc[...] = jnp.zeros(acc_sc.shape, F32)
    heads = [slice(h * Q_BLOCK, (h + 1) * Q_BLOCK) for h in range(MLA_HEADS)]
    ones_rows = jnp.ones((SUM_ROWS, KEY_TILE), BF16)

    def process(tiles, diagonal_last):
        nt = len(tiles)
        kbs = [k_ref[0, pl.ds(pl.multiple_of(kt * KEY_TILE, KEY_TILE), KEY_TILE), :] for kt in tiles]
        vts = [jnp.concatenate([vt_ref[kt], ones_rows], axis=0) for kt in tiles]
        ss = [[None] * MLA_HEADS for _ in range(nt)]
        if nt == 2 and not diagonal_last:
            kb_pair = jnp.concatenate(kbs, axis=0)
            for h, hs in enumerate(heads):
                s = jnp.dot(kb_pair, qt_ref[0, :, hs], preferred_element_type=F32)
                ss[0][h], ss[1][h] = s[:KEY_TILE], s[KEY_TILE:]
        else:
            for t in range(nt):
                for h, hs in enumerate(heads):
                    s = jnp.dot(kbs[t], qt_ref[0, :, hs], preferred_element_type=F32)
                    if diagonal_last and t == nt - 1:
                        s = s + bias_ref[...]
                    ss[t][h] = s
        tmax = [[jnp.max(ss[t][h], axis=0, keepdims=True) for h in range(MLA_HEADS)] for t in range(nt)]
        scale = [[None] * MLA_HEADS for _ in range(nt)]
        mrun = [[None] * MLA_HEADS for _ in range(nt)]
        for h, hs in enumerate(heads):
            m = m_sc[:, hs]
            for t in range(nt):
                m_new = jnp.maximum(m, tmax[t][h])
                scale[t][h] = jnp.exp2(m - m_new)
                mrun[t][h] = m_new
                m = m_new
            m_sc[:, hs] = m
        pvs = [[jnp.dot(vts[t], jnp.exp2(ss[t][h] - mrun[t][h]).astype(BF16), preferred_element_type=F32)
                for h in range(MLA_HEADS)] for t in range(nt)]
        for h, hs in enumerate(heads):
            l, acc = l_sc[:, hs], acc_sc[:, hs]
            for t in range(nt):
                acc = scale[t][h] * acc + pvs[t][h][:KV_LORA]
                l = scale[t][h] * l + pvs[t][h][KV_LORA:KV_LORA + 1]
            l_sc[:, hs], acc_sc[:, hs] = l, acc

    def body(j, carry):
        process([2 * j, 2 * j + 1], False)
        return carry

    lax.fori_loop(0, i // 2, body, 0)

    @pl.when(i % 2 == 1)
    def _():
        process([i - 1], False)

    process([i], True)

    o_lat_t = (acc_sc[...] / l_sc[...]).astype(BF16)
    outs = []
    for h, hs in enumerate(heads):
        outs.append(lax.dot_general(o_lat_t[:, hs], wuv_ref[h], _TN, preferred_element_type=F32))
    o_ref[0] = jnp.concatenate(outs, axis=1).astype(BF16)


def _attn_prompt(qt, kcat, vt, wuv, b, l):
    assert Q_BLOCK == KEY_TILE and Q_BLOCK % CHUNK == 0
    nq = l // Q_BLOCK
    nkt = l // KEY_TILE
    cols = MLA_HEADS * Q_BLOCK
    pos_chunk = jnp.arange(Q_BLOCK, dtype=jnp.int32) // CHUNK
    bias = jnp.where(pos_chunk[:, None] <= pos_chunk[None, :], 0.0, -jnp.inf).astype(F32)
    return pl.pallas_call(
        _attn_prompt_kernel,
        grid=(b, nq),
        in_specs=[pl.BlockSpec((1, QCAT, cols), lambda bb, i: (bb * nq + i, 0, 0)),
                  pl.BlockSpec((1, l, QCAT), lambda bb, i: (bb, 0, 0)),
                  pl.BlockSpec((nkt, KV_LORA, KEY_TILE), lambda bb, i: (bb, 0, 0)),
                  _full(bias.shape), _full(wuv.shape)],
        out_specs=pl.BlockSpec((1, Q_BLOCK, MLA_HEADS * V_DIM), lambda bb, i: (bb, i, 0)),
        out_shape=jax.ShapeDtypeStruct((b, l, MLA_HEADS * V_DIM), BF16),
        scratch_shapes=[pltpu.VMEM((1, cols), F32), pltpu.VMEM((1, cols), F32),
                        pltpu.VMEM((KV_LORA, cols), F32)],
        compiler_params=_params(("parallel", "arbitrary")),
        name="attn_prompt",
    )(qt, kcat.reshape(b, l, QCAT), vt, bias, wuv)


def _attn_sample_kernel(q_ref, cckv_ref, ckr_ref, kn_ref, wuv_ref, o_ref, *, ls):
    q = q_ref[...].reshape(MLA_HEADS * ls, QCAT)
    ck = cckv_ref[0].astype(BF16)
    kr = ckr_ref[0].astype(BF16)
    kn = kn_ref[...]
    s_c = (lax.dot_general(q[:, :KV_LORA], ck, _NT, preferred_element_type=F32)
           + lax.dot_general(q[:, KV_LORA:KV_LORA + ROPE_DIM], kr, _NT, preferred_element_type=F32))
    s_n = lax.dot_general(q, kn, _NT, preferred_element_type=F32)
    m = jnp.maximum(jnp.max(s_c, axis=1, keepdims=True), jnp.max(s_n, axis=1, keepdims=True))
    p_c = jnp.exp2(s_c - m)
    p_n = jnp.exp2(s_n - m)
    den = jnp.sum(p_c, axis=1, keepdims=True) + jnp.sum(p_n, axis=1, keepdims=True)
    o_lat = (jnp.dot(p_c.astype(BF16), ck, preferred_element_type=F32)
             + jnp.dot(p_n.astype(BF16), kn[:, :KV_LORA], preferred_element_type=F32)) / den
    o_ref[0] = _head_out(o_lat, wuv_ref, ls).astype(BF16)


def _attn_sample(qcat, kcat, cache_ckv, cache_krope, wuv, b, ls):
    past = cache_ckv.shape[1]
    assert past % CHUNK == 0 and ls <= CHUNK
    return pl.pallas_call(
        functools.partial(_attn_sample_kernel, ls=ls),
        grid=(b,),
        in_specs=[pl.BlockSpec((MLA_HEADS, ls, QCAT), lambda bb: (0, bb, 0)),
                  pl.BlockSpec((1, past, KV_LORA), lambda bb: (bb, 0, 0)),
                  pl.BlockSpec((1, past, ROPE_DIM), lambda bb: (bb, 0, 0)),
                  pl.BlockSpec((ls, QCAT), lambda bb: (bb, 0)),
                  _full(wuv.shape)],
        out_specs=pl.BlockSpec((1, ls, MLA_HEADS * V_DIM), lambda bb: (bb, 0, 0)),
        out_shape=jax.ShapeDtypeStruct((b, ls, MLA_HEADS * V_DIM), BF16),
        compiler_params=_params(("parallel",)),
        name="attn_sample",
    )(qcat, cache_ckv, cache_krope, kcat, wuv)


def _cumsum_rows(x):
    c = x.shape[0]
    row = lax.broadcasted_iota(jnp.int32, x.shape, 0)
    s = 1
    while s < c:
        x = x + jnp.where(row >= s, pltpu.roll(x, s, 0), 0.0)
        s *= 2
    return x


def _gla_kernel(q_ref, k_ref, v_ref, la_ref, sgr_ref, gn_ref, s0_ref, o_ref, sout_ref, st_sc, *, c):
    j = pl.program_id(1)

    @pl.when(j == 0)
    def _():
        st_sc[...] = s0_ref[...]

    half = 32
    row = lax.broadcasted_iota(jnp.int32, (c, GLA_DK), 0)
    rr = lax.broadcasted_iota(jnp.int32, (c, c), 0)
    cc = lax.broadcasted_iota(jnp.int32, (c, c), 1)
    causal = cc <= rr
    same_half = (rr < half) == (cc < half)
    gn = gn_ref[...]
    n_chunks = q_ref.shape[1] // c
    seq_heads = [(s, h) for s in range(GLA_BATCH) for h in range(GLA_HEADS)]
    chains = [(s, h, ci) for s, h in seq_heads for ci in range(n_chunks)]
    ksl = lambda h: slice(h * GLA_DK, (h + 1) * GLA_DK)
    vsl = lambda h: slice(h * GLA_DV, (h + 1) * GLA_DV)
    rsl = lambda ci: slice(ci * c, (ci + 1) * c)

    prep = {}
    for s, h, ci in chains:
        q = q_ref[s, rsl(ci), ksl(h)].astype(F32)
        k = k_ref[s, rsl(ci), ksl(h)].astype(F32)
        b = _cumsum_rows(la_ref[s, rsl(ci), ksl(h)])
        if c > half:
            mid = jnp.where(row < half, b[half // 2 - 1:half // 2, :], b[half + half // 2 - 1:half + half // 2, :])
        else:
            mid = jnp.broadcast_to(b[c // 2 - 1:c // 2, :], b.shape)
        last = b[c - 1:c, :]
        ops = {"qe": (q * jnp.exp(b - mid)).astype(BF16), "ke": (k * jnp.exp(mid - b)).astype(BF16),
               "q0": (q * jnp.exp(b)).astype(BF16), "kd": (k * jnp.exp(last - b)).astype(BF16), "last": last}
        if c > half:
            edge = b[half - 1:half, :]
            ops["qo"] = (q * jnp.exp(jnp.minimum(b - edge, 0.0))).astype(BF16)
            ops["ko"] = (k * jnp.exp(jnp.minimum(edge - b, 0.0))).astype(BF16)
        prep[s, h, ci] = ops
    att = {}
    for key in chains:
        a = lax.dot_general(prep[key]["qe"], prep[key]["ke"], _NT, preferred_element_type=F32)
        if c > half:
            a_off = lax.dot_general(prep[key]["qo"], prep[key]["ko"], _NT, preferred_element_type=F32)
            a = jnp.where(same_half, a, a_off)
        att[key] = jnp.where(causal, a, 0.0).astype(BF16)
    intra, upd, decay = {}, {}, {}
    for s, h, ci in chains:
        v = v_ref[s, rsl(ci), vsl(h)]
        intra[s, h, ci] = jnp.dot(att[s, h, ci], v, preferred_element_type=F32)
        upd[s, h, ci] = lax.dot_general(prep[s, h, ci]["kd"], v, _TN, preferred_element_type=F32)
        dcol = jnp.exp(jnp.transpose(jnp.broadcast_to(prep[s, h, ci]["last"], (GLA_DK, GLA_DK))))
        decay[s, h, ci] = jnp.concatenate([dcol, dcol], axis=1)
    for s, h in seq_heads:
        st = st_sc[s, h]
        for ci in range(n_chunks):
            o = jnp.dot(prep[s, h, ci]["q0"], st.astype(BF16), preferred_element_type=F32) + intra[s, h, ci]
            st = decay[s, h, ci] * st + upd[s, h, ci]
            on = _rms(o, gn) * sgr_ref[s, rsl(ci), vsl(h)].astype(F32)
            o_ref[s, rsl(ci), vsl(h)] = on.astype(BF16)
        st_sc[s, h] = st

    @pl.when(j == pl.num_programs(1) - 1)
    def _():
        sout_ref[...] = st_sc[...]


def _gla(gq, gk, gv, la, sgr, gn, s0, b, l):
    c = min(GLA_CHUNK, l)
    step = min(GLA_STEP_TOKENS, l)
    nc = l // step
    nqk = GLA_HEADS * GLA_DK
    nv = GLA_HEADS * GLA_DV
    r3 = lambda a: a.reshape(b, l, a.shape[-1])
    assert b % GLA_BATCH == 0 and l % step == 0 and step % c == 0
    tok = lambda width: pl.BlockSpec((GLA_BATCH, step, width), lambda bb, j: (bb, j, 0))
    st = pl.BlockSpec((GLA_BATCH, GLA_HEADS, GLA_DK, GLA_DV), lambda bb, j: (bb, 0, 0, 0))
    return pl.pallas_call(
        functools.partial(_gla_kernel, c=c),
        grid=(b // GLA_BATCH, nc),
        in_specs=[tok(nqk), tok(nqk), tok(nv), tok(nqk), tok(nv), _full(gn.shape), st],
        out_specs=[tok(nv), st],
        out_shape=[jax.ShapeDtypeStruct((b, l, nv), BF16),
                   jax.ShapeDtypeStruct((b, GLA_HEADS, GLA_DK, GLA_DV), F32)],
        scratch_shapes=[pltpu.VMEM((GLA_BATCH, GLA_HEADS, GLA_DK, GLA_DV), F32)],
        compiler_params=_params(("parallel", "arbitrary")),
        name="gla",
    )(r3(gq), r3(gk), r3(gv), r3(la), r3(sgr), gn, s0)


def _merge_kernel(oa_ref, ob_ref, sga_ref, sgb_ref, x_ref, wa_ref, wb_ref, wo_ref, gffn_ref,
                  wr_ref, br_ref, cnt0_ref, x1_ref, h2_ref, route_ref, gcol_ref, cnt_ref, meta_ref, run_sc):
    assert MERGE_PARTS == 1
    tm = x_ref.shape[0]
    pr = tm // MERGE_PARTS
    wr = wr_ref[...]
    whi = wr.astype(BF16)
    wlo = (wr - whi.astype(F32)).astype(BF16)
    ridx = lax.broadcasted_iota(jnp.int32, (EXPERTS_PER_GROUP, pr), 0)
    eidx = lax.broadcasted_iota(jnp.int32, (N_EXPERTS, pr), 0)
    earlier = jnp.where(lax.broadcasted_iota(jnp.int32, (pr, pr), 0) < lax.broadcasted_iota(jnp.int32, (pr, pr), 1),
                        1.0, 0.0).astype(BF16)
    big = jnp.int32(1 << 20)

    def top(vals):
        vmax = jnp.max(vals, axis=0, keepdims=True)
        imax = jnp.min(jnp.where(vals == vmax, ridx, big), axis=0, keepdims=True)
        return vmax, imax

    h2s = []
    for part in range(MERGE_PARTS):
        rows = slice(part * pr, (part + 1) * pr)
        ya = jnp.dot(oa_ref[rows, :], wa_ref[...], preferred_element_type=F32)
        yb = jnp.dot(ob_ref[rows, :], wb_ref[...], preferred_element_type=F32)
        merged = (sga_ref[rows, :].astype(F32) * ya + sgb_ref[rows, :].astype(F32) * yb).astype(BF16)
        x1 = x_ref[rows, :] + jnp.dot(merged, wo_ref[...], preferred_element_type=F32)
        x1_ref[rows, :] = x1
        h2 = _rms(x1, gffn_ref[...])
        _store_token_major(h2_ref, h2, part * pr)
        h2s.append(h2)

    routed = []
    for part in range(MERGE_PARTS):
        rows = slice(part * pr, (part + 1) * pr)
        hi = h2s[part].astype(BF16)
        lo = (h2s[part] - hi.astype(F32)).astype(BF16)
        logits = (lax.dot_general(whi, hi, _NT, preferred_element_type=F32)
                  + lax.dot_general(whi, lo, _NT, preferred_element_type=F32)
                  + lax.dot_general(wlo, hi, _NT, preferred_element_type=F32)) + br_ref[...]
        gl = jnp.where(ridx < N_GROUPS, logits[N_EXPERTS:N_EXPERTS + EXPERTS_PER_GROUP], -jnp.inf)
        gmax, g_top = top(gl)
        p_top = 1.0 / jnp.sum(jnp.exp(gl - gmax), axis=0, keepdims=True)
        e_sel = logits[:EXPERTS_PER_GROUP]
        for g in range(1, N_GROUPS):
            e_sel = jnp.where(g_top == g, logits[g * EXPERTS_PER_GROUP:(g + 1) * EXPERTS_PER_GROUP], e_sel)
        v1, i1 = top(e_sel)
        v2, i2 = top(jnp.where(ridx == i1, -jnp.inf, e_sel))
        e21 = jnp.exp(v2 - v1)
        w1 = p_top / (1.0 + e21)
        w2 = p_top * e21 / (1.0 + e21)
        base = g_top * EXPERTS_PER_GROUP
        id0 = base + i1
        id1 = base + i2
        gates = jnp.where(ridx == 0, w1, jnp.where(ridx == 1, w2, 0.0))
        gcol_ref[rows, :] = jnp.transpose(
            jnp.concatenate([gates, jnp.zeros((LANES - EXPERTS_PER_GROUP, pr), F32)], axis=0))
        oh0 = jnp.where(eidx == id0, 1.0, 0.0)
        oh1 = jnp.where(eidx == id1, 1.0, 0.0)
        p0 = jnp.dot(oh0.astype(BF16), earlier, preferred_element_type=F32)
        p1 = jnp.dot(oh1.astype(BF16), earlier, preferred_element_type=F32)
        c0 = jnp.sum(oh0, axis=1, keepdims=True)
        c1 = jnp.sum(oh1, axis=1, keepdims=True)
        routed.append((id0, id1, oh0, oh1, p0, p1, c0, c1))

    @pl.when(pl.program_id(0) == 0)
    def _():
        run_sc[...] = cnt0_ref[:, :1]

    run = run_sc[...]
    lane = lax.broadcasted_iota(jnp.int32, (N_EXPERTS, LANES), 1)
    for part, (id0, id1, oh0, oh1, p0, p1, c0, c1) in enumerate(routed):
        rank0 = jnp.sum(oh0 * (run + p0), axis=0, keepdims=True)
        rank1 = jnp.sum(oh1 * (run + c0 + p1), axis=0, keepdims=True)
        length = c0 + c1
        windows = jnp.ceil(length * (1.0 / RUN_WINDOW)) * RUN_WINDOW
        windows_b = jnp.broadcast_to(windows, (N_EXPERTS, LANES))
        stage_off = (_cumsum_rows(windows_b) - windows_b)[:, :1]
        pos0 = jnp.sum(oh0 * (stage_off + p0), axis=0, keepdims=True)
        pos1 = jnp.sum(oh1 * (stage_off + c0 + p1), axis=0, keepdims=True)
        meta_ref[...] = jnp.where(lane == 0, run.astype(jnp.int32), jnp.where(lane == 1, length.astype(jnp.int32), 0))
        run = run + length
        vals = (id0, id1, rank0.astype(jnp.int32), rank1.astype(jnp.int32),
                pos0.astype(jnp.int32), pos1.astype(jnp.int32))
        out = jnp.zeros((EXPERTS_PER_GROUP, pr), jnp.int32)
        for r, v in enumerate(vals):
            out = jnp.where(ridx == r, v, out)
        route_ref[:, part * pr:(part + 1) * pr] = out
    run_sc[...] = run
    cnt_ref[...] = jnp.broadcast_to(run, cnt_ref.shape)


def _merge_tile(n):
    return min(MERGE_TILE, max(ROW_TILE, n // 4))


def _merge(oa, ob, sga, sgb, x2, w, cnt0):
    n = x2.shape[0]
    tm = _merge_tile(n)
    assert n % tm == 0
    row = lambda width: pl.BlockSpec((tm, width), lambda i: (i, 0))
    weights = [w["w_a"], w["w_b"], w["w_o"], w["g_ffn"], w["w_router"], w["b_router"], cnt0]
    return pl.pallas_call(
        _merge_kernel,
        grid=(n // tm,),
        in_specs=[row(MLA_HEADS * V_DIM), row(GLA_HEADS * GLA_DV), row(D_MODEL), row(D_MODEL),
                  row(D_MODEL)] + [_full(a.shape) for a in weights],
        out_specs=[row(D_MODEL), pl.BlockSpec((tm * SLABS, LANES), lambda i: (i, 0)),
                   pl.BlockSpec((EXPERTS_PER_GROUP, tm), lambda i: (0, i)), row(LANES),
                   _full((N_EXPERTS, LANES)), pl.BlockSpec((N_EXPERTS, LANES), lambda i: (i, 0))],
        out_shape=[jax.ShapeDtypeStruct((n, D_MODEL), F32), jax.ShapeDtypeStruct((n * SLABS, LANES), F32),
                   jax.ShapeDtypeStruct((EXPERTS_PER_GROUP, n), jnp.int32),
                   jax.ShapeDtypeStruct((n, LANES), F32),
                   jax.ShapeDtypeStruct((N_EXPERTS, LANES), F32),
                   jax.ShapeDtypeStruct((n // tm * N_EXPERTS, LANES), jnp.int32)],
        scratch_shapes=[pltpu.VMEM((N_EXPERTS, 1), F32)],
        compiler_params=_params(("arbitrary",)),
        name="merge",
    )(oa, ob, sga, sgb, x2, *weights)


ROW_DMA_UNROLL = 8


def _row_tile(ref, r):
    return ref.at[pl.ds(pl.multiple_of(r * SLABS, SLABS), SLABS), :]


def _scatter_kernel(ps_ref, cnt_ref, pe_ref, *rest, tiles):
    ng = len(tiles)
    dest_refs, h2_refs = rest[:ng], rest[ng:2 * ng]
    xd_hbm, zero_sc, sem, zsem = rest[2 * ng:]
    tm = h2_refs[0].shape[0] // SLABS

    def zero_padding():
        zero_sc[...] = jnp.zeros(zero_sc.shape, F32)

        def pad_copies(e, act):
            lo = ps_ref[e] + cnt_ref[e]
            pad = pe_ref[e] - lo
            bit = FFN_ROWS // 2
            while bit >= 1:
                @pl.when((pad & bit) != 0)
                def _(bit=bit):
                    first = lo + (pad & ~(2 * bit - 1))
                    act(pltpu.make_async_copy(
                        zero_sc.at[pl.ds(0, bit * SLABS), :],
                        xd_hbm.at[pl.ds(pl.multiple_of(first * SLABS, SLABS), bit * SLABS), :], zsem.at[0]))
                bit //= 2

        def start_pads(e, c):
            pad_copies(e, lambda cp: cp.start())
            return c

        def wait_pads(e, c):
            pad_copies(e, lambda cp: cp.wait())
            return c

        lax.fori_loop(0, N_EXPERTS, start_pads, 0)
        lax.fori_loop(0, N_EXPERTS, wait_pads, 0)

        def block_copy(b):
            start_row = pl.multiple_of(b * (FFN_ROWS * SLABS), FFN_ROWS * SLABS)
            return pltpu.make_async_copy(zero_sc, xd_hbm.at[pl.ds(start_row, FFN_ROWS * SLABS), :], zsem.at[0])

        first_unused = pe_ref[N_EXPERTS - 1] // FFN_ROWS
        n_blocks = xd_hbm.shape[0] // (FFN_ROWS * SLABS)

        def start_block(b, c):
            block_copy(b).start()
            return c

        def wait_block(b, c):
            block_copy(b).wait()
            return c

        lax.fori_loop(first_unused, n_blocks, start_block, 0)
        lax.fori_loop(first_unused, n_blocks, wait_block, 0)

    step = pl.program_id(0)
    pl.when(step == 0)(zero_padding)

    first = 0
    for g, nt in enumerate(tiles):
        @pl.when((step >= first) & (step < first + nt))
        def _(dest_ref=dest_refs[g], h2_ref=h2_refs[g]):
            def body(t, carry):
                for k in range(2):
                    d = dest_ref[0, 0, k * tm + t]
                    pltpu.make_async_copy(_row_tile(h2_ref, t), _row_tile(xd_hbm, d), sem.at[0]).start(priority=k)
                return carry

            lax.fori_loop(0, tm, body, 0, unroll=ROW_DMA_UNROLL)
            for k in range(2):
                pltpu.make_async_copy(h2_ref, xd_hbm.at[pl.ds(0, tm * SLABS), :], sem.at[0]).wait()
        first += nt


def _scatter(pad_start, counts, pad_end, dest_blocks, h2s, total_rows):
    tm = ROW_TILE
    tiles = [d.shape[0] for d in dest_blocks]
    firsts = [sum(tiles[:g]) for g in range(len(tiles))]
    local = lambda g: (lambda i, *_: jnp.clip(i - firsts[g], 0, tiles[g] - 1))
    in_specs = ([pl.BlockSpec((1, 1, 2 * tm), (lambda i, *_, f=local(g): (f(i), 0, 0)), memory_space=pltpu.SMEM)
                 for g in range(len(tiles))]
                + [pl.BlockSpec((tm * SLABS, LANES), (lambda i, *_, f=local(g): (f(i), 0)))
                   for g in range(len(tiles))])
    grid_spec = pltpu.PrefetchScalarGridSpec(
        num_scalar_prefetch=3,
        grid=(sum(tiles),),
        in_specs=in_specs,
        out_specs=pl.BlockSpec(memory_space=pl.ANY),
        scratch_shapes=[pltpu.VMEM((FFN_ROWS * SLABS, LANES), F32), pltpu.SemaphoreType.DMA((1,)),
                        pltpu.SemaphoreType.DMA((1,))],
    )
    return pl.pallas_call(
        functools.partial(_scatter_kernel, tiles=tuple(tiles)),
        grid_spec=grid_spec,
        out_shape=jax.ShapeDtypeStruct((total_rows * SLABS, LANES), F32),
        compiler_params=_params(("arbitrary",)),
        name="moe_scatter",
    )(pad_start, counts, pad_end, *dest_blocks, *h2s)


def _ffn_kernel(be_ref, nbu_ref, x_ref, wg_ref, wu_ref, wd_ref, out_ref, wg_sc, wu_sc, wd_sc):
    i = pl.program_id(0)
    used = i < nbu_ref[0]
    new_expert = (i == 0) | (be_ref[i] != be_ref[jnp.maximum(i - 1, 0)])

    @pl.when(used & new_expert)
    def _():
        wg_sc[...] = wg_ref[0].astype(BF16)
        wu_sc[...] = wu_ref[0].astype(BF16)
        wd_sc[...] = wd_ref[0].astype(BF16)

    @pl.when(used)
    def _():
        x = _load_token_major(x_ref, FFN_ROWS).astype(BF16)
        g = jnp.dot(x, wg_sc[...], preferred_element_type=F32)
        u = jnp.dot(x, wu_sc[...], preferred_element_type=F32)
        mid = (g * _sigmoid(g) * u).astype(BF16)
        _store_token_major(out_ref, jnp.dot(mid, wd_sc[...], preferred_element_type=F32))

    @pl.when(jnp.logical_not(used))
    def _():
        out_ref[...] = jnp.zeros(out_ref.shape, F32)


def _ffn(block_expert, nb_used, xd, wg, wu, wd):
    nb = block_expert.shape[0]
    blk = lambda i, be, nbu: (jnp.minimum(i, nbu[0] - 1), 0)
    wsel = lambda i, be, nbu: (be[jnp.minimum(i, nbu[0] - 1)], 0, 0)
    grid_spec = pltpu.PrefetchScalarGridSpec(
        num_scalar_prefetch=2,
        grid=(nb,),
        in_specs=[pl.BlockSpec((FFN_ROWS * SLABS, LANES), blk),
                  pl.BlockSpec((1, D_MODEL, EXPERT_FF), wsel),
                  pl.BlockSpec((1, D_MODEL, EXPERT_FF), wsel),
                  pl.BlockSpec((1, EXPERT_FF, D_MODEL), wsel)],
        out_specs=pl.BlockSpec((FFN_ROWS * SLABS, LANES), lambda i, be, nbu: (i, 0)),
        scratch_shapes=[pltpu.VMEM((D_MODEL, EXPERT_FF), BF16), pltpu.VMEM((D_MODEL, EXPERT_FF), BF16),
                        pltpu.VMEM((EXPERT_FF, D_MODEL), BF16)],
    )
    return pl.pallas_call(
        _ffn_kernel,
        grid_spec=grid_spec,
        out_shape=jax.ShapeDtypeStruct((nb * FFN_ROWS * SLABS, LANES), F32),
        compiler_params=_params(("arbitrary",)),
        name="expert_ffn",
    )(block_expert, nb_used, xd, wg, wu, wd)


def _combine_kernel(ps_ref, meta_ref, metan_ref, pos_ref, x1_ref, gcol_ref, gfin_ref, eo_hbm, y_ref,
                    stage, ybuf, sem):
    i = pl.program_id(0)
    n = pl.num_programs(0)
    tm = x1_ref.shape[0]
    slot = i % 2

    def windows(meta, s, act):
        def per_expert(e, first):
            src = ps_ref[e] + meta[e, 0]
            n_win = (meta[e, 1] + RUN_WINDOW - 1) // RUN_WINDOW

            def per_window(wi, c):
                act(pltpu.make_async_copy(
                    eo_hbm.at[pl.ds(pl.multiple_of((src + wi * RUN_WINDOW) * SLABS, SLABS), RUN_WINDOW * SLABS), :],
                    stage.at[s, pl.ds(pl.multiple_of((first + wi * RUN_WINDOW) * SLABS, SLABS), RUN_WINDOW * SLABS), :],
                    sem.at[s]))
                return c

            lax.fori_loop(0, n_win, per_window, 0)
            return first + n_win * RUN_WINDOW

        lax.fori_loop(0, N_EXPERTS, per_expert, 0)

    @pl.when(i == 0)
    def _():
        windows(meta_ref, 0, lambda cp: cp.start())

    @pl.when(i + 1 < n)
    def _():
        windows(metan_ref, 1 - slot, lambda cp: cp.start())

    windows(meta_ref, slot, lambda cp: cp.wait())

    def body(t, carry):
        for k in range(2):
            p = pos_ref[0, 0, k * tm + t]
            _row_tile(ybuf.at[k], t)[...] = _row_tile(stage.at[slot], p)[...]
        return carry

    lax.fori_loop(0, tm, body, 0, unroll=ROW_DMA_UNROLL)
    g = gcol_ref[...]
    y = x1_ref[...] + (g[:, 0:1] * _load_token_major(ybuf.at[0], tm) + g[:, 1:2] * _load_token_major(ybuf.at[1], tm))
    y_ref[...] = _rms(y, gfin_ref[...])


def _combine(pad_start, meta, pos_blocks, x1, gcol, g_final, eo):
    n = x1.shape[0]
    tm = _merge_tile(n)
    nt = n // tm
    stage_rows = 2 * tm + N_EXPERTS * RUN_WINDOW
    grid_spec = pltpu.PrefetchScalarGridSpec(
        num_scalar_prefetch=1,
        grid=(nt,),
        in_specs=[pl.BlockSpec((N_EXPERTS, LANES), lambda i, ps: (i, 0), memory_space=pltpu.SMEM),
                  pl.BlockSpec((N_EXPERTS, LANES), lambda i, ps: (jnp.minimum(i + 1, nt - 1), 0),
                               memory_space=pltpu.SMEM),
                  pl.BlockSpec((1, 1, 2 * tm), lambda i, ps: (i, 0, 0), memory_space=pltpu.SMEM),
                  pl.BlockSpec((tm, D_MODEL), lambda i, ps: (i, 0)),
                  pl.BlockSpec((tm, LANES), lambda i, ps: (i, 0)),
                  pl.BlockSpec(g_final.shape, lambda i, ps: (0, 0)),
                  pl.BlockSpec(memory_space=pl.ANY)],
        out_specs=pl.BlockSpec((tm, D_MODEL), lambda i, ps: (i, 0)),
        scratch_shapes=[pltpu.VMEM((2, stage_rows * SLABS, LANES), F32),
                        pltpu.VMEM((2, tm * SLABS, LANES), F32), pltpu.SemaphoreType.DMA((2,))],
    )
    return pl.pallas_call(
        _combine_kernel,
        grid_spec=grid_spec,
        out_shape=jax.ShapeDtypeStruct((n, D_MODEL), F32),
        compiler_params=_params(("arbitrary",)),
        name="combine",
    )(pad_start, meta, meta, pos_blocks, x1, gcol, g_final, eo)


def _dest_kernel(ps_ref, route_ref, o_ref):
    route = route_ref[...]
    start = jnp.zeros(route.shape, jnp.int32)
    for e in range(N_EXPERTS):
        start = jnp.where(route == e, ps_ref[e], start)
    o_ref[...] = start + pltpu.roll(route, EXPERTS_PER_GROUP - 2, 0)


def _dest_rows(pad_start, route):
    n = route.shape[1]
    tile = min(n, 8192)
    assert n % tile == 0
    grid_spec = pltpu.PrefetchScalarGridSpec(
        num_scalar_prefetch=1,
        grid=(n // tile,),
        in_specs=[pl.BlockSpec((EXPERTS_PER_GROUP, tile), lambda i, ps: (0, i))],
        out_specs=pl.BlockSpec((EXPERTS_PER_GROUP, tile), lambda i, ps: (0, i)),
    )
    return pl.pallas_call(
        _dest_kernel,
        grid_spec=grid_spec,
        out_shape=jax.ShapeDtypeStruct(route.shape, jnp.int32),
        compiler_params=_params(("arbitrary",)),
        name="moe_dest",
    )(pad_start, route)


def _layout(cnt, n_tokens):
    counts = cnt[:, 0].astype(jnp.int32)
    padded = (counts + FFN_ROWS - 1) // FFN_ROWS * FFN_ROWS
    pad_end = jnp.cumsum(padded).astype(jnp.int32)
    pad_start = pad_end - padded
    nb = -(-2 * n_tokens // FFN_ROWS) + N_EXPERTS + 1
    block_start = jnp.arange(nb, dtype=jnp.int32) * FFN_ROWS
    block_expert = jnp.minimum(jnp.sum((block_start[:, None] >= pad_end[None, :]).astype(jnp.int32), axis=1),
                               N_EXPERTS - 1).astype(jnp.int32)
    nb_used = (pad_end[N_EXPERTS - 1:] // FFN_ROWS).astype(jnp.int32)
    return pad_start, counts, pad_end, block_expert, nb_used, nb * FFN_ROWS


def _dest_blocks(pad_start, route):
    n = route.shape[1]
    dest = _dest_rows(pad_start, route)[:2]
    tm = ROW_TILE
    return dest.reshape(2, n // tm, tm).transpose(1, 0, 2).reshape(n // tm, 1, 2 * tm)


def _rope_tables(l, past, rows):
    half = ROPE_DIM // 2
    inv = ROPE_THETA ** (-jnp.arange(half, dtype=F32) / half)
    pos = (past + (jnp.arange(rows, dtype=jnp.int32) % l)).astype(F32)
    ang = pos[:, None] * inv[None, :]
    pad = jnp.zeros((rows, LANES - ROPE_DIM), F32)
    cos = jnp.concatenate([jnp.cos(ang), jnp.cos(ang), pad], axis=1)
    sin = jnp.concatenate([-jnp.sin(ang), jnp.sin(ang), pad], axis=1)
    return cos, sin


def _mixers(x, ckv_past, kpe_past, s0, w):
    b, l, d = x.shape
    n = b * l
    x2 = x.reshape(n, d)
    past = 0 if ckv_past is None else ckv_past.shape[1]
    cos_t, sin_t = _rope_tables(l, past, max(l, ROW_TILE))
    prompt = ckv_past is None
    (qcat, kcat, ckv, kpe, gq, gk, gv, la, sgr, sga, sgb, *maybe_vt) = _inproj(x2, cos_t, sin_t, w, prompt)
    if prompt:
        oa = _attn_prompt(qcat, kcat, maybe_vt[0], w["w_uv"], b, l)
    else:
        oa = _attn_sample(qcat, kcat, ckv_past, kpe_past, w["w_uv"], b, l)
    ob, s_new = _gla(gq, gk, gv, la, sgr, w["g_gla"], s0, b, l)
    merge_in = (oa.reshape(n, -1), ob.reshape(n, -1), sga, sgb, x2)
    return merge_in, (ckv.reshape(b, l, KV_LORA), kpe.reshape(b, l, ROPE_DIM), s_new)


def _layer(groups, w, g_final):
    fronts = [_mixers(*g, w) for g in groups]
    cnt = jnp.zeros((N_EXPERTS, LANES), F32)
    merged = []
    for merge_in, _ in fronts:
        x1, h2, route, gcol, cnt, meta = _merge(*merge_in, w, cnt)
        merged.append((x1, h2, route, gcol, meta))
    n_tokens = sum(m[0].shape[0] for m in merged)
    pad_start, counts, pad_end, block_expert, nb_used, total_rows = _layout(cnt, n_tokens)
    dests = [_dest_blocks(pad_start, m[2]) for m in merged]
    xd = _scatter(pad_start, counts, pad_end, dests, [m[1] for m in merged], total_rows)
    eo = _ffn(block_expert, nb_used, xd, w["w_eg"], w["w_eu"], w["w_ed"])
    outs = []
    for (x1, h2, route, gcol, meta), (_, extras), g in zip(merged, fronts, groups):
        n = x1.shape[0]
        tm = _merge_tile(n)
        pos_blocks = route[4:6].reshape(2, n // tm, tm).transpose(1, 0, 2).reshape(n // tm, 1, 2 * tm)
        y = _combine(pad_start, meta, pos_blocks, x1, gcol, g_final, eo)
        outs.append((y.reshape(g[0].shape),) + extras)
    return outs


def _prep_weights(w_in, g_norm_mix, g_qnorm, w_uq, g_kvnorm, w_ukv, w_gate2, b_gate2, g_gla_norm,
                  w_branch_a, w_branch_b, w_out, g_norm_ffn, w_router_group, b_router_group,
                  w_router_expert, b_router_expert, w_exp_gate, w_exp_up, w_exp_down):
    nqk = GLA_HEADS * GLA_DK
    nv = GLA_HEADS * GLA_DV
    o = 0
    parts = {}
    for name, width in (("cq", Q_LORA), ("ckv", KV_LORA), ("kpe", ROPE_DIM), ("gq", nqk), ("gk", nqk),
                        ("gv", nv), ("glr", GATE_RANK), ("gr", nv), ("ga", D_MODEL), ("gb", D_MODEL)):
        parts[name] = w_in[:, o:o + width]
        o += width
    padc = lambda a, width: jnp.pad(a, ((0, 0), (0, width - a.shape[1])))
    w_small = jnp.concatenate([parts["cq"], parts["ckv"], padc(parts["kpe"], LANES),
                               padc(parts["glr"], LANES)], axis=1).astype(BF16)
    uq = w_uq.reshape(Q_LORA, MLA_HEADS, NOPE_DIM + ROPE_DIM)
    ukv = w_ukv.reshape(KV_LORA, MLA_HEADS, NOPE_DIM + V_DIM)
    lat = _fold_q(uq[:, :, :NOPE_DIM].transpose(1, 0, 2), ukv[:, :, :NOPE_DIM].transpose(1, 0, 2))
    q_rope = uq[:, :, NOPE_DIM:].transpose(1, 0, 2)
    w_qcat = jnp.concatenate([lat, q_rope, jnp.zeros((MLA_HEADS, Q_LORA, QCAT - KV_LORA - ROPE_DIM), F32)],
                             axis=2)
    w_qcat = w_qcat.transpose(1, 0, 2).reshape(Q_LORA, MLA_HEADS * QCAT).astype(BF16)
    w_router = jnp.concatenate([w_router_expert.T, w_router_group.T,
                                jnp.zeros((ROUTER_ROWS - N_EXPERTS - N_GROUPS, D_MODEL), F32)], axis=0)
    b_router = jnp.concatenate([b_router_expert, b_router_group,
                                jnp.zeros((ROUTER_ROWS - N_EXPERTS - N_GROUPS,), F32)]).reshape(ROUTER_ROWS, 1)
    return {
        "g_mix": g_norm_mix.reshape(1, D_MODEL), "w_small": w_small,
        "g_qn": g_qnorm.reshape(1, Q_LORA), "g_kvn": g_kvnorm.reshape(1, KV_LORA), "w_qcat": w_qcat,
        "w_g2": jnp.pad(w_gate2, ((0, LANES - GATE_RANK), (0, 0))).astype(BF16),
        "b_g2": b_gate2.reshape(1, nqk),
        "w_gqk": jnp.concatenate([parts["gq"], parts["gk"]], axis=1).astype(BF16),
        "w_gv": parts["gv"].astype(BF16), "w_gr": parts["gr"].astype(BF16),
        "w_ga": parts["ga"].astype(BF16), "w_gb": parts["gb"].astype(BF16),
        "w_uv": ukv[:, :, NOPE_DIM:].transpose(1, 0, 2).astype(BF16),
        "g_gla": g_gla_norm.reshape(1, GLA_DV),
        "w_a": w_branch_a.astype(BF16), "w_b": w_branch_b.astype(BF16), "w_o": w_out.astype(BF16),
        "g_ffn": g_norm_ffn.reshape(1, D_MODEL), "w_router": w_router, "b_router": b_router,
        "w_eg": w_exp_gate, "w_eu": w_exp_up, "w_ed": w_exp_down,
    }


def kernel(x_prompt, x_sample, cache_ckv, cache_krope, state_gla, w_in, g_norm_mix, g_qnorm, w_uq, g_kvnorm, w_ukv, w_gate2, b_gate2, g_gla_norm, w_branch_a, w_branch_b, w_out, g_norm_ffn, w_router_group, b_router_group, w_router_expert, b_router_expert, w_exp_gate, w_exp_up, w_exp_down, g_norm_final):
    depth = w_in.shape[0]
    assert depth == 1, "the final norm is fused into the last layer's combine step"
    gfin = g_norm_final.reshape(1, D_MODEL)
    w = _prep_weights(w_in[0], g_norm_mix[0], g_qnorm[0], w_uq[0], g_kvnorm[0], w_ukv[0], w_gate2[0],
                      b_gate2[0], g_gla_norm[0], w_branch_a[0], w_branch_b[0], w_out[0], g_norm_ffn[0],
                      w_router_group[0], b_router_group[0], w_router_expert[0], b_router_expert[0],
                      w_exp_gate[0], w_exp_up[0], w_exp_down[0])
    bp = x_prompt.shape[0]
    zero_state = jnp.zeros((bp, GLA_HEADS, GLA_DK, GLA_DV), F32)
    (yp, c1, k1, s1), (ys, c2, k2, s2) = _layer(
        [(x_prompt, None, None, zero_state),
         (x_sample, cache_ckv[0], cache_krope[0], state_gla[0].astype(F32))], w, gfin)
    return (yp, ys, c1[None], k1[None], s1[None], c2[None], k2[None], s2[None])
```

```python
import functools
import math

import jax
import jax.numpy as jnp
from jax import lax
from jax.experimental import pallas as pl
from jax.experimental.pallas import tpu as pltpu

F32 = jnp.float32
BF16 = jnp.bfloat16

D_MODEL = 1024
CHUNK = 64
EPS = 1e-6
MLA_HEADS = 8
Q_LORA = 256
KV_LORA = 128
NOPE_DIM = 64
ROPE_DIM = 32
V_DIM = 64
ROPE_THETA = 10000.0
GLA_HEADS = 4
GLA_DK = 128
GLA_DV = 256
GATE_RANK = 16
GATE_TEMP = 16.0
N_GROUPS = 4
EXPERTS_PER_GROUP = 8
N_EXPERTS = 32
EXPERT_FF = 512

LANES = 128
QCAT = 2 * LANES
ROW_TILE = 256
MERGE_TILE = 512
MERGE_PARTS = 1
RUN_WINDOW = 32
GLA_BATCH = 2
GLA_STEP_TOKENS = 256
Q_BLOCK = ROW_TILE
KEY_TILE = ROW_TILE
SUM_ROWS = 16
GLA_CHUNK = 64
FFN_ROWS = 512
ROUTER_ROWS = 48
VMEM_LIMIT = 56 * 1024 * 1024
LOG2E = 1.4426950408889634
ATT_SCALE = LOG2E / math.sqrt(NOPE_DIM + ROPE_DIM)

_NT = (((1,), (1,)), ((), ()))
_TN = (((0,), (0,)), ((), ()))


def _params(sem):
    return pltpu.CompilerParams(dimension_semantics=sem, vmem_limit_bytes=VMEM_LIMIT)


def _rms(x, g):
    return x * lax.rsqrt(jnp.mean(x * x, axis=-1, keepdims=True) + EPS) * g


def _sigmoid(x):
    return 1.0 / (1.0 + jnp.exp(-x))


def _full(shape):
    n = len(shape)
    return pl.BlockSpec(shape, lambda *_: (0,) * n)


SLABS = D_MODEL // LANES


def _store_token_major(ref, x, first_row=0):
    rows = x.shape[0]
    for j in range(SLABS):
        ref[pl.ds(first_row * SLABS + j, rows, stride=SLABS), :] = x[:, j * LANES:(j + 1) * LANES]


def _load_token_major(ref, rows):
    return jnp.concatenate([ref[pl.ds(j, rows, stride=SLABS), :] for j in range(SLABS)], axis=1)


def _fold_q_kernel(wq_ref, wk_ref, o_ref):
    o_ref[0] = lax.dot_general(wq_ref[0], wk_ref[0], _NT, precision=lax.Precision.HIGHEST,
                               preferred_element_type=F32)


def _fold_q(wq_nope, wk_nope):
    return pl.pallas_call(
        _fold_q_kernel,
        grid=(MLA_HEADS,),
        in_specs=[pl.BlockSpec((1, Q_LORA, NOPE_DIM), lambda h: (h, 0, 0)),
                  pl.BlockSpec((1, KV_LORA, NOPE_DIM), lambda h: (h, 0, 0))],
        out_specs=pl.BlockSpec((1, Q_LORA, KV_LORA), lambda h: (h, 0, 0)),
        out_shape=jax.ShapeDtypeStruct((MLA_HEADS, Q_LORA, KV_LORA), F32),
        compiler_params=_params(("arbitrary",)),
        name="fold_q",
    )(wq_nope, wk_nope)


def _inproj_kernel(x_ref, cos_ref, sin_ref, gmix_ref, wsm_ref, gqn_ref, gkvn_ref, wqc_ref,
                   wg2_ref, bg2_ref, wgqk_ref, wgv_ref, wgr_ref, wga_ref, wgb_ref,
                   qcat_ref, kcat_ref, ckv_ref, kpe_ref, gq_ref, gk_ref, gv_ref, la_ref,
                   sgr_ref, sga_ref, sgb_ref, *maybe_vt_ref, transposed):
    hb = _rms(x_ref[...], gmix_ref[...]).astype(BF16)
    zs = jnp.dot(hb, wsm_ref[...], preferred_element_type=F32)
    cos = cos_ref[...]
    sin = sin_ref[...]
    first_half = lax.broadcasted_iota(jnp.int32, cos.shape, 1) < ROPE_DIM // 2

    def rope(v):
        rot = jnp.where(first_half, pltpu.roll(v, LANES - ROPE_DIM // 2, 1),
                        pltpu.roll(v, ROPE_DIM // 2, 1))
        return v * cos + rot * sin

    cqn = _rms(zs[:, :Q_LORA], gqn_ref[...]).astype(BF16)
    qc = jnp.dot(cqn, wqc_ref[...], preferred_element_type=F32)
    for h in range(MLA_HEADS):
        lat = qc[:, h * QCAT:h * QCAT + LANES]
        pe = qc[:, h * QCAT + LANES:(h + 1) * QCAT]
        if transposed:
            lat_t = jnp.transpose(lat * ATT_SCALE).astype(BF16)
            pe_t = jnp.transpose(rope(pe) * ATT_SCALE).astype(BF16)
            for blk in range(lat.shape[0] // Q_BLOCK):
                cols = slice(blk * Q_BLOCK, (blk + 1) * Q_BLOCK)
                qcat_ref[blk, :LANES, h * Q_BLOCK:(h + 1) * Q_BLOCK] = lat_t[:, cols]
                qcat_ref[blk, LANES:, h * Q_BLOCK:(h + 1) * Q_BLOCK] = pe_t[:, cols]
        else:
            qcat_ref[h, :, :LANES] = (lat * ATT_SCALE).astype(BF16)
            qcat_ref[h, :, LANES:] = (rope(pe) * ATT_SCALE).astype(BF16)

    ckv = _rms(zs[:, Q_LORA:Q_LORA + KV_LORA], gkvn_ref[...])
    ckv_ref[...] = ckv
    if transposed:
        maybe_vt_ref[0][0] = jnp.transpose(ckv).astype(BF16)
    kpe = rope(zs[:, Q_LORA + KV_LORA:Q_LORA + KV_LORA + LANES])
    kpe_ref[...] = kpe[:, :ROPE_DIM]
    kcat_ref[:, :LANES] = ckv.astype(BF16)
    kcat_ref[:, LANES:] = kpe.astype(BF16)

    glr = zs[:, Q_LORA + KV_LORA + LANES:].astype(BF16)
    xg = jnp.dot(glr, wg2_ref[...], preferred_element_type=F32) + bg2_ref[...]
    la_ref[...] = (jnp.minimum(xg, 0.0) - jnp.log(1.0 + jnp.exp(-jnp.abs(xg)))) * (1.0 / GATE_TEMP)

    zqk = jnp.dot(hb, wgqk_ref[...], preferred_element_type=F32)
    nqk = GLA_HEADS * GLA_DK
    gq_ref[...] = (zqk[:, :nqk] * (GLA_DK ** -0.5)).astype(BF16)
    gk_ref[...] = zqk[:, nqk:].astype(BF16)
    gv_ref[...] = jnp.dot(hb, wgv_ref[...], preferred_element_type=F32).astype(BF16)
    gr = jnp.dot(hb, wgr_ref[...], preferred_element_type=F32)
    sgr_ref[...] = (gr * _sigmoid(gr)).astype(BF16)
    sga_ref[...] = _sigmoid(jnp.dot(hb, wga_ref[...], preferred_element_type=F32)).astype(BF16)
    sgb_ref[...] = _sigmoid(jnp.dot(hb, wgb_ref[...], preferred_element_type=F32)).astype(BF16)


def _inproj(x2, cos_t, sin_t, w, transposed):
    n = x2.shape[0]
    tm = ROW_TILE
    nt = n // tm
    tab_blocks = cos_t.shape[0] // tm
    row = lambda width: pl.BlockSpec((tm, width), lambda i: (i, 0))
    tab = pl.BlockSpec((tm, LANES), lambda i: (i % tab_blocks, 0))
    nqk = GLA_HEADS * GLA_DK
    nv = GLA_HEADS * GLA_DV
    weights = [w["g_mix"], w["w_small"], w["g_qn"], w["g_kvn"], w["w_qcat"], w["w_g2"], w["b_g2"],
               w["w_gqk"], w["w_gv"], w["w_gr"], w["w_ga"], w["w_gb"]]
    out_shape = [
        jax.ShapeDtypeStruct((MLA_HEADS, n, QCAT), BF16),
        jax.ShapeDtypeStruct((n, QCAT), BF16),
        jax.ShapeDtypeStruct((n, KV_LORA), F32),
        jax.ShapeDtypeStruct((n, ROPE_DIM), F32),
        jax.ShapeDtypeStruct((n, nqk), BF16),
        jax.ShapeDtypeStruct((n, nqk), BF16),
        jax.ShapeDtypeStruct((n, nv), BF16),
        jax.ShapeDtypeStruct((n, nqk), F32),
        jax.ShapeDtypeStruct((n, nv), BF16),
        jax.ShapeDtypeStruct((n, D_MODEL), BF16),
        jax.ShapeDtypeStruct((n, D_MODEL), BF16),
    ]
    out_specs = [
        pl.BlockSpec((MLA_HEADS, tm, QCAT), lambda i: (0, i, 0)),
        row(QCAT), row(KV_LORA), row(ROPE_DIM), row(nqk), row(nqk), row(nv), row(nqk), row(nv),
        row(D_MODEL), row(D_MODEL),
    ]
    if transposed:
        qb = tm // Q_BLOCK
        out_shape[0] = jax.ShapeDtypeStruct((n // Q_BLOCK, QCAT, MLA_HEADS * Q_BLOCK), BF16)
        out_specs[0] = pl.BlockSpec((qb, QCAT, MLA_HEADS * Q_BLOCK), lambda i: (i, 0, 0))
        out_shape.append(jax.ShapeDtypeStruct((nt, KV_LORA, tm), BF16))
        out_specs.append(pl.BlockSpec((1, KV_LORA, tm), lambda i: (i, 0, 0)))
    return pl.pallas_call(
        functools.partial(_inproj_kernel, transposed=transposed),
        grid=(nt,),
        in_specs=[row(D_MODEL), tab, tab] + [_full(a.shape) for a in weights],
        out_specs=out_specs,
        out_shape=out_shape,
        compiler_params=_params(("parallel",)),
        name="inproj",
    )(x2, cos_t, sin_t, *weights)


def _head_out(o_lat, wuv_ref, rows):
    outs = []
    for h in range(MLA_HEADS):
        oh = o_lat[h * rows:(h + 1) * rows].astype(BF16)
        outs.append(jnp.dot(oh, wuv_ref[h], preferred_element_type=F32))
    return jnp.concatenate(outs, axis=1)


def _attn_prompt_kernel(qt_ref, k_ref, vt_ref, bias_ref, wuv_ref, o_ref, m_sc, l_sc, acc_sc):
    i = pl.program_id(1)
    m_sc[...] = jnp.full(m_sc.shape, -jnp.inf, F32)
    l_sc[...] = jnp.zeros(l_sc.shape, F32)
    acc_sc[...] = jnp.zeros(acc_sc.shape, F32)
    heads = [slice(h * Q_BLOCK, (h + 1) * Q_BLOCK) for h in range(MLA_HEADS)]
    ones_rows = jnp.ones((SUM_ROWS, KEY_TILE), BF16)

    def process(tiles, diagonal_last):
        nt = len(tiles)
        kbs = [k_ref[0, pl.ds(pl.multiple_of(kt * KEY_TILE, KEY_TILE), KEY_TILE), :] for kt in tiles]
        vts = [jnp.concatenate([vt_ref[kt], ones_rows], axis=0) for kt in tiles]
        ss = [[None] * MLA_HEADS for _ in range(nt)]
        for t in range(nt):
            for h, hs in enumerate(heads):
                s = jnp.dot(kbs[t], qt_ref[0, :, hs], preferred_element_type=F32)
                if diagonal_last and t == nt - 1:
                    s = s + bias_ref[...]
                ss[t][h] = s
        tmax = [[jnp.max(ss[t][h], axis=0, keepdims=True) for h in range(MLA_HEADS)] for t in range(nt)]
        scale = [[None] * MLA_HEADS for _ in range(nt)]
        mrun = [[None] * MLA_HEADS for _ in range(nt)]
        for h, hs in enumerate(heads):
            m = m_sc[:, hs]
            for t in range(nt):
                m_new = jnp.maximum(m, tmax[t][h])
                scale[t][h] = jnp.exp2(m - m_new)
                mrun[t][h] = m_new
                m = m_new
            m_sc[:, hs] = m
        pvs = [[jnp.dot(vts[t], jnp.exp2(ss[t][h] - mrun[t][h]).astype(BF16), preferred_element_type=F32)
                for h in range(MLA_HEADS)] for t in range(nt)]
        for h, hs in enumerate(heads):
            l, acc = l_sc[:, hs], acc_sc[:, hs]
            for t in range(nt):
                acc = scale[t][h] * acc + pvs[t][h][:KV_LORA]
                l = scale[t][h] * l + pvs[t][h][KV_LORA:KV_LORA + 1]
            l_sc[:, hs], acc_sc[:, hs] = l, acc

    def body(j, carry):
        process([2 * j, 2 * j + 1], False)
        return carry

    lax.fori_loop(0, i // 2, body, 0)

    @pl.when(i % 2 == 1)
    def _():
        process([i - 1], False)

    process([i], True)

    o_lat_t = (acc_sc[...] / l_sc[...]).astype(BF16)
    outs = []
    for h, hs in enumerate(heads):
        outs.append(lax.dot_general(o_lat_t[:, hs], wuv_ref[h], _TN, preferred_element_type=F32))
    o_ref[0] = jnp.concatenate(outs, axis=1).astype(BF16)


def _attn_prompt(qt, kcat, vt, wuv, b, l):
    assert Q_BLOCK == KEY_TILE and Q_BLOCK % CHUNK == 0
    nq = l // Q_BLOCK
    nkt = l // KEY_TILE
    cols = MLA_HEADS * Q_BLOCK
    pos_chunk = jnp.arange(Q_BLOCK, dtype=jnp.int32) // CHUNK
    bias = jnp.where(pos_chunk[:, None] <= pos_chunk[None, :], 0.0, -jnp.inf).astype(F32)
    return pl.pallas_call(
        _attn_prompt_kernel,
        grid=(b, nq),
        in_specs=[pl.BlockSpec((1, QCAT, cols), lambda bb, i: (bb * nq + i, 0, 0)),
                  pl.BlockSpec((1, l, QCAT), lambda bb, i: (bb, 0, 0)),
                  pl.BlockSpec((nkt, KV_LORA, KEY_TILE), lambda bb, i: (bb, 0, 0)),
                  _full(bias.shape), _full(wuv.shape)],
        out_specs=pl.BlockSpec((1, Q_BLOCK, MLA_HEADS * V_DIM), lambda bb, i: (bb, i, 0)),
        out_shape=jax.ShapeDtypeStruct((b, l, MLA_HEADS * V_DIM), BF16),
        scratch_shapes=[pltpu.VMEM((1, cols), F32), pltpu.VMEM((1, cols), F32),
                        pltpu.VMEM((KV_LORA, cols), F32)],
        compiler_params=_params(("parallel", "arbitrary")),
        name="attn_prompt",
    )(qt, kcat.reshape(b, l, QCAT), vt, bias, wuv)


def _attn_sample_kernel(q_ref, cckv_ref, ckr_ref, kn_ref, wuv_ref, o_ref, *, ls):
    q = q_ref[...].reshape(MLA_HEADS * ls, QCAT)
    ck = cckv_ref[0].astype(BF16)
    kr = ckr_ref[0].astype(BF16)
    kn = kn_ref[...]
    s_c = (lax.dot_general(q[:, :KV_LORA], ck, _NT, preferred_element_type=F32)
           + lax.dot_general(q[:, KV_LORA:KV_LORA + ROPE_DIM], kr, _NT, preferred_element_type=F32))
    s_n = lax.dot_general(q, kn, _NT, preferred_element_type=F32)
    m = jnp.maximum(jnp.max(s_c, axis=1, keepdims=True), jnp.max(s_n, axis=1, keepdims=True))
    p_c = jnp.exp2(s_c - m)
    p_n = jnp.exp2(s_n - m)
    den = jnp.sum(p_c, axis=1, keepdims=True) + jnp.sum(p_n, axis=1, keepdims=True)
    o_lat = (jnp.dot(p_c.astype(BF16), ck, preferred_element_type=F32)
             + jnp.dot(p_n.astype(BF16), kn[:, :KV_LORA], preferred_element_type=F32)) / den
    o_ref[0] = _head_out(o_lat, wuv_ref, ls).astype(BF16)


def _attn_sample(qcat, kcat, cache_ckv, cache_krope, wuv, b, ls):
    past = cache_ckv.shape[1]
    assert past % CHUNK == 0 and ls <= CHUNK
    return pl.pallas_call(
        functools.partial(_attn_sample_kernel, ls=ls),
        grid=(b,),
        in_specs=[pl.BlockSpec((MLA_HEADS, ls, QCAT), lambda bb: (0, bb, 0)),
                  pl.BlockSpec((1, past, KV_LORA), lambda bb: (bb, 0, 0)),
                  pl.BlockSpec((1, past, ROPE_DIM), lambda bb: (bb, 0, 0)),
                  pl.BlockSpec((ls, QCAT), lambda bb: (bb, 0)),
                  _full(wuv.shape)],
        out_specs=pl.BlockSpec((1, ls, MLA_HEADS * V_DIM), lambda bb: (bb, 0, 0)),
        out_shape=jax.ShapeDtypeStruct((b, ls, MLA_HEADS * V_DIM), BF16),
        compiler_params=_params(("parallel",)),
        name="attn_sample",
    )(qcat, cache_ckv, cache_krope, kcat, wuv)


def _cumsum_rows(x):
    c = x.shape[0]
    row = lax.broadcasted_iota(jnp.int32, x.shape, 0)
    s = 1
    while s < c:
        x = x + jnp.where(row >= s, pltpu.roll(x, s, 0), 0.0)
        s *= 2
    return x


def _gla_kernel(q_ref, k_ref, v_ref, la_ref, sgr_ref, gn_ref, s0_ref, o_ref, sout_ref, st_sc, *, c):
    j = pl.program_id(1)

    @pl.when(j == 0)
    def _():
        st_sc[...] = s0_ref[...]

    half = 32
    row = lax.broadcasted_iota(jnp.int32, (c, GLA_DK), 0)
    rr = lax.broadcasted_iota(jnp.int32, (c, c), 0)
    cc = lax.broadcasted_iota(jnp.int32, (c, c), 1)
    causal = cc <= rr
    same_half = (rr < half) == (cc < half)
    gn = gn_ref[...]
    n_chunks = q_ref.shape[1] // c
    seq_heads = [(s, h) for s in range(GLA_BATCH) for h in range(GLA_HEADS)]
    chains = [(s, h, ci) for s, h in seq_heads for ci in range(n_chunks)]
    ksl = lambda h: slice(h * GLA_DK, (h + 1) * GLA_DK)
    vsl = lambda h: slice(h * GLA_DV, (h + 1) * GLA_DV)
    rsl = lambda ci: slice(ci * c, (ci + 1) * c)

    prep = {}
    for s, h, ci in chains:
        q = q_ref[s, rsl(ci), ksl(h)].astype(F32)
        k = k_ref[s, rsl(ci), ksl(h)].astype(F32)
        b = _cumsum_rows(la_ref[s, rsl(ci), ksl(h)])
        if c > half:
            mid = jnp.where(row < half, b[half // 2 - 1:half // 2, :], b[half + half // 2 - 1:half + half // 2, :])
        else:
            mid = jnp.broadcast_to(b[c // 2 - 1:c // 2, :], b.shape)
        last = b[c - 1:c, :]
        ops = {"qe": (q * jnp.exp(b - mid)).astype(BF16), "ke": (k * jnp.exp(mid - b)).astype(BF16),
               "q0": (q * jnp.exp(b)).astype(BF16), "kd": (k * jnp.exp(last - b)).astype(BF16), "last": last}
        if c > half:
            edge = b[half - 1:half, :]
            ops["qo"] = (q * jnp.exp(jnp.minimum(b - edge, 0.0))).astype(BF16)
            ops["ko"] = (k * jnp.exp(jnp.minimum(edge - b, 0.0))).astype(BF16)
        prep[s, h, ci] = ops
    att = {}
    for key in chains:
        a = lax.dot_general(prep[key]["qe"], prep[key]["ke"], _NT, preferred_element_type=F32)
        if c > half:
            a_off = lax.dot_general(prep[key]["qo"], prep[key]["ko"], _NT, preferred_element_type=F32)
            a = jnp.where(same_half, a, a_off)
        att[key] = jnp.where(causal, a, 0.0).astype(BF16)
    intra, upd, decay = {}, {}, {}
    for s, h, ci in chains:
        v = v_ref[s, rsl(ci), vsl(h)]
        intra[s, h, ci] = jnp.dot(att[s, h, ci], v, preferred_element_type=F32)
        upd[s, h, ci] = lax.dot_general(prep[s, h, ci]["kd"], v, _TN, preferred_element_type=F32)
        dcol = jnp.exp(jnp.transpose(jnp.broadcast_to(prep[s, h, ci]["last"], (GLA_DK, GLA_DK))))
        decay[s, h, ci] = jnp.concatenate([dcol, dcol], axis=1)
    for s, h in seq_heads:
        st = st_sc[s, h]
        for ci in range(n_chunks):
            o = jnp.dot(prep[s, h, ci]["q0"], st.astype(BF16), preferred_element_type=F32) + intra[s, h, ci]
            st = decay[s, h, ci] * st + upd[s, h, ci]
            on = _rms(o, gn) * sgr_ref[s, rsl(ci), vsl(h)].astype(F32)
            o_ref[s, rsl(ci), vsl(h)] = on.astype(BF16)
        st_sc[s, h] = st

    @pl.when(j == pl.num_programs(1) - 1)
    def _():
        sout_ref[...] = st_sc[...]


def _gla(gq, gk, gv, la, sgr, gn, s0, b, l):
    c = min(GLA_CHUNK, l)
    step = min(GLA_STEP_TOKENS, l)
    nc = l // step
    nqk = GLA_HEADS * GLA_DK
    nv = GLA_HEADS * GLA_DV
    r3 = lambda a: a.reshape(b, l, a.shape[-1])
    assert b % GLA_BATCH == 0 and l % step == 0 and step % c == 0
    tok = lambda width: pl.BlockSpec((GLA_BATCH, step, width), lambda bb, j: (bb, j, 0))
    st = pl.BlockSpec((GLA_BATCH, GLA_HEADS, GLA_DK, GLA_DV), lambda bb, j: (bb, 0, 0, 0))
    return pl.pallas_call(
        functools.partial(_gla_kernel, c=c),
        grid=(b // GLA_BATCH, nc),
        in_specs=[tok(nqk), tok(nqk), tok(nv), tok(nqk), tok(nv), _full(gn.shape), st],
        out_specs=[tok(nv), st],
        out_shape=[jax.ShapeDtypeStruct((b, l, nv), BF16),
                   jax.ShapeDtypeStruct((b, GLA_HEADS, GLA_DK, GLA_DV), F32)],
        scratch_shapes=[pltpu.VMEM((GLA_BATCH, GLA_HEADS, GLA_DK, GLA_DV), F32)],
        compiler_params=_params(("parallel", "arbitrary")),
        name="gla",
    )(r3(gq), r3(gk), r3(gv), r3(la), r3(sgr), gn, s0)


def _merge_kernel(oa_ref, ob_ref, sga_ref, sgb_ref, x_ref, wa_ref, wb_ref, wo_ref, gffn_ref,
                  wr_ref, br_ref, cnt0_ref, x1_ref, h2_ref, route_ref, gcol_ref, cnt_ref, meta_ref, run_sc):
    assert MERGE_PARTS == 1
    tm = x_ref.shape[0]
    pr = tm // MERGE_PARTS
    wr = wr_ref[...]
    whi = wr.astype(BF16)
    wlo = (wr - whi.astype(F32)).astype(BF16)
    ridx = lax.broadcasted_iota(jnp.int32, (EXPERTS_PER_GROUP, pr), 0)
    eidx = lax.broadcasted_iota(jnp.int32, (N_EXPERTS, pr), 0)
    earlier = jnp.where(lax.broadcasted_iota(jnp.int32, (pr, pr), 0) < lax.broadcasted_iota(jnp.int32, (pr, pr), 1),
                        1.0, 0.0).astype(BF16)
    big = jnp.int32(1 << 20)

    def top(vals):
        vmax = jnp.max(vals, axis=0, keepdims=True)
        imax = jnp.min(jnp.where(vals == vmax, ridx, big), axis=0, keepdims=True)
        return vmax, imax

    h2s = []
    for part in range(MERGE_PARTS):
        rows = slice(part * pr, (part + 1) * pr)
        ya = jnp.dot(oa_ref[rows, :], wa_ref[...], preferred_element_type=F32)
        yb = jnp.dot(ob_ref[rows, :], wb_ref[...], preferred_element_type=F32)
        merged = (sga_ref[rows, :].astype(F32) * ya + sgb_ref[rows, :].astype(F32) * yb).astype(BF16)
        x1 = x_ref[rows, :] + jnp.dot(merged, wo_ref[...], preferred_element_type=F32)
        x1_ref[rows, :] = x1
        h2 = _rms(x1, gffn_ref[...])
        _store_token_major(h2_ref, h2, part * pr)
        h2s.append(h2)

    routed = []
    for part in range(MERGE_PARTS):
        rows = slice(part * pr, (part + 1) * pr)
        hi = h2s[part].astype(BF16)
        lo = (h2s[part] - hi.astype(F32)).astype(BF16)
        logits = (lax.dot_general(whi, hi, _NT, preferred_element_type=F32)
                  + lax.dot_general(whi, lo, _NT, preferred_element_type=F32)
                  + lax.dot_general(wlo, hi, _NT, preferred_element_type=F32)) + br_ref[...]
        gl = jnp.where(ridx < N_GROUPS, logits[N_EXPERTS:N_EXPERTS + EXPERTS_PER_GROUP], -jnp.inf)
        gmax, g_top = top(gl)
        p_top = 1.0 / jnp.sum(jnp.exp(gl - gmax), axis=0, keepdims=True)
        e_sel = logits[:EXPERTS_PER_GROUP]
        for g in range(1, N_GROUPS):
            e_sel = jnp.where(g_top == g, logits[g * EXPERTS_PER_GROUP:(g + 1) * EXPERTS_PER_GROUP], e_sel)
        v1, i1 = top(e_sel)
        v2, i2 = top(jnp.where(ridx == i1, -jnp.inf, e_sel))
        e21 = jnp.exp(v2 - v1)
        w1 = p_top / (1.0 + e21)
        w2 = p_top * e21 / (1.0 + e21)
        base = g_top * EXPERTS_PER_GROUP
        id0 = base + i1
        id1 = base + i2
        gates = jnp.where(ridx == 0, w1, jnp.where(ridx == 1, w2, 0.0))
        gcol_ref[rows, :] = jnp.transpose(
            jnp.concatenate([gates, jnp.zeros((LANES - EXPERTS_PER_GROUP, pr), F32)], axis=0))
        oh0 = jnp.where(eidx == id0, 1.0, 0.0)
        oh1 = jnp.where(eidx == id1, 1.0, 0.0)
        p0 = jnp.dot(oh0.astype(BF16), earlier, preferred_element_type=F32)
        p1 = jnp.dot(oh1.astype(BF16), earlier, preferred_element_type=F32)
        c0 = jnp.sum(oh0, axis=1, keepdims=True)
        c1 = jnp.sum(oh1, axis=1, keepdims=True)
        routed.append((id0, id1, oh0, oh1, p0, p1, c0, c1))

    @pl.when(pl.program_id(0) == 0)
    def _():
        run_sc[...] = cnt0_ref[:, :1]

    run = run_sc[...]
    lane = lax.broadcasted_iota(jnp.int32, (N_EXPERTS, LANES), 1)
    for part, (id0, id1, oh0, oh1, p0, p1, c0, c1) in enumerate(routed):
        rank0 = jnp.sum(oh0 * (run + p0), axis=0, keepdims=True)
        rank1 = jnp.sum(oh1 * (run + c0 + p1), axis=0, keepdims=True)
        length = c0 + c1
        windows = jnp.ceil(length * (1.0 / RUN_WINDOW)) * RUN_WINDOW
        windows_b = jnp.broadcast_to(windows, (N_EXPERTS, LANES))
        stage_off = (_cumsum_rows(windows_b) - windows_b)[:, :1]
        pos0 = jnp.sum(oh0 * (stage_off + p0), axis=0, keepdims=True)
        pos1 = jnp.sum(oh1 * (stage_off + c0 + p1), axis=0, keepdims=True)
        meta_ref[...] = jnp.where(lane == 0, run.astype(jnp.int32), jnp.where(lane == 1, length.astype(jnp.int32), 0))
        run = run + length
        vals = (id0, id1, rank0.astype(jnp.int32), rank1.astype(jnp.int32),
                pos0.astype(jnp.int32), pos1.astype(jnp.int32))
        out = jnp.zeros((EXPERTS_PER_GROUP, pr), jnp.int32)
        for r, v in enumerate(vals):
            out = jnp.where(ridx == r, v, out)
        route_ref[:, part * pr:(part + 1) * pr] = out
    run_sc[...] = run
    cnt_ref[...] = jnp.broadcast_to(run, cnt_ref.shape)


def _merge_tile(n):
    return min(MERGE_TILE, max(ROW_TILE, n // 4))


def _merge(oa, ob, sga, sgb, x2, w, cnt0):
    n = x2.shape[0]
    tm = _merge_tile(n)
    assert n % tm == 0
    row = lambda width: pl.BlockSpec((tm, width), lambda i: (i, 0))
    weights = [w["w_a"], w["w_b"], w["w_o"], w["g_ffn"], w["w_router"], w["b_router"], cnt0]
    return pl.pallas_call(
        _merge_kernel,
        grid=(n // tm,),
        in_specs=[row(MLA_HEADS * V_DIM), row(GLA_HEADS * GLA_DV), row(D_MODEL), row(D_MODEL),
                  row(D_MODEL)] + [_full(a.shape) for a in weights],
        out_specs=[row(D_MODEL), pl.BlockSpec((tm * SLABS, LANES), lambda i: (i, 0)),
                   pl.BlockSpec((EXPERTS_PER_GROUP, tm), lambda i: (0, i)), row(LANES),
                   _full((N_EXPERTS, LANES)), pl.BlockSpec((N_EXPERTS, LANES), lambda i: (i, 0))],
        out_shape=[jax.ShapeDtypeStruct((n, D_MODEL), F32), jax.ShapeDtypeStruct((n * SLABS, LANES), F32),
                   jax.ShapeDtypeStruct((EXPERTS_PER_GROUP, n), jnp.int32),
                   jax.ShapeDtypeStruct((n, LANES), F32),
                   jax.ShapeDtypeStruct((N_EXPERTS, LANES), F32),
                   jax.ShapeDtypeStruct((n // tm * N_EXPERTS, LANES), jnp.int32)],
        scratch_shapes=[pltpu.VMEM((N_EXPERTS, 1), F32)],
        compiler_params=_params(("arbitrary",)),
        name="merge",
    )(oa, ob, sga, sgb, x2, *weights)


ROW_DMA_UNROLL = 8


def _row_tile(ref, r):
    return ref.at[pl.ds(pl.multiple_of(r * SLABS, SLABS), SLABS), :]


def _scatter_kernel(ps_ref, cnt_ref, pe_ref, *rest, tiles):
    ng = len(tiles)
    meta_refs, prev_refs, pos_refs, h2_refs = (rest[k * ng:(k + 1) * ng] for k in range(4))
    xd_hbm, zero_sc, stage, sem, zsem = rest[4 * ng:]

    def zero_padding():
        zero_sc[...] = jnp.zeros(zero_sc.shape, F32)

        def pad_copies(e, act):
            lo = ps_ref[e] + cnt_ref[e]
            pad = pe_ref[e] - lo
            bit = FFN_ROWS
            while bit >= 1:
                @pl.when((pad & bit) != 0)
                def _(bit=bit):
                    first = lo + (pad & ~(2 * bit - 1))
                    act(pltpu.make_async_copy(
                        zero_sc.at[pl.ds(0, bit * SLABS), :],
                        xd_hbm.at[pl.ds(pl.multiple_of(first * SLABS, SLABS), bit * SLABS), :], zsem.at[0]))
                bit //= 2

        def start_pads(e, c):
            pad_copies(e, lambda cp: cp.start())
            return c

        def wait_pads(e, c):
            pad_copies(e, lambda cp: cp.wait())
            return c

        lax.fori_loop(0, N_EXPERTS, start_pads, 0)
        lax.fori_loop(0, N_EXPERTS, wait_pads, 0)

        def block_copy(b):
            start_row = pl.multiple_of(b * (FFN_ROWS * SLABS), FFN_ROWS * SLABS)
            return pltpu.make_async_copy(zero_sc, xd_hbm.at[pl.ds(start_row, FFN_ROWS * SLABS), :], zsem.at[0])

        first_unused = pe_ref[N_EXPERTS - 1] // FFN_ROWS
        n_blocks = xd_hbm.shape[0] // (FFN_ROWS * SLABS)

        def start_block(b, c):
            block_copy(b).start()
            return c

        def wait_block(b, c):
            block_copy(b).wait()
            return c

        lax.fori_loop(first_unused, n_blocks, start_block, 0)
        lax.fori_loop(first_unused, n_blocks, wait_block, 0)

    step = pl.program_id(0)
    pl.when(step == 0)(zero_padding)

    def windows(meta, s, act):
        def per_expert(e, first):
            dst = ps_ref[e] + meta[e, 0]
            n_win = (meta[e, 1] + RUN_WINDOW - 1) // RUN_WINDOW

            def per_window(wi, c):
                act(pltpu.make_async_copy(
                    stage.at[s, pl.ds(pl.multiple_of((first + wi * RUN_WINDOW) * SLABS, SLABS), RUN_WINDOW * SLABS), :],
                    xd_hbm.at[pl.ds(pl.multiple_of((dst + wi * RUN_WINDOW) * SLABS, SLABS), RUN_WINDOW * SLABS), :],
                    sem.at[s]))
                return c

            lax.fori_loop(0, n_win, per_window, 0)
            return first + n_win * RUN_WINDOW

        lax.fori_loop(0, N_EXPERTS, per_expert, 0)

    first_step = 0
    for g, nt in enumerate(tiles):
        @pl.when((step >= first_step) & (step < first_step + nt))
        def _(g=g, nt=nt, first_step=first_step):
            meta_ref, prev_ref, pos_ref, h2_ref = meta_refs[g], prev_refs[g], pos_refs[g], h2_refs[g]
            tm = h2_ref.shape[0] // SLABS
            local = step - first_step
            slot = local % 2
            stage_rows = 2 * tm + N_EXPERTS * RUN_WINDOW
            stage[slot, pl.ds(0, stage_rows * SLABS), :] = jnp.zeros((stage_rows * SLABS, LANES), F32)

            def body(t, carry):
                for k in range(2):
                    p = pos_ref[0, 0, k * tm + t]
                    _row_tile(stage.at[slot], p)[...] = _row_tile(h2_ref, t)[...]
                return carry

            lax.fori_loop(0, tm, body, 0, unroll=ROW_DMA_UNROLL)

            @pl.when(local > 0)
            def _():
                windows(prev_ref, 1 - slot, lambda cp: cp.wait())

            windows(meta_ref, slot, lambda cp: cp.start())

            @pl.when(local == nt - 1)
            def _():
                windows(meta_ref, slot, lambda cp: cp.wait())
        first_step += nt


def _scatter(pad_start, counts, pad_end, metas, pos_blocks, h2s, total_rows):
    tms = [p.shape[2] // 2 for p in pos_blocks]
    tiles = [p.shape[0] for p in pos_blocks]
    firsts = [sum(tiles[:g]) for g in range(len(tiles))]
    ng = len(tiles)
    local = lambda g, shift=0: (lambda i, *_: jnp.clip(i - firsts[g] - shift, 0, tiles[g] - 1))
    smem = pltpu.SMEM
    in_specs = ([pl.BlockSpec((N_EXPERTS, LANES), (lambda i, *_, f=local(g): (f(i), 0)), memory_space=smem)
                 for g in range(ng)]
                + [pl.BlockSpec((N_EXPERTS, LANES), (lambda i, *_, f=local(g, 1): (f(i), 0)), memory_space=smem)
                   for g in range(ng)]
                + [pl.BlockSpec((1, 1, 2 * tms[g]), (lambda i, *_, f=local(g): (f(i), 0, 0)), memory_space=smem)
                   for g in range(ng)]
                + [pl.BlockSpec((tms[g] * SLABS, LANES), (lambda i, *_, f=local(g): (f(i), 0)))
                   for g in range(ng)])
    stage_rows = 2 * max(tms) + N_EXPERTS * RUN_WINDOW
    grid_spec = pltpu.PrefetchScalarGridSpec(
        num_scalar_prefetch=3,
        grid=(sum(tiles),),
        in_specs=in_specs,
        out_specs=pl.BlockSpec(memory_space=pl.ANY),
        scratch_shapes=[pltpu.VMEM((FFN_ROWS * SLABS, LANES), F32),
                        pltpu.VMEM((2, stage_rows * SLABS, LANES), F32),
                        pltpu.SemaphoreType.DMA((2,)), pltpu.SemaphoreType.DMA((1,))],
    )
    return pl.pallas_call(
        functools.partial(_scatter_kernel, tiles=tuple(tiles)),
        grid_spec=grid_spec,
        out_shape=jax.ShapeDtypeStruct((total_rows * SLABS, LANES), F32),
        compiler_params=_params(("arbitrary",)),
        name="moe_scatter",
    )(pad_start, counts, pad_end, *metas, *metas, *pos_blocks, *h2s)


def _ffn_kernel(be_ref, nbu_ref, x_ref, wg_ref, wu_ref, wd_ref, out_ref, wg_sc, wu_sc, wd_sc):
    i = pl.program_id(0)
    used = i < nbu_ref[0]
    new_expert = (i == 0) | (be_ref[i] != be_ref[jnp.maximum(i - 1, 0)])

    @pl.when(used & new_expert)
    def _():
        wg_sc[...] = wg_ref[0].astype(BF16)
        wu_sc[...] = wu_ref[0].astype(BF16)
        wd_sc[...] = wd_ref[0].astype(BF16)

    @pl.when(used)
    def _():
        x = _load_token_major(x_ref, FFN_ROWS).astype(BF16)
        g = jnp.dot(x, wg_sc[...], preferred_element_type=F32)
        u = jnp.dot(x, wu_sc[...], preferred_element_type=F32)
        mid = (g * _sigmoid(g) * u).astype(BF16)
        _store_token_major(out_ref, jnp.dot(mid, wd_sc[...], preferred_element_type=F32))

    @pl.when(jnp.logical_not(used))
    def _():
        out_ref[...] = jnp.zeros(out_ref.shape, F32)


def _ffn(block_expert, nb_used, xd, wg, wu, wd):
    nb = block_expert.shape[0]
    blk = lambda i, be, nbu: (jnp.minimum(i, nbu[0] - 1), 0)
    wsel = lambda i, be, nbu: (be[jnp.minimum(i, nbu[0] - 1)], 0, 0)
    grid_spec = pltpu.PrefetchScalarGridSpec(
        num_scalar_prefetch=2,
        grid=(nb,),
        in_specs=[pl.BlockSpec((FFN_ROWS * SLABS, LANES), blk),
                  pl.BlockSpec((1, D_MODEL, EXPERT_FF), wsel),
                  pl.BlockSpec((1, D_MODEL, EXPERT_FF), wsel),
                  pl.BlockSpec((1, EXPERT_FF, D_MODEL), wsel)],
        out_specs=pl.BlockSpec((FFN_ROWS * SLABS, LANES), lambda i, be, nbu: (i, 0)),
        scratch_shapes=[pltpu.VMEM((D_MODEL, EXPERT_FF), BF16), pltpu.VMEM((D_MODEL, EXPERT_FF), BF16),
                        pltpu.VMEM((EXPERT_FF, D_MODEL), BF16)],
    )
    return pl.pallas_call(
        _ffn_kernel,
        grid_spec=grid_spec,
        out_shape=jax.ShapeDtypeStruct((nb * FFN_ROWS * SLABS, LANES), F32),
        compiler_params=_params(("arbitrary",)),
        name="expert_ffn",
    )(block_expert, nb_used, xd, wg, wu, wd)


def _combine_kernel(ps_ref, meta_ref, metan_ref, pos_ref, x1_ref, gcol_ref, gfin_ref, eo_hbm, y_ref,
                    stage, ybuf, sem):
    i = pl.program_id(0)
    n = pl.num_programs(0)
    tm = x1_ref.shape[0]
    slot = i % 2

    def windows(meta, s, act):
        def per_expert(e, first):
            src = ps_ref[e] + meta[e, 0]
            n_win = (meta[e, 1] + RUN_WINDOW - 1) // RUN_WINDOW

            def per_window(wi, c):
                act(pltpu.make_async_copy(
                    eo_hbm.at[pl.ds(pl.multiple_of((src + wi * RUN_WINDOW) * SLABS, SLABS), RUN_WINDOW * SLABS), :],
                    stage.at[s, pl.ds(pl.multiple_of((first + wi * RUN_WINDOW) * SLABS, SLABS), RUN_WINDOW * SLABS), :],
                    sem.at[s]))
                return c

            lax.fori_loop(0, n_win, per_window, 0)
            return first + n_win * RUN_WINDOW

        lax.fori_loop(0, N_EXPERTS, per_expert, 0)

    @pl.when(i == 0)
    def _():
        windows(meta_ref, 0, lambda cp: cp.start())

    @pl.when(i + 1 < n)
    def _():
        windows(metan_ref, 1 - slot, lambda cp: cp.start())

    windows(meta_ref, slot, lambda cp: cp.wait())

    def body(t, carry):
        for k in range(2):
            p = pos_ref[0, 0, k * tm + t]
            _row_tile(ybuf.at[k], t)[...] = _row_tile(stage.at[slot], p)[...]
        return carry

    lax.fori_loop(0, tm, body, 0, unroll=ROW_DMA_UNROLL)
    g = gcol_ref[...]
    y = x1_ref[...] + (g[:, 0:1] * _load_token_major(ybuf.at[0], tm) + g[:, 1:2] * _load_token_major(ybuf.at[1], tm))
    y_ref[...] = _rms(y, gfin_ref[...])


def _combine(pad_start, meta, pos_blocks, x1, gcol, g_final, eo):
    n = x1.shape[0]
    tm = _merge_tile(n)
    nt = n // tm
    stage_rows = 2 * tm + N_EXPERTS * RUN_WINDOW
    grid_spec = pltpu.PrefetchScalarGridSpec(
        num_scalar_prefetch=1,
        grid=(nt,),
        in_specs=[pl.BlockSpec((N_EXPERTS, LANES), lambda i, ps: (i, 0), memory_space=pltpu.SMEM),
                  pl.BlockSpec((N_EXPERTS, LANES), lambda i, ps: (jnp.minimum(i + 1, nt - 1), 0),
                               memory_space=pltpu.SMEM),
                  pl.BlockSpec((1, 1, 2 * tm), lambda i, ps: (i, 0, 0), memory_space=pltpu.SMEM),
                  pl.BlockSpec((tm, D_MODEL), lambda i, ps: (i, 0)),
                  pl.BlockSpec((tm, LANES), lambda i, ps: (i, 0)),
                  pl.BlockSpec(g_final.shape, lambda i, ps: (0, 0)),
                  pl.BlockSpec(memory_space=pl.ANY)],
        out_specs=pl.BlockSpec((tm, D_MODEL), lambda i, ps: (i, 0)),
        scratch_shapes=[pltpu.VMEM((2, stage_rows * SLABS, LANES), F32),
                        pltpu.VMEM((2, tm * SLABS, LANES), F32), pltpu.SemaphoreType.DMA((2,))],
    )
    return pl.pallas_call(
        _combine_kernel,
        grid_spec=grid_spec,
        out_shape=jax.ShapeDtypeStruct((n, D_MODEL), F32),
        compiler_params=_params(("arbitrary",)),
        name="combine",
    )(pad_start, meta, meta, pos_blocks, x1, gcol, g_final, eo)


def _layout(cnt, n_tokens):
    counts = cnt[:, 0].astype(jnp.int32)
    padded = (counts + RUN_WINDOW + FFN_ROWS - 1) // FFN_ROWS * FFN_ROWS
    pad_end = jnp.cumsum(padded).astype(jnp.int32)
    pad_start = pad_end - padded
    nb = -(-(2 * n_tokens + N_EXPERTS * RUN_WINDOW) // FFN_ROWS) + N_EXPERTS + 1
    block_start = jnp.arange(nb, dtype=jnp.int32) * FFN_ROWS
    block_expert = jnp.minimum(jnp.sum((block_start[:, None] >= pad_end[None, :]).astype(jnp.int32), axis=1),
                               N_EXPERTS - 1).astype(jnp.int32)
    nb_used = (pad_end[N_EXPERTS - 1:] // FFN_ROWS).astype(jnp.int32)
    return pad_start, counts, pad_end, block_expert, nb_used, nb * FFN_ROWS


def _rope_tables(l, past, rows):
    half = ROPE_DIM // 2
    inv = ROPE_THETA ** (-jnp.arange(half, dtype=F32) / half)
    pos = (past + (jnp.arange(rows, dtype=jnp.int32) % l)).astype(F32)
    ang = pos[:, None] * inv[None, :]
    pad = jnp.zeros((rows, LANES - ROPE_DIM), F32)
    cos = jnp.concatenate([jnp.cos(ang), jnp.cos(ang), pad], axis=1)
    sin = jnp.concatenate([-jnp.sin(ang), jnp.sin(ang), pad], axis=1)
    return cos, sin


def _mixers(x, ckv_past, kpe_past, s0, w):
    b, l, d = x.shape
    n = b * l
    x2 = x.reshape(n, d)
    past = 0 if ckv_past is None else ckv_past.shape[1]
    cos_t, sin_t = _rope_tables(l, past, max(l, ROW_TILE))
    prompt = ckv_past is None
    (qcat, kcat, ckv, kpe, gq, gk, gv, la, sgr, sga, sgb, *maybe_vt) = _inproj(x2, cos_t, sin_t, w, prompt)
    if prompt:
        oa = _attn_prompt(qcat, kcat, maybe_vt[0], w["w_uv"], b, l)
    else:
        oa = _attn_sample(qcat, kcat, ckv_past, kpe_past, w["w_uv"], b, l)
    ob, s_new = _gla(gq, gk, gv, la, sgr, w["g_gla"], s0, b, l)
    merge_in = (oa.reshape(n, -1), ob.reshape(n, -1), sga, sgb, x2)
    return merge_in, (ckv.reshape(b, l, KV_LORA), kpe.reshape(b, l, ROPE_DIM), s_new)


def _layer(groups, w, g_final):
    fronts = [_mixers(*g, w) for g in groups]
    cnt = jnp.zeros((N_EXPERTS, LANES), F32)
    merged = []
    for merge_in, _ in fronts:
        x1, h2, route, gcol, cnt, meta = _merge(*merge_in, w, cnt)
        merged.append((x1, h2, route, gcol, meta))
    n_tokens = sum(m[0].shape[0] for m in merged)
    pad_start, counts, pad_end, block_expert, nb_used, total_rows = _layout(cnt, n_tokens)
    pos_blocks = []
    for x1, h2, route, gcol, meta in merged:
        n = x1.shape[0]
        tm = _merge_tile(n)
        pos_blocks.append(route[4:6].reshape(2, n // tm, tm).transpose(1, 0, 2).reshape(n // tm, 1, 2 * tm))
    xd = _scatter(pad_start, counts, pad_end, [m[4] for m in merged], pos_blocks, [m[1] for m in merged],
                  total_rows)
    eo = _ffn(block_expert, nb_used, xd, w["w_eg"], w["w_eu"], w["w_ed"])
    outs = []
    for (x1, h2, route, gcol, meta), pos, (_, extras), g in zip(merged, pos_blocks, fronts, groups):
        y = _combine(pad_start, meta, pos, x1, gcol, g_final, eo)
        outs.append((y.reshape(g[0].shape),) + extras)
    return outs


def _prep_weights(w_in, g_norm_mix, g_qnorm, w_uq, g_kvnorm, w_ukv, w_gate2, b_gate2, g_gla_norm,
                  w_branch_a, w_branch_b, w_out, g_norm_ffn, w_router_group, b_router_group,
                  w_router_expert, b_router_expert, w_exp_gate, w_exp_up, w_exp_down):
    nqk = GLA_HEADS * GLA_DK
    nv = GLA_HEADS * GLA_DV
    o = 0
    parts = {}
    for name, width in (("cq", Q_LORA), ("ckv", KV_LORA), ("kpe", ROPE_DIM), ("gq", nqk), ("gk", nqk),
                        ("gv", nv), ("glr", GATE_RANK), ("gr", nv), ("ga", D_MODEL), ("gb", D_MODEL)):
        parts[name] = w_in[:, o:o + width]
        o += width
    padc = lambda a, width: jnp.pad(a, ((0, 0), (0, width - a.shape[1])))
    w_small = jnp.concatenate([parts["cq"], parts["ckv"], padc(parts["kpe"], LANES),
                               padc(parts["glr"], LANES)], axis=1).astype(BF16)
    uq = w_uq.reshape(Q_LORA, MLA_HEADS, NOPE_DIM + ROPE_DIM)
    ukv = w_ukv.reshape(KV_LORA, MLA_HEADS, NOPE_DIM + V_DIM)
    lat = _fold_q(uq[:, :, :NOPE_DIM].transpose(1, 0, 2), ukv[:, :, :NOPE_DIM].transpose(1, 0, 2))
    q_rope = uq[:, :, NOPE_DIM:].transpose(1, 0, 2)
    w_qcat = jnp.concatenate([lat, q_rope, jnp.zeros((MLA_HEADS, Q_LORA, QCAT - KV_LORA - ROPE_DIM), F32)],
                             axis=2)
    w_qcat = w_qcat.transpose(1, 0, 2).reshape(Q_LORA, MLA_HEADS * QCAT).astype(BF16)
    w_router = jnp.concatenate([w_router_expert.T, w_router_group.T,
                                jnp.zeros((ROUTER_ROWS - N_EXPERTS - N_GROUPS, D_MODEL), F32)], axis=0)
    b_router = jnp.concatenate([b_router_expert, b_router_group,
                                jnp.zeros((ROUTER_ROWS - N_EXPERTS - N_GROUPS,), F32)]).reshape(ROUTER_ROWS, 1)
    return {
        "g_mix": g_norm_mix.reshape(1, D_MODEL), "w_small": w_small,
        "g_qn": g_qnorm.reshape(1, Q_LORA), "g_kvn": g_kvnorm.reshape(1, KV_LORA), "w_qcat": w_qcat,
        "w_g2": jnp.pad(w_gate2, ((0, LANES - GATE_RANK), (0, 0))).astype(BF16),
        "b_g2": b_gate2.reshape(1, nqk),
        "w_gqk": jnp.concatenate([parts["gq"], parts["gk"]], axis=1).astype(BF16),
        "w_gv": parts["gv"].astype(BF16), "w_gr": parts["gr"].astype(BF16),
        "w_ga": parts["ga"].astype(BF16), "w_gb": parts["gb"].astype(BF16),
        "w_uv": ukv[:, :, NOPE_DIM:].transpose(1, 0, 2).astype(BF16),
        "g_gla": g_gla_norm.reshape(1, GLA_DV),
        "w_a": w_branch_a.astype(BF16), "w_b": w_branch_b.astype(BF16), "w_o": w_out.astype(BF16),
        "g_ffn": g_norm_ffn.reshape(1, D_MODEL), "w_router": w_router, "b_router": b_router,
        "w_eg": w_exp_gate, "w_eu": w_exp_up, "w_ed": w_exp_down,
    }


def kernel(x_prompt, x_sample, cache_ckv, cache_krope, state_gla, w_in, g_norm_mix, g_qnorm, w_uq, g_kvnorm, w_ukv, w_gate2, b_gate2, g_gla_norm, w_branch_a, w_branch_b, w_out, g_norm_ffn, w_router_group, b_router_group, w_router_expert, b_router_expert, w_exp_gate, w_exp_up, w_exp_down, g_norm_final):
    depth = w_in.shape[0]
    assert depth == 1, "the final norm is fused into the last layer's combine step"
    gfin = g_norm_final.reshape(1, D_MODEL)
    w = _prep_weights(w_in[0], g_norm_mix[0], g_qnorm[0], w_uq[0], g_kvnorm[0], w_ukv[0], w_gate2[0],
                      b_gate2[0], g_gla_norm[0], w_branch_a[0], w_branch_b[0], w_out[0], g_norm_ffn[0],
                      w_router_group[0], b_router_group[0], w_router_expert[0], b_router_expert[0],
                      w_exp_gate[0], w_exp_up[0], w_exp_down[0])
    bp = x_prompt.shape[0]
    zero_state = jnp.zeros((bp, GLA_HEADS, GLA_DK, GLA_DV), F32)
    (yp, c1, k1, s1), (ys, c2, k2, s2) = _layer(
        [(x_prompt, None, None, zero_state),
         (x_sample, cache_ckv[0], cache_krope[0], state_gla[0].astype(F32))], w, gfin)
    return (yp, ys, c1[None], k1[None], s1[None], c2[None], k2[None], s2[None])
```

```python
import functools
import math

import jax
import jax.numpy as jnp
from jax import lax
from jax.experimental import pallas as pl
from jax.experimental.pallas import tpu as pltpu

F32 = jnp.float32
BF16 = jnp.bfloat16

D_MODEL = 1024
CHUNK = 64
EPS = 1e-6
MLA_HEADS = 8
Q_LORA = 256
KV_LORA = 128
NOPE_DIM = 64
ROPE_DIM = 32
V_DIM = 64
ROPE_THETA = 10000.0
GLA_HEADS = 4
GLA_DK = 128
GLA_DV = 256
GATE_RANK = 16
GATE_TEMP = 16.0
N_GROUPS = 4
EXPERTS_PER_GROUP = 8
N_EXPERTS = 32
EXPERT_FF = 512

LANES = 128
QCAT = 2 * LANES
ROW_TILE = 256
MERGE_TILE = 512
MERGE_PARTS = 1
RUN_WINDOW = 32
GLA_BATCH = 2
GLA_STEP_TOKENS = 256
Q_BLOCK = ROW_TILE
KEY_TILE = ROW_TILE
ATT_SAMPLE_SEQS = 4
SUM_ROWS = 16
GLA_CHUNK = 64
FFN_ROWS = 512
ROUTER_ROWS = 48
VMEM_LIMIT = 56 * 1024 * 1024
LOG2E = 1.4426950408889634
ATT_SCALE = LOG2E / math.sqrt(NOPE_DIM + ROPE_DIM)

_NT = (((1,), (1,)), ((), ()))
_TN = (((0,), (0,)), ((), ()))


def _params(sem):
    return pltpu.CompilerParams(dimension_semantics=sem, vmem_limit_bytes=VMEM_LIMIT)


def _rms(x, g):
    return x * lax.rsqrt(jnp.mean(x * x, axis=-1, keepdims=True) + EPS) * g


def _sigmoid(x):
    return 1.0 / (1.0 + jnp.exp(-x))


def _full(shape):
    n = len(shape)
    return pl.BlockSpec(shape, lambda *_: (0,) * n)


SLABS = D_MODEL // LANES


def _store_token_major(ref, x, first_row=0):
    rows = x.shape[0]
    for j in range(SLABS):
        ref[pl.ds(first_row * SLABS + j, rows, stride=SLABS), :] = x[:, j * LANES:(j + 1) * LANES]


def _load_token_major(ref, rows):
    return jnp.concatenate([ref[pl.ds(j, rows, stride=SLABS), :] for j in range(SLABS)], axis=1)


def _fold_q_kernel(wq_ref, wk_ref, o_ref):
    o_ref[0] = lax.dot_general(wq_ref[0], wk_ref[0], _NT, precision=lax.Precision.HIGHEST,
                               preferred_element_type=F32)


def _fold_q(wq_nope, wk_nope):
    return pl.pallas_call(
        _fold_q_kernel,
        grid=(MLA_HEADS,),
        in_specs=[pl.BlockSpec((1, Q_LORA, NOPE_DIM), lambda h: (h, 0, 0)),
                  pl.BlockSpec((1, KV_LORA, NOPE_DIM), lambda h: (h, 0, 0))],
        out_specs=pl.BlockSpec((1, Q_LORA, KV_LORA), lambda h: (h, 0, 0)),
        out_shape=jax.ShapeDtypeStruct((MLA_HEADS, Q_LORA, KV_LORA), F32),
        compiler_params=_params(("arbitrary",)),
        name="fold_q",
    )(wq_nope, wk_nope)


_W_SMALL = Q_LORA + KV_LORA + 2 * LANES
_IN_COLS = {}
_off = 0
for _name, _width in (("small", _W_SMALL), ("gqk", 2 * GLA_HEADS * GLA_DK), ("gv", GLA_HEADS * GLA_DV),
                      ("gr", GLA_HEADS * GLA_DV), ("ga", D_MODEL), ("gb", D_MODEL)):
    _IN_COLS[_name] = slice(_off, _off + _width)
    _off += _width


def _inproj_kernel(x_ref, cos_ref, sin_ref, gmix_ref, win_ref, gqn_ref, gkvn_ref, wqc_ref,
                   wg2_ref, bg2_ref,
                   qcat_ref, kcat_ref, ckv_ref, kpe_ref, gq_ref, gk_ref, gv_ref, la_ref,
                   sgr_ref, sga_ref, sgb_ref, *maybe_vt_ref, transposed):
    hb = _rms(x_ref[...], gmix_ref[...]).astype(BF16)
    proj = lambda name: jnp.dot(hb, win_ref[:, _IN_COLS[name]], preferred_element_type=F32)
    zs = proj("small")
    cos = cos_ref[...]
    sin = sin_ref[...]
    first_half = lax.broadcasted_iota(jnp.int32, cos.shape, 1) < ROPE_DIM // 2

    def rope(v):
        rot = jnp.where(first_half, pltpu.roll(v, LANES - ROPE_DIM // 2, 1),
                        pltpu.roll(v, ROPE_DIM // 2, 1))
        return v * cos + rot * sin

    cqn = _rms(zs[:, :Q_LORA], gqn_ref[...]).astype(BF16)
    qc = jnp.dot(cqn, wqc_ref[...], preferred_element_type=F32)
    for h in range(MLA_HEADS):
        lat = qc[:, h * QCAT:h * QCAT + LANES]
        pe = qc[:, h * QCAT + LANES:(h + 1) * QCAT]
        if transposed:
            lat_t = jnp.transpose(lat * ATT_SCALE).astype(BF16)
            pe_t = jnp.transpose(rope(pe) * ATT_SCALE).astype(BF16)
            for blk in range(lat.shape[0] // Q_BLOCK):
                cols = slice(blk * Q_BLOCK, (blk + 1) * Q_BLOCK)
                qcat_ref[blk, :LANES, h * Q_BLOCK:(h + 1) * Q_BLOCK] = lat_t[:, cols]
                qcat_ref[blk, LANES:, h * Q_BLOCK:(h + 1) * Q_BLOCK] = pe_t[:, cols]
        else:
            qcat_ref[h, :, :LANES] = (lat * ATT_SCALE).astype(BF16)
            qcat_ref[h, :, LANES:] = (rope(pe) * ATT_SCALE).astype(BF16)

    ckv = _rms(zs[:, Q_LORA:Q_LORA + KV_LORA], gkvn_ref[...])
    ckv_ref[...] = ckv
    if transposed:
        maybe_vt_ref[0][0] = jnp.transpose(ckv).astype(BF16)
    kpe = rope(zs[:, Q_LORA + KV_LORA:Q_LORA + KV_LORA + LANES])
    kpe_ref[...] = kpe[:, :ROPE_DIM]
    kcat_ref[:, :LANES] = ckv.astype(BF16)
    kcat_ref[:, LANES:] = kpe.astype(BF16)

    glr = zs[:, Q_LORA + KV_LORA + LANES:].astype(BF16)
    xg = jnp.dot(glr, wg2_ref[...], preferred_element_type=F32) + bg2_ref[...]
    la_ref[...] = (jnp.minimum(xg, 0.0) - jnp.log(1.0 + jnp.exp(-jnp.abs(xg)))) * (1.0 / GATE_TEMP)

    zqk = proj("gqk")
    nqk = GLA_HEADS * GLA_DK
    gq_ref[...] = (zqk[:, :nqk] * (GLA_DK ** -0.5)).astype(BF16)
    gk_ref[...] = zqk[:, nqk:].astype(BF16)
    gv_ref[...] = proj("gv").astype(BF16)
    gr = proj("gr")
    sgr_ref[...] = (gr * _sigmoid(gr)).astype(BF16)
    sga_ref[...] = _sigmoid(proj("ga")).astype(BF16)
    sgb_ref[...] = _sigmoid(proj("gb")).astype(BF16)


def _inproj(x2, cos_t, sin_t, w, transposed):
    n = x2.shape[0]
    tm = ROW_TILE
    nt = n // tm
    tab_blocks = cos_t.shape[0] // tm
    row = lambda width: pl.BlockSpec((tm, width), lambda i: (i, 0))
    tab = pl.BlockSpec((tm, LANES), lambda i: (i % tab_blocks, 0))
    nqk = GLA_HEADS * GLA_DK
    nv = GLA_HEADS * GLA_DV
    weights = [w["g_mix"], w["w_in"], w["g_qn"], w["g_kvn"], w["w_qcat"], w["w_g2"], w["b_g2"]]
    out_shape = [
        jax.ShapeDtypeStruct((MLA_HEADS, n, QCAT), BF16),
        jax.ShapeDtypeStruct((n, QCAT), BF16),
        jax.ShapeDtypeStruct((n, KV_LORA), F32),
        jax.ShapeDtypeStruct((n, ROPE_DIM), F32),
        jax.ShapeDtypeStruct((n, nqk), BF16),
        jax.ShapeDtypeStruct((n, nqk), BF16),
        jax.ShapeDtypeStruct((n, nv), BF16),
        jax.ShapeDtypeStruct((n, nqk), F32),
        jax.ShapeDtypeStruct((n, nv), BF16),
        jax.ShapeDtypeStruct((n, D_MODEL), BF16),
        jax.ShapeDtypeStruct((n, D_MODEL), BF16),
    ]
    out_specs = [
        pl.BlockSpec((MLA_HEADS, tm, QCAT), lambda i: (0, i, 0)),
        row(QCAT), row(KV_LORA), row(ROPE_DIM), row(nqk), row(nqk), row(nv), row(nqk), row(nv),
        row(D_MODEL), row(D_MODEL),
    ]
    if transposed:
        qb = tm // Q_BLOCK
        out_shape[0] = jax.ShapeDtypeStruct((n // Q_BLOCK, QCAT, MLA_HEADS * Q_BLOCK), BF16)
        out_specs[0] = pl.BlockSpec((qb, QCAT, MLA_HEADS * Q_BLOCK), lambda i: (i, 0, 0))
        out_shape.append(jax.ShapeDtypeStruct((nt, KV_LORA, tm), BF16))
        out_specs.append(pl.BlockSpec((1, KV_LORA, tm), lambda i: (i, 0, 0)))
    return pl.pallas_call(
        functools.partial(_inproj_kernel, transposed=transposed),
        grid=(nt,),
        in_specs=[row(D_MODEL), tab, tab] + [_full(a.shape) for a in weights],
        out_specs=out_specs,
        out_shape=out_shape,
        compiler_params=_params(("parallel",)),
        name="inproj",
    )(x2, cos_t, sin_t, *weights)


def _head_out(o_lat, wuv_ref, rows):
    outs = []
    for h in range(MLA_HEADS):
        oh = o_lat[h * rows:(h + 1) * rows].astype(BF16)
        outs.append(jnp.dot(oh, wuv_ref[h], preferred_element_type=F32))
    return jnp.concatenate(outs, axis=1)


def _attn_prompt_kernel(qt_ref, k_ref, vt_ref, bias_ref, wuv_ref, o_ref, m_sc, l_sc, acc_sc):
    i = pl.program_id(1)
    m_sc[...] = jnp.full(m_sc.shape, -jnp.inf, F32)
    l_sc[...] = jnp.zeros(l_sc.shape, F32)
    acc_sc[...] = jnp.zeros(acc_sc.shape, F32)
    heads = [slice(h * Q_BLOCK, (h + 1) * Q_BLOCK) for h in range(MLA_HEADS)]
    ones_rows = jnp.ones((SUM_ROWS, KEY_TILE), BF16)

    def process(tiles, diagonal_last):
        nt = len(tiles)
        kbs = [k_ref[0, pl.ds(pl.multiple_of(kt * KEY_TILE, KEY_TILE), KEY_TILE), :] for kt in tiles]
        vts = [jnp.concatenate([vt_ref[kt], ones_rows], axis=0) for kt in tiles]
        ss = [[None] * MLA_HEADS for _ in range(nt)]
        for t in range(nt):
            for h, hs in enumerate(heads):
                s = jnp.dot(kbs[t], qt_ref[0, :, hs], preferred_element_type=F32)
                if diagonal_last and t == nt - 1:
                    s = s + bias_ref[...]
                ss[t][h] = s
        tmax = [[jnp.max(ss[t][h], axis=0, keepdims=True) for h in range(MLA_HEADS)] for t in range(nt)]
        scale = [[None] * MLA_HEADS for _ in range(nt)]
        mrun = [[None] * MLA_HEADS for _ in range(nt)]
        for h, hs in enumerate(heads):
            m = m_sc[:, hs]
            for t in range(nt):
                m_new = jnp.maximum(m, tmax[t][h])
                scale[t][h] = jnp.exp2(m - m_new)
                mrun[t][h] = m_new
                m = m_new
            m_sc[:, hs] = m
        pvs = [[jnp.dot(vts[t], jnp.exp2(ss[t][h] - mrun[t][h]).astype(BF16), preferred_element_type=F32)
                for h in range(MLA_HEADS)] for t in range(nt)]
        for h, hs in enumerate(heads):
            l, acc = l_sc[:, hs], acc_sc[:, hs]
            for t in range(nt):
                acc = scale[t][h] * acc + pvs[t][h][:KV_LORA]
                l = scale[t][h] * l + pvs[t][h][KV_LORA:KV_LORA + 1]
            l_sc[:, hs], acc_sc[:, hs] = l, acc

    def body(j, carry):
        process([2 * j, 2 * j + 1], False)
        return carry

    lax.fori_loop(0, i // 2, body, 0)

    @pl.when(i % 2 == 1)
    def _():
        process([i - 1], False)

    process([i], True)

    o_lat_t = (acc_sc[...] / l_sc[...]).astype(BF16)
    outs = []
    for h, hs in enumerate(heads):
        outs.append(lax.dot_general(o_lat_t[:, hs], wuv_ref[h], _TN, preferred_element_type=F32))
    o_ref[0] = jnp.concatenate(outs, axis=1).astype(BF16)


def _attn_prompt(qt, kcat, vt, wuv, b, l):
    assert Q_BLOCK == KEY_TILE and Q_BLOCK % CHUNK == 0
    nq = l // Q_BLOCK
    nkt = l // KEY_TILE
    cols = MLA_HEADS * Q_BLOCK
    pos_chunk = jnp.arange(Q_BLOCK, dtype=jnp.int32) // CHUNK
    bias = jnp.where(pos_chunk[:, None] <= pos_chunk[None, :], 0.0, -jnp.inf).astype(F32)
    return pl.pallas_call(
        _attn_prompt_kernel,
        grid=(b, nq),
        in_specs=[pl.BlockSpec((1, QCAT, cols), lambda bb, i: (bb * nq + i, 0, 0)),
                  pl.BlockSpec((1, l, QCAT), lambda bb, i: (bb, 0, 0)),
                  pl.BlockSpec((nkt, KV_LORA, KEY_TILE), lambda bb, i: (bb, 0, 0)),
                  _full(bias.shape), _full(wuv.shape)],
        out_specs=pl.BlockSpec((1, Q_BLOCK, MLA_HEADS * V_DIM), lambda bb, i: (bb, i, 0)),
        out_shape=jax.ShapeDtypeStruct((b, l, MLA_HEADS * V_DIM), BF16),
        scratch_shapes=[pltpu.VMEM((1, cols), F32), pltpu.VMEM((1, cols), F32),
                        pltpu.VMEM((KV_LORA, cols), F32)],
        compiler_params=_params(("parallel", "arbitrary")),
        name="attn_prompt",
    )(qt, kcat.reshape(b, l, QCAT), vt, bias, wuv)


def _attn_sample_kernel(q_ref, cckv_ref, ckr_ref, kn_ref, wuv_ref, o_ref, *, ls, seqs):
    for s in range(seqs):
        rows = slice(s * ls, (s + 1) * ls)
        q = q_ref[:, rows, :].reshape(MLA_HEADS * ls, QCAT)
        ck = cckv_ref[s].astype(BF16)
        kr = ckr_ref[s].astype(BF16)
        kn = kn_ref[rows, :]
        s_c = (lax.dot_general(q[:, :KV_LORA], ck, _NT, preferred_element_type=F32)
               + lax.dot_general(q[:, KV_LORA:KV_LORA + ROPE_DIM], kr, _NT, preferred_element_type=F32))
        s_n = lax.dot_general(q, kn, _NT, preferred_element_type=F32)
        m = jnp.maximum(jnp.max(s_c, axis=1, keepdims=True), jnp.max(s_n, axis=1, keepdims=True))
        p_c = jnp.exp2(s_c - m)
        p_n = jnp.exp2(s_n - m)
        den = jnp.sum(p_c, axis=1, keepdims=True) + jnp.sum(p_n, axis=1, keepdims=True)
        o_lat = (jnp.dot(p_c.astype(BF16), ck, preferred_element_type=F32)
                 + jnp.dot(p_n.astype(BF16), kn[:, :KV_LORA], preferred_element_type=F32)) / den
        o_ref[s] = _head_out(o_lat, wuv_ref, ls).astype(BF16)


def _attn_sample(qcat, kcat, cache_ckv, cache_krope, wuv, b, ls):
    past = cache_ckv.shape[1]
    assert past % CHUNK == 0 and ls <= CHUNK
    seqs = ATT_SAMPLE_SEQS if b % ATT_SAMPLE_SEQS == 0 else 1
    return pl.pallas_call(
        functools.partial(_attn_sample_kernel, ls=ls, seqs=seqs),
        grid=(b // seqs,),
        in_specs=[pl.BlockSpec((MLA_HEADS, seqs * ls, QCAT), lambda bb: (0, bb, 0)),
                  pl.BlockSpec((seqs, past, KV_LORA), lambda bb: (bb, 0, 0)),
                  pl.BlockSpec((seqs, past, ROPE_DIM), lambda bb: (bb, 0, 0)),
                  pl.BlockSpec((seqs * ls, QCAT), lambda bb: (bb, 0)),
                  _full(wuv.shape)],
        out_specs=pl.BlockSpec((seqs, ls, MLA_HEADS * V_DIM), lambda bb: (bb, 0, 0)),
        out_shape=jax.ShapeDtypeStruct((b, ls, MLA_HEADS * V_DIM), BF16),
        compiler_params=_params(("parallel",)),
        name="attn_sample",
    )(qcat, cache_ckv, cache_krope, kcat, wuv)


def _cumsum_rows(x):
    c = x.shape[0]
    row = lax.broadcasted_iota(jnp.int32, x.shape, 0)
    s = 1
    while s < c:
        x = x + jnp.where(row >= s, pltpu.roll(x, s, 0), 0.0)
        s *= 2
    return x


def _gla_kernel(q_ref, k_ref, v_ref, la_ref, sgr_ref, gn_ref, s0_ref, o_ref, sout_ref, st_sc, *, c):
    j = pl.program_id(1)

    @pl.when(j == 0)
    def _():
        st_sc[...] = s0_ref[...]

    half = 32
    row = lax.broadcasted_iota(jnp.int32, (c, GLA_DK), 0)
    rr = lax.broadcasted_iota(jnp.int32, (c, c), 0)
    cc = lax.broadcasted_iota(jnp.int32, (c, c), 1)
    causal = cc <= rr
    same_half = (rr < half) == (cc < half)
    gn = gn_ref[...]
    n_chunks = q_ref.shape[1] // c
    seq_heads = [(s, h) for s in range(GLA_BATCH) for h in range(GLA_HEADS)]
    chains = [(s, h, ci) for s, h in seq_heads for ci in range(n_chunks)]
    ksl = lambda h: slice(h * GLA_DK, (h + 1) * GLA_DK)
    vsl = lambda h: slice(h * GLA_DV, (h + 1) * GLA_DV)
    rsl = lambda ci: slice(ci * c, (ci + 1) * c)

    prep = {}
    for s, h, ci in chains:
        q = q_ref[s, rsl(ci), ksl(h)].astype(F32)
        k = k_ref[s, rsl(ci), ksl(h)].astype(F32)
        b = _cumsum_rows(la_ref[s, rsl(ci), ksl(h)])
        if c > half:
            mid = jnp.where(row < half, b[half // 2 - 1:half // 2, :], b[half + half // 2 - 1:half + half // 2, :])
        else:
            mid = jnp.broadcast_to(b[c // 2 - 1:c // 2, :], b.shape)
        last = b[c - 1:c, :]
        ops = {"qe": (q * jnp.exp(b - mid)).astype(BF16), "ke": (k * jnp.exp(mid - b)).astype(BF16),
               "q0": (q * jnp.exp(b)).astype(BF16), "kd": (k * jnp.exp(last - b)).astype(BF16), "last": last}
        if c > half:
            edge = b[half - 1:half, :]
            ops["qo"] = (q * jnp.exp(jnp.minimum(b - edge, 0.0))).astype(BF16)
            ops["ko"] = (k * jnp.exp(jnp.minimum(edge - b, 0.0))).astype(BF16)
        prep[s, h, ci] = ops
    att = {}
    for key in chains:
        a = lax.dot_general(prep[key]["qe"], prep[key]["ke"], _NT, preferred_element_type=F32)
        if c > half:
            a_off = lax.dot_general(prep[key]["qo"], prep[key]["ko"], _NT, preferred_element_type=F32)
            a = jnp.where(same_half, a, a_off)
        att[key] = jnp.where(causal, a, 0.0).astype(BF16)
    intra, upd, decay = {}, {}, {}
    for s, h, ci in chains:
        v = v_ref[s, rsl(ci), vsl(h)]
        intra[s, h, ci] = jnp.dot(att[s, h, ci], v, preferred_element_type=F32)
        upd[s, h, ci] = lax.dot_general(prep[s, h, ci]["kd"], v, _TN, preferred_element_type=F32)
        dcol = jnp.exp(jnp.transpose(jnp.broadcast_to(prep[s, h, ci]["last"], (GLA_DK, GLA_DK))))
        decay[s, h, ci] = jnp.concatenate([dcol, dcol], axis=1)
    for s, h in seq_heads:
        st = st_sc[s, h]
        for ci in range(n_chunks):
            o = jnp.dot(prep[s, h, ci]["q0"], st.astype(BF16), preferred_element_type=F32) + intra[s, h, ci]
            st = decay[s, h, ci] * st + upd[s, h, ci]
            on = _rms(o, gn) * sgr_ref[s, rsl(ci), vsl(h)].astype(F32)
            o_ref[s, rsl(ci), vsl(h)] = on.astype(BF16)
        st_sc[s, h] = st

    @pl.when(j == pl.num_programs(1) - 1)
    def _():
        sout_ref[...] = st_sc[...]


def _gla(gq, gk, gv, la, sgr, gn, s0, b, l):
    c = min(GLA_CHUNK, l)
    step = min(GLA_STEP_TOKENS, l)
    nc = l // step
    nqk = GLA_HEADS * GLA_DK
    nv = GLA_HEADS * GLA_DV
    r3 = lambda a: a.reshape(b, l, a.shape[-1])
    assert b % GLA_BATCH == 0 and l % step == 0 and step % c == 0
    tok = lambda width: pl.BlockSpec((GLA_BATCH, step, width), lambda bb, j: (bb, j, 0))
    st = pl.BlockSpec((GLA_BATCH, GLA_HEADS, GLA_DK, GLA_DV), lambda bb, j: (bb, 0, 0, 0))
    return pl.pallas_call(
        functools.partial(_gla_kernel, c=c),
        grid=(b // GLA_BATCH, nc),
        in_specs=[tok(nqk), tok(nqk), tok(nv), tok(nqk), tok(nv), _full(gn.shape), st],
        out_specs=[tok(nv), st],
        out_shape=[jax.ShapeDtypeStruct((b, l, nv), BF16),
                   jax.ShapeDtypeStruct((b, GLA_HEADS, GLA_DK, GLA_DV), F32)],
        scratch_shapes=[pltpu.VMEM((GLA_BATCH, GLA_HEADS, GLA_DK, GLA_DV), F32)],
        compiler_params=_params(("parallel", "arbitrary")),
        name="gla",
    )(r3(gq), r3(gk), r3(gv), r3(la), r3(sgr), gn, s0)


_ROWS_A = slice(0, MLA_HEADS * V_DIM)
_ROWS_B = slice(_ROWS_A.stop, _ROWS_A.stop + GLA_HEADS * GLA_DV)
_ROWS_O = slice(_ROWS_B.stop, _ROWS_B.stop + D_MODEL)


def _merge_kernel(oa_ref, ob_ref, sga_ref, sgb_ref, x_ref, wabo_ref, gffn_ref,
                  wr_ref, br_ref, cnt0_ref, x1_ref, h2_ref, route_ref, gcol_ref, cnt_ref, meta_ref, run_sc):
    assert MERGE_PARTS == 1
    tm = x_ref.shape[0]
    pr = tm // MERGE_PARTS
    wr = wr_ref[...]
    whi = wr.astype(BF16)
    wlo = (wr - whi.astype(F32)).astype(BF16)
    ridx = lax.broadcasted_iota(jnp.int32, (EXPERTS_PER_GROUP, pr), 0)
    eidx = lax.broadcasted_iota(jnp.int32, (N_EXPERTS, pr), 0)
    earlier = jnp.where(lax.broadcasted_iota(jnp.int32, (pr, pr), 0) < lax.broadcasted_iota(jnp.int32, (pr, pr), 1),
                        1.0, 0.0).astype(BF16)
    big = jnp.int32(1 << 20)

    def top(vals):
        vmax = jnp.max(vals, axis=0, keepdims=True)
        imax = jnp.min(jnp.where(vals == vmax, ridx, big), axis=0, keepdims=True)
        return vmax, imax

    h2s = []
    for part in range(MERGE_PARTS):
        rows = slice(part * pr, (part + 1) * pr)
        ya = jnp.dot(oa_ref[rows, :], wabo_ref[_ROWS_A, :], preferred_element_type=F32)
        yb = jnp.dot(ob_ref[rows, :], wabo_ref[_ROWS_B, :], preferred_element_type=F32)
        merged = (sga_ref[rows, :].astype(F32) * ya + sgb_ref[rows, :].astype(F32) * yb).astype(BF16)
        x1 = x_ref[rows, :] + jnp.dot(merged, wabo_ref[_ROWS_O, :], preferred_element_type=F32)
        x1_ref[rows, :] = x1
        h2 = _rms(x1, gffn_ref[...])
        _store_token_major(h2_ref, h2, part * pr)
        h2s.append(h2)

    routed = []
    for part in range(MERGE_PARTS):
        rows = slice(part * pr, (part + 1) * pr)
        hi = h2s[part].astype(BF16)
        lo = (h2s[part] - hi.astype(F32)).astype(BF16)
        logits = (lax.dot_general(whi, hi, _NT, preferred_element_type=F32)
                  + lax.dot_general(whi, lo, _NT, preferred_element_type=F32)
                  + lax.dot_general(wlo, hi, _NT, preferred_element_type=F32)) + br_ref[...]
        gl = jnp.where(ridx < N_GROUPS, logits[N_EXPERTS:N_EXPERTS + EXPERTS_PER_GROUP], -jnp.inf)
        gmax, g_top = top(gl)
        p_top = 1.0 / jnp.sum(jnp.exp(gl - gmax), axis=0, keepdims=True)
        e_sel = logits[:EXPERTS_PER_GROUP]
        for g in range(1, N_GROUPS):
            e_sel = jnp.where(g_top == g, logits[g * EXPERTS_PER_GROUP:(g + 1) * EXPERTS_PER_GROUP], e_sel)
        v1, i1 = top(e_sel)
        v2, i2 = top(jnp.where(ridx == i1, -jnp.inf, e_sel))
        e21 = jnp.exp(v2 - v1)
        w1 = p_top / (1.0 + e21)
        w2 = p_top * e21 / (1.0 + e21)
        base = g_top * EXPERTS_PER_GROUP
        id0 = base + i1
        id1 = base + i2
        gates = jnp.where(ridx == 0, w1, jnp.where(ridx == 1, w2, 0.0))
        gcol_ref[rows, :] = jnp.transpose(
            jnp.concatenate([gates, jnp.zeros((LANES - EXPERTS_PER_GROUP, pr), F32)], axis=0))
        oh0 = jnp.where(eidx == id0, 1.0, 0.0)
        oh1 = jnp.where(eidx == id1, 1.0, 0.0)
        p0 = jnp.dot(oh0.astype(BF16), earlier, preferred_element_type=F32)
        p1 = jnp.dot(oh1.astype(BF16), earlier, preferred_element_type=F32)
        c0 = jnp.sum(oh0, axis=1, keepdims=True)
        c1 = jnp.sum(oh1, axis=1, keepdims=True)
        routed.append((id0, id1, oh0, oh1, p0, p1, c0, c1))

    @pl.when(pl.program_id(0) == 0)
    def _():
        run_sc[...] = cnt0_ref[:, :1]

    run = run_sc[...]
    lane = lax.broadcasted_iota(jnp.int32, (N_EXPERTS, LANES), 1)
    for part, (id0, id1, oh0, oh1, p0, p1, c0, c1) in enumerate(routed):
        rank0 = jnp.sum(oh0 * (run + p0), axis=0, keepdims=True)
        rank1 = jnp.sum(oh1 * (run + c0 + p1), axis=0, keepdims=True)
        length = c0 + c1
        windows = jnp.ceil(length * (1.0 / RUN_WINDOW)) * RUN_WINDOW
        windows_b = jnp.broadcast_to(windows, (N_EXPERTS, LANES))
        stage_off = (_cumsum_rows(windows_b) - windows_b)[:, :1]
        pos0 = jnp.sum(oh0 * (stage_off + p0), axis=0, keepdims=True)
        pos1 = jnp.sum(oh1 * (stage_off + c0 + p1), axis=0, keepdims=True)
        win_row = (lane * RUN_WINDOW).astype(F32)
        expert_of = jnp.sum(jnp.where(stage_off + windows <= win_row, 1.0, 0.0), axis=0, keepdims=True)
        n_win = jnp.sum(windows, axis=0, keepdims=True) * (1.0 / RUN_WINDOW)
        expert_of = jnp.minimum(expert_of, N_EXPERTS - 1.0)
        eidx_w = lax.broadcasted_iota(jnp.int32, (N_EXPERTS, LANES), 0).astype(F32)
        rel_row = jnp.sum(jnp.where(eidx_w == expert_of, run - stage_off, 0.0), axis=0, keepdims=True) + win_row[:1]
        r8 = lax.broadcasted_iota(jnp.int32, (EXPERTS_PER_GROUP, LANES), 0)
        meta_ref[...] = jnp.where(r8 == 0, expert_of.astype(jnp.int32), jnp.where(
            r8 == 1, rel_row.astype(jnp.int32), jnp.where(r8 == 2, n_win.astype(jnp.int32), 0)))
        run = run + length
        vals = (id0, id1, rank0.astype(jnp.int32), rank1.astype(jnp.int32),
                pos0.astype(jnp.int32), pos1.astype(jnp.int32))
        out = jnp.zeros((EXPERTS_PER_GROUP, pr), jnp.int32)
        for r, v in enumerate(vals):
            out = jnp.where(ridx == r, v, out)
        route_ref[:, part * pr:(part + 1) * pr] = out
    run_sc[...] = run
    cnt_ref[...] = jnp.broadcast_to(run, cnt_ref.shape)


def _merge_tile(n):
    return min(MERGE_TILE, max(ROW_TILE, n // 4))


def _merge(oa, ob, sga, sgb, x2, w, cnt0):
    n = x2.shape[0]
    tm = _merge_tile(n)
    assert n % tm == 0
    row = lambda width: pl.BlockSpec((tm, width), lambda i: (i, 0))
    weights = [w["w_abo"], w["g_ffn"], w["w_router"], w["b_router"], cnt0]
    return pl.pallas_call(
        _merge_kernel,
        grid=(n // tm,),
        in_specs=[row(MLA_HEADS * V_DIM), row(GLA_HEADS * GLA_DV), row(D_MODEL), row(D_MODEL),
                  row(D_MODEL)] + [_full(a.shape) for a in weights],
        out_specs=[row(D_MODEL), pl.BlockSpec((tm * SLABS, LANES), lambda i: (i, 0)),
                   pl.BlockSpec((EXPERTS_PER_GROUP, tm), lambda i: (0, i)), row(LANES),
                   _full((N_EXPERTS, LANES)), pl.BlockSpec((EXPERTS_PER_GROUP, LANES), lambda i: (i, 0))],
        out_shape=[jax.ShapeDtypeStruct((n, D_MODEL), F32), jax.ShapeDtypeStruct((n * SLABS, LANES), F32),
                   jax.ShapeDtypeStruct((EXPERTS_PER_GROUP, n), jnp.int32),
                   jax.ShapeDtypeStruct((n, LANES), F32),
                   jax.ShapeDtypeStruct((N_EXPERTS, LANES), F32),
                   jax.ShapeDtypeStruct((n // tm * EXPERTS_PER_GROUP, LANES), jnp.int32)],
        scratch_shapes=[pltpu.VMEM((N_EXPERTS, 1), F32)],
        compiler_params=_params(("arbitrary",)),
        name="merge",
    )(oa, ob, sga, sgb, x2, *weights)


ROW_DMA_UNROLL = 8


def _row_tile(ref, r):
    return ref.at[pl.ds(pl.multiple_of(r * SLABS, SLABS), SLABS), :]


def _scatter_kernel(ps_ref, cnt_ref, pe_ref, *rest, tiles):
    ng = len(tiles)
    meta_refs, prev_refs, pos_refs, h2_refs = (rest[k * ng:(k + 1) * ng] for k in range(4))
    xd_hbm, zero_sc, stage, sem, zsem = rest[4 * ng:]

    def zero_padding():
        zero_sc[...] = jnp.zeros(zero_sc.shape, F32)

        def pad_copies(e, act):
            lo = ps_ref[e] + cnt_ref[e]
            pad = pe_ref[e] - lo
            bit = FFN_ROWS
            while bit >= 1:
                @pl.when((pad & bit) != 0)
                def _(bit=bit):
                    first = lo + (pad & ~(2 * bit - 1))
                    act(pltpu.make_async_copy(
                        zero_sc.at[pl.ds(0, bit * SLABS), :],
                        xd_hbm.at[pl.ds(pl.multiple_of(first * SLABS, SLABS), bit * SLABS), :], zsem.at[0]))
                bit //= 2

        def start_pads(e, c):
            pad_copies(e, lambda cp: cp.start())
            return c

        def wait_pads(e, c):
            pad_copies(e, lambda cp: cp.wait())
            return c

        lax.fori_loop(0, N_EXPERTS, start_pads, 0)
        lax.fori_loop(0, N_EXPERTS, wait_pads, 0)

        def block_copy(b):
            start_row = pl.multiple_of(b * (FFN_ROWS * SLABS), FFN_ROWS * SLABS)
            return pltpu.make_async_copy(zero_sc, xd_hbm.at[pl.ds(start_row, FFN_ROWS * SLABS), :], zsem.at[0])

        first_unused = pe_ref[N_EXPERTS - 1] // FFN_ROWS
        n_blocks = xd_hbm.shape[0] // (FFN_ROWS * SLABS)

        def start_block(b, c):
            block_copy(b).start()
            return c

        def wait_block(b, c):
            block_copy(b).wait()
            return c

        lax.fori_loop(first_unused, n_blocks, start_block, 0)
        lax.fori_loop(first_unused, n_blocks, wait_block, 0)

    step = pl.program_id(0)
    pl.when(step == 0)(zero_padding)

    def windows(meta, s, act):
        def per_window(j, c):
            row = ps_ref[meta[0, j]] + meta[1, j]
            act(pltpu.make_async_copy(
                stage.at[s, pl.ds(pl.multiple_of(j * (RUN_WINDOW * SLABS), RUN_WINDOW * SLABS), RUN_WINDOW * SLABS), :],
                xd_hbm.at[pl.ds(pl.multiple_of(row * SLABS, SLABS), RUN_WINDOW * SLABS), :],
                sem.at[s]))
            return c

        lax.fori_loop(0, meta[2, 0], per_window, 0)

    first_step = 0
    for g, nt in enumerate(tiles):
        @pl.when((step >= first_step) & (step < first_step + nt))
        def _(g=g, nt=nt, first_step=first_step):
            meta_ref, prev_ref, pos_ref, h2_ref = meta_refs[g], prev_refs[g], pos_refs[g], h2_refs[g]
            tm = h2_ref.shape[0] // SLABS
            local = step - first_step
            slot = local % 2
            stage_rows = 2 * tm + N_EXPERTS * RUN_WINDOW

            @pl.when(local < 2)
            def _():
                stage[slot, pl.ds(0, stage_rows * SLABS), :] = jnp.zeros((stage_rows * SLABS, LANES), F32)

            def body(t, carry):
                for k in range(2):
                    p = pos_ref[0, 0, k * tm + t]
                    _row_tile(stage.at[slot], p)[...] = _row_tile(h2_ref, t)[...]
                return carry

            lax.fori_loop(0, tm, body, 0, unroll=ROW_DMA_UNROLL)

            @pl.when(local > 0)
            def _():
                windows(prev_ref, 1 - slot, lambda cp: cp.wait())

            windows(meta_ref, slot, lambda cp: cp.start())

            @pl.when(local == nt - 1)
            def _():
                windows(meta_ref, slot, lambda cp: cp.wait())
        first_step += nt


def _scatter(pad_start, counts, pad_end, metas, pos_blocks, h2s, total_rows):
    tms = [p.shape[2] // 2 for p in pos_blocks]
    tiles = [p.shape[0] for p in pos_blocks]
    firsts = [sum(tiles[:g]) for g in range(len(tiles))]
    ng = len(tiles)
    local = lambda g, shift=0: (lambda i, *_: jnp.clip(i - firsts[g] - shift, 0, tiles[g] - 1))
    smem = pltpu.SMEM
    in_specs = ([pl.BlockSpec((EXPERTS_PER_GROUP, LANES), (lambda i, *_, f=local(g): (f(i), 0)), memory_space=smem)
                 for g in range(ng)]
                + [pl.BlockSpec((EXPERTS_PER_GROUP, LANES), (lambda i, *_, f=local(g, 1): (f(i), 0)), memory_space=smem)
                   for g in range(ng)]
                + [pl.BlockSpec((1, 1, 2 * tms[g]), (lambda i, *_, f=local(g): (f(i), 0, 0)), memory_space=smem)
                   for g in range(ng)]
                + [pl.BlockSpec((tms[g] * SLABS, LANES), (lambda i, *_, f=local(g): (f(i), 0)))
                   for g in range(ng)])
    stage_rows = 2 * max(tms) + N_EXPERTS * RUN_WINDOW
    grid_spec = pltpu.PrefetchScalarGridSpec(
        num_scalar_prefetch=3,
        grid=(sum(tiles),),
        in_specs=in_specs,
        out_specs=pl.BlockSpec(memory_space=pl.ANY),
        scratch_shapes=[pltpu.VMEM((FFN_ROWS * SLABS, LANES), F32),
                        pltpu.VMEM((2, stage_rows * SLABS, LANES), F32),
                        pltpu.SemaphoreType.DMA((2,)), pltpu.SemaphoreType.DMA((1,))],
    )
    return pl.pallas_call(
        functools.partial(_scatter_kernel, tiles=tuple(tiles)),
        grid_spec=grid_spec,
        out_shape=jax.ShapeDtypeStruct((total_rows * SLABS, LANES), F32),
        compiler_params=_params(("arbitrary",)),
        name="moe_scatter",
    )(pad_start, counts, pad_end, *metas, *metas, *pos_blocks, *h2s)


def _ffn_kernel(be_ref, nbu_ref, x_ref, wg_ref, wu_ref, wd_ref, out_ref, wg_sc, wu_sc, wd_sc):
    i = pl.program_id(0)
    used = i < nbu_ref[0]
    new_expert = (i == 0) | (be_ref[i] != be_ref[jnp.maximum(i - 1, 0)])

    @pl.when(used & new_expert)
    def _():
        wg_sc[...] = wg_ref[0].astype(BF16)
        wu_sc[...] = wu_ref[0].astype(BF16)
        wd_sc[...] = wd_ref[0].astype(BF16)

    @pl.when(used)
    def _():
        x = _load_token_major(x_ref, FFN_ROWS).astype(BF16)
        g = jnp.dot(x, wg_sc[...], preferred_element_type=F32)
        u = jnp.dot(x, wu_sc[...], preferred_element_type=F32)
        mid = (g * _sigmoid(g) * u).astype(BF16)
        _store_token_major(out_ref, jnp.dot(mid, wd_sc[...], preferred_element_type=F32))

    @pl.when(jnp.logical_not(used))
    def _():
        out_ref[...] = jnp.zeros(out_ref.shape, F32)


def _ffn(block_expert, nb_used, xd, wg, wu, wd):
    nb = block_expert.shape[0]
    blk = lambda i, be, nbu: (jnp.minimum(i, nbu[0] - 1), 0)
    wsel = lambda i, be, nbu: (be[jnp.minimum(i, nbu[0] - 1)], 0, 0)
    grid_spec = pltpu.PrefetchScalarGridSpec(
        num_scalar_prefetch=2,
        grid=(nb,),
        in_specs=[pl.BlockSpec((FFN_ROWS * SLABS, LANES), blk),
                  pl.BlockSpec((1, D_MODEL, EXPERT_FF), wsel),
                  pl.BlockSpec((1, D_MODEL, EXPERT_FF), wsel),
                  pl.BlockSpec((1, EXPERT_FF, D_MODEL), wsel)],
        out_specs=pl.BlockSpec((FFN_ROWS * SLABS, LANES), lambda i, be, nbu: (i, 0)),
        scratch_shapes=[pltpu.VMEM((D_MODEL, EXPERT_FF), BF16), pltpu.VMEM((D_MODEL, EXPERT_FF), BF16),
                        pltpu.VMEM((EXPERT_FF, D_MODEL), BF16)],
    )
    return pl.pallas_call(
        _ffn_kernel,
        grid_spec=grid_spec,
        out_shape=jax.ShapeDtypeStruct((nb * FFN_ROWS * SLABS, LANES), F32),
        compiler_params=_params(("arbitrary",)),
        name="expert_ffn",
    )(block_expert, nb_used, xd, wg, wu, wd)


def _combine_kernel(ps_ref, meta_ref, metan_ref, pos_ref, x1_ref, gcol_ref, gfin_ref, eo_hbm, y_ref,
                    stage, ybuf, sem):
    i = pl.program_id(0)
    n = pl.num_programs(0)
    tm = x1_ref.shape[0]
    slot = i % 2

    def windows(meta, s, act):
        def per_window(j, c):
            row = ps_ref[meta[0, j]] + meta[1, j]
            act(pltpu.make_async_copy(
                eo_hbm.at[pl.ds(pl.multiple_of(row * SLABS, SLABS), RUN_WINDOW * SLABS), :],
                stage.at[s, pl.ds(pl.multiple_of(j * (RUN_WINDOW * SLABS), RUN_WINDOW * SLABS), RUN_WINDOW * SLABS), :],
                sem.at[s]))
            return c

        lax.fori_loop(0, meta[2, 0], per_window, 0)

    @pl.when(i == 0)
    def _():
        windows(meta_ref, 0, lambda cp: cp.start())

    @pl.when(i + 1 < n)
    def _():
        windows(metan_ref, 1 - slot, lambda cp: cp.start())

    windows(meta_ref, slot, lambda cp: cp.wait())

    def body(t, carry):
        for k in range(2):
            p = pos_ref[0, 0, k * tm + t]
            _row_tile(ybuf.at[k], t)[...] = _row_tile(stage.at[slot], p)[...]
        return carry

    lax.fori_loop(0, tm, body, 0, unroll=ROW_DMA_UNROLL)
    g = gcol_ref[...]
    y = x1_ref[...] + (g[:, 0:1] * _load_token_major(ybuf.at[0], tm) + g[:, 1:2] * _load_token_major(ybuf.at[1], tm))
    y_ref[...] = _rms(y, gfin_ref[...])


def _combine(pad_start, meta, pos_blocks, x1, gcol, g_final, eo):
    n = x1.shape[0]
    tm = _merge_tile(n)
    nt = n // tm
    stage_rows = 2 * tm + N_EXPERTS * RUN_WINDOW
    grid_spec = pltpu.PrefetchScalarGridSpec(
        num_scalar_prefetch=1,
        grid=(nt,),
        in_specs=[pl.BlockSpec((EXPERTS_PER_GROUP, LANES), lambda i, ps: (i, 0), memory_space=pltpu.SMEM),
                  pl.BlockSpec((EXPERTS_PER_GROUP, LANES), lambda i, ps: (jnp.minimum(i + 1, nt - 1), 0),
                               memory_space=pltpu.SMEM),
                  pl.BlockSpec((1, 1, 2 * tm), lambda i, ps: (i, 0, 0), memory_space=pltpu.SMEM),
                  pl.BlockSpec((tm, D_MODEL), lambda i, ps: (i, 0)),
                  pl.BlockSpec((tm, LANES), lambda i, ps: (i, 0)),
                  pl.BlockSpec(g_final.shape, lambda i, ps: (0, 0)),
                  pl.BlockSpec(memory_space=pl.ANY)],
        out_specs=pl.BlockSpec((tm, D_MODEL), lambda i, ps: (i, 0)),
        scratch_shapes=[pltpu.VMEM((2, stage_rows * SLABS, LANES), F32),
                        pltpu.VMEM((2, tm * SLABS, LANES), F32), pltpu.SemaphoreType.DMA((2,))],
    )
    return pl.pallas_call(
        _combine_kernel,
        grid_spec=grid_spec,
        out_shape=jax.ShapeDtypeStruct((n, D_MODEL), F32),
        compiler_params=_params(("arbitrary",)),
        name="combine",
    )(pad_start, meta, meta, pos_blocks, x1, gcol, g_final, eo)


def _layout(cnt, n_tokens):
    counts = cnt[:, 0].astype(jnp.int32)
    padded = (counts + RUN_WINDOW + FFN_ROWS - 1) // FFN_ROWS * FFN_ROWS
    pad_end = jnp.cumsum(padded).astype(jnp.int32)
    pad_start = pad_end - padded
    nb = -(-(2 * n_tokens + N_EXPERTS * RUN_WINDOW) // FFN_ROWS) + N_EXPERTS + 1
    block_start = jnp.arange(nb, dtype=jnp.int32) * FFN_ROWS
    block_expert = jnp.minimum(jnp.sum((block_start[:, None] >= pad_end[None, :]).astype(jnp.int32), axis=1),
                               N_EXPERTS - 1).astype(jnp.int32)
    nb_used = (pad_end[N_EXPERTS - 1:] // FFN_ROWS).astype(jnp.int32)
    return pad_start, counts, pad_end, block_expert, nb_used, nb * FFN_ROWS


def _rope_tables(l, past, rows):
    half = ROPE_DIM // 2
    inv = ROPE_THETA ** (-jnp.arange(half, dtype=F32) / half)
    pos = (past + (jnp.arange(rows, dtype=jnp.int32) % l)).astype(F32)
    ang = pos[:, None] * inv[None, :]
    pad = jnp.zeros((rows, LANES - ROPE_DIM), F32)
    cos = jnp.concatenate([jnp.cos(ang), jnp.cos(ang), pad], axis=1)
    sin = jnp.concatenate([-jnp.sin(ang), jnp.sin(ang), pad], axis=1)
    return cos, sin


def _mixers(x, ckv_past, kpe_past, s0, w):
    b, l, d = x.shape
    n = b * l
    x2 = x.reshape(n, d)
    past = 0 if ckv_past is None else ckv_past.shape[1]
    cos_t, sin_t = _rope_tables(l, past, max(l, ROW_TILE))
    prompt = ckv_past is None
    (qcat, kcat, ckv, kpe, gq, gk, gv, la, sgr, sga, sgb, *maybe_vt) = _inproj(x2, cos_t, sin_t, w, prompt)
    if prompt:
        oa = _attn_prompt(qcat, kcat, maybe_vt[0], w["w_uv"], b, l)
    else:
        oa = _attn_sample(qcat, kcat, ckv_past, kpe_past, w["w_uv"], b, l)
    ob, s_new = _gla(gq, gk, gv, la, sgr, w["g_gla"], s0, b, l)
    merge_in = (oa.reshape(n, -1), ob.reshape(n, -1), sga, sgb, x2)
    return merge_in, (ckv.reshape(b, l, KV_LORA), kpe.reshape(b, l, ROPE_DIM), s_new)


def _layer(groups, w, g_final):
    fronts = [_mixers(*g, w) for g in groups]
    cnt = jnp.zeros((N_EXPERTS, LANES), F32)
    merged = []
    for merge_in, _ in fronts:
        x1, h2, route, gcol, cnt, meta = _merge(*merge_in, w, cnt)
        merged.append((x1, h2, route, gcol, meta))
    n_tokens = sum(m[0].shape[0] for m in merged)
    pad_start, counts, pad_end, block_expert, nb_used, total_rows = _layout(cnt, n_tokens)
    pos_blocks = []
    for x1, h2, route, gcol, meta in merged:
        n = x1.shape[0]
        tm = _merge_tile(n)
        pos_blocks.append(route[4:6].reshape(2, n // tm, tm).transpose(1, 0, 2).reshape(n // tm, 1, 2 * tm))
    xd = _scatter(pad_start, counts, pad_end, [m[4] for m in merged], pos_blocks, [m[1] for m in merged],
                  total_rows)
    eo = _ffn(block_expert, nb_used, xd, w["w_eg"], w["w_eu"], w["w_ed"])
    outs = []
    for (x1, h2, route, gcol, meta), pos, (_, extras), g in zip(merged, pos_blocks, fronts, groups):
        y = _combine(pad_start, meta, pos, x1, gcol, g_final, eo)
        outs.append((y.reshape(g[0].shape),) + extras)
    return outs


def _prep_weights(w_in, g_norm_mix, g_qnorm, w_uq, g_kvnorm, w_ukv, w_gate2, b_gate2, g_gla_norm,
                  w_branch_a, w_branch_b, w_out, g_norm_ffn, w_router_group, b_router_group,
                  w_router_expert, b_router_expert, w_exp_gate, w_exp_up, w_exp_down):
    nqk = GLA_HEADS * GLA_DK
    nv = GLA_HEADS * GLA_DV
    o = 0
    parts = {}
    for name, width in (("cq", Q_LORA), ("ckv", KV_LORA), ("kpe", ROPE_DIM), ("gq", nqk), ("gk", nqk),
                        ("gv", nv), ("glr", GATE_RANK), ("gr", nv), ("ga", D_MODEL), ("gb", D_MODEL)):
        parts[name] = w_in[:, o:o + width]
        o += width
    padc = lambda a, width: jnp.pad(a, ((0, 0), (0, width - a.shape[1])))
    w_in_all = jnp.concatenate([parts["cq"], parts["ckv"], padc(parts["kpe"], LANES), padc(parts["glr"], LANES),
                                parts["gq"], parts["gk"], parts["gv"], parts["gr"], parts["ga"], parts["gb"]],
                               axis=1).astype(BF16)
    assert w_in_all.shape[1] == _IN_COLS["gb"].stop
    w_abo = jnp.concatenate([w_branch_a, w_branch_b, w_out], axis=0).astype(BF16)
    uq = w_uq.reshape(Q_LORA, MLA_HEADS, NOPE_DIM + ROPE_DIM)
    ukv = w_ukv.reshape(KV_LORA, MLA_HEADS, NOPE_DIM + V_DIM)
    lat = _fold_q(uq[:, :, :NOPE_DIM].transpose(1, 0, 2), ukv[:, :, :NOPE_DIM].transpose(1, 0, 2))
    q_rope = uq[:, :, NOPE_DIM:].transpose(1, 0, 2)
    w_qcat = jnp.concatenate([lat, q_rope, jnp.zeros((MLA_HEADS, Q_LORA, QCAT - KV_LORA - ROPE_DIM), F32)],
                             axis=2)
    w_qcat = w_qcat.transpose(1, 0, 2).reshape(Q_LORA, MLA_HEADS * QCAT).astype(BF16)
    w_router = jnp.concatenate([w_router_expert.T, w_router_group.T,
                                jnp.zeros((ROUTER_ROWS - N_EXPERTS - N_GROUPS, D_MODEL), F32)], axis=0)
    b_router = jnp.concatenate([b_router_expert, b_router_group,
                                jnp.zeros((ROUTER_ROWS - N_EXPERTS - N_GROUPS,), F32)]).reshape(ROUTER_ROWS, 1)
    return {
        "g_mix": g_norm_mix.reshape(1, D_MODEL), "w_in": w_in_all,
        "g_qn": g_qnorm.reshape(1, Q_LORA), "g_kvn": g_kvnorm.reshape(1, KV_LORA), "w_qcat": w_qcat,
        "w_g2": jnp.pad(w_gate2, ((0, LANES - GATE_RANK), (0, 0))).astype(BF16),
        "b_g2": b_gate2.reshape(1, nqk),
        "w_uv": ukv[:, :, NOPE_DIM:].transpose(1, 0, 2).astype(BF16),
        "g_gla": g_gla_norm.reshape(1, GLA_DV),
        "w_abo": w_abo,
        "g_ffn": g_norm_ffn.reshape(1, D_MODEL), "w_router": w_router, "b_router": b_router,
        "w_eg": w_exp_gate, "w_eu": w_exp_up, "w_ed": w_exp_down,
    }


def kernel(x_prompt, x_sample, cache_ckv, cache_krope, state_gla, w_in, g_norm_mix, g_qnorm, w_uq, g_kvnorm, w_ukv, w_gate2, b_gate2, g_gla_norm, w_branch_a, w_branch_b, w_out, g_norm_ffn, w_router_group, b_router_group, w_router_expert, b_router_expert, w_exp_gate, w_exp_up, w_exp_down, g_norm_final):
    depth = w_in.shape[0]
    assert depth == 1, "the final norm is fused into the last layer's combine step"
    gfin = g_norm_final.reshape(1, D_MODEL)
    w = _prep_weights(w_in[0], g_norm_mix[0], g_qnorm[0], w_uq[0], g_kvnorm[0], w_ukv[0], w_gate2[0],
                      b_gate2[0], g_gla_norm[0], w_branch_a[0], w_branch_b[0], w_out[0], g_norm_ffn[0],
                      w_router_group[0], b_router_group[0], w_router_expert[0], b_router_expert[0],
                      w_exp_gate[0], w_exp_up[0], w_exp_down[0])
    bp = x_prompt.shape[0]
    zero_state = jnp.zeros((bp, GLA_HEADS, GLA_DK, GLA_DV), F32)
    (yp, c1, k1, s1), (ys, c2, k2, s2) = _layer(
        [(x_prompt, None, None, zero_state),
         (x_sample, cache_ckv[0], cache_krope[0], state_gla[0].astype(F32))], w, gfin)
    return (yp, ys, c1[None], k1[None], s1[None], c2[None], k2[None], s2[None])
```

```python
import functools
import math

import jax
import jax.numpy as jnp
from jax import lax
from jax.experimental import pallas as pl
from jax.experimental.pallas import tpu as pltpu

F32 = jnp.float32
BF16 = jnp.bfloat16

D_MODEL = 1024
CHUNK = 64
EPS = 1e-6
MLA_HEADS = 8
Q_LORA = 256
KV_LORA = 128
NOPE_DIM = 64
ROPE_DIM = 32
V_DIM = 64
ROPE_THETA = 10000.0
GLA_HEADS = 4
GLA_DK = 128
GLA_DV = 256
GATE_RANK = 16
GATE_TEMP = 16.0
N_GROUPS = 4
EXPERTS_PER_GROUP = 8
N_EXPERTS = 32
EXPERT_FF = 512

LANES = 128
QCAT = 2 * LANES
ROW_TILE = 256
MERGE_TILE = 512
MERGE_PARTS = 1
RUN_WINDOW = 32
GLA_BATCH = 2
GLA_STEP_TOKENS = 256
Q_BLOCK = ROW_TILE
KEY_TILE = ROW_TILE
ATT_SAMPLE_SEQS = 4
SUM_ROWS = 16
GLA_CHUNK = 64
FFN_ROWS = 512
ROUTER_ROWS = 48
VMEM_LIMIT = 56 * 1024 * 1024
LOG2E = 1.4426950408889634
ATT_SCALE = LOG2E / math.sqrt(NOPE_DIM + ROPE_DIM)

_NT = (((1,), (1,)), ((), ()))
_TN = (((0,), (0,)), ((), ()))


def _params(sem):
    return pltpu.CompilerParams(dimension_semantics=sem, vmem_limit_bytes=VMEM_LIMIT)


def _rms(x, g):
    return x * lax.rsqrt(jnp.mean(x * x, axis=-1, keepdims=True) + EPS) * g


def _sigmoid(x):
    return 1.0 / (1.0 + jnp.exp(-x))


def _full(shape):
    n = len(shape)
    return pl.BlockSpec(shape, lambda *_: (0,) * n)


SLABS = D_MODEL // LANES


def _store_token_major(ref, x, first_row=0):
    rows = x.shape[0]
    for j in range(SLABS):
        ref[pl.ds(first_row * SLABS + j, rows, stride=SLABS), :] = x[:, j * LANES:(j + 1) * LANES]


def _load_token_major(ref, rows):
    return jnp.concatenate([ref[pl.ds(j, rows, stride=SLABS), :] for j in range(SLABS)], axis=1)


def _fold_q_kernel(wq_ref, wk_ref, o_ref):
    o_ref[0] = lax.dot_general(wq_ref[0], wk_ref[0], _NT, precision=lax.Precision.HIGHEST,
                               preferred_element_type=F32)


def _fold_q(wq_nope, wk_nope):
    return pl.pallas_call(
        _fold_q_kernel,
        grid=(MLA_HEADS,),
        in_specs=[pl.BlockSpec((1, Q_LORA, NOPE_DIM), lambda h: (h, 0, 0)),
                  pl.BlockSpec((1, KV_LORA, NOPE_DIM), lambda h: (h, 0, 0))],
        out_specs=pl.BlockSpec((1, Q_LORA, KV_LORA), lambda h: (h, 0, 0)),
        out_shape=jax.ShapeDtypeStruct((MLA_HEADS, Q_LORA, KV_LORA), F32),
        compiler_params=_params(("arbitrary",)),
        name="fold_q",
    )(wq_nope, wk_nope)


_W_SMALL = Q_LORA + KV_LORA + 2 * LANES
_IN_COLS = {}
_off = 0
for _name, _width in (("small", _W_SMALL), ("gqk", 2 * GLA_HEADS * GLA_DK), ("gv", GLA_HEADS * GLA_DV),
                      ("gr", GLA_HEADS * GLA_DV), ("ga", D_MODEL), ("gb", D_MODEL)):
    _IN_COLS[_name] = slice(_off, _off + _width)
    _off += _width


def _inproj_kernel(x_ref, cos_ref, sin_ref, gmix_ref, win_ref, gqn_ref, gkvn_ref, wqc_ref,
                   wg2_ref, bg2_ref,
                   qcat_ref, kcat_ref, ckv_ref, kpe_ref, gq_ref, gk_ref, gv_ref, la_ref,
                   sgr_ref, sga_ref, sgb_ref, *maybe_vt_ref, transposed):
    hb = _rms(x_ref[...], gmix_ref[...]).astype(BF16)
    proj = lambda name: jnp.dot(hb, win_ref[:, _IN_COLS[name]], preferred_element_type=F32)
    zs = proj("small")
    cos = cos_ref[...]
    sin = sin_ref[...]
    first_half = lax.broadcasted_iota(jnp.int32, cos.shape, 1) < ROPE_DIM // 2

    def rope(v):
        rot = jnp.where(first_half, pltpu.roll(v, LANES - ROPE_DIM // 2, 1),
                        pltpu.roll(v, ROPE_DIM // 2, 1))
        return v * cos + rot * sin

    cqn = _rms(zs[:, :Q_LORA], gqn_ref[...]).astype(BF16)
    qc = jnp.dot(cqn, wqc_ref[...], preferred_element_type=F32)
    for h in range(MLA_HEADS):
        lat = qc[:, h * QCAT:h * QCAT + LANES]
        pe = qc[:, h * QCAT + LANES:(h + 1) * QCAT]
        if transposed:
            lat_t = jnp.transpose(lat * ATT_SCALE).astype(BF16)
            pe_t = jnp.transpose(rope(pe) * ATT_SCALE).astype(BF16)
            for blk in range(lat.shape[0] // Q_BLOCK):
                cols = slice(blk * Q_BLOCK, (blk + 1) * Q_BLOCK)
                qcat_ref[blk, :LANES, h * Q_BLOCK:(h + 1) * Q_BLOCK] = lat_t[:, cols]
                qcat_ref[blk, LANES:, h * Q_BLOCK:(h + 1) * Q_BLOCK] = pe_t[:, cols]
        else:
            qcat_ref[h, :, :LANES] = (lat * ATT_SCALE).astype(BF16)
            qcat_ref[h, :, LANES:] = (rope(pe) * ATT_SCALE).astype(BF16)

    ckv = _rms(zs[:, Q_LORA:Q_LORA + KV_LORA], gkvn_ref[...])
    ckv_ref[...] = ckv
    if transposed:
        maybe_vt_ref[0][0] = jnp.transpose(ckv).astype(BF16)
    kpe = rope(zs[:, Q_LORA + KV_LORA:Q_LORA + KV_LORA + LANES])
    kpe_ref[...] = kpe[:, :ROPE_DIM]
    kcat_ref[:, :LANES] = ckv.astype(BF16)
    kcat_ref[:, LANES:] = kpe.astype(BF16)

    glr = zs[:, Q_LORA + KV_LORA + LANES:].astype(BF16)
    xg = jnp.dot(glr, wg2_ref[...], preferred_element_type=F32) + bg2_ref[...]
    la_ref[...] = (jnp.minimum(xg, 0.0) - jnp.log(1.0 + jnp.exp(-jnp.abs(xg)))) * (1.0 / GATE_TEMP)

    zqk = proj("gqk")
    nqk = GLA_HEADS * GLA_DK
    gq_ref[...] = (zqk[:, :nqk] * (GLA_DK ** -0.5)).astype(BF16)
    gk_ref[...] = zqk[:, nqk:].astype(BF16)
    gv_ref[...] = proj("gv").astype(BF16)
    gr = proj("gr")
    sgr_ref[...] = (gr * _sigmoid(gr)).astype(BF16)
    sga_ref[...] = _sigmoid(proj("ga")).astype(BF16)
    sgb_ref[...] = _sigmoid(proj("gb")).astype(BF16)


def _inproj(x2, cos_t, sin_t, w, transposed):
    n = x2.shape[0]
    tm = ROW_TILE
    nt = n // tm
    tab_blocks = cos_t.shape[0] // tm
    row = lambda width: pl.BlockSpec((tm, width), lambda i: (i, 0))
    tab = pl.BlockSpec((tm, LANES), lambda i: (i % tab_blocks, 0))
    nqk = GLA_HEADS * GLA_DK
    nv = GLA_HEADS * GLA_DV
    weights = [w["g_mix"], w["w_in"], w["g_qn"], w["g_kvn"], w["w_qcat"], w["w_g2"], w["b_g2"]]
    out_shape = [
        jax.ShapeDtypeStruct((MLA_HEADS, n, QCAT), BF16),
        jax.ShapeDtypeStruct((n, QCAT), BF16),
        jax.ShapeDtypeStruct((n, KV_LORA), F32),
        jax.ShapeDtypeStruct((n, ROPE_DIM), F32),
        jax.ShapeDtypeStruct((n, nqk), BF16),
        jax.ShapeDtypeStruct((n, nqk), BF16),
        jax.ShapeDtypeStruct((n, nv), BF16),
        jax.ShapeDtypeStruct((n, nqk), F32),
        jax.ShapeDtypeStruct((n, nv), BF16),
        jax.ShapeDtypeStruct((n, D_MODEL), BF16),
        jax.ShapeDtypeStruct((n, D_MODEL), BF16),
    ]
    out_specs = [
        pl.BlockSpec((MLA_HEADS, tm, QCAT), lambda i: (0, i, 0)),
        row(QCAT), row(KV_LORA), row(ROPE_DIM), row(nqk), row(nqk), row(nv), row(nqk), row(nv),
        row(D_MODEL), row(D_MODEL),
    ]
    if transposed:
        qb = tm // Q_BLOCK
        out_shape[0] = jax.ShapeDtypeStruct((n // Q_BLOCK, QCAT, MLA_HEADS * Q_BLOCK), BF16)
        out_specs[0] = pl.BlockSpec((qb, QCAT, MLA_HEADS * Q_BLOCK), lambda i: (i, 0, 0))
        out_shape.append(jax.ShapeDtypeStruct((nt, KV_LORA, tm), BF16))
        out_specs.append(pl.BlockSpec((1, KV_LORA, tm), lambda i: (i, 0, 0)))
    return pl.pallas_call(
        functools.partial(_inproj_kernel, transposed=transposed),
        grid=(nt,),
        in_specs=[row(D_MODEL), tab, tab] + [_full(a.shape) for a in weights],
        out_specs=out_specs,
        out_shape=out_shape,
        compiler_params=_params(("parallel",)),
        name="inproj",
    )(x2, cos_t, sin_t, *weights)


def _head_out(o_lat, wuv_ref, rows):
    outs = []
    for h in range(MLA_HEADS):
        oh = o_lat[h * rows:(h + 1) * rows].astype(BF16)
        outs.append(jnp.dot(oh, wuv_ref[h], preferred_element_type=F32))
    return jnp.concatenate(outs, axis=1)


def _attn_prompt_kernel(qt_ref, k_ref, vt_ref, bias_ref, wuv_ref, o_ref, m_sc, l_sc, acc_sc):
    i = pl.program_id(1)
    m_sc[...] = jnp.full(m_sc.shape, -jnp.inf, F32)
    l_sc[...] = jnp.zeros(l_sc.shape, F32)
    acc_sc[...] = jnp.zeros(acc_sc.shape, F32)
    heads = [slice(h * Q_BLOCK, (h + 1) * Q_BLOCK) for h in range(MLA_HEADS)]
    ones_rows = jnp.ones((SUM_ROWS, KEY_TILE), BF16)

    def process(tiles, diagonal_last):
        nt = len(tiles)
        kbs = [k_ref[0, pl.ds(pl.multiple_of(kt * KEY_TILE, KEY_TILE), KEY_TILE), :] for kt in tiles]
        vts = [jnp.concatenate([vt_ref[kt], ones_rows], axis=0) for kt in tiles]
        ss = [[None] * MLA_HEADS for _ in range(nt)]
        for t in range(nt):
            for h, hs in enumerate(heads):
                s = jnp.dot(kbs[t], qt_ref[0, :, hs], preferred_element_type=F32)
                if diagonal_last and t == nt - 1:
                    s = s + bias_ref[...]
                ss[t][h] = s
        tmax = [[jnp.max(ss[t][h], axis=0, keepdims=True) for h in range(MLA_HEADS)] for t in range(nt)]
        scale = [[None] * MLA_HEADS for _ in range(nt)]
        mrun = [[None] * MLA_HEADS for _ in range(nt)]
        for h, hs in enumerate(heads):
            m = m_sc[:, hs]
            for t in range(nt):
                m_new = jnp.maximum(m, tmax[t][h])
                scale[t][h] = jnp.exp2(m - m_new)
                mrun[t][h] = m_new
                m = m_new
            m_sc[:, hs] = m
        pvs = [[jnp.dot(vts[t], jnp.exp2(ss[t][h] - mrun[t][h]).astype(BF16), preferred_element_type=F32)
                for h in range(MLA_HEADS)] for t in range(nt)]
        for h, hs in enumerate(heads):
            l, acc = l_sc[:, hs], acc_sc[:, hs]
            for t in range(nt):
                acc = scale[t][h] * acc + pvs[t][h][:KV_LORA]
                l = scale[t][h] * l + pvs[t][h][KV_LORA:KV_LORA + 1]
            l_sc[:, hs], acc_sc[:, hs] = l, acc

    def body(j, carry):
        process([2 * j, 2 * j + 1], False)
        return carry

    lax.fori_loop(0, i // 2, body, 0)

    @pl.when(i % 2 == 1)
    def _():
        process([i - 1, i], True)

    @pl.when(i % 2 == 0)
    def _():
        process([i], True)

    o_lat_t = (acc_sc[...] / l_sc[...]).astype(BF16)
    outs = []
    for h, hs in enumerate(heads):
        outs.append(lax.dot_general(o_lat_t[:, hs], wuv_ref[h], _TN, preferred_element_type=F32))
    o_ref[0] = jnp.concatenate(outs, axis=1).astype(BF16)


def _attn_prompt(qt, kcat, vt, wuv, b, l):
    assert Q_BLOCK == KEY_TILE and Q_BLOCK % CHUNK == 0
    nq = l // Q_BLOCK
    nkt = l // KEY_TILE
    cols = MLA_HEADS * Q_BLOCK
    pos_chunk = jnp.arange(Q_BLOCK, dtype=jnp.int32) // CHUNK
    bias = jnp.where(pos_chunk[:, None] <= pos_chunk[None, :], 0.0, -jnp.inf).astype(F32)
    return pl.pallas_call(
        _attn_prompt_kernel,
        grid=(b, nq),
        in_specs=[pl.BlockSpec((1, QCAT, cols), lambda bb, i: (bb * nq + i, 0, 0)),
                  pl.BlockSpec((1, l, QCAT), lambda bb, i: (bb, 0, 0)),
                  pl.BlockSpec((nkt, KV_LORA, KEY_TILE), lambda bb, i: (bb, 0, 0)),
                  _full(bias.shape), _full(wuv.shape)],
        out_specs=pl.BlockSpec((1, Q_BLOCK, MLA_HEADS * V_DIM), lambda bb, i: (bb, i, 0)),
        out_shape=jax.ShapeDtypeStruct((b, l, MLA_HEADS * V_DIM), BF16),
        scratch_shapes=[pltpu.VMEM((1, cols), F32), pltpu.VMEM((1, cols), F32),
                        pltpu.VMEM((KV_LORA, cols), F32)],
        compiler_params=_params(("parallel", "arbitrary")),
        name="attn_prompt",
    )(qt, kcat.reshape(b, l, QCAT), vt, bias, wuv)


def _attn_sample_kernel(q_ref, cckv_ref, ckr_ref, kn_ref, wuv_ref, o_ref, *, ls, seqs):
    for s in range(seqs):
        rows = slice(s * ls, (s + 1) * ls)
        q = q_ref[:, rows, :].reshape(MLA_HEADS * ls, QCAT)
        ck = cckv_ref[s].astype(BF16)
        kr = ckr_ref[s].astype(BF16)
        kn = kn_ref[rows, :]
        s_c = (lax.dot_general(q[:, :KV_LORA], ck, _NT, preferred_element_type=F32)
               + lax.dot_general(q[:, KV_LORA:KV_LORA + ROPE_DIM], kr, _NT, preferred_element_type=F32))
        s_n = lax.dot_general(q, kn, _NT, preferred_element_type=F32)
        m = jnp.maximum(jnp.max(s_c, axis=1, keepdims=True), jnp.max(s_n, axis=1, keepdims=True))
        p_c = jnp.exp2(s_c - m)
        p_n = jnp.exp2(s_n - m)
        den = jnp.sum(p_c, axis=1, keepdims=True) + jnp.sum(p_n, axis=1, keepdims=True)
        o_lat = (jnp.dot(p_c.astype(BF16), ck, preferred_element_type=F32)
                 + jnp.dot(p_n.astype(BF16), kn[:, :KV_LORA], preferred_element_type=F32)) / den
        o_ref[s] = _head_out(o_lat, wuv_ref, ls).astype(BF16)


def _attn_sample(qcat, kcat, cache_ckv, cache_krope, wuv, b, ls):
    past = cache_ckv.shape[1]
    assert past % CHUNK == 0 and ls <= CHUNK
    seqs = ATT_SAMPLE_SEQS if b % ATT_SAMPLE_SEQS == 0 else 1
    return pl.pallas_call(
        functools.partial(_attn_sample_kernel, ls=ls, seqs=seqs),
        grid=(b // seqs,),
        in_specs=[pl.BlockSpec((MLA_HEADS, seqs * ls, QCAT), lambda bb: (0, bb, 0)),
                  pl.BlockSpec((seqs, past, KV_LORA), lambda bb: (bb, 0, 0)),
                  pl.BlockSpec((seqs, past, ROPE_DIM), lambda bb: (bb, 0, 0)),
                  pl.BlockSpec((seqs * ls, QCAT), lambda bb: (bb, 0)),
                  _full(wuv.shape)],
        out_specs=pl.BlockSpec((seqs, ls, MLA_HEADS * V_DIM), lambda bb: (bb, 0, 0)),
        out_shape=jax.ShapeDtypeStruct((b, ls, MLA_HEADS * V_DIM), BF16),
        compiler_params=_params(("parallel",)),
        name="attn_sample",
    )(qcat, cache_ckv, cache_krope, kcat, wuv)


def _cumsum_rows(x):
    c = x.shape[0]
    row = lax.broadcasted_iota(jnp.int32, x.shape, 0)
    s = 1
    while s < c:
        x = x + jnp.where(row >= s, pltpu.roll(x, s, 0), 0.0)
        s *= 2
    return x


def _gla_kernel(q_ref, k_ref, v_ref, la_ref, sgr_ref, gn_ref, s0_ref, o_ref, sout_ref, st_sc, *, c):
    j = pl.program_id(1)

    @pl.when(j == 0)
    def _():
        st_sc[...] = s0_ref[...]

    half = 32
    row = lax.broadcasted_iota(jnp.int32, (c, GLA_DK), 0)
    rr = lax.broadcasted_iota(jnp.int32, (c, c), 0)
    cc = lax.broadcasted_iota(jnp.int32, (c, c), 1)
    causal = cc <= rr
    same_half = (rr < half) == (cc < half)
    gn = gn_ref[...]
    n_chunks = q_ref.shape[1] // c
    seq_heads = [(s, h) for s in range(GLA_BATCH) for h in range(GLA_HEADS)]
    chains = [(s, h, ci) for s, h in seq_heads for ci in range(n_chunks)]
    ksl = lambda h: slice(h * GLA_DK, (h + 1) * GLA_DK)
    vsl = lambda h: slice(h * GLA_DV, (h + 1) * GLA_DV)
    rsl = lambda ci: slice(ci * c, (ci + 1) * c)

    prep = {}
    for s, h, ci in chains:
        q = q_ref[s, rsl(ci), ksl(h)].astype(F32)
        k = k_ref[s, rsl(ci), ksl(h)].astype(F32)
        b = _cumsum_rows(la_ref[s, rsl(ci), ksl(h)])
        if c > half:
            mid = jnp.where(row < half, b[half // 2 - 1:half // 2, :], b[half + half // 2 - 1:half + half // 2, :])
        else:
            mid = jnp.broadcast_to(b[c // 2 - 1:c // 2, :], b.shape)
        last = b[c - 1:c, :]
        ops = {"qe": (q * jnp.exp(b - mid)).astype(BF16), "ke": (k * jnp.exp(mid - b)).astype(BF16),
               "q0": (q * jnp.exp(b)).astype(BF16), "kd": (k * jnp.exp(last - b)).astype(BF16), "last": last}
        if c > half:
            edge = b[half - 1:half, :]
            ops["qo"] = (q * jnp.exp(jnp.minimum(b - edge, 0.0))).astype(BF16)
            ops["ko"] = (k * jnp.exp(jnp.minimum(edge - b, 0.0))).astype(BF16)
        prep[s, h, ci] = ops
    att = {}
    for key in chains:
        a = lax.dot_general(prep[key]["qe"], prep[key]["ke"], _NT, preferred_element_type=F32)
        if c > half:
            a_off = lax.dot_general(prep[key]["qo"], prep[key]["ko"], _NT, preferred_element_type=F32)
            a = jnp.where(same_half, a, a_off)
        att[key] = jnp.where(causal, a, 0.0).astype(BF16)
    intra, upd, decay = {}, {}, {}
    for s, h, ci in chains:
        v = v_ref[s, rsl(ci), vsl(h)]
        intra[s, h, ci] = jnp.dot(att[s, h, ci], v, preferred_element_type=F32)
        upd[s, h, ci] = lax.dot_general(prep[s, h, ci]["kd"], v, _TN, preferred_element_type=F32)
        dcol = jnp.exp(jnp.transpose(jnp.broadcast_to(prep[s, h, ci]["last"], (GLA_DK, GLA_DK))))
        decay[s, h, ci] = jnp.concatenate([dcol, dcol], axis=1)
    for s, h in seq_heads:
        st = st_sc[s, h]
        for ci in range(n_chunks):
            o = jnp.dot(prep[s, h, ci]["q0"], st.astype(BF16), preferred_element_type=F32) + intra[s, h, ci]
            st = decay[s, h, ci] * st + upd[s, h, ci]
            on = _rms(o, gn) * sgr_ref[s, rsl(ci), vsl(h)].astype(F32)
            o_ref[s, rsl(ci), vsl(h)] = on.astype(BF16)
        st_sc[s, h] = st

    @pl.when(j == pl.num_programs(1) - 1)
    def _():
        sout_ref[...] = st_sc[...]


def _gla(gq, gk, gv, la, sgr, gn, s0, b, l):
    c = min(GLA_CHUNK, l)
    step = min(GLA_STEP_TOKENS, l)
    nc = l // step
    nqk = GLA_HEADS * GLA_DK
    nv = GLA_HEADS * GLA_DV
    r3 = lambda a: a.reshape(b, l, a.shape[-1])
    assert b % GLA_BATCH == 0 and l % step == 0 and step % c == 0
    tok = lambda width: pl.BlockSpec((GLA_BATCH, step, width), lambda bb, j: (bb, j, 0))
    st = pl.BlockSpec((GLA_BATCH, GLA_HEADS, GLA_DK, GLA_DV), lambda bb, j: (bb, 0, 0, 0))
    return pl.pallas_call(
        functools.partial(_gla_kernel, c=c),
        grid=(b // GLA_BATCH, nc),
        in_specs=[tok(nqk), tok(nqk), tok(nv), tok(nqk), tok(nv), _full(gn.shape), st],
        out_specs=[tok(nv), st],
        out_shape=[jax.ShapeDtypeStruct((b, l, nv), BF16),
                   jax.ShapeDtypeStruct((b, GLA_HEADS, GLA_DK, GLA_DV), F32)],
        scratch_shapes=[pltpu.VMEM((GLA_BATCH, GLA_HEADS, GLA_DK, GLA_DV), F32)],
        compiler_params=_params(("parallel", "arbitrary")),
        name="gla",
    )(r3(gq), r3(gk), r3(gv), r3(la), r3(sgr), gn, s0)


_ROWS_A = slice(0, MLA_HEADS * V_DIM)
_ROWS_B = slice(_ROWS_A.stop, _ROWS_A.stop + GLA_HEADS * GLA_DV)
_ROWS_O = slice(_ROWS_B.stop, _ROWS_B.stop + D_MODEL)


def _merge_kernel(oa_ref, ob_ref, sga_ref, sgb_ref, x_ref, wabo_ref, gffn_ref,
                  wr_ref, br_ref, cnt0_ref, x1_ref, h2_ref, route_ref, gcol_ref, cnt_ref, meta_ref, run_sc):
    assert MERGE_PARTS == 1
    tm = x_ref.shape[0]
    pr = tm // MERGE_PARTS
    wr = wr_ref[...]
    whi = wr.astype(BF16)
    wlo = (wr - whi.astype(F32)).astype(BF16)
    ridx = lax.broadcasted_iota(jnp.int32, (EXPERTS_PER_GROUP, pr), 0)
    eidx = lax.broadcasted_iota(jnp.int32, (N_EXPERTS, pr), 0)
    earlier = jnp.where(lax.broadcasted_iota(jnp.int32, (pr, pr), 0) < lax.broadcasted_iota(jnp.int32, (pr, pr), 1),
                        1.0, 0.0).astype(BF16)
    big = jnp.int32(1 << 20)

    def top(vals):
        vmax = jnp.max(vals, axis=0, keepdims=True)
        imax = jnp.min(jnp.where(vals == vmax, ridx, big), axis=0, keepdims=True)
        return vmax, imax

    h2s = []
    for part in range(MERGE_PARTS):
        rows = slice(part * pr, (part + 1) * pr)
        ya = jnp.dot(oa_ref[rows, :], wabo_ref[_ROWS_A, :], preferred_element_type=F32)
        yb = jnp.dot(ob_ref[rows, :], wabo_ref[_ROWS_B, :], preferred_element_type=F32)
        merged = (sga_ref[rows, :].astype(F32) * ya + sgb_ref[rows, :].astype(F32) * yb).astype(BF16)
        x1 = x_ref[rows, :] + jnp.dot(merged, wabo_ref[_ROWS_O, :], preferred_element_type=F32)
        x1_ref[rows, :] = x1
        h2 = _rms(x1, gffn_ref[...])
        _store_token_major(h2_ref, h2, part * pr)
        h2s.append(h2)

    routed = []
    for part in range(MERGE_PARTS):
        rows = slice(part * pr, (part + 1) * pr)
        hi = h2s[part].astype(BF16)
        lo = (h2s[part] - hi.astype(F32)).astype(BF16)
        logits = (lax.dot_general(whi, hi, _NT, preferred_element_type=F32)
                  + lax.dot_general(whi, lo, _NT, preferred_element_type=F32)
                  + lax.dot_general(wlo, hi, _NT, preferred_element_type=F32)) + br_ref[...]
        gl = jnp.where(ridx < N_GROUPS, logits[N_EXPERTS:N_EXPERTS + EXPERTS_PER_GROUP], -jnp.inf)
        gmax, g_top = top(gl)
        p_top = 1.0 / jnp.sum(jnp.exp(gl - gmax), axis=0, keepdims=True)
        e_sel = logits[:EXPERTS_PER_GROUP]
        for g in range(1, N_GROUPS):
            e_sel = jnp.where(g_top == g, logits[g * EXPERTS_PER_GROUP:(g + 1) * EXPERTS_PER_GROUP], e_sel)
        v1, i1 = top(e_sel)
        v2, i2 = top(jnp.where(ridx == i1, -jnp.inf, e_sel))
        e21 = jnp.exp(v2 - v1)
        w1 = p_top / (1.0 + e21)
        w2 = p_top * e21 / (1.0 + e21)
        base = g_top * EXPERTS_PER_GROUP
        id0 = base + i1
        id1 = base + i2
        gates = jnp.where(ridx == 0, w1, jnp.where(ridx == 1, w2, 0.0))
        gcol_ref[rows, :] = jnp.transpose(
            jnp.concatenate([gates, jnp.zeros((LANES - EXPERTS_PER_GROUP, pr), F32)], axis=0))
        oh0 = jnp.where(eidx == id0, 1.0, 0.0)
        oh1 = jnp.where(eidx == id1, 1.0, 0.0)
        p0 = jnp.dot(oh0.astype(BF16), earlier, preferred_element_type=F32)
        p1 = jnp.dot(oh1.astype(BF16), earlier, preferred_element_type=F32)
        c0 = jnp.sum(oh0, axis=1, keepdims=True)
        c1 = jnp.sum(oh1, axis=1, keepdims=True)
        routed.append((id0, id1, oh0, oh1, p0, p1, c0, c1))

    @pl.when(pl.program_id(0) == 0)
    def _():
        run_sc[...] = cnt0_ref[:, :1]

    run = run_sc[...]
    lane = lax.broadcasted_iota(jnp.int32, (N_EXPERTS, LANES), 1)
    for part, (id0, id1, oh0, oh1, p0, p1, c0, c1) in enumerate(routed):
        rank0 = jnp.sum(oh0 * (run + p0), axis=0, keepdims=True)
        rank1 = jnp.sum(oh1 * (run + c0 + p1), axis=0, keepdims=True)
        length = c0 + c1
        windows = jnp.ceil(length * (1.0 / RUN_WINDOW)) * RUN_WINDOW
        windows_b = jnp.broadcast_to(windows, (N_EXPERTS, LANES))
        stage_off = (_cumsum_rows(windows_b) - windows_b)[:, :1]
        pos0 = jnp.sum(oh0 * (stage_off + p0), axis=0, keepdims=True)
        pos1 = jnp.sum(oh1 * (stage_off + c0 + p1), axis=0, keepdims=True)
        win_row = (lane * RUN_WINDOW).astype(F32)
        expert_of = jnp.sum(jnp.where(stage_off + windows <= win_row, 1.0, 0.0), axis=0, keepdims=True)
        n_win = jnp.sum(windows, axis=0, keepdims=True) * (1.0 / RUN_WINDOW)
        expert_of = jnp.minimum(expert_of, N_EXPERTS - 1.0)
        eidx_w = lax.broadcasted_iota(jnp.int32, (N_EXPERTS, LANES), 0).astype(F32)
        rel_row = jnp.sum(jnp.where(eidx_w == expert_of, run - stage_off, 0.0), axis=0, keepdims=True) + win_row[:1]
        r8 = lax.broadcasted_iota(jnp.int32, (EXPERTS_PER_GROUP, LANES), 0)
        meta_ref[...] = jnp.where(r8 == 0, expert_of.astype(jnp.int32), jnp.where(
            r8 == 1, rel_row.astype(jnp.int32), jnp.where(r8 == 2, n_win.astype(jnp.int32), 0)))
        run = run + length
        vals = (id0, id1, rank0.astype(jnp.int32), rank1.astype(jnp.int32),
                pos0.astype(jnp.int32), pos1.astype(jnp.int32))
        out = jnp.zeros((EXPERTS_PER_GROUP, pr), jnp.int32)
        for r, v in enumerate(vals):
            out = jnp.where(ridx == r, v, out)
        route_ref[:, part * pr:(part + 1) * pr] = out
    run_sc[...] = run
    cnt_ref[...] = jnp.broadcast_to(run, cnt_ref.shape)


def _merge_tile(n):
    return min(MERGE_TILE, max(ROW_TILE, n // 4))


def _merge(oa, ob, sga, sgb, x2, w, cnt0):
    n = x2.shape[0]
    tm = _merge_tile(n)
    assert n % tm == 0
    row = lambda width: pl.BlockSpec((tm, width), lambda i: (i, 0))
    weights = [w["w_abo"], w["g_ffn"], w["w_router"], w["b_router"], cnt0]
    return pl.pallas_call(
        _merge_kernel,
        grid=(n // tm,),
        in_specs=[row(MLA_HEADS * V_DIM), row(GLA_HEADS * GLA_DV), row(D_MODEL), row(D_MODEL),
                  row(D_MODEL)] + [_full(a.shape) for a in weights],
        out_specs=[row(D_MODEL), pl.BlockSpec((tm * SLABS, LANES), lambda i: (i, 0)),
                   pl.BlockSpec((EXPERTS_PER_GROUP, tm), lambda i: (0, i)), row(LANES),
                   _full((N_EXPERTS, LANES)), pl.BlockSpec((EXPERTS_PER_GROUP, LANES), lambda i: (i, 0))],
        out_shape=[jax.ShapeDtypeStruct((n, D_MODEL), F32), jax.ShapeDtypeStruct((n * SLABS, LANES), F32),
                   jax.ShapeDtypeStruct((EXPERTS_PER_GROUP, n), jnp.int32),
                   jax.ShapeDtypeStruct((n, LANES), F32),
                   jax.ShapeDtypeStruct((N_EXPERTS, LANES), F32),
                   jax.ShapeDtypeStruct((n // tm * EXPERTS_PER_GROUP, LANES), jnp.int32)],
        scratch_shapes=[pltpu.VMEM((N_EXPERTS, 1), F32)],
        compiler_params=_params(("arbitrary",)),
        name="merge",
    )(oa, ob, sga, sgb, x2, *weights)


ROW_DMA_UNROLL = 8


def _row_tile(ref, r):
    return ref.at[pl.ds(pl.multiple_of(r * SLABS, SLABS), SLABS), :]


def _scatter_kernel(ps_ref, cnt_ref, pe_ref, *rest, tiles):
    ng = len(tiles)
    meta_refs, prev_refs, pos_refs, h2_refs = (rest[k * ng:(k + 1) * ng] for k in range(4))
    xd_hbm, zero_sc, stage, sem, zsem = rest[4 * ng:]

    def zero_padding():
        zero_sc[...] = jnp.zeros(zero_sc.shape, F32)

        def pad_copies(e, act):
            lo = ps_ref[e] + cnt_ref[e]
            pad = pe_ref[e] - lo
            bit = FFN_ROWS
            while bit >= 1:
                @pl.when((pad & bit) != 0)
                def _(bit=bit):
                    first = lo + (pad & ~(2 * bit - 1))
                    act(pltpu.make_async_copy(
                        zero_sc.at[pl.ds(0, bit * SLABS), :],
                        xd_hbm.at[pl.ds(pl.multiple_of(first * SLABS, SLABS), bit * SLABS), :], zsem.at[0]))
                bit //= 2

        def start_pads(e, c):
            pad_copies(e, lambda cp: cp.start())
            return c

        def wait_pads(e, c):
            pad_copies(e, lambda cp: cp.wait())
            return c

        lax.fori_loop(0, N_EXPERTS, start_pads, 0)
        lax.fori_loop(0, N_EXPERTS, wait_pads, 0)

        def block_copy(b):
            start_row = pl.multiple_of(b * (FFN_ROWS * SLABS), FFN_ROWS * SLABS)
            return pltpu.make_async_copy(zero_sc, xd_hbm.at[pl.ds(start_row, FFN_ROWS * SLABS), :], zsem.at[0])

        first_unused = pe_ref[N_EXPERTS - 1] // FFN_ROWS
        n_blocks = xd_hbm.shape[0] // (FFN_ROWS * SLABS)

        def start_block(b, c):
            block_copy(b).start()
            return c

        def wait_block(b, c):
            block_copy(b).wait()
            return c

        lax.fori_loop(first_unused, n_blocks, start_block, 0)
        lax.fori_loop(first_unused, n_blocks, wait_block, 0)

    step = pl.program_id(0)
    pl.when(step == 0)(zero_padding)

    def windows(meta, s, act):
        def per_window(j, c):
            row = ps_ref[meta[0, j]] + meta[1, j]
            act(pltpu.make_async_copy(
                stage.at[s, pl.ds(pl.multiple_of(j * (RUN_WINDOW * SLABS), RUN_WINDOW * SLABS), RUN_WINDOW * SLABS), :],
                xd_hbm.at[pl.ds(pl.multiple_of(row * SLABS, SLABS), RUN_WINDOW * SLABS), :],
                sem.at[s]))
            return c

        lax.fori_loop(0, meta[2, 0], per_window, 0)

    first_step = 0
    for g, nt in enumerate(tiles):
        @pl.when((step >= first_step) & (step < first_step + nt))
        def _(g=g, nt=nt, first_step=first_step):
            meta_ref, prev_ref, pos_ref, h2_ref = meta_refs[g], prev_refs[g], pos_refs[g], h2_refs[g]
            tm = h2_ref.shape[0] // SLABS
            local = step - first_step
            slot = local % 2
            stage_rows = 2 * tm + N_EXPERTS * RUN_WINDOW

            @pl.when(local < 2)
            def _():
                stage[slot, pl.ds(0, stage_rows * SLABS), :] = jnp.zeros((stage_rows * SLABS, LANES), F32)

            def body(t, carry):
                for k in range(2):
                    p = pl.multiple_of(pos_ref[0, 0, k * tm + t], SLABS)
                    stage[slot, pl.ds(p, SLABS), :] = _row_tile(h2_ref, t)[...]
                return carry

            lax.fori_loop(0, tm, body, 0, unroll=ROW_DMA_UNROLL)

            @pl.when(local > 0)
            def _():
                windows(prev_ref, 1 - slot, lambda cp: cp.wait())

            windows(meta_ref, slot, lambda cp: cp.start())

            @pl.when(local == nt - 1)
            def _():
                windows(meta_ref, slot, lambda cp: cp.wait())
        first_step += nt


def _scatter(pad_start, counts, pad_end, metas, pos_blocks, h2s, total_rows):
    tms = [p.shape[2] // 2 for p in pos_blocks]
    tiles = [p.shape[0] for p in pos_blocks]
    firsts = [sum(tiles[:g]) for g in range(len(tiles))]
    ng = len(tiles)
    local = lambda g, shift=0: (lambda i, *_: jnp.clip(i - firsts[g] - shift, 0, tiles[g] - 1))
    smem = pltpu.SMEM
    in_specs = ([pl.BlockSpec((EXPERTS_PER_GROUP, LANES), (lambda i, *_, f=local(g): (f(i), 0)), memory_space=smem)
                 for g in range(ng)]
                + [pl.BlockSpec((EXPERTS_PER_GROUP, LANES), (lambda i, *_, f=local(g, 1): (f(i), 0)), memory_space=smem)
                   for g in range(ng)]
                + [pl.BlockSpec((1, 1, 2 * tms[g]), (lambda i, *_, f=local(g): (f(i), 0, 0)), memory_space=smem)
                   for g in range(ng)]
                + [pl.BlockSpec((tms[g] * SLABS, LANES), (lambda i, *_, f=local(g): (f(i), 0)))
                   for g in range(ng)])
    stage_rows = 2 * max(tms) + N_EXPERTS * RUN_WINDOW
    grid_spec = pltpu.PrefetchScalarGridSpec(
        num_scalar_prefetch=3,
        grid=(sum(tiles),),
        in_specs=in_specs,
        out_specs=pl.BlockSpec(memory_space=pl.ANY),
        scratch_shapes=[pltpu.VMEM((FFN_ROWS * SLABS, LANES), F32),
                        pltpu.VMEM((2, stage_rows * SLABS, LANES), F32),
                        pltpu.SemaphoreType.DMA((2,)), pltpu.SemaphoreType.DMA((1,))],
    )
    return pl.pallas_call(
        functools.partial(_scatter_kernel, tiles=tuple(tiles)),
        grid_spec=grid_spec,
        out_shape=jax.ShapeDtypeStruct((total_rows * SLABS, LANES), F32),
        compiler_params=_params(("arbitrary",)),
        name="moe_scatter",
    )(pad_start, counts, pad_end, *metas, *metas, *pos_blocks, *h2s)


def _ffn_kernel(be_ref, nbu_ref, x_ref, wg_ref, wu_ref, wd_ref, out_ref, wg_sc, wu_sc, wd_sc):
    i = pl.program_id(0)
    used = i < nbu_ref[0]
    new_expert = (i == 0) | (be_ref[i] != be_ref[jnp.maximum(i - 1, 0)])

    @pl.when(used & new_expert)
    def _():
        wg_sc[...] = wg_ref[0].astype(BF16)
        wu_sc[...] = wu_ref[0].astype(BF16)
        wd_sc[...] = wd_ref[0].astype(BF16)

    @pl.when(used)
    def _():
        x = _load_token_major(x_ref, FFN_ROWS).astype(BF16)
        g = jnp.dot(x, wg_sc[...], preferred_element_type=F32)
        u = jnp.dot(x, wu_sc[...], preferred_element_type=F32)
        mid = (g * _sigmoid(g) * u).astype(BF16)
        _store_token_major(out_ref, jnp.dot(mid, wd_sc[...], preferred_element_type=F32))

    @pl.when(jnp.logical_not(used))
    def _():
        out_ref[...] = jnp.zeros(out_ref.shape, F32)


def _ffn(block_expert, nb_used, xd, wg, wu, wd):
    nb = block_expert.shape[0]
    blk = lambda i, be, nbu: (jnp.minimum(i, nbu[0] - 1), 0)
    wsel = lambda i, be, nbu: (be[jnp.minimum(i, nbu[0] - 1)], 0, 0)
    grid_spec = pltpu.PrefetchScalarGridSpec(
        num_scalar_prefetch=2,
        grid=(nb,),
        in_specs=[pl.BlockSpec((FFN_ROWS * SLABS, LANES), blk),
                  pl.BlockSpec((1, D_MODEL, EXPERT_FF), wsel),
                  pl.BlockSpec((1, D_MODEL, EXPERT_FF), wsel),
                  pl.BlockSpec((1, EXPERT_FF, D_MODEL), wsel)],
        out_specs=pl.BlockSpec((FFN_ROWS * SLABS, LANES), lambda i, be, nbu: (i, 0)),
        scratch_shapes=[pltpu.VMEM((D_MODEL, EXPERT_FF), BF16), pltpu.VMEM((D_MODEL, EXPERT_FF), BF16),
                        pltpu.VMEM((EXPERT_FF, D_MODEL), BF16)],
    )
    return pl.pallas_call(
        _ffn_kernel,
        grid_spec=grid_spec,
        out_shape=jax.ShapeDtypeStruct((nb * FFN_ROWS * SLABS, LANES), F32),
        compiler_params=_params(("arbitrary",)),
        name="expert_ffn",
    )(block_expert, nb_used, xd, wg, wu, wd)


def _combine_kernel(ps_ref, meta_ref, metan_ref, pos_ref, x1_ref, gcol_ref, gfin_ref, eo_hbm, y_ref,
                    stage, ybuf, sem):
    i = pl.program_id(0)
    n = pl.num_programs(0)
    tm = x1_ref.shape[0]
    slot = i % 2

    def windows(meta, s, act):
        def per_window(j, c):
            row = ps_ref[meta[0, j]] + meta[1, j]
            act(pltpu.make_async_copy(
                eo_hbm.at[pl.ds(pl.multiple_of(row * SLABS, SLABS), RUN_WINDOW * SLABS), :],
                stage.at[s, pl.ds(pl.multiple_of(j * (RUN_WINDOW * SLABS), RUN_WINDOW * SLABS), RUN_WINDOW * SLABS), :],
                sem.at[s]))
            return c

        lax.fori_loop(0, meta[2, 0], per_window, 0)

    @pl.when(i == 0)
    def _():
        windows(meta_ref, 0, lambda cp: cp.start())

    @pl.when(i + 1 < n)
    def _():
        windows(metan_ref, 1 - slot, lambda cp: cp.start())

    windows(meta_ref, slot, lambda cp: cp.wait())

    def body(t, carry):
        for k in range(2):
            p = pl.multiple_of(pos_ref[0, 0, k * tm + t], SLABS)
            _row_tile(ybuf.at[k], t)[...] = stage[slot, pl.ds(p, SLABS), :]
        return carry

    lax.fori_loop(0, tm, body, 0, unroll=ROW_DMA_UNROLL)
    g = gcol_ref[...]
    y = x1_ref[...] + (g[:, 0:1] * _load_token_major(ybuf.at[0], tm) + g[:, 1:2] * _load_token_major(ybuf.at[1], tm))
    y_ref[...] = _rms(y, gfin_ref[...])


def _combine(pad_start, meta, pos_blocks, x1, gcol, g_final, eo):
    n = x1.shape[0]
    tm = _merge_tile(n)
    nt = n // tm
    stage_rows = 2 * tm + N_EXPERTS * RUN_WINDOW
    grid_spec = pltpu.PrefetchScalarGridSpec(
        num_scalar_prefetch=1,
        grid=(nt,),
        in_specs=[pl.BlockSpec((EXPERTS_PER_GROUP, LANES), lambda i, ps: (i, 0), memory_space=pltpu.SMEM),
                  pl.BlockSpec((EXPERTS_PER_GROUP, LANES), lambda i, ps: (jnp.minimum(i + 1, nt - 1), 0),
                               memory_space=pltpu.SMEM),
                  pl.BlockSpec((1, 1, 2 * tm), lambda i, ps: (i, 0, 0), memory_space=pltpu.SMEM),
                  pl.BlockSpec((tm, D_MODEL), lambda i, ps: (i, 0)),
                  pl.BlockSpec((tm, LANES), lambda i, ps: (i, 0)),
                  pl.BlockSpec(g_final.shape, lambda i, ps: (0, 0)),
                  pl.BlockSpec(memory_space=pl.ANY)],
        out_specs=pl.BlockSpec((tm, D_MODEL), lambda i, ps: (i, 0)),
        scratch_shapes=[pltpu.VMEM((2, stage_rows * SLABS, LANES), F32),
                        pltpu.VMEM((2, tm * SLABS, LANES), F32), pltpu.SemaphoreType.DMA((2,))],
    )
    return pl.pallas_call(
        _combine_kernel,
        grid_spec=grid_spec,
        out_shape=jax.ShapeDtypeStruct((n, D_MODEL), F32),
        compiler_params=_params(("arbitrary",)),
        name="combine",
    )(pad_start, meta, meta, pos_blocks, x1, gcol, g_final, eo)


def _layout(cnt, n_tokens):
    counts = cnt[:, 0].astype(jnp.int32)
    padded = (counts + RUN_WINDOW + FFN_ROWS - 1) // FFN_ROWS * FFN_ROWS
    pad_end = jnp.cumsum(padded).astype(jnp.int32)
    pad_start = pad_end - padded
    nb = -(-(2 * n_tokens + N_EXPERTS * RUN_WINDOW) // FFN_ROWS) + N_EXPERTS + 1
    block_start = jnp.arange(nb, dtype=jnp.int32) * FFN_ROWS
    block_expert = jnp.minimum(jnp.sum((block_start[:, None] >= pad_end[None, :]).astype(jnp.int32), axis=1),
                               N_EXPERTS - 1).astype(jnp.int32)
    nb_used = (pad_end[N_EXPERTS - 1:] // FFN_ROWS).astype(jnp.int32)
    return pad_start, counts, pad_end, block_expert, nb_used, nb * FFN_ROWS


def _rope_tables(l, past, rows):
    half = ROPE_DIM // 2
    inv = ROPE_THETA ** (-jnp.arange(half, dtype=F32) / half)
    pos = (past + (jnp.arange(rows, dtype=jnp.int32) % l)).astype(F32)
    ang = pos[:, None] * inv[None, :]
    pad = jnp.zeros((rows, LANES - ROPE_DIM), F32)
    cos = jnp.concatenate([jnp.cos(ang), jnp.cos(ang), pad], axis=1)
    sin = jnp.concatenate([-jnp.sin(ang), jnp.sin(ang), pad], axis=1)
    return cos, sin


def _mixers(x, ckv_past, kpe_past, s0, w):
    b, l, d = x.shape
    n = b * l
    x2 = x.reshape(n, d)
    past = 0 if ckv_past is None else ckv_past.shape[1]
    cos_t, sin_t = _rope_tables(l, past, max(l, ROW_TILE))
    prompt = ckv_past is None
    (qcat, kcat, ckv, kpe, gq, gk, gv, la, sgr, sga, sgb, *maybe_vt) = _inproj(x2, cos_t, sin_t, w, prompt)
    if prompt:
        oa = _attn_prompt(qcat, kcat, maybe_vt[0], w["w_uv"], b, l)
    else:
        oa = _attn_sample(qcat, kcat, ckv_past, kpe_past, w["w_uv"], b, l)
    ob, s_new = _gla(gq, gk, gv, la, sgr, w["g_gla"], s0, b, l)
    merge_in = (oa.reshape(n, -1), ob.reshape(n, -1), sga, sgb, x2)
    return merge_in, (ckv.reshape(b, l, KV_LORA), kpe.reshape(b, l, ROPE_DIM), s_new)


def _layer(groups, w, g_final):
    fronts = [_mixers(*g, w) for g in groups]
    cnt = jnp.zeros((N_EXPERTS, LANES), F32)
    merged = []
    for merge_in, _ in fronts:
        x1, h2, route, gcol, cnt, meta = _merge(*merge_in, w, cnt)
        merged.append((x1, h2, route, gcol, meta))
    n_tokens = sum(m[0].shape[0] for m in merged)
    pad_start, counts, pad_end, block_expert, nb_used, total_rows = _layout(cnt, n_tokens)
    pos_blocks = []
    for x1, h2, route, gcol, meta in merged:
        n = x1.shape[0]
        tm = _merge_tile(n)
        pos_blocks.append((route[4:6] * SLABS).reshape(2, n // tm, tm).transpose(1, 0, 2)
                          .reshape(n // tm, 1, 2 * tm))
    xd = _scatter(pad_start, counts, pad_end, [m[4] for m in merged], pos_blocks, [m[1] for m in merged],
                  total_rows)
    eo = _ffn(block_expert, nb_used, xd, w["w_eg"], w["w_eu"], w["w_ed"])
    outs = []
    for (x1, h2, route, gcol, meta), pos, (_, extras), g in zip(merged, pos_blocks, fronts, groups):
        y = _combine(pad_start, meta, pos, x1, gcol, g_final, eo)
        outs.append((y.reshape(g[0].shape),) + extras)
    return outs


def _prep_weights(w_in, g_norm_mix, g_qnorm, w_uq, g_kvnorm, w_ukv, w_gate2, b_gate2, g_gla_norm,
                  w_branch_a, w_branch_b, w_out, g_norm_ffn, w_router_group, b_router_group,
                  w_router_expert, b_router_expert, w_exp_gate, w_exp_up, w_exp_down):
    nqk = GLA_HEADS * GLA_DK
    nv = GLA_HEADS * GLA_DV
    o = 0
    parts = {}
    for name, width in (("cq", Q_LORA), ("ckv", KV_LORA), ("kpe", ROPE_DIM), ("gq", nqk), ("gk", nqk),
                        ("gv", nv), ("glr", GATE_RANK), ("gr", nv), ("ga", D_MODEL), ("gb", D_MODEL)):
        parts[name] = w_in[:, o:o + width]
        o += width
    padc = lambda a, width: jnp.pad(a, ((0, 0), (0, width - a.shape[1])))
    w_in_all = jnp.concatenate([parts["cq"], parts["ckv"], padc(parts["kpe"], LANES), padc(parts["glr"], LANES),
                                parts["gq"], parts["gk"], parts["gv"], parts["gr"], parts["ga"], parts["gb"]],
                               axis=1).astype(BF16)
    assert w_in_all.shape[1] == _IN_COLS["gb"].stop
    w_abo = jnp.concatenate([w_branch_a, w_branch_b, w_out], axis=0).astype(BF16)
    uq = w_uq.reshape(Q_LORA, MLA_HEADS, NOPE_DIM + ROPE_DIM)
    ukv = w_ukv.reshape(KV_LORA, MLA_HEADS, NOPE_DIM + V_DIM)
    lat = _fold_q(uq[:, :, :NOPE_DIM].transpose(1, 0, 2), ukv[:, :, :NOPE_DIM].transpose(1, 0, 2))
    q_rope = uq[:, :, NOPE_DIM:].transpose(1, 0, 2)
    w_qcat = jnp.concatenate([lat, q_rope, jnp.zeros((MLA_HEADS, Q_LORA, QCAT - KV_LORA - ROPE_DIM), F32)],
                             axis=2)
    w_qcat = w_qcat.transpose(1, 0, 2).reshape(Q_LORA, MLA_HEADS * QCAT).astype(BF16)
    w_router = jnp.concatenate([w_router_expert.T, w_router_group.T,
                                jnp.zeros((ROUTER_ROWS - N_EXPERTS - N_GROUPS, D_MODEL), F32)], axis=0)
    b_router = jnp.concatenate([b_router_expert, b_router_group,
                                jnp.zeros((ROUTER_ROWS - N_EXPERTS - N_GROUPS,), F32)]).reshape(ROUTER_ROWS, 1)
    return {
        "g_mix": g_norm_mix.reshape(1, D_MODEL), "w_in": w_in_all,
        "g_qn": g_qnorm.reshape(1, Q_LORA), "g_kvn": g_kvnorm.reshape(1, KV_LORA), "w_qcat": w_qcat,
        "w_g2": jnp.pad(w_gate2, ((0, LANES - GATE_RANK), (0, 0))).astype(BF16),
        "b_g2": b_gate2.reshape(1, nqk),
        "w_uv": ukv[:, :, NOPE_DIM:].transpose(1, 0, 2).astype(BF16),
        "g_gla": g_gla_norm.reshape(1, GLA_DV),
        "w_abo": w_abo,
        "g_ffn": g_norm_ffn.reshape(1, D_MODEL), "w_router": w_router, "b_router": b_router,
        "w_eg": w_exp_gate, "w_eu": w_exp_up, "w_ed": w_exp_down,
    }


def kernel(x_prompt, x_sample, cache_ckv, cache_krope, state_gla, w_in, g_norm_mix, g_qnorm, w_uq, g_kvnorm, w_ukv, w_gate2, b_gate2, g_gla_norm, w_branch_a, w_branch_b, w_out, g_norm_ffn, w_router_group, b_router_group, w_router_expert, b_router_expert, w_exp_gate, w_exp_up, w_exp_down, g_norm_final):
    depth = w_in.shape[0]
    assert depth == 1, "the final norm is fused into the last layer's combine step"
    gfin = g_norm_final.reshape(1, D_MODEL)
    drop = lambda a: a.reshape(a.shape[1:])
    lift = lambda a: a.reshape((1,) + a.shape)
    w = _prep_weights(*[drop(a) for a in (
        w_in, g_norm_mix, g_qnorm, w_uq, g_kvnorm, w_ukv, w_gate2, b_gate2, g_gla_norm, w_branch_a, w_branch_b,
        w_out, g_norm_ffn, w_router_group, b_router_group, w_router_expert, b_router_expert, w_exp_gate,
        w_exp_up, w_exp_down)])
    bp = x_prompt.shape[0]
    zero_state = jnp.zeros((bp, GLA_HEADS, GLA_DK, GLA_DV), F32)
    (yp, c1, k1, s1), (ys, c2, k2, s2) = _layer(
        [(x_prompt, None, None, zero_state),
         (x_sample, drop(cache_ckv), drop(cache_krope), drop(state_gla).astype(F32))], w, gfin)
    return (yp, ys, lift(c1), lift(k1), lift(s1), lift(c2), lift(k2), lift(s2))
```

```python
import functools
import math

import jax
import jax.numpy as jnp
from jax import lax
from jax.experimental import pallas as pl
from jax.experimental.pallas import tpu as pltpu

F32 = jnp.float32
BF16 = jnp.bfloat16

D_MODEL = 1024
CHUNK = 64
EPS = 1e-6
MLA_HEADS = 8
Q_LORA = 256
KV_LORA = 128
NOPE_DIM = 64
ROPE_DIM = 32
V_DIM = 64
ROPE_THETA = 10000.0
GLA_HEADS = 4
GLA_DK = 128
GLA_DV = 256
GATE_RANK = 16
GATE_TEMP = 16.0
N_GROUPS = 4
EXPERTS_PER_GROUP = 8
N_EXPERTS = 32
EXPERT_FF = 512

LANES = 128
QCAT = 2 * LANES
ROW_TILE = 256
INPROJ_TILE = 512
MERGE_TILE = 512
MERGE_PARTS = 1
RUN_WINDOW = 32
GLA_BATCH = 2
GLA_STEP_TOKENS = 256
Q_BLOCK = ROW_TILE
KEY_TILE = ROW_TILE
ATT_SAMPLE_SEQS = 4
SUM_ROWS = 16
GLA_CHUNK = 64
FFN_ROWS = 512
ROUTER_ROWS = 48
VMEM_LIMIT = 56 * 1024 * 1024
LOG2E = 1.4426950408889634
ATT_SCALE = LOG2E / math.sqrt(NOPE_DIM + ROPE_DIM)

_NT = (((1,), (1,)), ((), ()))
_TN = (((0,), (0,)), ((), ()))


def _params(sem):
    return pltpu.CompilerParams(dimension_semantics=sem, vmem_limit_bytes=VMEM_LIMIT)


def _rms(x, g):
    return x * lax.rsqrt(jnp.mean(x * x, axis=-1, keepdims=True) + EPS) * g


def _sigmoid(x):
    return 1.0 / (1.0 + jnp.exp(-x))


def _full(shape):
    n = len(shape)
    return pl.BlockSpec(shape, lambda *_: (0,) * n)


SLABS = D_MODEL // LANES


def _store_token_major(ref, x, first_row=0):
    rows = x.shape[0]
    for j in range(SLABS):
        ref[pl.ds(first_row * SLABS + j, rows, stride=SLABS), :] = x[:, j * LANES:(j + 1) * LANES]


def _load_token_major(ref, rows):
    return jnp.concatenate([ref[pl.ds(j, rows, stride=SLABS), :] for j in range(SLABS)], axis=1)


def _fold_q_kernel(wq_ref, wk_ref, o_ref):
    o_ref[0] = lax.dot_general(wq_ref[0], wk_ref[0], _NT, precision=lax.Precision.HIGHEST,
                               preferred_element_type=F32)


def _fold_q(wq_nope, wk_nope):
    return pl.pallas_call(
        _fold_q_kernel,
        grid=(MLA_HEADS,),
        in_specs=[pl.BlockSpec((1, Q_LORA, NOPE_DIM), lambda h: (h, 0, 0)),
                  pl.BlockSpec((1, KV_LORA, NOPE_DIM), lambda h: (h, 0, 0))],
        out_specs=pl.BlockSpec((1, Q_LORA, KV_LORA), lambda h: (h, 0, 0)),
        out_shape=jax.ShapeDtypeStruct((MLA_HEADS, Q_LORA, KV_LORA), F32),
        compiler_params=_params(("arbitrary",)),
        name="fold_q",
    )(wq_nope, wk_nope)


_W_SMALL = Q_LORA + KV_LORA + 2 * LANES
_IN_COLS = {}
_off = 0
for _name, _width in (("small", _W_SMALL), ("gqk", 2 * GLA_HEADS * GLA_DK), ("gv", GLA_HEADS * GLA_DV),
                      ("gr", GLA_HEADS * GLA_DV), ("ga", D_MODEL), ("gb", D_MODEL)):
    _IN_COLS[_name] = slice(_off, _off + _width)
    _off += _width


def _inproj_kernel(x_ref, cos_ref, sin_ref, gmix_ref, win_ref, gqn_ref, gkvn_ref, wqc_ref,
                   wg2_ref, bg2_ref,
                   qcat_ref, kcat_ref, ckv_ref, kpe_ref, gq_ref, gk_ref, gv_ref, la_ref,
                   sgr_ref, sga_ref, sgb_ref, *maybe_vt_ref, transposed):
    hb = _rms(x_ref[...], gmix_ref[...]).astype(BF16)
    proj = lambda name: jnp.dot(hb, win_ref[:, _IN_COLS[name]], preferred_element_type=F32)
    zs = proj("small")
    cos = cos_ref[...]
    sin = sin_ref[...]
    first_half = lax.broadcasted_iota(jnp.int32, cos.shape, 1) < ROPE_DIM // 2

    def rope(v):
        rot = jnp.where(first_half, pltpu.roll(v, LANES - ROPE_DIM // 2, 1),
                        pltpu.roll(v, ROPE_DIM // 2, 1))
        return v * cos + rot * sin

    cqn = _rms(zs[:, :Q_LORA], gqn_ref[...]).astype(BF16)
    qc = jnp.dot(cqn, wqc_ref[...], preferred_element_type=F32)
    for h in range(MLA_HEADS):
        lat = qc[:, h * QCAT:h * QCAT + LANES]
        pe = qc[:, h * QCAT + LANES:(h + 1) * QCAT]
        if transposed:
            lat_t = jnp.transpose(lat * ATT_SCALE).astype(BF16)
            pe_t = jnp.transpose(rope(pe) * ATT_SCALE).astype(BF16)
            for blk in range(lat.shape[0] // Q_BLOCK):
                cols = slice(blk * Q_BLOCK, (blk + 1) * Q_BLOCK)
                qcat_ref[blk, :LANES, h * Q_BLOCK:(h + 1) * Q_BLOCK] = lat_t[:, cols]
                qcat_ref[blk, LANES:, h * Q_BLOCK:(h + 1) * Q_BLOCK] = pe_t[:, cols]
        else:
            qcat_ref[h, :, :LANES] = (lat * ATT_SCALE).astype(BF16)
            qcat_ref[h, :, LANES:] = (rope(pe) * ATT_SCALE).astype(BF16)

    ckv = _rms(zs[:, Q_LORA:Q_LORA + KV_LORA], gkvn_ref[...])
    ckv_ref[...] = ckv
    if transposed:
        for blk in range(ckv.shape[0] // KEY_TILE):
            maybe_vt_ref[0][blk] = jnp.transpose(ckv[blk * KEY_TILE:(blk + 1) * KEY_TILE]).astype(BF16)
    kpe = rope(zs[:, Q_LORA + KV_LORA:Q_LORA + KV_LORA + LANES])
    kpe_ref[...] = kpe[:, :ROPE_DIM]
    kcat_ref[:, :LANES] = ckv.astype(BF16)
    kcat_ref[:, LANES:] = kpe.astype(BF16)

    glr = zs[:, Q_LORA + KV_LORA + LANES:].astype(BF16)
    xg = jnp.dot(glr, wg2_ref[...], preferred_element_type=F32) + bg2_ref[...]
    la_ref[...] = (jnp.minimum(xg, 0.0) - jnp.log(1.0 + jnp.exp(-jnp.abs(xg)))) * (1.0 / GATE_TEMP)

    zqk = proj("gqk")
    nqk = GLA_HEADS * GLA_DK
    gq_ref[...] = (zqk[:, :nqk] * (GLA_DK ** -0.5)).astype(BF16)
    gk_ref[...] = zqk[:, nqk:].astype(BF16)
    gv_ref[...] = proj("gv").astype(BF16)
    gr = proj("gr")
    sgr_ref[...] = (gr * _sigmoid(gr)).astype(BF16)
    sga_ref[...] = _sigmoid(proj("ga")).astype(BF16)
    sgb_ref[...] = _sigmoid(proj("gb")).astype(BF16)


def _inproj(x2, cos_t, sin_t, w, transposed):
    n = x2.shape[0]
    tm = min(INPROJ_TILE, n)
    assert n % tm == 0 and cos_t.shape[0] % tm == 0
    nt = n // tm
    tab_blocks = cos_t.shape[0] // tm
    row = lambda width: pl.BlockSpec((tm, width), lambda i: (i, 0))
    tab = pl.BlockSpec((tm, LANES), lambda i: (i % tab_blocks, 0))
    nqk = GLA_HEADS * GLA_DK
    nv = GLA_HEADS * GLA_DV
    weights = [w["g_mix"], w["w_in"], w["g_qn"], w["g_kvn"], w["w_qcat"], w["w_g2"], w["b_g2"]]
    out_shape = [
        jax.ShapeDtypeStruct((MLA_HEADS, n, QCAT), BF16),
        jax.ShapeDtypeStruct((n, QCAT), BF16),
        jax.ShapeDtypeStruct((n, KV_LORA), F32),
        jax.ShapeDtypeStruct((n, ROPE_DIM), F32),
        jax.ShapeDtypeStruct((n, nqk), BF16),
        jax.ShapeDtypeStruct((n, nqk), BF16),
        jax.ShapeDtypeStruct((n, nv), BF16),
        jax.ShapeDtypeStruct((n, nqk), F32),
        jax.ShapeDtypeStruct((n, nv), BF16),
        jax.ShapeDtypeStruct((n, D_MODEL), BF16),
        jax.ShapeDtypeStruct((n, D_MODEL), BF16),
    ]
    out_specs = [
        pl.BlockSpec((MLA_HEADS, tm, QCAT), lambda i: (0, i, 0)),
        row(QCAT), row(KV_LORA), row(ROPE_DIM), row(nqk), row(nqk), row(nv), row(nqk), row(nv),
        row(D_MODEL), row(D_MODEL),
    ]
    if transposed:
        qb = tm // Q_BLOCK
        out_shape[0] = jax.ShapeDtypeStruct((n // Q_BLOCK, QCAT, MLA_HEADS * Q_BLOCK), BF16)
        out_specs[0] = pl.BlockSpec((qb, QCAT, MLA_HEADS * Q_BLOCK), lambda i: (i, 0, 0))
        kb = tm // KEY_TILE
        out_shape.append(jax.ShapeDtypeStruct((n // KEY_TILE, KV_LORA, KEY_TILE), BF16))
        out_specs.append(pl.BlockSpec((kb, KV_LORA, KEY_TILE), lambda i: (i, 0, 0)))
    resident = lambda a: pl.BlockSpec(a.shape, lambda i: (0,) * a.ndim, pipeline_mode=pl.Buffered(1))
    return pl.pallas_call(
        functools.partial(_inproj_kernel, transposed=transposed),
        grid=(nt,),
        in_specs=[row(D_MODEL), tab, tab] + [resident(a) for a in weights],
        out_specs=out_specs,
        out_shape=out_shape,
        compiler_params=_params(("parallel",)),
        name="inproj",
    )(x2, cos_t, sin_t, *weights)


def _head_out(o_lat, wuv_ref, rows):
    outs = []
    for h in range(MLA_HEADS):
        oh = o_lat[h * rows:(h + 1) * rows].astype(BF16)
        outs.append(jnp.dot(oh, wuv_ref[h], preferred_element_type=F32))
    return jnp.concatenate(outs, axis=1)


def _attn_prompt_kernel(qt_ref, k_ref, vt_ref, bias_ref, wuv_ref, o_ref, m_sc, l_sc, acc_sc):
    i = pl.program_id(1)
    m_sc[...] = jnp.full(m_sc.shape, -jnp.inf, F32)
    l_sc[...] = jnp.zeros(l_sc.shape, F32)
    acc_sc[...] = jnp.zeros(acc_sc.shape, F32)
    heads = [slice(h * Q_BLOCK, (h + 1) * Q_BLOCK) for h in range(MLA_HEADS)]
    ones_rows = jnp.ones((SUM_ROWS, KEY_TILE), BF16)

    def process(tiles, diagonal_last):
        nt = len(tiles)
        kbs = [k_ref[0, pl.ds(pl.multiple_of(kt * KEY_TILE, KEY_TILE), KEY_TILE), :] for kt in tiles]
        vts = [jnp.concatenate([vt_ref[kt], ones_rows], axis=0) for kt in tiles]
        ss = [[None] * MLA_HEADS for _ in range(nt)]
        for t in range(nt):
            for h, hs in enumerate(heads):
                s = jnp.dot(kbs[t], qt_ref[0, :, hs], preferred_element_type=F32)
                if diagonal_last and t == nt - 1:
                    s = s + bias_ref[...]
                ss[t][h] = s
        tmax = [[jnp.max(ss[t][h], axis=0, keepdims=True) for h in range(MLA_HEADS)] for t in range(nt)]
        scale = [[None] * MLA_HEADS for _ in range(nt)]
        mrun = [[None] * MLA_HEADS for _ in range(nt)]
        for h, hs in enumerate(heads):
            m = m_sc[:, hs]
            for t in range(nt):
                m_new = jnp.maximum(m, tmax[t][h])
                scale[t][h] = jnp.exp2(m - m_new)
                mrun[t][h] = m_new
                m = m_new
            m_sc[:, hs] = m
        pvs = [[jnp.dot(vts[t], jnp.exp2(ss[t][h] - mrun[t][h]).astype(BF16), preferred_element_type=F32)
                for h in range(MLA_HEADS)] for t in range(nt)]
        for h, hs in enumerate(heads):
            l, acc = l_sc[:, hs], acc_sc[:, hs]
            for t in range(nt):
                acc = scale[t][h] * acc + pvs[t][h][:KV_LORA]
                l = scale[t][h] * l + pvs[t][h][KV_LORA:KV_LORA + 1]
            l_sc[:, hs], acc_sc[:, hs] = l, acc

    def body(j, carry):
        process([2 * j, 2 * j + 1], False)
        return carry

    lax.fori_loop(0, i // 2, body, 0)

    @pl.when(i % 2 == 1)
    def _():
        process([i - 1, i], True)

    @pl.when(i % 2 == 0)
    def _():
        process([i], True)

    o_lat_t = (acc_sc[...] / l_sc[...]).astype(BF16)
    outs = []
    for h, hs in enumerate(heads):
        outs.append(lax.dot_general(o_lat_t[:, hs], wuv_ref[h], _TN, preferred_element_type=F32))
    o_ref[0] = jnp.concatenate(outs, axis=1).astype(BF16)


def _attn_prompt(qt, kcat, vt, wuv, b, l):
    assert Q_BLOCK == KEY_TILE and Q_BLOCK % CHUNK == 0
    nq = l // Q_BLOCK
    nkt = l // KEY_TILE
    cols = MLA_HEADS * Q_BLOCK
    pos_chunk = jnp.arange(Q_BLOCK, dtype=jnp.int32) // CHUNK
    bias = jnp.where(pos_chunk[:, None] <= pos_chunk[None, :], 0.0, -jnp.inf).astype(F32)
    return pl.pallas_call(
        _attn_prompt_kernel,
        grid=(b, nq),
        in_specs=[pl.BlockSpec((1, QCAT, cols), lambda bb, i: (bb * nq + i, 0, 0)),
                  pl.BlockSpec((1, l, QCAT), lambda bb, i: (bb, 0, 0)),
                  pl.BlockSpec((nkt, KV_LORA, KEY_TILE), lambda bb, i: (bb, 0, 0)),
                  _full(bias.shape), _full(wuv.shape)],
        out_specs=pl.BlockSpec((1, Q_BLOCK, MLA_HEADS * V_DIM), lambda bb, i: (bb, i, 0)),
        out_shape=jax.ShapeDtypeStruct((b, l, MLA_HEADS * V_DIM), BF16),
        scratch_shapes=[pltpu.VMEM((1, cols), F32), pltpu.VMEM((1, cols), F32),
                        pltpu.VMEM((KV_LORA, cols), F32)],
        compiler_params=_params(("parallel", "arbitrary")),
        name="attn_prompt",
    )(qt, kcat.reshape(b, l, QCAT), vt, bias, wuv)


def _attn_sample_kernel(q_ref, cckv_ref, ckr_ref, kn_ref, wuv_ref, o_ref, *, ls, seqs):
    for s in range(seqs):
        rows = slice(s * ls, (s + 1) * ls)
        q = q_ref[:, rows, :].reshape(MLA_HEADS * ls, QCAT)
        ck = cckv_ref[s].astype(BF16)
        kr = ckr_ref[s].astype(BF16)
        kn = kn_ref[rows, :]
        s_c = (lax.dot_general(q[:, :KV_LORA], ck, _NT, preferred_element_type=F32)
               + lax.dot_general(q[:, KV_LORA:KV_LORA + ROPE_DIM], kr, _NT, preferred_element_type=F32))
        s_n = lax.dot_general(q, kn, _NT, preferred_element_type=F32)
        m = jnp.maximum(jnp.max(s_c, axis=1, keepdims=True), jnp.max(s_n, axis=1, keepdims=True))
        p_c = jnp.exp2(s_c - m)
        p_n = jnp.exp2(s_n - m)
        den = jnp.sum(p_c, axis=1, keepdims=True) + jnp.sum(p_n, axis=1, keepdims=True)
        o_lat = (jnp.dot(p_c.astype(BF16), ck, preferred_element_type=F32)
                 + jnp.dot(p_n.astype(BF16), kn[:, :KV_LORA], preferred_element_type=F32)) / den
        o_ref[s] = _head_out(o_lat, wuv_ref, ls).astype(BF16)


def _attn_sample(qcat, kcat, cache_ckv, cache_krope, wuv, b, ls):
    past = cache_ckv.shape[1]
    assert past % CHUNK == 0 and ls <= CHUNK
    seqs = ATT_SAMPLE_SEQS if b % ATT_SAMPLE_SEQS == 0 else 1
    return pl.pallas_call(
        functools.partial(_attn_sample_kernel, ls=ls, seqs=seqs),
        grid=(b // seqs,),
        in_specs=[pl.BlockSpec((MLA_HEADS, seqs * ls, QCAT), lambda bb: (0, bb, 0)),
                  pl.BlockSpec((seqs, past, KV_LORA), lambda bb: (bb, 0, 0)),
                  pl.BlockSpec((seqs, past, ROPE_DIM), lambda bb: (bb, 0, 0)),
                  pl.BlockSpec((seqs * ls, QCAT), lambda bb: (bb, 0)),
                  _full(wuv.shape)],
        out_specs=pl.BlockSpec((seqs, ls, MLA_HEADS * V_DIM), lambda bb: (bb, 0, 0)),
        out_shape=jax.ShapeDtypeStruct((b, ls, MLA_HEADS * V_DIM), BF16),
        compiler_params=_params(("parallel",)),
        name="attn_sample",
    )(qcat, cache_ckv, cache_krope, kcat, wuv)


def _cumsum_rows(x):
    c = x.shape[0]
    row = lax.broadcasted_iota(jnp.int32, x.shape, 0)
    s = 1
    while s < c:
        x = x + jnp.where(row >= s, pltpu.roll(x, s, 0), 0.0)
        s *= 2
    return x


def _gla_kernel(q_ref, k_ref, v_ref, la_ref, sgr_ref, gn_ref, s0_ref, o_ref, sout_ref, st_sc, *, c):
    j = pl.program_id(1)

    @pl.when(j == 0)
    def _():
        st_sc[...] = s0_ref[...]

    half = 32
    row = lax.broadcasted_iota(jnp.int32, (c, GLA_DK), 0)
    rr = lax.broadcasted_iota(jnp.int32, (c, c), 0)
    cc = lax.broadcasted_iota(jnp.int32, (c, c), 1)
    causal = cc <= rr
    same_half = (rr < half) == (cc < half)
    gn = gn_ref[...]
    n_chunks = q_ref.shape[1] // c
    seq_heads = [(s, h) for s in range(GLA_BATCH) for h in range(GLA_HEADS)]
    chains = [(s, h, ci) for s, h in seq_heads for ci in range(n_chunks)]
    ksl = lambda h: slice(h * GLA_DK, (h + 1) * GLA_DK)
    vsl = lambda h: slice(h * GLA_DV, (h + 1) * GLA_DV)
    rsl = lambda ci: slice(ci * c, (ci + 1) * c)

    prep = {}
    for s, h, ci in chains:
        q = q_ref[s, rsl(ci), ksl(h)].astype(F32)
        k = k_ref[s, rsl(ci), ksl(h)].astype(F32)
        b = _cumsum_rows(la_ref[s, rsl(ci), ksl(h)])
        if c > half:
            mid = jnp.where(row < half, b[half // 2 - 1:half // 2, :], b[half + half // 2 - 1:half + half // 2, :])
        else:
            mid = jnp.broadcast_to(b[c // 2 - 1:c // 2, :], b.shape)
        last = b[c - 1:c, :]
        ops = {"qe": (q * jnp.exp(b - mid)).astype(BF16), "ke": (k * jnp.exp(mid - b)).astype(BF16),
               "q0": (q * jnp.exp(b)).astype(BF16), "kd": (k * jnp.exp(last - b)).astype(BF16), "last": last}
        if c > half:
            edge = b[half - 1:half, :]
            ops["qo"] = (q * jnp.exp(jnp.minimum(b - edge, 0.0))).astype(BF16)
            ops["ko"] = (k * jnp.exp(jnp.minimum(edge - b, 0.0))).astype(BF16)
        prep[s, h, ci] = ops
    att = {}
    for key in chains:
        a = lax.dot_general(prep[key]["qe"], prep[key]["ke"], _NT, preferred_element_type=F32)
        if c > half:
            a_off = lax.dot_general(prep[key]["qo"], prep[key]["ko"], _NT, preferred_element_type=F32)
            a = jnp.where(same_half, a, a_off)
        att[key] = jnp.where(causal, a, 0.0).astype(BF16)
    intra, upd, decay = {}, {}, {}
    for s, h, ci in chains:
        v = v_ref[s, rsl(ci), vsl(h)]
        intra[s, h, ci] = jnp.dot(att[s, h, ci], v, preferred_element_type=F32)
        upd[s, h, ci] = lax.dot_general(prep[s, h, ci]["kd"], v, _TN, preferred_element_type=F32)
        dcol = jnp.exp(jnp.transpose(jnp.broadcast_to(prep[s, h, ci]["last"], (GLA_DK, GLA_DK))))
        decay[s, h, ci] = jnp.concatenate([dcol, dcol], axis=1)
    for s, h in seq_heads:
        st = st_sc[s, h]
        for ci in range(n_chunks):
            o = jnp.dot(prep[s, h, ci]["q0"], st.astype(BF16), preferred_element_type=F32) + intra[s, h, ci]
            st = decay[s, h, ci] * st + upd[s, h, ci]
            on = _rms(o, gn) * sgr_ref[s, rsl(ci), vsl(h)].astype(F32)
            o_ref[s, rsl(ci), vsl(h)] = on.astype(BF16)
        st_sc[s, h] = st

    @pl.when(j == pl.num_programs(1) - 1)
    def _():
        sout_ref[...] = st_sc[...]


def _gla(gq, gk, gv, la, sgr, gn, s0, b, l):
    c = min(GLA_CHUNK, l)
    step = min(GLA_STEP_TOKENS, l)
    nc = l // step
    nqk = GLA_HEADS * GLA_DK
    nv = GLA_HEADS * GLA_DV
    r3 = lambda a: a.reshape(b, l, a.shape[-1])
    assert b % GLA_BATCH == 0 and l % step == 0 and step % c == 0
    tok = lambda width: pl.BlockSpec((GLA_BATCH, step, width), lambda bb, j: (bb, j, 0))
    st = pl.BlockSpec((GLA_BATCH, GLA_HEADS, GLA_DK, GLA_DV), lambda bb, j: (bb, 0, 0, 0))
    return pl.pallas_call(
        functools.partial(_gla_kernel, c=c),
        grid=(b // GLA_BATCH, nc),
        in_specs=[tok(nqk), tok(nqk), tok(nv), tok(nqk), tok(nv), _full(gn.shape), st],
        out_specs=[tok(nv), st],
        out_shape=[jax.ShapeDtypeStruct((b, l, nv), BF16),
                   jax.ShapeDtypeStruct((b, GLA_HEADS, GLA_DK, GLA_DV), F32)],
        scratch_shapes=[pltpu.VMEM((GLA_BATCH, GLA_HEADS, GLA_DK, GLA_DV), F32)],
        compiler_params=_params(("parallel", "arbitrary")),
        name="gla",
    )(r3(gq), r3(gk), r3(gv), r3(la), r3(sgr), gn, s0)


_ROWS_A = slice(0, MLA_HEADS * V_DIM)
_ROWS_B = slice(_ROWS_A.stop, _ROWS_A.stop + GLA_HEADS * GLA_DV)
_ROWS_O = slice(_ROWS_B.stop, _ROWS_B.stop + D_MODEL)


def _merge_kernel(oa_ref, ob_ref, sga_ref, sgb_ref, x_ref, wabo_ref, gffn_ref,
                  wr_ref, br_ref, cnt0_ref, x1_ref, h2_ref, route_ref, gcol_ref, cnt_ref, meta_ref, run_sc):
    assert MERGE_PARTS == 1
    tm = x_ref.shape[0]
    pr = tm // MERGE_PARTS
    wr = wr_ref[...]
    whi = wr.astype(BF16)
    wlo = (wr - whi.astype(F32)).astype(BF16)
    ridx = lax.broadcasted_iota(jnp.int32, (EXPERTS_PER_GROUP, pr), 0)
    eidx = lax.broadcasted_iota(jnp.int32, (N_EXPERTS, pr), 0)
    earlier = jnp.where(lax.broadcasted_iota(jnp.int32, (pr, pr), 0) < lax.broadcasted_iota(jnp.int32, (pr, pr), 1),
                        1.0, 0.0).astype(BF16)
    big = jnp.int32(1 << 20)

    def top(vals):
        vmax = jnp.max(vals, axis=0, keepdims=True)
        imax = jnp.min(jnp.where(vals == vmax, ridx, big), axis=0, keepdims=True)
        return vmax, imax

    h2s = []
    for part in range(MERGE_PARTS):
        rows = slice(part * pr, (part + 1) * pr)
        ya = jnp.dot(oa_ref[rows, :], wabo_ref[_ROWS_A, :], preferred_element_type=F32)
        yb = jnp.dot(ob_ref[rows, :], wabo_ref[_ROWS_B, :], preferred_element_type=F32)
        merged = (sga_ref[rows, :].astype(F32) * ya + sgb_ref[rows, :].astype(F32) * yb).astype(BF16)
        x1 = x_ref[rows, :] + jnp.dot(merged, wabo_ref[_ROWS_O, :], preferred_element_type=F32)
        x1_ref[rows, :] = x1
        h2 = _rms(x1, gffn_ref[...])
        _store_token_major(h2_ref, h2, part * pr)
        h2s.append(h2)

    routed = []
    for part in range(MERGE_PARTS):
        rows = slice(part * pr, (part + 1) * pr)
        hi = h2s[part].astype(BF16)
        lo = (h2s[part] - hi.astype(F32)).astype(BF16)
        logits = (lax.dot_general(whi, hi, _NT, preferred_element_type=F32)
                  + lax.dot_general(whi, lo, _NT, preferred_element_type=F32)
                  + lax.dot_general(wlo, hi, _NT, preferred_element_type=F32)) + br_ref[...]
        gl = jnp.where(ridx < N_GROUPS, logits[N_EXPERTS:N_EXPERTS + EXPERTS_PER_GROUP], -jnp.inf)
        gmax, g_top = top(gl)
        p_top = 1.0 / jnp.sum(jnp.exp(gl - gmax), axis=0, keepdims=True)
        e_sel = logits[:EXPERTS_PER_GROUP]
        for g in range(1, N_GROUPS):
            e_sel = jnp.where(g_top == g, logits[g * EXPERTS_PER_GROUP:(g + 1) * EXPERTS_PER_GROUP], e_sel)
        v1, i1 = top(e_sel)
        v2, i2 = top(jnp.where(ridx == i1, -jnp.inf, e_sel))
        e21 = jnp.exp(v2 - v1)
        w1 = p_top / (1.0 + e21)
        w2 = p_top * e21 / (1.0 + e21)
        base = g_top * EXPERTS_PER_GROUP
        id0 = base + i1
        id1 = base + i2
        gates = jnp.where(ridx == 0, w1, jnp.where(ridx == 1, w2, 0.0))
        gcol_ref[rows, :] = jnp.transpose(
            jnp.concatenate([gates, jnp.zeros((LANES - EXPERTS_PER_GROUP, pr), F32)], axis=0))
        oh0 = jnp.where(eidx == id0, 1.0, 0.0)
        oh1 = jnp.where(eidx == id1, 1.0, 0.0)
        p0 = jnp.dot(oh0.astype(BF16), earlier, preferred_element_type=F32)
        p1 = jnp.dot(oh1.astype(BF16), earlier, preferred_element_type=F32)
        c0 = jnp.sum(oh0, axis=1, keepdims=True)
        c1 = jnp.sum(oh1, axis=1, keepdims=True)
        routed.append((id0, id1, oh0, oh1, p0, p1, c0, c1))

    @pl.when(pl.program_id(0) == 0)
    def _():
        run_sc[...] = cnt0_ref[:, :1]

    run = run_sc[...]
    lane = lax.broadcasted_iota(jnp.int32, (N_EXPERTS, LANES), 1)
    for part, (id0, id1, oh0, oh1, p0, p1, c0, c1) in enumerate(routed):
        rank0 = jnp.sum(oh0 * (run + p0), axis=0, keepdims=True)
        rank1 = jnp.sum(oh1 * (run + c0 + p1), axis=0, keepdims=True)
        length = c0 + c1
        windows = jnp.ceil(length * (1.0 / RUN_WINDOW)) * RUN_WINDOW
        windows_b = jnp.broadcast_to(windows, (N_EXPERTS, LANES))
        stage_off = (_cumsum_rows(windows_b) - windows_b)[:, :1]
        pos0 = jnp.sum(oh0 * (stage_off + p0), axis=0, keepdims=True)
        pos1 = jnp.sum(oh1 * (stage_off + c0 + p1), axis=0, keepdims=True)
        win_row = (lane * RUN_WINDOW).astype(F32)
        expert_of = jnp.sum(jnp.where(stage_off + windows <= win_row, 1.0, 0.0), axis=0, keepdims=True)
        n_win = jnp.sum(windows, axis=0, keepdims=True) * (1.0 / RUN_WINDOW)
        expert_of = jnp.minimum(expert_of, N_EXPERTS - 1.0)
        eidx_w = lax.broadcasted_iota(jnp.int32, (N_EXPERTS, LANES), 0).astype(F32)
        rel_row = jnp.sum(jnp.where(eidx_w == expert_of, run - stage_off, 0.0), axis=0, keepdims=True) + win_row[:1]
        r8 = lax.broadcasted_iota(jnp.int32, (EXPERTS_PER_GROUP, LANES), 0)
        meta_ref[...] = jnp.where(r8 == 0, expert_of.astype(jnp.int32), jnp.where(
            r8 == 1, rel_row.astype(jnp.int32), jnp.where(r8 == 2, n_win.astype(jnp.int32), 0)))
        run = run + length
        vals = (id0, id1, rank0.astype(jnp.int32), rank1.astype(jnp.int32),
                pos0.astype(jnp.int32), pos1.astype(jnp.int32))
        out = jnp.zeros((EXPERTS_PER_GROUP, pr), jnp.int32)
        for r, v in enumerate(vals):
            out = jnp.where(ridx == r, v, out)
        route_ref[:, part * pr:(part + 1) * pr] = out
    run_sc[...] = run
    cnt_ref[...] = jnp.broadcast_to(run, cnt_ref.shape)


def _merge_tile(n):
    return min(MERGE_TILE, max(ROW_TILE, n // 4))


def _merge(oa, ob, sga, sgb, x2, w, cnt0):
    n = x2.shape[0]
    tm = _merge_tile(n)
    assert n % tm == 0
    row = lambda width: pl.BlockSpec((tm, width), lambda i: (i, 0))
    weights = [w["w_abo"], w["g_ffn"], w["w_router"], w["b_router"], cnt0]
    return pl.pallas_call(
        _merge_kernel,
        grid=(n // tm,),
        in_specs=[row(MLA_HEADS * V_DIM), row(GLA_HEADS * GLA_DV), row(D_MODEL), row(D_MODEL),
                  row(D_MODEL)] + [_full(a.shape) for a in weights],
        out_specs=[row(D_MODEL), pl.BlockSpec((tm * SLABS, LANES), lambda i: (i, 0)),
                   pl.BlockSpec((EXPERTS_PER_GROUP, tm), lambda i: (0, i)), row(LANES),
                   _full((N_EXPERTS, LANES)), pl.BlockSpec((EXPERTS_PER_GROUP, LANES), lambda i: (i, 0))],
        out_shape=[jax.ShapeDtypeStruct((n, D_MODEL), F32), jax.ShapeDtypeStruct((n * SLABS, LANES), F32),
                   jax.ShapeDtypeStruct((EXPERTS_PER_GROUP, n), jnp.int32),
                   jax.ShapeDtypeStruct((n, LANES), F32),
                   jax.ShapeDtypeStruct((N_EXPERTS, LANES), F32),
                   jax.ShapeDtypeStruct((n // tm * EXPERTS_PER_GROUP, LANES), jnp.int32)],
        scratch_shapes=[pltpu.VMEM((N_EXPERTS, 1), F32)],
        compiler_params=_params(("arbitrary",)),
        name="merge",
    )(oa, ob, sga, sgb, x2, *weights)


ROW_DMA_UNROLL = 8


def _row_tile(ref, r):
    return ref.at[pl.ds(pl.multiple_of(r * SLABS, SLABS), SLABS), :]


def _scatter_kernel(ps_ref, cnt_ref, pe_ref, *rest, tiles):
    ng = len(tiles)
    meta_refs, prev_refs, pos_refs, h2_refs = (rest[k * ng:(k + 1) * ng] for k in range(4))
    xd_hbm, zero_sc, stage, sem, zsem = rest[4 * ng:]

    def zero_padding():
        zero_sc[...] = jnp.zeros(zero_sc.shape, F32)

        def pad_copies(e, act):
            lo = ps_ref[e] + cnt_ref[e]
            pad = pe_ref[e] - lo
            bit = FFN_ROWS
            while bit >= 1:
                @pl.when((pad & bit) != 0)
                def _(bit=bit):
                    first = lo + (pad & ~(2 * bit - 1))
                    act(pltpu.make_async_copy(
                        zero_sc.at[pl.ds(0, bit * SLABS), :],
                        xd_hbm.at[pl.ds(pl.multiple_of(first * SLABS, SLABS), bit * SLABS), :], zsem.at[0]))
                bit //= 2

        def start_pads(e, c):
            pad_copies(e, lambda cp: cp.start())
            return c

        def wait_pads(e, c):
            pad_copies(e, lambda cp: cp.wait())
            return c

        lax.fori_loop(0, N_EXPERTS, start_pads, 0)
        lax.fori_loop(0, N_EXPERTS, wait_pads, 0)

        def block_copy(b):
            start_row = pl.multiple_of(b * (FFN_ROWS * SLABS), FFN_ROWS * SLABS)
            return pltpu.make_async_copy(zero_sc, xd_hbm.at[pl.ds(start_row, FFN_ROWS * SLABS), :], zsem.at[0])

        first_unused = pe_ref[N_EXPERTS - 1] // FFN_ROWS
        n_blocks = xd_hbm.shape[0] // (FFN_ROWS * SLABS)

        def start_block(b, c):
            block_copy(b).start()
            return c

        def wait_block(b, c):
            block_copy(b).wait()
            return c

        lax.fori_loop(first_unused, n_blocks, start_block, 0)
        lax.fori_loop(first_unused, n_blocks, wait_block, 0)

    step = pl.program_id(0)
    pl.when(step == 0)(zero_padding)

    def windows(meta, s, act):
        def per_window(j, c):
            row = ps_ref[meta[0, j]] + meta[1, j]
            act(pltpu.make_async_copy(
                stage.at[s, pl.ds(pl.multiple_of(j * (RUN_WINDOW * SLABS), RUN_WINDOW * SLABS), RUN_WINDOW * SLABS), :],
                xd_hbm.at[pl.ds(pl.multiple_of(row * SLABS, SLABS), RUN_WINDOW * SLABS), :],
                sem.at[s]))
            return c

        lax.fori_loop(0, meta[2, 0], per_window, 0)

    first_step = 0
    for g, nt in enumerate(tiles):
        @pl.when((step >= first_step) & (step < first_step + nt))
        def _(g=g, nt=nt, first_step=first_step):
            meta_ref, prev_ref, pos_ref, h2_ref = meta_refs[g], prev_refs[g], pos_refs[g], h2_refs[g]
            tm = h2_ref.shape[0] // SLABS
            local = step - first_step
            slot = local % 2
            stage_rows = 2 * tm + N_EXPERTS * RUN_WINDOW

            @pl.when(local < 2)
            def _():
                stage[slot, pl.ds(0, stage_rows * SLABS), :] = jnp.zeros((stage_rows * SLABS, LANES), F32)

            def body(t, carry):
                for k in range(2):
                    p = pl.multiple_of(pos_ref[0, 0, k * tm + t], SLABS)
                    stage[slot, pl.ds(p, SLABS), :] = _row_tile(h2_ref, t)[...]
                return carry

            lax.fori_loop(0, tm, body, 0, unroll=ROW_DMA_UNROLL)

            @pl.when(local > 0)
            def _():
                windows(prev_ref, 1 - slot, lambda cp: cp.wait())

            windows(meta_ref, slot, lambda cp: cp.start())

            @pl.when(local == nt - 1)
            def _():
                windows(meta_ref, slot, lambda cp: cp.wait())
        first_step += nt


def _scatter(pad_start, counts, pad_end, metas, pos_blocks, h2s, total_rows):
    tms = [p.shape[2] // 2 for p in pos_blocks]
    tiles = [p.shape[0] for p in pos_blocks]
    firsts = [sum(tiles[:g]) for g in range(len(tiles))]
    ng = len(tiles)
    local = lambda g, shift=0: (lambda i, *_: jnp.clip(i - firsts[g] - shift, 0, tiles[g] - 1))
    smem = pltpu.SMEM
    in_specs = ([pl.BlockSpec((EXPERTS_PER_GROUP, LANES), (lambda i, *_, f=local(g): (f(i), 0)), memory_space=smem)
                 for g in range(ng)]
                + [pl.BlockSpec((EXPERTS_PER_GROUP, LANES), (lambda i, *_, f=local(g, 1): (f(i), 0)), memory_space=smem)
                   for g in range(ng)]
                + [pl.BlockSpec((1, 1, 2 * tms[g]), (lambda i, *_, f=local(g): (f(i), 0, 0)), memory_space=smem)
                   for g in range(ng)]
                + [pl.BlockSpec((tms[g] * SLABS, LANES), (lambda i, *_, f=local(g): (f(i), 0)))
                   for g in range(ng)])
    stage_rows = 2 * max(tms) + N_EXPERTS * RUN_WINDOW
    grid_spec = pltpu.PrefetchScalarGridSpec(
        num_scalar_prefetch=3,
        grid=(sum(tiles),),
        in_specs=in_specs,
        out_specs=pl.BlockSpec(memory_space=pl.ANY),
        scratch_shapes=[pltpu.VMEM((FFN_ROWS * SLABS, LANES), F32),
                        pltpu.VMEM((2, stage_rows * SLABS, LANES), F32),
                        pltpu.SemaphoreType.DMA((2,)), pltpu.SemaphoreType.DMA((1,))],
    )
    return pl.pallas_call(
        functools.partial(_scatter_kernel, tiles=tuple(tiles)),
        grid_spec=grid_spec,
        out_shape=jax.ShapeDtypeStruct((total_rows * SLABS, LANES), F32),
        compiler_params=_params(("arbitrary",)),
        name="moe_scatter",
    )(pad_start, counts, pad_end, *metas, *metas, *pos_blocks, *h2s)


def _ffn_kernel(be_ref, nbu_ref, x_ref, wg_ref, wu_ref, wd_ref, out_ref, wg_sc, wu_sc, wd_sc):
    i = pl.program_id(0)
    used = i < nbu_ref[0]
    new_expert = (i == 0) | (be_ref[i] != be_ref[jnp.maximum(i - 1, 0)])

    @pl.when(used & new_expert)
    def _():
        wg_sc[...] = wg_ref[0].astype(BF16)
        wu_sc[...] = wu_ref[0].astype(BF16)
        wd_sc[...] = wd_ref[0].astype(BF16)

    @pl.when(used)
    def _():
        x = _load_token_major(x_ref, FFN_ROWS).astype(BF16)
        g = jnp.dot(x, wg_sc[...], preferred_element_type=F32)
        u = jnp.dot(x, wu_sc[...], preferred_element_type=F32)
        mid = (g * _sigmoid(g) * u).astype(BF16)
        _store_token_major(out_ref, jnp.dot(mid, wd_sc[...], preferred_element_type=F32))

    @pl.when(jnp.logical_not(used))
    def _():
        out_ref[...] = jnp.zeros(out_ref.shape, F32)


def _ffn(block_expert, nb_used, xd, wg, wu, wd):
    nb = block_expert.shape[0]
    blk = lambda i, be, nbu: (jnp.minimum(i, nbu[0] - 1), 0)
    wsel = lambda i, be, nbu: (be[jnp.minimum(i, nbu[0] - 1)], 0, 0)
    grid_spec = pltpu.PrefetchScalarGridSpec(
        num_scalar_prefetch=2,
        grid=(nb,),
        in_specs=[pl.BlockSpec((FFN_ROWS * SLABS, LANES), blk),
                  pl.BlockSpec((1, D_MODEL, EXPERT_FF), wsel),
                  pl.BlockSpec((1, D_MODEL, EXPERT_FF), wsel),
                  pl.BlockSpec((1, EXPERT_FF, D_MODEL), wsel)],
        out_specs=pl.BlockSpec((FFN_ROWS * SLABS, LANES), lambda i, be, nbu: (i, 0)),
        scratch_shapes=[pltpu.VMEM((D_MODEL, EXPERT_FF), BF16), pltpu.VMEM((D_MODEL, EXPERT_FF), BF16),
                        pltpu.VMEM((EXPERT_FF, D_MODEL), BF16)],
    )
    return pl.pallas_call(
        _ffn_kernel,
        grid_spec=grid_spec,
        out_shape=jax.ShapeDtypeStruct((nb * FFN_ROWS * SLABS, LANES), F32),
        compiler_params=_params(("arbitrary",)),
        name="expert_ffn",
    )(block_expert, nb_used, xd, wg, wu, wd)


def _combine_kernel(ps_ref, meta_ref, metan_ref, pos_ref, x1_ref, gcol_ref, gfin_ref, eo_hbm, y_ref,
                    stage, ybuf, sem):
    i = pl.program_id(0)
    n = pl.num_programs(0)
    tm = x1_ref.shape[0]
    slot = i % 2

    def windows(meta, s, act):
        def per_window(j, c):
            row = ps_ref[meta[0, j]] + meta[1, j]
            act(pltpu.make_async_copy(
                eo_hbm.at[pl.ds(pl.multiple_of(row * SLABS, SLABS), RUN_WINDOW * SLABS), :],
                stage.at[s, pl.ds(pl.multiple_of(j * (RUN_WINDOW * SLABS), RUN_WINDOW * SLABS), RUN_WINDOW * SLABS), :],
                sem.at[s]))
            return c

        lax.fori_loop(0, meta[2, 0], per_window, 0)

    @pl.when(i == 0)
    def _():
        windows(meta_ref, 0, lambda cp: cp.start())

    @pl.when(i + 1 < n)
    def _():
        windows(metan_ref, 1 - slot, lambda cp: cp.start())

    windows(meta_ref, slot, lambda cp: cp.wait())

    def body(t, carry):
        for k in range(2):
            p = pl.multiple_of(pos_ref[0, 0, k * tm + t], SLABS)
            _row_tile(ybuf.at[k], t)[...] = stage[slot, pl.ds(p, SLABS), :]
        return carry

    lax.fori_loop(0, tm, body, 0, unroll=ROW_DMA_UNROLL)
    g = gcol_ref[...]
    y = x1_ref[...] + (g[:, 0:1] * _load_token_major(ybuf.at[0], tm) + g[:, 1:2] * _load_token_major(ybuf.at[1], tm))
    y_ref[...] = _rms(y, gfin_ref[...])


def _combine(pad_start, meta, pos_blocks, x1, gcol, g_final, eo):
    n = x1.shape[0]
    tm = _merge_tile(n)
    nt = n // tm
    stage_rows = 2 * tm + N_EXPERTS * RUN_WINDOW
    grid_spec = pltpu.PrefetchScalarGridSpec(
        num_scalar_prefetch=1,
        grid=(nt,),
        in_specs=[pl.BlockSpec((EXPERTS_PER_GROUP, LANES), lambda i, ps: (i, 0), memory_space=pltpu.SMEM),
                  pl.BlockSpec((EXPERTS_PER_GROUP, LANES), lambda i, ps: (jnp.minimum(i + 1, nt - 1), 0),
                               memory_space=pltpu.SMEM),
                  pl.BlockSpec((1, 1, 2 * tm), lambda i, ps: (i, 0, 0), memory_space=pltpu.SMEM),
                  pl.BlockSpec((tm, D_MODEL), lambda i, ps: (i, 0)),
                  pl.BlockSpec((tm, LANES), lambda i, ps: (i, 0)),
                  pl.BlockSpec(g_final.shape, lambda i, ps: (0, 0)),
                  pl.BlockSpec(memory_space=pl.ANY)],
        out_specs=pl.BlockSpec((tm, D_MODEL), lambda i, ps: (i, 0)),
        scratch_shapes=[pltpu.VMEM((2, stage_rows * SLABS, LANES), F32),
                        pltpu.VMEM((2, tm * SLABS, LANES), F32), pltpu.SemaphoreType.DMA((2,))],
    )
    return pl.pallas_call(
        _combine_kernel,
        grid_spec=grid_spec,
        out_shape=jax.ShapeDtypeStruct((n, D_MODEL), F32),
        compiler_params=_params(("arbitrary",)),
        name="combine",
    )(pad_start, meta, meta, pos_blocks, x1, gcol, g_final, eo)


def _layout(cnt, n_tokens):
    counts = cnt[:, 0].astype(jnp.int32)
    padded = (counts + RUN_WINDOW + FFN_ROWS - 1) // FFN_ROWS * FFN_ROWS
    pad_end = jnp.cumsum(padded).astype(jnp.int32)
    pad_start = pad_end - padded
    nb = -(-(2 * n_tokens + N_EXPERTS * RUN_WINDOW) // FFN_ROWS) + N_EXPERTS + 1
    block_start = jnp.arange(nb, dtype=jnp.int32) * FFN_ROWS
    block_expert = jnp.minimum(jnp.sum((block_start[:, None] >= pad_end[None, :]).astype(jnp.int32), axis=1),
                               N_EXPERTS - 1).astype(jnp.int32)
    nb_used = (pad_end[N_EXPERTS - 1:] // FFN_ROWS).astype(jnp.int32)
    return pad_start, counts, pad_end, block_expert, nb_used, nb * FFN_ROWS


def _rope_tables(l, past, rows):
    half = ROPE_DIM // 2
    inv = ROPE_THETA ** (-jnp.arange(half, dtype=F32) / half)
    pos = (past + (jnp.arange(rows, dtype=jnp.int32) % l)).astype(F32)
    ang = pos[:, None] * inv[None, :]
    pad = jnp.zeros((rows, LANES - ROPE_DIM), F32)
    cos = jnp.concatenate([jnp.cos(ang), jnp.cos(ang), pad], axis=1)
    sin = jnp.concatenate([-jnp.sin(ang), jnp.sin(ang), pad], axis=1)
    return cos, sin


def _mixers(x, ckv_past, kpe_past, s0, w):
    b, l, d = x.shape
    n = b * l
    x2 = x.reshape(n, d)
    past = 0 if ckv_past is None else ckv_past.shape[1]
    cos_t, sin_t = _rope_tables(l, past, max(l, INPROJ_TILE))
    prompt = ckv_past is None
    (qcat, kcat, ckv, kpe, gq, gk, gv, la, sgr, sga, sgb, *maybe_vt) = _inproj(x2, cos_t, sin_t, w, prompt)
    if prompt:
        oa = _attn_prompt(qcat, kcat, maybe_vt[0], w["w_uv"], b, l)
    else:
        oa = _attn_sample(qcat, kcat, ckv_past, kpe_past, w["w_uv"], b, l)
    ob, s_new = _gla(gq, gk, gv, la, sgr, w["g_gla"], s0, b, l)
    merge_in = (oa.reshape(n, -1), ob.reshape(n, -1), sga, sgb, x2)
    return merge_in, (ckv.reshape(b, l, KV_LORA), kpe.reshape(b, l, ROPE_DIM), s_new)


def _layer(groups, w, g_final):
    fronts = [_mixers(*g, w) for g in groups]
    cnt = jnp.zeros((N_EXPERTS, LANES), F32)
    merged = []
    for merge_in, _ in fronts:
        x1, h2, route, gcol, cnt, meta = _merge(*merge_in, w, cnt)
        merged.append((x1, h2, route, gcol, meta))
    n_tokens = sum(m[0].shape[0] for m in merged)
    pad_start, counts, pad_end, block_expert, nb_used, total_rows = _layout(cnt, n_tokens)
    pos_blocks = []
    for x1, h2, route, gcol, meta in merged:
        n = x1.shape[0]
        tm = _merge_tile(n)
        pos_blocks.append((route[4:6] * SLABS).reshape(2, n // tm, tm).transpose(1, 0, 2)
                          .reshape(n // tm, 1, 2 * tm))
    xd = _scatter(pad_start, counts, pad_end, [m[4] for m in merged], pos_blocks, [m[1] for m in merged],
                  total_rows)
    eo = _ffn(block_expert, nb_used, xd, w["w_eg"], w["w_eu"], w["w_ed"])
    outs = []
    for (x1, h2, route, gcol, meta), pos, (_, extras), g in zip(merged, pos_blocks, fronts, groups):
        y = _combine(pad_start, meta, pos, x1, gcol, g_final, eo)
        outs.append((y.reshape(g[0].shape),) + extras)
    return outs


def _prep_weights(w_in, g_norm_mix, g_qnorm, w_uq, g_kvnorm, w_ukv, w_gate2, b_gate2, g_gla_norm,
                  w_branch_a, w_branch_b, w_out, g_norm_ffn, w_router_group, b_router_group,
                  w_router_expert, b_router_expert, w_exp_gate, w_exp_up, w_exp_down):
    nqk = GLA_HEADS * GLA_DK
    nv = GLA_HEADS * GLA_DV
    o = 0
    parts = {}
    for name, width in (("cq", Q_LORA), ("ckv", KV_LORA), ("kpe", ROPE_DIM), ("gq", nqk), ("gk", nqk),
                        ("gv", nv), ("glr", GATE_RANK), ("gr", nv), ("ga", D_MODEL), ("gb", D_MODEL)):
        parts[name] = w_in[:, o:o + width]
        o += width
    padc = lambda a, width: jnp.pad(a, ((0, 0), (0, width - a.shape[1])))
    w_in_all = jnp.concatenate([parts["cq"], parts["ckv"], padc(parts["kpe"], LANES), padc(parts["glr"], LANES),
                                parts["gq"], parts["gk"], parts["gv"], parts["gr"], parts["ga"], parts["gb"]],
                               axis=1).astype(BF16)
    assert w_in_all.shape[1] == _IN_COLS["gb"].stop
    w_abo = jnp.concatenate([w_branch_a, w_branch_b, w_out], axis=0).astype(BF16)
    uq = w_uq.reshape(Q_LORA, MLA_HEADS, NOPE_DIM + ROPE_DIM)
    ukv = w_ukv.reshape(KV_LORA, MLA_HEADS, NOPE_DIM + V_DIM)
    lat = _fold_q(uq[:, :, :NOPE_DIM].transpose(1, 0, 2), ukv[:, :, :NOPE_DIM].transpose(1, 0, 2))
    q_rope = uq[:, :, NOPE_DIM:].transpose(1, 0, 2)
    w_qcat = jnp.concatenate([lat, q_rope, jnp.zeros((MLA_HEADS, Q_LORA, QCAT - KV_LORA - ROPE_DIM), F32)],
                             axis=2)
    w_qcat = w_qcat.transpose(1, 0, 2).reshape(Q_LORA, MLA_HEADS * QCAT).astype(BF16)
    w_router = jnp.concatenate([w_router_expert.T, w_router_group.T,
                                jnp.zeros((ROUTER_ROWS - N_EXPERTS - N_GROUPS, D_MODEL), F32)], axis=0)
    b_router = jnp.concatenate([b_router_expert, b_router_group,
                                jnp.zeros((ROUTER_ROWS - N_EXPERTS - N_GROUPS,), F32)]).reshape(ROUTER_ROWS, 1)
    return {
        "g_mix": g_norm_mix.reshape(1, D_MODEL), "w_in": w_in_all,
        "g_qn": g_qnorm.reshape(1, Q_LORA), "g_kvn": g_kvnorm.reshape(1, KV_LORA), "w_qcat": w_qcat,
        "w_g2": jnp.pad(w_gate2, ((0, LANES - GATE_RANK), (0, 0))).astype(BF16),
        "b_g2": b_gate2.reshape(1, nqk),
        "w_uv": ukv[:, :, NOPE_DIM:].transpose(1, 0, 2).astype(BF16),
        "g_gla": g_gla_norm.reshape(1, GLA_DV),
        "w_abo": w_abo,
        "g_ffn": g_norm_ffn.reshape(1, D_MODEL), "w_router": w_router, "b_router": b_router,
        "w_eg": w_exp_gate, "w_eu": w_exp_up, "w_ed": w_exp_down,
    }


def kernel(x_prompt, x_sample, cache_ckv, cache_krope, state_gla, w_in, g_norm_mix, g_qnorm, w_uq, g_kvnorm, w_ukv, w_gate2, b_gate2, g_gla_norm, w_branch_a, w_branch_b, w_out, g_norm_ffn, w_router_group, b_router_group, w_router_expert, b_router_expert, w_exp_gate, w_exp_up, w_exp_down, g_norm_final):
    depth = w_in.shape[0]
    assert depth == 1, "the final norm is fused into the last layer's combine step"
    gfin = g_norm_final.reshape(1, D_MODEL)
    drop = lambda a: a.reshape(a.shape[1:])
    lift = lambda a: a.reshape((1,) + a.shape)
    w = _prep_weights(*[drop(a) for a in (
        w_in, g_norm_mix, g_qnorm, w_uq, g_kvnorm, w_ukv, w_gate2, b_gate2, g_gla_norm, w_branch_a, w_branch_b,
        w_out, g_norm_ffn, w_router_group, b_router_group, w_router_expert, b_router_expert, w_exp_gate,
        w_exp_up, w_exp_down)])
    bp = x_prompt.shape[0]
    zero_state = jnp.zeros((bp, GLA_HEADS, GLA_DK, GLA_DV), F32)
    (yp, c1, k1, s1), (ys, c2, k2, s2) = _layer(
        [(x_prompt, None, None, zero_state),
         (x_sample, drop(cache_ckv), drop(cache_krope), drop(state_gla).astype(F32))], w, gfin)
    return (yp, ys, lift(c1), lift(k1), lift(s1), lift(c2), lift(k2), lift(s2))
```

```python
import functools
import math

import jax
import jax.numpy as jnp
from jax import lax
from jax.experimental import pallas as pl
from jax.experimental.pallas import tpu as pltpu

F32 = jnp.float32
BF16 = jnp.bfloat16

D_MODEL = 1024
CHUNK = 64
EPS = 1e-6
MLA_HEADS = 8
Q_LORA = 256
KV_LORA = 128
NOPE_DIM = 64
ROPE_DIM = 32
V_DIM = 64
ROPE_THETA = 10000.0
GLA_HEADS = 4
GLA_DK = 128
GLA_DV = 256
GATE_RANK = 16
GATE_TEMP = 16.0
N_GROUPS = 4
EXPERTS_PER_GROUP = 8
N_EXPERTS = 32
EXPERT_FF = 512

LANES = 128
QCAT = 2 * LANES
ROW_TILE = 256
INPROJ_TILE = 512
MERGE_TILE = 512
MERGE_PARTS = 1
RUN_WINDOW = 32
GLA_BATCH = 2
GLA_STEP_TOKENS = 512
Q_BLOCK = ROW_TILE
KEY_TILE = ROW_TILE
ATT_SAMPLE_SEQS = 4
SUM_ROWS = 16
GLA_CHUNK = 64
FFN_ROWS = 512
ROUTER_ROWS = 48
VMEM_LIMIT = 56 * 1024 * 1024
LOG2E = 1.4426950408889634
ATT_SCALE = LOG2E / math.sqrt(NOPE_DIM + ROPE_DIM)

_NT = (((1,), (1,)), ((), ()))
_TN = (((0,), (0,)), ((), ()))


def _params(sem):
    return pltpu.CompilerParams(dimension_semantics=sem, vmem_limit_bytes=VMEM_LIMIT)


def _rms(x, g):
    return x * lax.rsqrt(jnp.mean(x * x, axis=-1, keepdims=True) + EPS) * g


def _sigmoid(x):
    return 1.0 / (1.0 + jnp.exp(-x))


def _full(shape):
    n = len(shape)
    return pl.BlockSpec(shape, lambda *_: (0,) * n)


SLABS = D_MODEL // LANES


def _store_token_major(ref, x, first_row=0):
    rows = x.shape[0]
    for j in range(SLABS):
        ref[pl.ds(first_row * SLABS + j, rows, stride=SLABS), :] = x[:, j * LANES:(j + 1) * LANES]


def _load_token_major(ref, rows):
    return jnp.concatenate([ref[pl.ds(j, rows, stride=SLABS), :] for j in range(SLABS)], axis=1)


def _fold_q_kernel(wq_ref, wk_ref, o_ref):
    o_ref[0] = lax.dot_general(wq_ref[0], wk_ref[0], _NT, precision=lax.Precision.HIGHEST,
                               preferred_element_type=F32)


def _fold_q(wq_nope, wk_nope):
    return pl.pallas_call(
        _fold_q_kernel,
        grid=(MLA_HEADS,),
        in_specs=[pl.BlockSpec((1, Q_LORA, NOPE_DIM), lambda h: (h, 0, 0)),
                  pl.BlockSpec((1, KV_LORA, NOPE_DIM), lambda h: (h, 0, 0))],
        out_specs=pl.BlockSpec((1, Q_LORA, KV_LORA), lambda h: (h, 0, 0)),
        out_shape=jax.ShapeDtypeStruct((MLA_HEADS, Q_LORA, KV_LORA), F32),
        compiler_params=_params(("arbitrary",)),
        name="fold_q",
    )(wq_nope, wk_nope)


_W_SMALL = Q_LORA + KV_LORA + 2 * LANES
_IN_COLS = {}
_off = 0
for _name, _width in (("small", _W_SMALL), ("gqk", 2 * GLA_HEADS * GLA_DK), ("gv", GLA_HEADS * GLA_DV),
                      ("gr", GLA_HEADS * GLA_DV), ("ga", D_MODEL), ("gb", D_MODEL)):
    _IN_COLS[_name] = slice(_off, _off + _width)
    _off += _width


def _inproj_kernel(x_ref, cos_ref, sin_ref, gmix_ref, win_ref, gqn_ref, gkvn_ref, wqc_ref,
                   wg2_ref, bg2_ref,
                   qcat_ref, kcat_ref, ckv_ref, kpe_ref, gq_ref, gk_ref, gv_ref, la_ref,
                   sgr_ref, sga_ref, sgb_ref, *maybe_vt_ref, transposed):
    hb = _rms(x_ref[...], gmix_ref[...]).astype(BF16)
    proj = lambda name: jnp.dot(hb, win_ref[:, _IN_COLS[name]], preferred_element_type=F32)
    zs = proj("small")
    cos = cos_ref[...]
    sin = sin_ref[...]
    first_half = lax.broadcasted_iota(jnp.int32, cos.shape, 1) < ROPE_DIM // 2

    def rope(v):
        rot = jnp.where(first_half, pltpu.roll(v, LANES - ROPE_DIM // 2, 1),
                        pltpu.roll(v, ROPE_DIM // 2, 1))
        return v * cos + rot * sin

    cqn = _rms(zs[:, :Q_LORA], gqn_ref[...]).astype(BF16)
    qc = jnp.dot(cqn, wqc_ref[...], preferred_element_type=F32)
    for h in range(MLA_HEADS):
        lat = qc[:, h * QCAT:h * QCAT + LANES]
        pe = qc[:, h * QCAT + LANES:(h + 1) * QCAT]
        if transposed:
            lat_t = jnp.transpose(lat * ATT_SCALE).astype(BF16)
            pe_t = jnp.transpose(rope(pe) * ATT_SCALE).astype(BF16)
            for blk in range(lat.shape[0] // Q_BLOCK):
                cols = slice(blk * Q_BLOCK, (blk + 1) * Q_BLOCK)
                qcat_ref[blk, :LANES, h * Q_BLOCK:(h + 1) * Q_BLOCK] = lat_t[:, cols]
                qcat_ref[blk, LANES:, h * Q_BLOCK:(h + 1) * Q_BLOCK] = pe_t[:, cols]
        else:
            qcat_ref[h, :, :LANES] = (lat * ATT_SCALE).astype(BF16)
            qcat_ref[h, :, LANES:] = (rope(pe) * ATT_SCALE).astype(BF16)

    ckv = _rms(zs[:, Q_LORA:Q_LORA + KV_LORA], gkvn_ref[...])
    ckv_ref[...] = ckv
    if transposed:
        for blk in range(ckv.shape[0] // KEY_TILE):
            maybe_vt_ref[0][blk] = jnp.transpose(ckv[blk * KEY_TILE:(blk + 1) * KEY_TILE]).astype(BF16)
    kpe = rope(zs[:, Q_LORA + KV_LORA:Q_LORA + KV_LORA + LANES])
    kpe_ref[...] = kpe[:, :ROPE_DIM]
    kcat_ref[:, :LANES] = ckv.astype(BF16)
    kcat_ref[:, LANES:] = kpe.astype(BF16)

    glr = zs[:, Q_LORA + KV_LORA + LANES:].astype(BF16)
    xg = jnp.dot(glr, wg2_ref[...], preferred_element_type=F32) + bg2_ref[...]
    la_ref[...] = (jnp.minimum(xg, 0.0) - jnp.log(1.0 + jnp.exp(-jnp.abs(xg)))) * (1.0 / GATE_TEMP)

    zqk = proj("gqk")
    nqk = GLA_HEADS * GLA_DK
    gq_ref[...] = (zqk[:, :nqk] * (GLA_DK ** -0.5)).astype(BF16)
    gk_ref[...] = zqk[:, nqk:].astype(BF16)
    gv_ref[...] = proj("gv").astype(BF16)
    gr = proj("gr")
    sgr_ref[...] = (gr * _sigmoid(gr)).astype(BF16)
    sga_ref[...] = _sigmoid(proj("ga")).astype(BF16)
    sgb_ref[...] = _sigmoid(proj("gb")).astype(BF16)


def _inproj(x2, cos_t, sin_t, w, transposed):
    n = x2.shape[0]
    tm = min(INPROJ_TILE, n)
    assert n % tm == 0 and cos_t.shape[0] % tm == 0
    nt = n // tm
    tab_blocks = cos_t.shape[0] // tm
    row = lambda width: pl.BlockSpec((tm, width), lambda i: (i, 0))
    tab = pl.BlockSpec((tm, LANES), lambda i: (i % tab_blocks, 0))
    nqk = GLA_HEADS * GLA_DK
    nv = GLA_HEADS * GLA_DV
    weights = [w["g_mix"], w["w_in"], w["g_qn"], w["g_kvn"], w["w_qcat"], w["w_g2"], w["b_g2"]]
    out_shape = [
        jax.ShapeDtypeStruct((MLA_HEADS, n, QCAT), BF16),
        jax.ShapeDtypeStruct((n, QCAT), BF16),
        jax.ShapeDtypeStruct((n, KV_LORA), F32),
        jax.ShapeDtypeStruct((n, ROPE_DIM), F32),
        jax.ShapeDtypeStruct((n, nqk), BF16),
        jax.ShapeDtypeStruct((n, nqk), BF16),
        jax.ShapeDtypeStruct((n, nv), BF16),
        jax.ShapeDtypeStruct((n, nqk), F32),
        jax.ShapeDtypeStruct((n, nv), BF16),
        jax.ShapeDtypeStruct((n, D_MODEL), BF16),
        jax.ShapeDtypeStruct((n, D_MODEL), BF16),
    ]
    out_specs = [
        pl.BlockSpec((MLA_HEADS, tm, QCAT), lambda i: (0, i, 0)),
        row(QCAT), row(KV_LORA), row(ROPE_DIM), row(nqk), row(nqk), row(nv), row(nqk), row(nv),
        row(D_MODEL), row(D_MODEL),
    ]
    if transposed:
        qb = tm // Q_BLOCK
        out_shape[0] = jax.ShapeDtypeStruct((n // Q_BLOCK, QCAT, MLA_HEADS * Q_BLOCK), BF16)
        out_specs[0] = pl.BlockSpec((qb, QCAT, MLA_HEADS * Q_BLOCK), lambda i: (i, 0, 0))
        kb = tm // KEY_TILE
        out_shape.append(jax.ShapeDtypeStruct((n // KEY_TILE, KV_LORA, KEY_TILE), BF16))
        out_specs.append(pl.BlockSpec((kb, KV_LORA, KEY_TILE), lambda i: (i, 0, 0)))
    resident = lambda a: pl.BlockSpec(a.shape, lambda i: (0,) * a.ndim, pipeline_mode=pl.Buffered(1))
    return pl.pallas_call(
        functools.partial(_inproj_kernel, transposed=transposed),
        grid=(nt,),
        in_specs=[row(D_MODEL), tab, tab] + [resident(a) for a in weights],
        out_specs=out_specs,
        out_shape=out_shape,
        compiler_params=_params(("parallel",)),
        name="inproj",
    )(x2, cos_t, sin_t, *weights)


def _head_out(o_lat, wuv_ref, rows):
    outs = []
    for h in range(MLA_HEADS):
        oh = o_lat[h * rows:(h + 1) * rows].astype(BF16)
        outs.append(jnp.dot(oh, wuv_ref[h], preferred_element_type=F32))
    return jnp.concatenate(outs, axis=1)


def _attn_prompt_kernel(qt_ref, k_ref, vt_ref, bias_ref, wuv_ref, o_ref, m_sc, l_sc, acc_sc):
    i = pl.program_id(1)
    m_sc[...] = jnp.full(m_sc.shape, -jnp.inf, F32)
    l_sc[...] = jnp.zeros(l_sc.shape, F32)
    acc_sc[...] = jnp.zeros(acc_sc.shape, F32)
    heads = [slice(h * Q_BLOCK, (h + 1) * Q_BLOCK) for h in range(MLA_HEADS)]
    ones_rows = jnp.ones((SUM_ROWS, KEY_TILE), BF16)

    def process(tiles, diagonal_last):
        nt = len(tiles)
        kbs = [k_ref[0, pl.ds(pl.multiple_of(kt * KEY_TILE, KEY_TILE), KEY_TILE), :] for kt in tiles]
        vts = [jnp.concatenate([vt_ref[kt], ones_rows], axis=0) for kt in tiles]
        ss = [[None] * MLA_HEADS for _ in range(nt)]
        for t in range(nt):
            for h, hs in enumerate(heads):
                s = jnp.dot(kbs[t], qt_ref[0, :, hs], preferred_element_type=F32)
                if diagonal_last and t == nt - 1:
                    s = s + bias_ref[...]
                ss[t][h] = s
        tmax = [[jnp.max(ss[t][h], axis=0, keepdims=True) for h in range(MLA_HEADS)] for t in range(nt)]
        scale = [[None] * MLA_HEADS for _ in range(nt)]
        mrun = [[None] * MLA_HEADS for _ in range(nt)]
        for h, hs in enumerate(heads):
            m = m_sc[:, hs]
            for t in range(nt):
                m_new = jnp.maximum(m, tmax[t][h])
                scale[t][h] = jnp.exp2(m - m_new)
                mrun[t][h] = m_new
                m = m_new
            m_sc[:, hs] = m
        pvs = [[jnp.dot(vts[t], jnp.exp2(ss[t][h] - mrun[t][h]).astype(BF16), preferred_element_type=F32)
                for h in range(MLA_HEADS)] for t in range(nt)]
        for h, hs in enumerate(heads):
            l, acc = l_sc[:, hs], acc_sc[:, hs]
            for t in range(nt):
                acc = scale[t][h] * acc + pvs[t][h][:KV_LORA]
                l = scale[t][h] * l + pvs[t][h][KV_LORA:KV_LORA + 1]
            l_sc[:, hs], acc_sc[:, hs] = l, acc

    def body(j, carry):
        process([2 * j, 2 * j + 1], False)
        return carry

    lax.fori_loop(0, i // 2, body, 0)

    @pl.when(i % 2 == 1)
    def _():
        process([i - 1, i], True)

    @pl.when(i % 2 == 0)
    def _():
        process([i], True)

    o_lat_t = (acc_sc[...] / l_sc[...]).astype(BF16)
    outs = []
    for h, hs in enumerate(heads):
        outs.append(lax.dot_general(o_lat_t[:, hs], wuv_ref[h], _TN, preferred_element_type=F32))
    o_ref[0] = jnp.concatenate(outs, axis=1).astype(BF16)


def _attn_prompt(qt, kcat, vt, wuv, b, l):
    assert Q_BLOCK == KEY_TILE and Q_BLOCK % CHUNK == 0
    nq = l // Q_BLOCK
    nkt = l // KEY_TILE
    cols = MLA_HEADS * Q_BLOCK
    pos_chunk = jnp.arange(Q_BLOCK, dtype=jnp.int32) // CHUNK
    bias = jnp.where(pos_chunk[:, None] <= pos_chunk[None, :], 0.0, -jnp.inf).astype(F32)
    return pl.pallas_call(
        _attn_prompt_kernel,
        grid=(b, nq),
        in_specs=[pl.BlockSpec((1, QCAT, cols), lambda bb, i: (bb * nq + i, 0, 0)),
                  pl.BlockSpec((1, l, QCAT), lambda bb, i: (bb, 0, 0)),
                  pl.BlockSpec((nkt, KV_LORA, KEY_TILE), lambda bb, i: (bb, 0, 0)),
                  _full(bias.shape), _full(wuv.shape)],
        out_specs=pl.BlockSpec((1, Q_BLOCK, MLA_HEADS * V_DIM), lambda bb, i: (bb, i, 0)),
        out_shape=jax.ShapeDtypeStruct((b, l, MLA_HEADS * V_DIM), BF16),
        scratch_shapes=[pltpu.VMEM((1, cols), F32), pltpu.VMEM((1, cols), F32),
                        pltpu.VMEM((KV_LORA, cols), F32)],
        compiler_params=_params(("parallel", "arbitrary")),
        name="attn_prompt",
    )(qt, kcat.reshape(b, l, QCAT), vt, bias, wuv)


def _attn_sample_kernel(q_ref, cckv_ref, ckr_ref, kn_ref, wuv_ref, o_ref, *, ls, seqs):
    for s in range(seqs):
        rows = slice(s * ls, (s + 1) * ls)
        q = q_ref[:, rows, :].reshape(MLA_HEADS * ls, QCAT)
        ck = cckv_ref[s].astype(BF16)
        kr = ckr_ref[s].astype(BF16)
        kn = kn_ref[rows, :]
        s_c = (lax.dot_general(q[:, :KV_LORA], ck, _NT, preferred_element_type=F32)
               + lax.dot_general(q[:, KV_LORA:KV_LORA + ROPE_DIM], kr, _NT, preferred_element_type=F32))
        s_n = lax.dot_general(q, kn, _NT, preferred_element_type=F32)
        m = jnp.maximum(jnp.max(s_c, axis=1, keepdims=True), jnp.max(s_n, axis=1, keepdims=True))
        p_c = jnp.exp2(s_c - m)
        p_n = jnp.exp2(s_n - m)
        den = jnp.sum(p_c, axis=1, keepdims=True) + jnp.sum(p_n, axis=1, keepdims=True)
        o_lat = (jnp.dot(p_c.astype(BF16), ck, preferred_element_type=F32)
                 + jnp.dot(p_n.astype(BF16), kn[:, :KV_LORA], preferred_element_type=F32)) / den
        o_ref[s] = _head_out(o_lat, wuv_ref, ls).astype(BF16)


def _attn_sample(qcat, kcat, cache_ckv, cache_krope, wuv, b, ls):
    past = cache_ckv.shape[1]
    assert past % CHUNK == 0 and ls <= CHUNK
    seqs = ATT_SAMPLE_SEQS if b % ATT_SAMPLE_SEQS == 0 else 1
    return pl.pallas_call(
        functools.partial(_attn_sample_kernel, ls=ls, seqs=seqs),
        grid=(b // seqs,),
        in_specs=[pl.BlockSpec((MLA_HEADS, seqs * ls, QCAT), lambda bb: (0, bb, 0)),
                  pl.BlockSpec((seqs, past, KV_LORA), lambda bb: (bb, 0, 0)),
                  pl.BlockSpec((seqs, past, ROPE_DIM), lambda bb: (bb, 0, 0)),
                  pl.BlockSpec((seqs * ls, QCAT), lambda bb: (bb, 0)),
                  _full(wuv.shape)],
        out_specs=pl.BlockSpec((seqs, ls, MLA_HEADS * V_DIM), lambda bb: (bb, 0, 0)),
        out_shape=jax.ShapeDtypeStruct((b, ls, MLA_HEADS * V_DIM), BF16),
        compiler_params=_params(("parallel",)),
        name="attn_sample",
    )(qcat, cache_ckv, cache_krope, kcat, wuv)


def _cumsum_rows(x):
    c = x.shape[0]
    row = lax.broadcasted_iota(jnp.int32, x.shape, 0)
    s = 1
    while s < c:
        x = x + jnp.where(row >= s, pltpu.roll(x, s, 0), 0.0)
        s *= 2
    return x


def _gla_kernel(q_ref, k_ref, v_ref, la_ref, sgr_ref, gn_ref, s0_ref, o_ref, sout_ref, st_sc, *, c):
    j = pl.program_id(1)

    @pl.when(j == 0)
    def _():
        st_sc[...] = s0_ref[...]

    half = 32
    row = lax.broadcasted_iota(jnp.int32, (c, GLA_DK), 0)
    rr = lax.broadcasted_iota(jnp.int32, (c, c), 0)
    cc = lax.broadcasted_iota(jnp.int32, (c, c), 1)
    causal = cc <= rr
    same_half = (rr < half) == (cc < half)
    gn = gn_ref[...]
    n_chunks = q_ref.shape[1] // c
    seq_heads = [(s, h) for s in range(GLA_BATCH) for h in range(GLA_HEADS)]
    chains = [(s, h, ci) for s, h in seq_heads for ci in range(n_chunks)]
    ksl = lambda h: slice(h * GLA_DK, (h + 1) * GLA_DK)
    vsl = lambda h: slice(h * GLA_DV, (h + 1) * GLA_DV)
    rsl = lambda ci: slice(ci * c, (ci + 1) * c)

    prep = {}
    for s, h, ci in chains:
        q = q_ref[s, rsl(ci), ksl(h)].astype(F32)
        k = k_ref[s, rsl(ci), ksl(h)].astype(F32)
        b = _cumsum_rows(la_ref[s, rsl(ci), ksl(h)])
        if c > half:
            mid = jnp.where(row < half, b[half // 2 - 1:half // 2, :], b[half + half // 2 - 1:half + half // 2, :])
        else:
            mid = jnp.broadcast_to(b[c // 2 - 1:c // 2, :], b.shape)
        last = b[c - 1:c, :]
        ops = {"qe": (q * jnp.exp(b - mid)).astype(BF16), "ke": (k * jnp.exp(mid - b)).astype(BF16),
               "q0": (q * jnp.exp(b)).astype(BF16), "kd": (k * jnp.exp(last - b)).astype(BF16), "last": last}
        if c > half:
            edge = b[half - 1:half, :]
            ops["qo"] = (q * jnp.exp(jnp.minimum(b - edge, 0.0))).astype(BF16)
            ops["ko"] = (k * jnp.exp(jnp.minimum(edge - b, 0.0))).astype(BF16)
        prep[s, h, ci] = ops
    att = {}
    for key in chains:
        a = lax.dot_general(prep[key]["qe"], prep[key]["ke"], _NT, preferred_element_type=F32)
        if c > half:
            a_off = lax.dot_general(prep[key]["qo"], prep[key]["ko"], _NT, preferred_element_type=F32)
            a = jnp.where(same_half, a, a_off)
        att[key] = jnp.where(causal, a, 0.0).astype(BF16)
    intra, upd, decay = {}, {}, {}
    for s, h, ci in chains:
        v = v_ref[s, rsl(ci), vsl(h)]
        intra[s, h, ci] = jnp.dot(att[s, h, ci], v, preferred_element_type=F32)
        upd[s, h, ci] = lax.dot_general(prep[s, h, ci]["kd"], v, _TN, preferred_element_type=F32)
        dcol = jnp.exp(jnp.transpose(jnp.broadcast_to(prep[s, h, ci]["last"], (GLA_DK, GLA_DK))))
        decay[s, h, ci] = jnp.concatenate([dcol, dcol], axis=1)
    for s, h in seq_heads:
        st = st_sc[s, h]
        for ci in range(n_chunks):
            o = jnp.dot(prep[s, h, ci]["q0"], st.astype(BF16), preferred_element_type=F32) + intra[s, h, ci]
            st = decay[s, h, ci] * st + upd[s, h, ci]
            on = _rms(o, gn) * sgr_ref[s, rsl(ci), vsl(h)].astype(F32)
            o_ref[s, rsl(ci), vsl(h)] = on.astype(BF16)
        st_sc[s, h] = st

    @pl.when(j == pl.num_programs(1) - 1)
    def _():
        sout_ref[...] = st_sc[...]


def _gla(gq, gk, gv, la, sgr, gn, s0, b, l):
    c = min(GLA_CHUNK, l)
    step = min(GLA_STEP_TOKENS, l)
    nc = l // step
    nqk = GLA_HEADS * GLA_DK
    nv = GLA_HEADS * GLA_DV
    r3 = lambda a: a.reshape(b, l, a.shape[-1])
    assert b % GLA_BATCH == 0 and l % step == 0 and step % c == 0
    tok = lambda width: pl.BlockSpec((GLA_BATCH, step, width), lambda bb, j: (bb, j, 0))
    st = pl.BlockSpec((GLA_BATCH, GLA_HEADS, GLA_DK, GLA_DV), lambda bb, j: (bb, 0, 0, 0))
    return pl.pallas_call(
        functools.partial(_gla_kernel, c=c),
        grid=(b // GLA_BATCH, nc),
        in_specs=[tok(nqk), tok(nqk), tok(nv), tok(nqk), tok(nv), _full(gn.shape), st],
        out_specs=[tok(nv), st],
        out_shape=[jax.ShapeDtypeStruct((b, l, nv), BF16),
                   jax.ShapeDtypeStruct((b, GLA_HEADS, GLA_DK, GLA_DV), F32)],
        scratch_shapes=[pltpu.VMEM((GLA_BATCH, GLA_HEADS, GLA_DK, GLA_DV), F32)],
        compiler_params=_params(("parallel", "arbitrary")),
        name="gla",
    )(r3(gq), r3(gk), r3(gv), r3(la), r3(sgr), gn, s0)


_ROWS_A = slice(0, MLA_HEADS * V_DIM)
_ROWS_B = slice(_ROWS_A.stop, _ROWS_A.stop + GLA_HEADS * GLA_DV)
_ROWS_O = slice(_ROWS_B.stop, _ROWS_B.stop + D_MODEL)


def _merge_kernel(oa_ref, ob_ref, sga_ref, sgb_ref, x_ref, wabo_ref, gffn_ref,
                  wr_ref, br_ref, cnt0_ref, x1_ref, h2_ref, route_ref, gcol_ref, cnt_ref, meta_ref, run_sc):
    assert MERGE_PARTS == 1
    tm = x_ref.shape[0]
    pr = tm // MERGE_PARTS
    wr = wr_ref[...]
    whi = wr.astype(BF16)
    wlo = (wr - whi.astype(F32)).astype(BF16)
    ridx = lax.broadcasted_iota(jnp.int32, (EXPERTS_PER_GROUP, pr), 0)
    eidx = lax.broadcasted_iota(jnp.int32, (N_EXPERTS, pr), 0)
    earlier = jnp.where(lax.broadcasted_iota(jnp.int32, (pr, pr), 0) < lax.broadcasted_iota(jnp.int32, (pr, pr), 1),
                        1.0, 0.0).astype(BF16)
    big = jnp.int32(1 << 20)

    def top(vals):
        vmax = jnp.max(vals, axis=0, keepdims=True)
        imax = jnp.min(jnp.where(vals == vmax, ridx, big), axis=0, keepdims=True)
        return vmax, imax

    h2s = []
    for part in range(MERGE_PARTS):
        rows = slice(part * pr, (part + 1) * pr)
        ya = jnp.dot(oa_ref[rows, :], wabo_ref[_ROWS_A, :], preferred_element_type=F32)
        yb = jnp.dot(ob_ref[rows, :], wabo_ref[_ROWS_B, :], preferred_element_type=F32)
        merged = (sga_ref[rows, :].astype(F32) * ya + sgb_ref[rows, :].astype(F32) * yb).astype(BF16)
        x1 = x_ref[rows, :] + jnp.dot(merged, wabo_ref[_ROWS_O, :], preferred_element_type=F32)
        x1_ref[rows, :] = x1
        h2 = _rms(x1, gffn_ref[...])
        _store_token_major(h2_ref, h2, part * pr)
        h2s.append(h2)

    routed = []
    for part in range(MERGE_PARTS):
        rows = slice(part * pr, (part + 1) * pr)
        hi = h2s[part].astype(BF16)
        lo = (h2s[part] - hi.astype(F32)).astype(BF16)
        logits = (lax.dot_general(whi, hi, _NT, preferred_element_type=F32)
                  + lax.dot_general(whi, lo, _NT, preferred_element_type=F32)
                  + lax.dot_general(wlo, hi, _NT, preferred_element_type=F32)) + br_ref[...]
        gl = jnp.where(ridx < N_GROUPS, logits[N_EXPERTS:N_EXPERTS + EXPERTS_PER_GROUP], -jnp.inf)
        gmax, g_top = top(gl)
        p_top = 1.0 / jnp.sum(jnp.exp(gl - gmax), axis=0, keepdims=True)
        e_sel = logits[:EXPERTS_PER_GROUP]
        for g in range(1, N_GROUPS):
            e_sel = jnp.where(g_top == g, logits[g * EXPERTS_PER_GROUP:(g + 1) * EXPERTS_PER_GROUP], e_sel)
        v1, i1 = top(e_sel)
        v2, i2 = top(jnp.where(ridx == i1, -jnp.inf, e_sel))
        e21 = jnp.exp(v2 - v1)
        w1 = p_top / (1.0 + e21)
        w2 = p_top * e21 / (1.0 + e21)
        base = g_top * EXPERTS_PER_GROUP
        id0 = base + i1
        id1 = base + i2
        gates = jnp.where(ridx == 0, w1, jnp.where(ridx == 1, w2, 0.0))
        gcol_ref[rows, :] = jnp.transpose(
            jnp.concatenate([gates, jnp.zeros((LANES - EXPERTS_PER_GROUP, pr), F32)], axis=0))
        oh0 = jnp.where(eidx == id0, 1.0, 0.0)
        oh1 = jnp.where(eidx == id1, 1.0, 0.0)
        p0 = jnp.dot(oh0.astype(BF16), earlier, preferred_element_type=F32)
        p1 = jnp.dot(oh1.astype(BF16), earlier, preferred_element_type=F32)
        c0 = jnp.sum(oh0, axis=1, keepdims=True)
        c1 = jnp.sum(oh1, axis=1, keepdims=True)
        routed.append((id0, id1, oh0, oh1, p0, p1, c0, c1))

    @pl.when(pl.program_id(0) == 0)
    def _():
        run_sc[...] = cnt0_ref[:, :1]

    run = run_sc[...]
    lane = lax.broadcasted_iota(jnp.int32, (N_EXPERTS, LANES), 1)
    for part, (id0, id1, oh0, oh1, p0, p1, c0, c1) in enumerate(routed):
        rank0 = jnp.sum(oh0 * (run + p0), axis=0, keepdims=True)
        rank1 = jnp.sum(oh1 * (run + c0 + p1), axis=0, keepdims=True)
        length = c0 + c1
        windows = jnp.ceil(length * (1.0 / RUN_WINDOW)) * RUN_WINDOW
        windows_b = jnp.broadcast_to(windows, (N_EXPERTS, LANES))
        stage_off = (_cumsum_rows(windows_b) - windows_b)[:, :1]
        pos0 = jnp.sum(oh0 * (stage_off + p0), axis=0, keepdims=True)
        pos1 = jnp.sum(oh1 * (stage_off + c0 + p1), axis=0, keepdims=True)
        win_row = (lane * RUN_WINDOW).astype(F32)
        expert_of = jnp.sum(jnp.where(stage_off + windows <= win_row, 1.0, 0.0), axis=0, keepdims=True)
        n_win = jnp.sum(windows, axis=0, keepdims=True) * (1.0 / RUN_WINDOW)
        expert_of = jnp.minimum(expert_of, N_EXPERTS - 1.0)
        eidx_w = lax.broadcasted_iota(jnp.int32, (N_EXPERTS, LANES), 0).astype(F32)
        rel_row = jnp.sum(jnp.where(eidx_w == expert_of, run - stage_off, 0.0), axis=0, keepdims=True) + win_row[:1]
        r8 = lax.broadcasted_iota(jnp.int32, (EXPERTS_PER_GROUP, LANES), 0)
        meta_ref[...] = jnp.where(r8 == 0, expert_of.astype(jnp.int32), jnp.where(
            r8 == 1, rel_row.astype(jnp.int32), jnp.where(r8 == 2, n_win.astype(jnp.int32), 0)))
        run = run + length
        vals = (id0, id1, rank0.astype(jnp.int32), rank1.astype(jnp.int32),
                pos0.astype(jnp.int32), pos1.astype(jnp.int32))
        out = jnp.zeros((EXPERTS_PER_GROUP, pr), jnp.int32)
        for r, v in enumerate(vals):
            out = jnp.where(ridx == r, v, out)
        route_ref[:, part * pr:(part + 1) * pr] = out
    run_sc[...] = run
    cnt_ref[...] = jnp.broadcast_to(run, cnt_ref.shape)


def _merge_tile(n):
    return min(MERGE_TILE, max(ROW_TILE, n // 4))


def _merge(oa, ob, sga, sgb, x2, w, cnt0):
    n = x2.shape[0]
    tm = _merge_tile(n)
    assert n % tm == 0
    row = lambda width: pl.BlockSpec((tm, width), lambda i: (i, 0))
    weights = [w["w_abo"], w["g_ffn"], w["w_router"], w["b_router"], cnt0]
    return pl.pallas_call(
        _merge_kernel,
        grid=(n // tm,),
        in_specs=[row(MLA_HEADS * V_DIM), row(GLA_HEADS * GLA_DV), row(D_MODEL), row(D_MODEL),
                  row(D_MODEL)] + [_full(a.shape) for a in weights],
        out_specs=[row(D_MODEL), pl.BlockSpec((tm * SLABS, LANES), lambda i: (i, 0)),
                   pl.BlockSpec((EXPERTS_PER_GROUP, tm), lambda i: (0, i)), row(LANES),
                   _full((N_EXPERTS, LANES)), pl.BlockSpec((EXPERTS_PER_GROUP, LANES), lambda i: (i, 0))],
        out_shape=[jax.ShapeDtypeStruct((n, D_MODEL), F32), jax.ShapeDtypeStruct((n * SLABS, LANES), F32),
                   jax.ShapeDtypeStruct((EXPERTS_PER_GROUP, n), jnp.int32),
                   jax.ShapeDtypeStruct((n, LANES), F32),
                   jax.ShapeDtypeStruct((N_EXPERTS, LANES), F32),
                   jax.ShapeDtypeStruct((n // tm * EXPERTS_PER_GROUP, LANES), jnp.int32)],
        scratch_shapes=[pltpu.VMEM((N_EXPERTS, 1), F32)],
        compiler_params=_params(("arbitrary",)),
        name="merge",
    )(oa, ob, sga, sgb, x2, *weights)


ROW_DMA_UNROLL = 8


def _row_tile(ref, r):
    return ref.at[pl.ds(pl.multiple_of(r * SLABS, SLABS), SLABS), :]


def _scatter_kernel(ps_ref, cnt_ref, pe_ref, *rest, tiles):
    ng = len(tiles)
    meta_refs, prev_refs, pos_refs, h2_refs = (rest[k * ng:(k + 1) * ng] for k in range(4))
    xd_hbm, zero_sc, stage, sem, zsem = rest[4 * ng:]

    def zero_padding():
        zero_sc[...] = jnp.zeros(zero_sc.shape, F32)

        def pad_copies(e, act):
            lo = ps_ref[e] + cnt_ref[e]
            pad = pe_ref[e] - lo
            bit = FFN_ROWS
            while bit >= 1:
                @pl.when((pad & bit) != 0)
                def _(bit=bit):
                    first = lo + (pad & ~(2 * bit - 1))
                    act(pltpu.make_async_copy(
                        zero_sc.at[pl.ds(0, bit * SLABS), :],
                        xd_hbm.at[pl.ds(pl.multiple_of(first * SLABS, SLABS), bit * SLABS), :], zsem.at[0]))
                bit //= 2

        def start_pads(e, c):
            pad_copies(e, lambda cp: cp.start())
            return c

        def wait_pads(e, c):
            pad_copies(e, lambda cp: cp.wait())
            return c

        lax.fori_loop(0, N_EXPERTS, start_pads, 0)
        lax.fori_loop(0, N_EXPERTS, wait_pads, 0)

        def block_copy(b):
            start_row = pl.multiple_of(b * (FFN_ROWS * SLABS), FFN_ROWS * SLABS)
            return pltpu.make_async_copy(zero_sc, xd_hbm.at[pl.ds(start_row, FFN_ROWS * SLABS), :], zsem.at[0])

        first_unused = pe_ref[N_EXPERTS - 1] // FFN_ROWS
        n_blocks = xd_hbm.shape[0] // (FFN_ROWS * SLABS)

        def start_block(b, c):
            block_copy(b).start()
            return c

        def wait_block(b, c):
            block_copy(b).wait()
            return c

        lax.fori_loop(first_unused, n_blocks, start_block, 0)
        lax.fori_loop(first_unused, n_blocks, wait_block, 0)

    step = pl.program_id(0)
    pl.when(step == 0)(zero_padding)

    def windows(meta, s, act):
        def per_window(j, c):
            row = ps_ref[meta[0, j]] + meta[1, j]
            act(pltpu.make_async_copy(
                stage.at[s, pl.ds(pl.multiple_of(j * (RUN_WINDOW * SLABS), RUN_WINDOW * SLABS), RUN_WINDOW * SLABS), :],
                xd_hbm.at[pl.ds(pl.multiple_of(row * SLABS, SLABS), RUN_WINDOW * SLABS), :],
                sem.at[s]))
            return c

        lax.fori_loop(0, meta[2, 0], per_window, 0)

    first_step = 0
    for g, nt in enumerate(tiles):
        @pl.when((step >= first_step) & (step < first_step + nt))
        def _(g=g, nt=nt, first_step=first_step):
            meta_ref, prev_ref, pos_ref, h2_ref = meta_refs[g], prev_refs[g], pos_refs[g], h2_refs[g]
            tm = h2_ref.shape[0] // SLABS
            local = step - first_step
            slot = local % 2
            stage_rows = 2 * tm + N_EXPERTS * RUN_WINDOW

            @pl.when(local < 2)
            def _():
                stage[slot, pl.ds(0, stage_rows * SLABS), :] = jnp.zeros((stage_rows * SLABS, LANES), F32)

            def body(t, carry):
                for k in range(2):
                    p = pl.multiple_of(pos_ref[0, 0, k * tm + t], SLABS)
                    stage[slot, pl.ds(p, SLABS), :] = _row_tile(h2_ref, t)[...]
                return carry

            lax.fori_loop(0, tm, body, 0, unroll=ROW_DMA_UNROLL)

            @pl.when(local > 0)
            def _():
                windows(prev_ref, 1 - slot, lambda cp: cp.wait())

            windows(meta_ref, slot, lambda cp: cp.start())

            @pl.when(local == nt - 1)
            def _():
                windows(meta_ref, slot, lambda cp: cp.wait())
        first_step += nt


def _scatter(pad_start, counts, pad_end, metas, pos_blocks, h2s, total_rows):
    tms = [p.shape[2] // 2 for p in pos_blocks]
    tiles = [p.shape[0] for p in pos_blocks]
    firsts = [sum(tiles[:g]) for g in range(len(tiles))]
    ng = len(tiles)
    local = lambda g, shift=0: (lambda i, *_: jnp.clip(i - firsts[g] - shift, 0, tiles[g] - 1))
    smem = pltpu.SMEM
    in_specs = ([pl.BlockSpec((EXPERTS_PER_GROUP, LANES), (lambda i, *_, f=local(g): (f(i), 0)), memory_space=smem)
                 for g in range(ng)]
                + [pl.BlockSpec((EXPERTS_PER_GROUP, LANES), (lambda i, *_, f=local(g, 1): (f(i), 0)), memory_space=smem)
                   for g in range(ng)]
                + [pl.BlockSpec((1, 1, 2 * tms[g]), (lambda i, *_, f=local(g): (f(i), 0, 0)), memory_space=smem)
                   for g in range(ng)]
                + [pl.BlockSpec((tms[g] * SLABS, LANES), (lambda i, *_, f=local(g): (f(i), 0)))
                   for g in range(ng)])
    stage_rows = 2 * max(tms) + N_EXPERTS * RUN_WINDOW
    grid_spec = pltpu.PrefetchScalarGridSpec(
        num_scalar_prefetch=3,
        grid=(sum(tiles),),
        in_specs=in_specs,
        out_specs=pl.BlockSpec(memory_space=pl.ANY),
        scratch_shapes=[pltpu.VMEM((FFN_ROWS * SLABS, LANES), F32),
                        pltpu.VMEM((2, stage_rows * SLABS, LANES), F32),
                        pltpu.SemaphoreType.DMA((2,)), pltpu.SemaphoreType.DMA((1,))],
    )
    return pl.pallas_call(
        functools.partial(_scatter_kernel, tiles=tuple(tiles)),
        grid_spec=grid_spec,
        out_shape=jax.ShapeDtypeStruct((total_rows * SLABS, LANES), F32),
        compiler_params=_params(("arbitrary",)),
        name="moe_scatter",
    )(pad_start, counts, pad_end, *metas, *metas, *pos_blocks, *h2s)


def _ffn_kernel(be_ref, nbu_ref, x_ref, wg_ref, wu_ref, wd_ref, out_ref, wg_sc, wu_sc, wd_sc):
    i = pl.program_id(0)
    used = i < nbu_ref[0]
    new_expert = (i == 0) | (be_ref[i] != be_ref[jnp.maximum(i - 1, 0)])

    @pl.when(used & new_expert)
    def _():
        wg_sc[...] = wg_ref[0].astype(BF16)
        wu_sc[...] = wu_ref[0].astype(BF16)
        wd_sc[...] = wd_ref[0].astype(BF16)

    @pl.when(used)
    def _():
        x = _load_token_major(x_ref, FFN_ROWS).astype(BF16)
        g = jnp.dot(x, wg_sc[...], preferred_element_type=F32)
        u = jnp.dot(x, wu_sc[...], preferred_element_type=F32)
        mid = (g * _sigmoid(g) * u).astype(BF16)
        _store_token_major(out_ref, jnp.dot(mid, wd_sc[...], preferred_element_type=F32))

    @pl.when(jnp.logical_not(used))
    def _():
        out_ref[...] = jnp.zeros(out_ref.shape, F32)


def _ffn(block_expert, nb_used, xd, wg, wu, wd):
    nb = block_expert.shape[0]
    blk = lambda i, be, nbu: (jnp.minimum(i, nbu[0] - 1), 0)
    wsel = lambda i, be, nbu: (be[jnp.minimum(i, nbu[0] - 1)], 0, 0)
    grid_spec = pltpu.PrefetchScalarGridSpec(
        num_scalar_prefetch=2,
        grid=(nb,),
        in_specs=[pl.BlockSpec((FFN_ROWS * SLABS, LANES), blk),
                  pl.BlockSpec((1, D_MODEL, EXPERT_FF), wsel),
                  pl.BlockSpec((1, D_MODEL, EXPERT_FF), wsel),
                  pl.BlockSpec((1, EXPERT_FF, D_MODEL), wsel)],
        out_specs=pl.BlockSpec((FFN_ROWS * SLABS, LANES), lambda i, be, nbu: (i, 0)),
        scratch_shapes=[pltpu.VMEM((D_MODEL, EXPERT_FF), BF16), pltpu.VMEM((D_MODEL, EXPERT_FF), BF16),
                        pltpu.VMEM((EXPERT_FF, D_MODEL), BF16)],
    )
    return pl.pallas_call(
        _ffn_kernel,
        grid_spec=grid_spec,
        out_shape=jax.ShapeDtypeStruct((nb * FFN_ROWS * SLABS, LANES), F32),
        compiler_params=_params(("arbitrary",)),
        name="expert_ffn",
    )(block_expert, nb_used, xd, wg, wu, wd)


def _combine_kernel(ps_ref, meta_ref, metan_ref, pos_ref, x1_ref, gcol_ref, gfin_ref, eo_hbm, y_ref,
                    stage, ybuf, sem):
    i = pl.program_id(0)
    n = pl.num_programs(0)
    tm = x1_ref.shape[0]
    slot = i % 2

    def windows(meta, s, act):
        def per_window(j, c):
            row = ps_ref[meta[0, j]] + meta[1, j]
            act(pltpu.make_async_copy(
                eo_hbm.at[pl.ds(pl.multiple_of(row * SLABS, SLABS), RUN_WINDOW * SLABS), :],
                stage.at[s, pl.ds(pl.multiple_of(j * (RUN_WINDOW * SLABS), RUN_WINDOW * SLABS), RUN_WINDOW * SLABS), :],
                sem.at[s]))
            return c

        lax.fori_loop(0, meta[2, 0], per_window, 0)

    @pl.when(i == 0)
    def _():
        windows(meta_ref, 0, lambda cp: cp.start())

    @pl.when(i + 1 < n)
    def _():
        windows(metan_ref, 1 - slot, lambda cp: cp.start())

    windows(meta_ref, slot, lambda cp: cp.wait())

    def body(t, carry):
        for k in range(2):
            p = pl.multiple_of(pos_ref[0, 0, k * tm + t], SLABS)
            _row_tile(ybuf.at[k], t)[...] = stage[slot, pl.ds(p, SLABS), :]
        return carry

    lax.fori_loop(0, tm, body, 0, unroll=ROW_DMA_UNROLL)
    g = gcol_ref[...]
    y = x1_ref[...] + (g[:, 0:1] * _load_token_major(ybuf.at[0], tm) + g[:, 1:2] * _load_token_major(ybuf.at[1], tm))
    y_ref[...] = _rms(y, gfin_ref[...])


def _combine(pad_start, meta, pos_blocks, x1, gcol, g_final, eo):
    n = x1.shape[0]
    tm = _merge_tile(n)
    nt = n // tm
    stage_rows = 2 * tm + N_EXPERTS * RUN_WINDOW
    grid_spec = pltpu.PrefetchScalarGridSpec(
        num_scalar_prefetch=1,
        grid=(nt,),
        in_specs=[pl.BlockSpec((EXPERTS_PER_GROUP, LANES), lambda i, ps: (i, 0), memory_space=pltpu.SMEM),
                  pl.BlockSpec((EXPERTS_PER_GROUP, LANES), lambda i, ps: (jnp.minimum(i + 1, nt - 1), 0),
                               memory_space=pltpu.SMEM),
                  pl.BlockSpec((1, 1, 2 * tm), lambda i, ps: (i, 0, 0), memory_space=pltpu.SMEM),
                  pl.BlockSpec((tm, D_MODEL), lambda i, ps: (i, 0)),
                  pl.BlockSpec((tm, LANES), lambda i, ps: (i, 0)),
                  pl.BlockSpec(g_final.shape, lambda i, ps: (0, 0)),
                  pl.BlockSpec(memory_space=pl.ANY)],
        out_specs=pl.BlockSpec((tm, D_MODEL), lambda i, ps: (i, 0)),
        scratch_shapes=[pltpu.VMEM((2, stage_rows * SLABS, LANES), F32),
                        pltpu.VMEM((2, tm * SLABS, LANES), F32), pltpu.SemaphoreType.DMA((2,))],
    )
    return pl.pallas_call(
        _combine_kernel,
        grid_spec=grid_spec,
        out_shape=jax.ShapeDtypeStruct((n, D_MODEL), F32),
        compiler_params=_params(("arbitrary",)),
        name="combine",
    )(pad_start, meta, meta, pos_blocks, x1, gcol, g_final, eo)


def _layout(cnt, n_tokens):
    counts = cnt[:, 0].astype(jnp.int32)
    padded = (counts + RUN_WINDOW + FFN_ROWS - 1) // FFN_ROWS * FFN_ROWS
    pad_end = jnp.cumsum(padded).astype(jnp.int32)
    pad_start = pad_end - padded
    nb = -(-(2 * n_tokens + N_EXPERTS * RUN_WINDOW) // FFN_ROWS) + N_EXPERTS + 1
    block_start = jnp.arange(nb, dtype=jnp.int32) * FFN_ROWS
    block_expert = jnp.minimum(jnp.sum((block_start[:, None] >= pad_end[None, :]).astype(jnp.int32), axis=1),
                               N_EXPERTS - 1).astype(jnp.int32)
    nb_used = (pad_end[N_EXPERTS - 1:] // FFN_ROWS).astype(jnp.int32)
    return pad_start, counts, pad_end, block_expert, nb_used, nb * FFN_ROWS


def _rope_tables(l, past, rows):
    half = ROPE_DIM // 2
    inv = ROPE_THETA ** (-jnp.arange(half, dtype=F32) / half)
    pos = (past + (jnp.arange(rows, dtype=jnp.int32) % l)).astype(F32)
    ang = pos[:, None] * inv[None, :]
    pad = jnp.zeros((rows, LANES - ROPE_DIM), F32)
    cos = jnp.concatenate([jnp.cos(ang), jnp.cos(ang), pad], axis=1)
    sin = jnp.concatenate([-jnp.sin(ang), jnp.sin(ang), pad], axis=1)
    return cos, sin


def _mixers(x, ckv_past, kpe_past, s0, w):
    b, l, d = x.shape
    n = b * l
    x2 = x.reshape(n, d)
    past = 0 if ckv_past is None else ckv_past.shape[1]
    cos_t, sin_t = _rope_tables(l, past, max(l, INPROJ_TILE))
    prompt = ckv_past is None
    (qcat, kcat, ckv, kpe, gq, gk, gv, la, sgr, sga, sgb, *maybe_vt) = _inproj(x2, cos_t, sin_t, w, prompt)
    if prompt:
        oa = _attn_prompt(qcat, kcat, maybe_vt[0], w["w_uv"], b, l)
    else:
        oa = _attn_sample(qcat, kcat, ckv_past, kpe_past, w["w_uv"], b, l)
    ob, s_new = _gla(gq, gk, gv, la, sgr, w["g_gla"], s0, b, l)
    merge_in = (oa.reshape(n, -1), ob.reshape(n, -1), sga, sgb, x2)
    return merge_in, (ckv.reshape(b, l, KV_LORA), kpe.reshape(b, l, ROPE_DIM), s_new)


def _layer(groups, w, g_final):
    fronts = [_mixers(*g, w) for g in groups]
    cnt = jnp.zeros((N_EXPERTS, LANES), F32)
    merged = []
    for merge_in, _ in fronts:
        x1, h2, route, gcol, cnt, meta = _merge(*merge_in, w, cnt)
        merged.append((x1, h2, route, gcol, meta))
    n_tokens = sum(m[0].shape[0] for m in merged)
    pad_start, counts, pad_end, block_expert, nb_used, total_rows = _layout(cnt, n_tokens)
    pos_blocks = []
    for x1, h2, route, gcol, meta in merged:
        n = x1.shape[0]
        tm = _merge_tile(n)
        pos_blocks.append((route[4:6] * SLABS).reshape(2, n // tm, tm).transpose(1, 0, 2)
                          .reshape(n // tm, 1, 2 * tm))
    xd = _scatter(pad_start, counts, pad_end, [m[4] for m in merged], pos_blocks, [m[1] for m in merged],
                  total_rows)
    eo = _ffn(block_expert, nb_used, xd, w["w_eg"], w["w_eu"], w["w_ed"])
    outs = []
    for (x1, h2, route, gcol, meta), pos, (_, extras), g in zip(merged, pos_blocks, fronts, groups):
        y = _combine(pad_start, meta, pos, x1, gcol, g_final, eo)
        outs.append((y.reshape(g[0].shape),) + extras)
    return outs


def _prep_weights(w_in, g_norm_mix, g_qnorm, w_uq, g_kvnorm, w_ukv, w_gate2, b_gate2, g_gla_norm,
                  w_branch_a, w_branch_b, w_out, g_norm_ffn, w_router_group, b_router_group,
                  w_router_expert, b_router_expert, w_exp_gate, w_exp_up, w_exp_down):
    nqk = GLA_HEADS * GLA_DK
    nv = GLA_HEADS * GLA_DV
    o = 0
    parts = {}
    for name, width in (("cq", Q_LORA), ("ckv", KV_LORA), ("kpe", ROPE_DIM), ("gq", nqk), ("gk", nqk),
                        ("gv", nv), ("glr", GATE_RANK), ("gr", nv), ("ga", D_MODEL), ("gb", D_MODEL)):
        parts[name] = w_in[:, o:o + width]
        o += width
    padc = lambda a, width: jnp.pad(a, ((0, 0), (0, width - a.shape[1])))
    w_in_all = jnp.concatenate([parts["cq"], parts["ckv"], padc(parts["kpe"], LANES), padc(parts["glr"], LANES),
                                parts["gq"], parts["gk"], parts["gv"], parts["gr"], parts["ga"], parts["gb"]],
                               axis=1).astype(BF16)
    assert w_in_all.shape[1] == _IN_COLS["gb"].stop
    w_abo = jnp.concatenate([w_branch_a, w_branch_b, w_out], axis=0).astype(BF16)
    uq = w_uq.reshape(Q_LORA, MLA_HEADS, NOPE_DIM + ROPE_DIM)
    ukv = w_ukv.reshape(KV_LORA, MLA_HEADS, NOPE_DIM + V_DIM)
    lat = _fold_q(uq[:, :, :NOPE_DIM].transpose(1, 0, 2), ukv[:, :, :NOPE_DIM].transpose(1, 0, 2))
    q_rope = uq[:, :, NOPE_DIM:].transpose(1, 0, 2)
    w_qcat = jnp.concatenate([lat, q_rope, jnp.zeros((MLA_HEADS, Q_LORA, QCAT - KV_LORA - ROPE_DIM), F32)],
                             axis=2)
    w_qcat = w_qcat.transpose(1, 0, 2).reshape(Q_LORA, MLA_HEADS * QCAT).astype(BF16)
    w_router = jnp.concatenate([w_router_expert.T, w_router_group.T,
                                jnp.zeros((ROUTER_ROWS - N_EXPERTS - N_GROUPS, D_MODEL), F32)], axis=0)
    b_router = jnp.concatenate([b_router_expert, b_router_group,
                                jnp.zeros((ROUTER_ROWS - N_EXPERTS - N_GROUPS,), F32)]).reshape(ROUTER_ROWS, 1)
    return {
        "g_mix": g_norm_mix.reshape(1, D_MODEL), "w_in": w_in_all,
        "g_qn": g_qnorm.reshape(1, Q_LORA), "g_kvn": g_kvnorm.reshape(1, KV_LORA), "w_qcat": w_qcat,
        "w_g2": jnp.pad(w_gate2, ((0, LANES - GATE_RANK), (0, 0))).astype(BF16),
        "b_g2": b_gate2.reshape(1, nqk),
        "w_uv": ukv[:, :, NOPE_DIM:].transpose(1, 0, 2).astype(BF16),
        "g_gla": g_gla_norm.reshape(1, GLA_DV),
        "w_abo": w_abo,
        "g_ffn": g_norm_ffn.reshape(1, D_MODEL), "w_router": w_router, "b_router": b_router,
        "w_eg": w_exp_gate, "w_eu": w_exp_up, "w_ed": w_exp_down,
    }


def kernel(x_prompt, x_sample, cache_ckv, cache_krope, state_gla, w_in, g_norm_mix, g_qnorm, w_uq, g_kvnorm, w_ukv, w_gate2, b_gate2, g_gla_norm, w_branch_a, w_branch_b, w_out, g_norm_ffn, w_router_group, b_router_group, w_router_expert, b_router_expert, w_exp_gate, w_exp_up, w_exp_down, g_norm_final):
    depth = w_in.shape[0]
    assert depth == 1, "the final norm is fused into the last layer's combine step"
    gfin = g_norm_final.reshape(1, D_MODEL)
    drop = lambda a: a.reshape(a.shape[1:])
    lift = lambda a: a.reshape((1,) + a.shape)
    w = _prep_weights(*[drop(a) for a in (
        w_in, g_norm_mix, g_qnorm, w_uq, g_kvnorm, w_ukv, w_gate2, b_gate2, g_gla_norm, w_branch_a, w_branch_b,
        w_out, g_norm_ffn, w_router_group, b_router_group, w_router_expert, b_router_expert, w_exp_gate,
        w_exp_up, w_exp_down)])
    bp = x_prompt.shape[0]
    zero_state = jnp.zeros((bp, GLA_HEADS, GLA_DK, GLA_DV), F32)
    (yp, c1, k1, s1), (ys, c2, k2, s2) = _layer(
        [(x_prompt, None, None, zero_state),
         (x_sample, drop(cache_ckv), drop(cache_krope), drop(state_gla).astype(F32))], w, gfin)
    return (yp, ys, lift(c1), lift(k1), lift(s1), lift(c2), lift(k2), lift(s2))
```
